```python
import jax, jax.numpy as jnp
from jax import lax
import numpy as np

D_MODEL = 4096
BATCH = 8
SEQ = 4096
DEPTH = 1

D_RWKV = D_MODEL // 2
RWKV_HEAD = 64
RWKV_HEADS = D_RWKV // RWKV_HEAD
DECAY_RANK = 96
ICLR_RANK = 96
GATE_RANK = 256
D_CONV = D_MODEL // 2
CONV_WIDTH = 31
MEM_LEN = 256
XATTN_HEADS = 4
XATTN_HEAD_DIM = D_MODEL // XATTN_HEADS
D_FF = 4 * D_MODEL
DEEPNORM_ALPHA = float((2 * DEPTH) ** 0.25)
DEEPNORM_BETA = float((8 * DEPTH) ** -0.25)
LN_EPS = 1e-5
GN_EPS = 64e-5

N_RWKV_COLS = 3 * D_RWKV + DECAY_RANK + ICLR_RANK + GATE_RANK
N_CONV_COLS = 2 * D_CONV
N_GATE_COLS = 2 * D_MODEL
D_IN = N_RWKV_COLS + N_CONV_COLS + N_GATE_COLS
RWKV_SPLITS = [D_RWKV, 2 * D_RWKV, 3 * D_RWKV, 3 * D_RWKV + DECAY_RANK, 3 * D_RWKV + DECAY_RANK + ICLR_RANK]

kernel_name = "rwkv7_conformer_gated_hybrid_deepnorm"


def layer_norm(x, g, b, eps=LN_EPS):
    xf = x.astype(jnp.float32)
    mu = jnp.mean(xf, -1, keepdims=True)
    var = jnp.mean(jnp.square(xf - mu), -1, keepdims=True)
    return ((xf - mu) * lax.rsqrt(var + eps) * g.astype(jnp.float32) + b.astype(jnp.float32)).astype(x.dtype)


def token_shift(z, mu):
    prev = jnp.pad(z, ((0, 0), (1, 0), (0, 0)))[:, :-1]
    return z + (prev - z) * mu


def rwkv7_scan(r, decay, k, v, a, b):
    B, S, H, N = r.shape

    def step(state, inp):
        r_t, w_t, k_t, v_t, a_t, b_t = inp
        sa = jnp.einsum('bhvk,bhk->bhv', state, a_t)
        state = (state * w_t[:, :, None, :] + sa[..., None] * b_t[:, :, None, :]
                 + v_t[..., None] * k_t[:, :, None, :])
        return state, jnp.einsum('bhvk,bhk->bhv', state, r_t)

    s0 = jnp.zeros((B, H, N, N), jnp.float32)
    xs = tuple(jnp.swapaxes(t, 0, 1) for t in (r, decay, k, v, a, b))
    _, out = lax.scan(step, s0, xs)
    return jnp.swapaxes(out, 0, 1)


def rwkv7_time_mix(z, shift_mix, w0, w_up, a0, a_up, g_up, k_k, k_a, r_k, gn_g, gn_b):
    B, S, _ = z.shape
    f32 = jnp.float32
    z = token_shift(z, shift_mix)
    r, k, v, dw, da, dg = jnp.split(z, RWKV_SPLITS, axis=-1)
    w = -jax.nn.softplus(-(w0 + jnp.tanh(dw) @ w_up)) - 0.5
    decay = jnp.exp(-jnp.exp(w.astype(f32)))
    a = jax.nn.sigmoid(a0 + da @ a_up)
    g = jax.nn.sigmoid(dg) @ g_up
    heads = lambda t: t.reshape(B, S, RWKV_HEADS, RWKV_HEAD).astype(f32)
    kk = heads(k * k_k)
    kk = kk / jnp.maximum(jnp.sqrt(jnp.sum(kk * kk, -1, keepdims=True)), 1e-12)
    k = k * (1.0 + (a - 1.0) * k_a)
    rh, kh, vh, ah = heads(r), heads(k), heads(v), heads(a)
    o = rwkv7_scan(rh, heads(decay), kh, vh, -kk, kk * ah)
    mu = jnp.mean(o, -1, keepdims=True)
    var = jnp.mean(jnp.square(o - mu), -1, keepdims=True)
    o = ((o - mu) * lax.rsqrt(var + GN_EPS)).reshape(B, S, D_RWKV) * gn_g.astype(f32) + gn_b.astype(f32)
    bonus = jnp.sum(rh * kh * r_k.astype(f32), -1, keepdims=True) * vh
    o = o + bonus.reshape(B, S, D_RWKV)
    return (o * g.astype(f32)).astype(z.dtype)


def conformer_conv(z, conv_w, conv_b, ln_g, ln_b):
    u = z[..., :D_CONV] * jax.nn.sigmoid(z[..., D_CONV:])
    u = lax.conv_general_dilated(u, conv_w, window_strides=(1,), padding=[(CONV_WIDTH - 1, 0)],
                                 dimension_numbers=('NWC', 'WIO', 'NWC'),
                                 feature_group_count=D_CONV) + conv_b
    return jax.nn.silu(layer_norm(u, ln_g, ln_b))


def memory_cross_attention(h, mem_n, wq, wk, wv, wo):
    B, S, _ = h.shape
    M = mem_n.shape[1]
    q = (h @ wq).reshape(B, S, XATTN_HEADS, XATTN_HEAD_DIM)
    k = (mem_n @ wk).reshape(B, M, XATTN_HEADS, XATTN_HEAD_DIM)
    v = (mem_n @ wv).reshape(B, M, XATTN_HEADS, XATTN_HEAD_DIM)
    s = jnp.einsum('bshd,bmhd->bhsm', q, k).astype(jnp.float32) * (XATTN_HEAD_DIM ** -0.5)
    p = jax.nn.softmax(s, axis=-1).astype(v.dtype)
    o = jnp.einsum('bhsm,bmhd->bshd', p, v).reshape(B, S, D_MODEL)
    return o @ wo


def _fwd_setup_inputs(seed: int = 0) -> dict:
    key = jax.random.key(seed)
    ks = iter(jax.random.split(key, 48))
    L = DEPTH
    nrm = lambda shape, scale: jax.random.normal(next(ks), shape, jnp.float32) * scale
    gain = lambda shape: 1.0 + nrm(shape, 0.02)
    beta = DEEPNORM_BETA
    lin = jnp.linspace(0.0, 1.0, D_RWKV, dtype=jnp.float32)
    w0 = -7.0 + 5.0 * lin ** 0.85 + 0.5
    return {
        "x": nrm((BATCH, SEQ, D_MODEL), 1.0),
        "mem": nrm((BATCH, MEM_LEN, D_MODEL), 1.0),
        "w_in": nrm((L, D_MODEL, D_IN), D_MODEL ** -0.5),
        "rwkv_shift_mix": jax.random.uniform(next(ks), (L, N_RWKV_COLS), jnp.float32),
        "rwkv_w0": w0[None, :] + nrm((L, D_RWKV), 0.05),
        "rwkv_w_up": nrm((L, DECAY_RANK, D_RWKV), 0.1 * DECAY_RANK ** -0.5),
        "rwkv_a0": nrm((L, D_RWKV), 0.1),
        "rwkv_a_up": nrm((L, ICLR_RANK, D_RWKV), 0.1 * ICLR_RANK ** -0.5),
        "rwkv_g_up": nrm((L, GATE_RANK, D_RWKV), GATE_RANK ** -0.5),
        "rwkv_k_k": 0.85 + nrm((L, D_RWKV), 0.05),
        "rwkv_k_a": 1.0 + nrm((L, D_RWKV), 0.05),
        "rwkv_r_k": nrm((L, RWKV_HEADS, RWKV_HEAD), 0.1),
        "rwkv_gn_g": gain((L, D_RWKV)),
        "rwkv_gn_b": nrm((L, D_RWKV), 0.02),
        "conv_w": nrm((L, CONV_WIDTH, 1, D_CONV), CONV_WIDTH ** -0.5),
        "conv_b": nrm((L, D_CONV), 0.02),
        "conv_ln_g": gain((L, D_CONV)),
        "conv_ln_b": nrm((L, D_CONV), 0.02),
        "proj_rwkv": nrm((L, D_RWKV, D_MODEL), beta * D_RWKV ** -0.5),
        "proj_conv": nrm((L, D_CONV, D_MODEL), beta * D_CONV ** -0.5),
        "w_out": nrm((L, D_MODEL, D_MODEL), beta * D_MODEL ** -0.5),
        "ln1_g": gain((L, D_MODEL)),
        "ln1_b": nrm((L, D_MODEL), 0.02),
        "ln_mem_g": gain((D_MODEL,)),
        "ln_mem_b": nrm((D_MODEL,), 0.02),
        "xattn_wq": nrm((L, D_MODEL, D_MODEL), D_MODEL ** -0.5),
        "xattn_wk": nrm((L, D_MODEL, D_MODEL), D_MODEL ** -0.5),
        "xattn_wv": nrm((L, D_MODEL, D_MODEL), beta * D_MODEL ** -0.5),
        "xattn_wo": nrm((L, D_MODEL, D_MODEL), beta * D_MODEL ** -0.5),
        "ln2_g": gain((L, D_MODEL)),
        "ln2_b": nrm((L, D_MODEL), 0.02),
        "mlp_w1": nrm((L, D_MODEL, D_FF), beta * D_MODEL ** -0.5),
        "mlp_w2": nrm((L, D_FF, D_MODEL), beta * D_FF ** -0.5),
        "ln3_g": gain((L, D_MODEL)),
        "ln3_b": nrm((L, D_MODEL), 0.02),
    }


def _fwd_reference(x, mem, w_in, rwkv_shift_mix, rwkv_w0, rwkv_w_up, rwkv_a0, rwkv_a_up, rwkv_g_up,
              rwkv_k_k, rwkv_k_a, rwkv_r_k, rwkv_gn_g, rwkv_gn_b, conv_w, conv_b, conv_ln_g,
              conv_ln_b, proj_rwkv, proj_conv, w_out, ln1_g, ln1_b, ln_mem_g, ln_mem_b,
              xattn_wq, xattn_wk, xattn_wv, xattn_wo, ln2_g, ln2_b, mlp_w1, mlp_w2, ln3_g, ln3_b):
    alpha = DEEPNORM_ALPHA
    mem_n = layer_norm(mem, ln_mem_g, ln_mem_b)
    h = x
    for l in range(DEPTH):
        z = h @ w_in[l]
        z_rwkv = z[..., :N_RWKV_COLS]
        z_conv = z[..., N_RWKV_COLS:N_RWKV_COLS + N_CONV_COLS]
        z_gate = z[..., N_RWKV_COLS + N_CONV_COLS:]
        o_r = rwkv7_time_mix(z_rwkv, rwkv_shift_mix[l], rwkv_w0[l], rwkv_w_up[l], rwkv_a0[l],
                             rwkv_a_up[l], rwkv_g_up[l], rwkv_k_k[l], rwkv_k_a[l], rwkv_r_k[l],
                             rwkv_gn_g[l], rwkv_gn_b[l])
        o_c = conformer_conv(z_conv, conv_w[l], conv_b[l], conv_ln_g[l], conv_ln_b[l])
        gate_r = jax.nn.sigmoid(z_gate[..., :D_MODEL])
        gate_c = jax.nn.sigmoid(z_gate[..., D_MODEL:])
        merged = gate_r * (o_r @ proj_rwkv[l]) + gate_c * (o_c @ proj_conv[l])
        h = layer_norm(alpha * h + merged @ w_out[l], ln1_g[l], ln1_b[l])
        ca = memory_cross_attention(h, mem_n, xattn_wq[l], xattn_wk[l], xattn_wv[l], xattn_wo[l])
        h = layer_norm(alpha * h + ca, ln2_g[l], ln2_b[l])
        ff = jnp.square(jax.nn.relu(h @ mlp_w1[l])) @ mlp_w2[l]
        h = layer_norm(alpha * h + ff, ln3_g[l], ln3_b[l])
    return h


import jax as _jax
import jax.numpy as _jnp

TWIN_FORMAT = 'train_step'
FWD_PARAMS = ['x', 'mem', 'w_in', 'rwkv_shift_mix', 'rwkv_w0', 'rwkv_w_up', 'rwkv_a0', 'rwkv_a_up', 'rwkv_g_up', 'rwkv_k_k', 'rwkv_k_a', 'rwkv_r_k', 'rwkv_gn_g', 'rwkv_gn_b', 'conv_w', 'conv_b', 'conv_ln_g', 'conv_ln_b', 'proj_rwkv', 'proj_conv', 'w_out', 'ln1_g', 'ln1_b', 'ln_mem_g', 'ln_mem_b', 'xattn_wq', 'xattn_wk', 'xattn_wv', 'xattn_wo', 'ln2_g', 'ln2_b', 'mlp_w1', 'mlp_w2', 'ln3_g', 'ln3_b']
TWIN_WEIGHTS = ['w_in', 'rwkv_shift_mix', 'rwkv_w0', 'rwkv_w_up', 'rwkv_a0', 'rwkv_a_up', 'rwkv_g_up', 'rwkv_k_k', 'rwkv_k_a', 'rwkv_r_k', 'rwkv_gn_g', 'rwkv_gn_b', 'conv_w', 'conv_b', 'conv_ln_g', 'conv_ln_b', 'proj_rwkv', 'proj_conv', 'w_out', 'ln1_g', 'ln1_b', 'ln_mem_g', 'ln_mem_b', 'xattn_wq', 'xattn_wk', 'xattn_wv', 'xattn_wo', 'ln2_g', 'ln2_b', 'mlp_w1', 'mlp_w2', 'ln3_g', 'ln3_b']
TWIN_DIFF_INPUT = 'x'
TWIN_INPUTS = ['x', 'mem', 'w_in', 'rwkv_shift_mix', 'rwkv_w0', 'rwkv_w_up', 'rwkv_a0', 'rwkv_a_up', 'rwkv_g_up', 'rwkv_k_k', 'rwkv_k_a', 'rwkv_r_k', 'rwkv_gn_g', 'rwkv_gn_b', 'conv_w', 'conv_b', 'conv_ln_g', 'conv_ln_b', 'proj_rwkv', 'proj_conv', 'w_out', 'ln1_g', 'ln1_b', 'ln_mem_g', 'ln_mem_b', 'xattn_wq', 'xattn_wk', 'xattn_wv', 'xattn_wo', 'ln2_g', 'ln2_b', 'mlp_w1', 'mlp_w2', 'ln3_g', 'ln3_b', 'loss_target', 'm_w_in', 'm_rwkv_shift_mix', 'm_rwkv_w0', 'm_rwkv_w_up', 'm_rwkv_a0', 'm_rwkv_a_up', 'm_rwkv_g_up', 'm_rwkv_k_k', 'm_rwkv_k_a', 'm_rwkv_r_k', 'm_rwkv_gn_g', 'm_rwkv_gn_b', 'm_conv_w', 'm_conv_b', 'm_conv_ln_g', 'm_conv_ln_b', 'm_proj_rwkv', 'm_proj_conv', 'm_w_out', 'm_ln1_g', 'm_ln1_b', 'm_ln_mem_g', 'm_ln_mem_b', 'm_xattn_wq', 'm_xattn_wk', 'm_xattn_wv', 'm_xattn_wo', 'm_ln2_g', 'm_ln2_b', 'm_mlp_w1', 'm_mlp_w2', 'm_ln3_g', 'm_ln3_b', 'v_w_in', 'v_rwkv_shift_mix', 'v_rwkv_w0', 'v_rwkv_w_up', 'v_rwkv_a0', 'v_rwkv_a_up', 'v_rwkv_g_up', 'v_rwkv_k_k', 'v_rwkv_k_a', 'v_rwkv_r_k', 'v_rwkv_gn_g', 'v_rwkv_gn_b', 'v_conv_w', 'v_conv_b', 'v_conv_ln_g', 'v_conv_ln_b', 'v_proj_rwkv', 'v_proj_conv', 'v_w_out', 'v_ln1_g', 'v_ln1_b', 'v_ln_mem_g', 'v_ln_mem_b', 'v_xattn_wq', 'v_xattn_wk', 'v_xattn_wv', 'v_xattn_wo', 'v_ln2_g', 'v_ln2_b', 'v_mlp_w1', 'v_mlp_w2', 'v_ln3_g', 'v_ln3_b']
TWIN_OUTPUTS = ['loss', 'grad_x', 'grad_w_in', 'grad_rwkv_shift_mix', 'grad_rwkv_w0', 'grad_rwkv_w_up', 'grad_rwkv_a0', 'grad_rwkv_a_up', 'grad_rwkv_g_up', 'grad_rwkv_k_k', 'grad_rwkv_k_a', 'grad_rwkv_r_k', 'grad_rwkv_gn_g', 'grad_rwkv_gn_b', 'grad_conv_w', 'grad_conv_b', 'grad_conv_ln_g', 'grad_conv_ln_b', 'grad_proj_rwkv', 'grad_proj_conv', 'grad_w_out', 'grad_ln1_g', 'grad_ln1_b', 'grad_ln_mem_g', 'grad_ln_mem_b', 'grad_xattn_wq', 'grad_xattn_wk', 'grad_xattn_wv', 'grad_xattn_wo', 'grad_ln2_g', 'grad_ln2_b', 'grad_mlp_w1', 'grad_mlp_w2', 'grad_ln3_g', 'grad_ln3_b', 'delta_w_in', 'delta_rwkv_shift_mix', 'delta_rwkv_w0', 'delta_rwkv_w_up', 'delta_rwkv_a0', 'delta_rwkv_a_up', 'delta_rwkv_g_up', 'delta_rwkv_k_k', 'delta_rwkv_k_a', 'delta_rwkv_r_k', 'delta_rwkv_gn_g', 'delta_rwkv_gn_b', 'delta_conv_w', 'delta_conv_b', 'delta_conv_ln_g', 'delta_conv_ln_b', 'delta_proj_rwkv', 'delta_proj_conv', 'delta_w_out', 'delta_ln1_g', 'delta_ln1_b', 'delta_ln_mem_g', 'delta_ln_mem_b', 'delta_xattn_wq', 'delta_xattn_wk', 'delta_xattn_wv', 'delta_xattn_wo', 'delta_ln2_g', 'delta_ln2_b', 'delta_mlp_w1', 'delta_mlp_w2', 'delta_ln3_g', 'delta_ln3_b', 'new_m_w_in', 'new_m_rwkv_shift_mix', 'new_m_rwkv_w0', 'new_m_rwkv_w_up', 'new_m_rwkv_a0', 'new_m_rwkv_a_up', 'new_m_rwkv_g_up', 'new_m_rwkv_k_k', 'new_m_rwkv_k_a', 'new_m_rwkv_r_k', 'new_m_rwkv_gn_g', 'new_m_rwkv_gn_b', 'new_m_conv_w', 'new_m_conv_b', 'new_m_conv_ln_g', 'new_m_conv_ln_b', 'new_m_proj_rwkv', 'new_m_proj_conv', 'new_m_w_out', 'new_m_ln1_g', 'new_m_ln1_b', 'new_m_ln_mem_g', 'new_m_ln_mem_b', 'new_m_xattn_wq', 'new_m_xattn_wk', 'new_m_xattn_wv', 'new_m_xattn_wo', 'new_m_ln2_g', 'new_m_ln2_b', 'new_m_mlp_w1', 'new_m_mlp_w2', 'new_m_ln3_g', 'new_m_ln3_b', 'new_v_w_in', 'new_v_rwkv_shift_mix', 'new_v_rwkv_w0', 'new_v_rwkv_w_up', 'new_v_rwkv_a0', 'new_v_rwkv_a_up', 'new_v_rwkv_g_up', 'new_v_rwkv_k_k', 'new_v_rwkv_k_a', 'new_v_rwkv_r_k', 'new_v_rwkv_gn_g', 'new_v_rwkv_gn_b', 'new_v_conv_w', 'new_v_conv_b', 'new_v_conv_ln_g', 'new_v_conv_ln_b', 'new_v_proj_rwkv', 'new_v_proj_conv', 'new_v_w_out', 'new_v_ln1_g', 'new_v_ln1_b', 'new_v_ln_mem_g', 'new_v_ln_mem_b', 'new_v_xattn_wq', 'new_v_xattn_wk', 'new_v_xattn_wv', 'new_v_xattn_wo', 'new_v_ln2_g', 'new_v_ln2_b', 'new_v_mlp_w1', 'new_v_mlp_w2', 'new_v_ln3_g', 'new_v_ln3_b']
TWIN_LEAF_KINDS = {'loss': 'loss', 'grad_x': 'grad_x', 'grad_w_in': 'grad_w', 'grad_rwkv_shift_mix': 'grad_w', 'grad_rwkv_w0': 'grad_w', 'grad_rwkv_w_up': 'grad_w', 'grad_rwkv_a0': 'grad_w', 'grad_rwkv_a_up': 'grad_w', 'grad_rwkv_g_up': 'grad_w', 'grad_rwkv_k_k': 'grad_w', 'grad_rwkv_k_a': 'grad_w', 'grad_rwkv_r_k': 'grad_w', 'grad_rwkv_gn_g': 'grad_w', 'grad_rwkv_gn_b': 'grad_w', 'grad_conv_w': 'grad_w', 'grad_conv_b': 'grad_w', 'grad_conv_ln_g': 'grad_w', 'grad_conv_ln_b': 'grad_w', 'grad_proj_rwkv': 'grad_w', 'grad_proj_conv': 'grad_w', 'grad_w_out': 'grad_w', 'grad_ln1_g': 'grad_w', 'grad_ln1_b': 'grad_w', 'grad_ln_mem_g': 'grad_w', 'grad_ln_mem_b': 'grad_w', 'grad_xattn_wq': 'grad_w', 'grad_xattn_wk': 'grad_w', 'grad_xattn_wv': 'grad_w', 'grad_xattn_wo': 'grad_w', 'grad_ln2_g': 'grad_w', 'grad_ln2_b': 'grad_w', 'grad_mlp_w1': 'grad_w', 'grad_mlp_w2': 'grad_w', 'grad_ln3_g': 'grad_w', 'grad_ln3_b': 'grad_w', 'delta_w_in': 'delta_w', 'delta_rwkv_shift_mix': 'delta_w', 'delta_rwkv_w0': 'delta_w', 'delta_rwkv_w_up': 'delta_w', 'delta_rwkv_a0': 'delta_w', 'delta_rwkv_a_up': 'delta_w', 'delta_rwkv_g_up': 'delta_w', 'delta_rwkv_k_k': 'delta_w', 'delta_rwkv_k_a': 'delta_w', 'delta_rwkv_r_k': 'delta_w', 'delta_rwkv_gn_g': 'delta_w', 'delta_rwkv_gn_b': 'delta_w', 'delta_conv_w': 'delta_w', 'delta_conv_b': 'delta_w', 'delta_conv_ln_g': 'delta_w', 'delta_conv_ln_b': 'delta_w', 'delta_proj_rwkv': 'delta_w', 'delta_proj_conv': 'delta_w', 'delta_w_out': 'delta_w', 'delta_ln1_g': 'delta_w', 'delta_ln1_b': 'delta_w', 'delta_ln_mem_g': 'delta_w', 'delta_ln_mem_b': 'delta_w', 'delta_xattn_wq': 'delta_w', 'delta_xattn_wk': 'delta_w', 'delta_xattn_wv': 'delta_w', 'delta_xattn_wo': 'delta_w', 'delta_ln2_g': 'delta_w', 'delta_ln2_b': 'delta_w', 'delta_mlp_w1': 'delta_w', 'delta_mlp_w2': 'delta_w', 'delta_ln3_g': 'delta_w', 'delta_ln3_b': 'delta_w', 'new_m_w_in': 'new_m', 'new_m_rwkv_shift_mix': 'new_m', 'new_m_rwkv_w0': 'new_m', 'new_m_rwkv_w_up': 'new_m', 'new_m_rwkv_a0': 'new_m', 'new_m_rwkv_a_up': 'new_m', 'new_m_rwkv_g_up': 'new_m', 'new_m_rwkv_k_k': 'new_m', 'new_m_rwkv_k_a': 'new_m', 'new_m_rwkv_r_k': 'new_m', 'new_m_rwkv_gn_g': 'new_m', 'new_m_rwkv_gn_b': 'new_m', 'new_m_conv_w': 'new_m', 'new_m_conv_b': 'new_m', 'new_m_conv_ln_g': 'new_m', 'new_m_conv_ln_b': 'new_m', 'new_m_proj_rwkv': 'new_m', 'new_m_proj_conv': 'new_m', 'new_m_w_out': 'new_m', 'new_m_ln1_g': 'new_m', 'new_m_ln1_b': 'new_m', 'new_m_ln_mem_g': 'new_m', 'new_m_ln_mem_b': 'new_m', 'new_m_xattn_wq': 'new_m', 'new_m_xattn_wk': 'new_m', 'new_m_xattn_wv': 'new_m', 'new_m_xattn_wo': 'new_m', 'new_m_ln2_g': 'new_m', 'new_m_ln2_b': 'new_m', 'new_m_mlp_w1': 'new_m', 'new_m_mlp_w2': 'new_m', 'new_m_ln3_g': 'new_m', 'new_m_ln3_b': 'new_m', 'new_v_w_in': 'new_v', 'new_v_rwkv_shift_mix': 'new_v', 'new_v_rwkv_w0': 'new_v', 'new_v_rwkv_w_up': 'new_v', 'new_v_rwkv_a0': 'new_v', 'new_v_rwkv_a_up': 'new_v', 'new_v_rwkv_g_up': 'new_v', 'new_v_rwkv_k_k': 'new_v', 'new_v_rwkv_k_a': 'new_v', 'new_v_rwkv_r_k': 'new_v', 'new_v_rwkv_gn_g': 'new_v', 'new_v_rwkv_gn_b': 'new_v', 'new_v_conv_w': 'new_v', 'new_v_conv_b': 'new_v', 'new_v_conv_ln_g': 'new_v', 'new_v_conv_ln_b': 'new_v', 'new_v_proj_rwkv': 'new_v', 'new_v_proj_conv': 'new_v', 'new_v_w_out': 'new_v', 'new_v_ln1_g': 'new_v', 'new_v_ln1_b': 'new_v', 'new_v_ln_mem_g': 'new_v', 'new_v_ln_mem_b': 'new_v', 'new_v_xattn_wq': 'new_v', 'new_v_xattn_wk': 'new_v', 'new_v_xattn_wv': 'new_v', 'new_v_xattn_wo': 'new_v', 'new_v_ln2_g': 'new_v', 'new_v_ln2_b': 'new_v', 'new_v_mlp_w1': 'new_v', 'new_v_mlp_w2': 'new_v', 'new_v_ln3_g': 'new_v', 'new_v_ln3_b': 'new_v'}


def _forward(args):
    return _fwd_reference(*[args[k] for k in FWD_PARAMS])


def _output_shape():
    out = _jax.eval_shape(lambda: _forward(_fwd_setup_inputs(0)))
    return out.shape, out.dtype

N_MICROBATCH = 1
ADAM_LR = 0.001
ADAM_B1 = 0.9
ADAM_B2 = 0.999
ADAM_EPS = 1e-08
ADAM_WD = 0.01
ADAM_STEP = 10
PER_EXAMPLE_BATCH_AXIS = {'x': 0, 'mem': 0, 'loss_target': 0}
SHARED_INPUTS = []
_WEIGHT_DTYPES = {'w_in': _jnp.float32, 'rwkv_shift_mix': _jnp.float32, 'rwkv_w0': _jnp.float32, 'rwkv_w_up': _jnp.float32, 'rwkv_a0': _jnp.float32, 'rwkv_a_up': _jnp.float32, 'rwkv_g_up': _jnp.float32, 'rwkv_k_k': _jnp.float32, 'rwkv_k_a': _jnp.float32, 'rwkv_r_k': _jnp.float32, 'rwkv_gn_g': _jnp.float32, 'rwkv_gn_b': _jnp.float32, 'conv_w': _jnp.float32, 'conv_b': _jnp.float32, 'conv_ln_g': _jnp.float32, 'conv_ln_b': _jnp.float32, 'proj_rwkv': _jnp.float32, 'proj_conv': _jnp.float32, 'w_out': _jnp.float32, 'ln1_g': _jnp.float32, 'ln1_b': _jnp.float32, 'ln_mem_g': _jnp.float32, 'ln_mem_b': _jnp.float32, 'xattn_wq': _jnp.float32, 'xattn_wk': _jnp.float32, 'xattn_wv': _jnp.float32, 'xattn_wo': _jnp.float32, 'ln2_g': _jnp.float32, 'ln2_b': _jnp.float32, 'mlp_w1': _jnp.float32, 'mlp_w2': _jnp.float32, 'ln3_g': _jnp.float32, 'ln3_b': _jnp.float32}
MOMENT_SCALE = {'w_in': 4.471046e-03, 'rwkv_shift_mix': 1.035819e-02, 'rwkv_w0': 2.701415e-03, 'rwkv_w_up': 3.772131e-04, 'rwkv_a0': 2.895672e-03, 'rwkv_a_up': 2.669294e-03, 'rwkv_g_up': 5.583138e-03, 'rwkv_k_k': 6.216567e-03, 'rwkv_k_a': 6.481514e-03, 'rwkv_r_k': 1.161408e-02, 'rwkv_gn_g': 5.416317e-03, 'rwkv_gn_b': 1.019246e-02, 'conv_w': 6.337206e-03, 'conv_b': 1.869033e-02, 'conv_ln_g': 9.158752e-03, 'conv_ln_b': 1.161756e-02, 'proj_rwkv': 6.462045e-03, 'proj_conv': 8.299834e-03, 'w_out': 1.057258e-02, 'ln1_g': 2.455235e-01, 'ln1_b': 1.096702e-01, 'ln_mem_g': 2.086018e-03, 'ln_mem_b': 3.073309e-02, 'xattn_wq': 1.340969e-03, 'xattn_wk': 1.340074e-03, 'xattn_wv': 2.783740e-03, 'xattn_wo': 2.815572e-03, 'ln2_g': 2.454656e-01, 'ln2_b': 1.101174e-01, 'mlp_w1': 9.095979e-03, 'mlp_w2': 2.016760e-02, 'ln3_g': 7.992017e+00, 'ln3_b': 6.944493e-01}


def _to_microbatches(a, axis):
    t = _jnp.moveaxis(a, axis, 0)
    t = t.reshape((N_MICROBATCH, t.shape[0] // N_MICROBATCH) + t.shape[1:])
    return _jnp.moveaxis(t, 1, axis + 1)


def setup_inputs(seed: int = 0) -> dict:
    inp = _fwd_setup_inputs(seed)
    key = _jax.random.fold_in(_jax.random.key(seed), 7919)
    shape, _ = _output_shape()
    out = dict(inp)
    out["loss_target"] = _jax.random.normal(_jax.random.fold_in(key, 0), shape, _jnp.float32)
    for i, name in enumerate(TWIN_WEIGHTS):
        w = inp[name].astype(_jnp.float32)
        if MOMENT_SCALE is None:
            s = _jnp.sqrt(_jnp.mean(_jnp.square(w)) + 1e-30)
        else:
            s = MOMENT_SCALE[name]
        km, kv = _jax.random.split(_jax.random.fold_in(key, i + 1))
        out[name] = w
        out["m_" + name] = s * _jax.random.normal(km, w.shape, _jnp.float32)
        out["v_" + name] = (s * s) * _jax.random.uniform(kv, w.shape, _jnp.float32, 0.5, 1.5)
    if N_MICROBATCH > 1:
        for name, axis in PER_EXAMPLE_BATCH_AXIS.items():
            out[name] = _to_microbatches(out[name], axis)
    return {'x': out['x'], 'mem': out['mem'], 'w_in': out['w_in'], 'rwkv_shift_mix': out['rwkv_shift_mix'], 'rwkv_w0': out['rwkv_w0'], 'rwkv_w_up': out['rwkv_w_up'], 'rwkv_a0': out['rwkv_a0'], 'rwkv_a_up': out['rwkv_a_up'], 'rwkv_g_up': out['rwkv_g_up'], 'rwkv_k_k': out['rwkv_k_k'], 'rwkv_k_a': out['rwkv_k_a'], 'rwkv_r_k': out['rwkv_r_k'], 'rwkv_gn_g': out['rwkv_gn_g'], 'rwkv_gn_b': out['rwkv_gn_b'], 'conv_w': out['conv_w'], 'conv_b': out['conv_b'], 'conv_ln_g': out['conv_ln_g'], 'conv_ln_b': out['conv_ln_b'], 'proj_rwkv': out['proj_rwkv'], 'proj_conv': out['proj_conv'], 'w_out': out['w_out'], 'ln1_g': out['ln1_g'], 'ln1_b': out['ln1_b'], 'ln_mem_g': out['ln_mem_g'], 'ln_mem_b': out['ln_mem_b'], 'xattn_wq': out['xattn_wq'], 'xattn_wk': out['xattn_wk'], 'xattn_wv': out['xattn_wv'], 'xattn_wo': out['xattn_wo'], 'ln2_g': out['ln2_g'], 'ln2_b': out['ln2_b'], 'mlp_w1': out['mlp_w1'], 'mlp_w2': out['mlp_w2'], 'ln3_g': out['ln3_g'], 'ln3_b': out['ln3_b'], 'loss_target': out['loss_target'], 'm_w_in': out['m_w_in'], 'm_rwkv_shift_mix': out['m_rwkv_shift_mix'], 'm_rwkv_w0': out['m_rwkv_w0'], 'm_rwkv_w_up': out['m_rwkv_w_up'], 'm_rwkv_a0': out['m_rwkv_a0'], 'm_rwkv_a_up': out['m_rwkv_a_up'], 'm_rwkv_g_up': out['m_rwkv_g_up'], 'm_rwkv_k_k': out['m_rwkv_k_k'], 'm_rwkv_k_a': out['m_rwkv_k_a'], 'm_rwkv_r_k': out['m_rwkv_r_k'], 'm_rwkv_gn_g': out['m_rwkv_gn_g'], 'm_rwkv_gn_b': out['m_rwkv_gn_b'], 'm_conv_w': out['m_conv_w'], 'm_conv_b': out['m_conv_b'], 'm_conv_ln_g': out['m_conv_ln_g'], 'm_conv_ln_b': out['m_conv_ln_b'], 'm_proj_rwkv': out['m_proj_rwkv'], 'm_proj_conv': out['m_proj_conv'], 'm_w_out': out['m_w_out'], 'm_ln1_g': out['m_ln1_g'], 'm_ln1_b': out['m_ln1_b'], 'm_ln_mem_g': out['m_ln_mem_g'], 'm_ln_mem_b': out['m_ln_mem_b'], 'm_xattn_wq': out['m_xattn_wq'], 'm_xattn_wk': out['m_xattn_wk'], 'm_xattn_wv': out['m_xattn_wv'], 'm_xattn_wo': out['m_xattn_wo'], 'm_ln2_g': out['m_ln2_g'], 'm_ln2_b': out['m_ln2_b'], 'm_mlp_w1': out['m_mlp_w1'], 'm_mlp_w2': out['m_mlp_w2'], 'm_ln3_g': out['m_ln3_g'], 'm_ln3_b': out['m_ln3_b'], 'v_w_in': out['v_w_in'], 'v_rwkv_shift_mix': out['v_rwkv_shift_mix'], 'v_rwkv_w0': out['v_rwkv_w0'], 'v_rwkv_w_up': out['v_rwkv_w_up'], 'v_rwkv_a0': out['v_rwkv_a0'], 'v_rwkv_a_up': out['v_rwkv_a_up'], 'v_rwkv_g_up': out['v_rwkv_g_up'], 'v_rwkv_k_k': out['v_rwkv_k_k'], 'v_rwkv_k_a': out['v_rwkv_k_a'], 'v_rwkv_r_k': out['v_rwkv_r_k'], 'v_rwkv_gn_g': out['v_rwkv_gn_g'], 'v_rwkv_gn_b': out['v_rwkv_gn_b'], 'v_conv_w': out['v_conv_w'], 'v_conv_b': out['v_conv_b'], 'v_conv_ln_g': out['v_conv_ln_g'], 'v_conv_ln_b': out['v_conv_ln_b'], 'v_proj_rwkv': out['v_proj_rwkv'], 'v_proj_conv': out['v_proj_conv'], 'v_w_out': out['v_w_out'], 'v_ln1_g': out['v_ln1_g'], 'v_ln1_b': out['v_ln1_b'], 'v_ln_mem_g': out['v_ln_mem_g'], 'v_ln_mem_b': out['v_ln_mem_b'], 'v_xattn_wq': out['v_xattn_wq'], 'v_xattn_wk': out['v_xattn_wk'], 'v_xattn_wv': out['v_xattn_wv'], 'v_xattn_wo': out['v_xattn_wo'], 'v_ln2_g': out['v_ln2_g'], 'v_ln2_b': out['v_ln2_b'], 'v_mlp_w1': out['v_mlp_w1'], 'v_mlp_w2': out['v_mlp_w2'], 'v_ln3_g': out['v_ln3_g'], 'v_ln3_b': out['v_ln3_b']}


def _loss(weights, diff, rest, loss_target):
    with _jax.named_scope("forward"):
        args = {**rest, TWIN_DIFF_INPUT: diff, **{k: w.astype(_WEIGHT_DTYPES[k]) for k, w in weights.items()}}
        y = _forward(args)
    with _jax.named_scope("loss_head"):
        err = _jnp.square(y.astype(_jnp.float32) - loss_target)
        return 0.5 * _jnp.sum(_jnp.mean(err, axis=-1)) if err.ndim else 0.5 * err


def _adamw(w, g, m, v):
    m = ADAM_B1 * m + (1.0 - ADAM_B1) * g
    v = ADAM_B2 * v + (1.0 - ADAM_B2) * _jnp.square(g)
    m_hat = m / (1.0 - ADAM_B1 ** ADAM_STEP)
    v_hat = v / (1.0 - ADAM_B2 ** ADAM_STEP)
    delta = -ADAM_LR * (m_hat / (_jnp.sqrt(v_hat) + ADAM_EPS) + ADAM_WD * w)
    return delta, m, v


def reference(x, mem, w_in, rwkv_shift_mix, rwkv_w0, rwkv_w_up, rwkv_a0, rwkv_a_up, rwkv_g_up, rwkv_k_k, rwkv_k_a, rwkv_r_k, rwkv_gn_g, rwkv_gn_b, conv_w, conv_b, conv_ln_g, conv_ln_b, proj_rwkv, proj_conv, w_out, ln1_g, ln1_b, ln_mem_g, ln_mem_b, xattn_wq, xattn_wk, xattn_wv, xattn_wo, ln2_g, ln2_b, mlp_w1, mlp_w2, ln3_g, ln3_b, loss_target, m_w_in, m_rwkv_shift_mix, m_rwkv_w0, m_rwkv_w_up, m_rwkv_a0, m_rwkv_a_up, m_rwkv_g_up, m_rwkv_k_k, m_rwkv_k_a, m_rwkv_r_k, m_rwkv_gn_g, m_rwkv_gn_b, m_conv_w, m_conv_b, m_conv_ln_g, m_conv_ln_b, m_proj_rwkv, m_proj_conv, m_w_out, m_ln1_g, m_ln1_b, m_ln_mem_g, m_ln_mem_b, m_xattn_wq, m_xattn_wk, m_xattn_wv, m_xattn_wo, m_ln2_g, m_ln2_b, m_mlp_w1, m_mlp_w2, m_ln3_g, m_ln3_b, v_w_in, v_rwkv_shift_mix, v_rwkv_w0, v_rwkv_w_up, v_rwkv_a0, v_rwkv_a_up, v_rwkv_g_up, v_rwkv_k_k, v_rwkv_k_a, v_rwkv_r_k, v_rwkv_gn_g, v_rwkv_gn_b, v_conv_w, v_conv_b, v_conv_ln_g, v_conv_ln_b, v_proj_rwkv, v_proj_conv, v_w_out, v_ln1_g, v_ln1_b, v_ln_mem_g, v_ln_mem_b, v_xattn_wq, v_xattn_wk, v_xattn_wv, v_xattn_wo, v_ln2_g, v_ln2_b, v_mlp_w1, v_mlp_w2, v_ln3_g, v_ln3_b):
    given = dict(x=x, mem=mem, w_in=w_in, rwkv_shift_mix=rwkv_shift_mix, rwkv_w0=rwkv_w0, rwkv_w_up=rwkv_w_up, rwkv_a0=rwkv_a0, rwkv_a_up=rwkv_a_up, rwkv_g_up=rwkv_g_up, rwkv_k_k=rwkv_k_k, rwkv_k_a=rwkv_k_a, rwkv_r_k=rwkv_r_k, rwkv_gn_g=rwkv_gn_g, rwkv_gn_b=rwkv_gn_b, conv_w=conv_w, conv_b=conv_b, conv_ln_g=conv_ln_g, conv_ln_b=conv_ln_b, proj_rwkv=proj_rwkv, proj_conv=proj_conv, w_out=w_out, ln1_g=ln1_g, ln1_b=ln1_b, ln_mem_g=ln_mem_g, ln_mem_b=ln_mem_b, xattn_wq=xattn_wq, xattn_wk=xattn_wk, xattn_wv=xattn_wv, xattn_wo=xattn_wo, ln2_g=ln2_g, ln2_b=ln2_b, mlp_w1=mlp_w1, mlp_w2=mlp_w2, ln3_g=ln3_g, ln3_b=ln3_b, loss_target=loss_target, m_w_in=m_w_in, m_rwkv_shift_mix=m_rwkv_shift_mix, m_rwkv_w0=m_rwkv_w0, m_rwkv_w_up=m_rwkv_w_up, m_rwkv_a0=m_rwkv_a0, m_rwkv_a_up=m_rwkv_a_up, m_rwkv_g_up=m_rwkv_g_up, m_rwkv_k_k=m_rwkv_k_k, m_rwkv_k_a=m_rwkv_k_a, m_rwkv_r_k=m_rwkv_r_k, m_rwkv_gn_g=m_rwkv_gn_g, m_rwkv_gn_b=m_rwkv_gn_b, m_conv_w=m_conv_w, m_conv_b=m_conv_b, m_conv_ln_g=m_conv_ln_g, m_conv_ln_b=m_conv_ln_b, m_proj_rwkv=m_proj_rwkv, m_proj_conv=m_proj_conv, m_w_out=m_w_out, m_ln1_g=m_ln1_g, m_ln1_b=m_ln1_b, m_ln_mem_g=m_ln_mem_g, m_ln_mem_b=m_ln_mem_b, m_xattn_wq=m_xattn_wq, m_xattn_wk=m_xattn_wk, m_xattn_wv=m_xattn_wv, m_xattn_wo=m_xattn_wo, m_ln2_g=m_ln2_g, m_ln2_b=m_ln2_b, m_mlp_w1=m_mlp_w1, m_mlp_w2=m_mlp_w2, m_ln3_g=m_ln3_g, m_ln3_b=m_ln3_b, v_w_in=v_w_in, v_rwkv_shift_mix=v_rwkv_shift_mix, v_rwkv_w0=v_rwkv_w0, v_rwkv_w_up=v_rwkv_w_up, v_rwkv_a0=v_rwkv_a0, v_rwkv_a_up=v_rwkv_a_up, v_rwkv_g_up=v_rwkv_g_up, v_rwkv_k_k=v_rwkv_k_k, v_rwkv_k_a=v_rwkv_k_a, v_rwkv_r_k=v_rwkv_r_k, v_rwkv_gn_g=v_rwkv_gn_g, v_rwkv_gn_b=v_rwkv_gn_b, v_conv_w=v_conv_w, v_conv_b=v_conv_b, v_conv_ln_g=v_conv_ln_g, v_conv_ln_b=v_conv_ln_b, v_proj_rwkv=v_proj_rwkv, v_proj_conv=v_proj_conv, v_w_out=v_w_out, v_ln1_g=v_ln1_g, v_ln1_b=v_ln1_b, v_ln_mem_g=v_ln_mem_g, v_ln_mem_b=v_ln_mem_b, v_xattn_wq=v_xattn_wq, v_xattn_wk=v_xattn_wk, v_xattn_wv=v_xattn_wv, v_xattn_wo=v_xattn_wo, v_ln2_g=v_ln2_g, v_ln2_b=v_ln2_b, v_mlp_w1=v_mlp_w1, v_mlp_w2=v_mlp_w2, v_ln3_g=v_ln3_g, v_ln3_b=v_ln3_b)
    weights = {n: given[n] for n in TWIN_WEIGHTS}
    shared = {n: given[n] for n in SHARED_INPUTS}
    per_example = {n: given[n] for n in ['x', 'mem']}
    grad_fn = _jax.value_and_grad(_loss, argnums=(0, 1))

    def one_microbatch(ex, loss_target):
        ex = dict(ex)
        diff = ex.pop(TWIN_DIFF_INPUT)
        return grad_fn(weights, diff, {**shared, **ex}, loss_target)

    if N_MICROBATCH == 1:
        loss, (grad_w, grad_x) = one_microbatch(per_example, given["loss_target"])
    else:
        def body(carry, xs):
            loss_sum, grad_sum = carry
            l_k, (gw_k, gx_k) = one_microbatch(xs[0], xs[1])
            with _jax.named_scope("update"):
                return (loss_sum + l_k, _jax.tree.map(_jnp.add, grad_sum, gw_k)), gx_k

        init = (_jnp.zeros((), _jnp.float32), _jax.tree.map(_jnp.zeros_like, weights))
        (loss, grad_w), grad_x = _jax.lax.scan(body, init, (per_example, given["loss_target"]))
    with _jax.named_scope("update"):
        delta_w, new_m, new_v = {}, {}, {}
        for n in TWIN_WEIGHTS:
            delta_w[n], new_m[n], new_v[n] = _adamw(weights[n], grad_w[n], given["m_" + n], given["v_" + n])
    return (loss, grad_x, *[grad_w[n] for n in TWIN_WEIGHTS], *[delta_w[n] for n in TWIN_WEIGHTS],
            *[new_m[n] for n in TWIN_WEIGHTS], *[new_v[n] for n in TWIN_WEIGHTS])
```

```python
import functools
import math

import jax
import jax.numpy as jnp
from jax import lax
from jax.experimental import pallas as pl
from jax.experimental.pallas import tpu as pltpu

F32 = jnp.float32
BF16 = jnp.bfloat16

N_DEV = 8
RWKV_HEAD = 64
SCAN_CHUNK = 64
XATTN_HEADS = 4
CONV_HALO = 32
LN_EPS = 1e-5
GN_EPS = 64e-5
ALPHA = float(2.0 ** 0.25)
ADAM_LR, ADAM_B1, ADAM_B2, ADAM_EPS, ADAM_WD, ADAM_STEP = 0.001, 0.9, 0.999, 1e-08, 0.01, 10
LANES = 128
SUBLANES = 8
VMEM_LIMIT = 56 * 1024 * 1024
ADAM_BLOCK_ELEMS = 256 * 1024

WEIGHTS = ['w_in', 'rwkv_shift_mix', 'rwkv_w0', 'rwkv_w_up', 'rwkv_a0', 'rwkv_a_up', 'rwkv_g_up', 'rwkv_k_k',
           'rwkv_k_a', 'rwkv_r_k', 'rwkv_gn_g', 'rwkv_gn_b', 'conv_w', 'conv_b', 'conv_ln_g', 'conv_ln_b',
           'proj_rwkv', 'proj_conv', 'w_out', 'ln1_g', 'ln1_b', 'ln_mem_g', 'ln_mem_b', 'xattn_wq', 'xattn_wk',
           'xattn_wv', 'xattn_wo', 'ln2_g', 'ln2_b', 'mlp_w1', 'mlp_w2', 'ln3_g', 'ln3_b']
COL_SHARDED = ['w_in', 'rwkv_w_up', 'rwkv_a_up', 'rwkv_g_up', 'conv_w', 'proj_rwkv', 'proj_conv', 'mlp_w1']
ROW_SHARDED = ['w_out', 'xattn_wq', 'xattn_wk', 'xattn_wv', 'xattn_wo', 'mlp_w2']
BIG = ['w_in', 'rwkv_w_up', 'rwkv_a_up', 'rwkv_g_up', 'conv_w', 'proj_rwkv', 'proj_conv', 'w_out', 'xattn_wq',
       'xattn_wk', 'xattn_wv', 'xattn_wo', 'mlp_w1', 'mlp_w2']
SMALL = [w for w in WEIGHTS if w not in BIG]


def _cparams(dims):
    return pltpu.CompilerParams(dimension_semantics=dims, vmem_limit_bytes=VMEM_LIMIT)


def _tile(n, cands):
    for c in cands:
        if n % c == 0:
            return c
    return n


def matmul(a, b, mode, name, add=None, out_dtype=F32, b_dev=False, out_dev=False):
    if b_dev:
        assert mode in ('nn', 'nt') and b.shape[0] == N_DEV
        b_rows, b_cols = b.shape[1], N_DEV * b.shape[2]
    else:
        b_rows, b_cols = b.shape
    if mode == 'nn':
        (M, K), (K2, N) = a.shape, (b_rows, b_cols)
    elif mode == 'nt':
        (M, K), (N, K2) = a.shape, (b_rows, b_cols)
    else:
        (K, M), (K2, N) = a.shape, (b_rows, b_cols)
    assert K == K2, (a.shape, b.shape, mode)
    n_unit = N // N_DEV if (out_dev or (b_dev and mode == 'nn')) else N
    k_unit = K // N_DEV if (b_dev and mode == 'nt') else K
    tm = _tile(M, (1024, 512, 256, 128))
    tn = _tile(n_unit, (1024, 512, 256, 128))
    tk = _tile(k_unit, (2048, 1024, 512, 256, 128))
    nk = K // tk
    nb, kb = n_unit // tn, k_unit // tk
    dims = {'nn': ((1,), (0,)), 'nt': ((1,), (1,)), 'tn': ((0,), (0,))}[mode]

    def body(*refs):
        if add is None:
            a_ref, b_ref, o_ref, acc_ref = refs
        else:
            a_ref, b_ref, add_ref, o_ref, acc_ref = refs
        k = pl.program_id(2)

        @pl.when(k == 0)
        def _():
            acc_ref[...] = jnp.zeros_like(acc_ref)

        acc_ref[...] += lax.dot_general(a_ref[...].astype(BF16), b_ref[...].astype(BF16), (dims, ((), ())),
                                        preferred_element_type=F32)

        @pl.when(k == nk - 1)
        def _():
            r = acc_ref[...]
            if add is not None:
                r = r + add_ref[...]
            o_ref[...] = r.astype(out_dtype)

    if mode == 'nn':
        a_spec = pl.BlockSpec((tm, tk), lambda i, j, k: (i, k))
        b_spec = (pl.BlockSpec((None, tk, tn), lambda i, j, k: (j // nb, k, j % nb)) if b_dev
                  else pl.BlockSpec((tk, tn), lambda i, j, k: (k, j)))
    elif mode == 'nt':
        a_spec = pl.BlockSpec((tm, tk), lambda i, j, k: (i, k))
        b_spec = (pl.BlockSpec((None, tn, tk), lambda i, j, k: (k // kb, j, k % kb)) if b_dev
                  else pl.BlockSpec((tn, tk), lambda i, j, k: (j, k)))
    else:
        a_spec = pl.BlockSpec((tk, tm), lambda i, j, k: (k, i))
        b_spec = pl.BlockSpec((tk, tn), lambda i, j, k: (k, j))
    add_spec = pl.BlockSpec((tm, tn), lambda i, j, k: (i, j))
    if out_dev:
        o_spec = pl.BlockSpec((None, tm, tn), lambda i, j, k: (j // nb, i, j % nb))
        o_shape = (N_DEV, M, N // N_DEV)
    else:
        o_spec, o_shape = add_spec, (M, N)
    in_specs = [a_spec, b_spec] + ([add_spec] if add is not None else [])
    ops = (a, b) + ((add,) if add is not None else ())
    return pl.pallas_call(
        body, name=name, grid=(M // tm, N // tn, nk), in_specs=in_specs, out_specs=o_spec,
        out_shape=jax.ShapeDtypeStruct(o_shape, out_dtype), scratch_shapes=[pltpu.VMEM((tm, tn), F32)],
        compiler_params=_cparams(("parallel", "parallel", "arbitrary")))(*ops)


class _Rows2D:
    def __init__(self, n_rows, tm):
        self.n, self.tm = n_rows, tm
        self.grid = (1, n_rows // tm)

    def row(self, e):
        if isinstance(e, tuple):
            arr, width, cb = e
            return arr, (self.tm, width), pl.BlockSpec((self.tm, width), lambda g, i, cb=cb: (i, cb))
        return e, (self.tm, e.shape[1]), pl.BlockSpec((self.tm, e.shape[1]), lambda g, i: (i, 0))

    def par(self, p):
        return pl.BlockSpec(p.shape, lambda g, i: (0,) * p.ndim)

    def out(self, blk):
        return (self.n, blk[1]), pl.BlockSpec((self.tm, blk[1]), lambda g, i: (i, 0))


class _RowsHeads:
    def __init__(self, n_heads, n_rows, hb, ts):
        self.h, self.n, self.hb, self.ts = n_heads, n_rows, hb, ts
        self.grid = (n_heads // hb, n_rows // ts)

    def row(self, e):
        blk = (self.hb, self.ts, e.shape[2])
        return e, blk, pl.BlockSpec(blk, lambda g, i: (g, i, 0))

    def par(self, p):
        return pl.BlockSpec((self.hb, 1, p.shape[2]), lambda g, i: (g, 0, 0))

    def out(self, blk):
        return (self.h, self.n, blk[2]), pl.BlockSpec(blk, lambda g, i: (g, i, 0))


def _par_block(lay, p):
    return lay.par(p).block_shape


def ew_fwd(lay, fn, rows, params, name):
    rr = [lay.row(e) for e in rows]
    arrs = [r[0] for r in rr]
    blk_avals = [jax.ShapeDtypeStruct(r[1], r[0].dtype) for r in rr]
    par_avals = [jax.ShapeDtypeStruct(_par_block(lay, p), p.dtype) for p in params]
    outs = jax.eval_shape(fn, *blk_avals, *par_avals)
    out_full = [lay.out(o.shape) for o in outs]
    nr, npar = len(rows), len(params)

    def body(*refs):
        vals = [r[...] for r in refs[:nr + npar]]
        res = fn(*vals)
        for ref, v in zip(refs[nr + npar:], res):
            ref[...] = v.astype(ref.dtype)

    return pl.pallas_call(
        body, name=name, grid=lay.grid,
        in_specs=[r[2] for r in rr] + [lay.par(p) for p in params],
        out_specs=[o[1] for o in out_full],
        out_shape=[jax.ShapeDtypeStruct(o[0], a.dtype) for o, a in zip(out_full, outs)],
        compiler_params=_cparams(("parallel", "parallel")))(*arrs, *params)


def ew_bwd(lay, fn, rows, params, cots, wrt_rows, wrt_pars, name, dr_dtypes=None):
    rr = [lay.row(e) for e in rows]
    cc = [lay.row(e) for e in cots]
    nr, npar, nc = len(rows), len(params), len(cots)
    n_dr = len(wrt_rows)
    dr_full = [lay.out(rr[i][1]) for i in wrt_rows]
    dr_dtypes = dr_dtypes or [F32] * n_dr

    def body(*refs):
        rv = [r[...] for r in refs[:nr]]
        pv = [r[...] for r in refs[nr:nr + npar]]
        cv = tuple(r[...] for r in refs[nr + npar:nr + npar + nc])
        outs = refs[nr + npar + nc:]

        def f(*wrt):
            r2, p2 = list(rv), list(pv)
            for idx, v in zip(wrt_rows, wrt[:n_dr]):
                r2[idx] = v
            for idx, v in zip(wrt_pars, wrt[n_dr:]):
                p2[idx] = v
            return fn(*r2, *p2)

        _, vjp = jax.vjp(f, *[rv[i] for i in wrt_rows], *[pv[i] for i in wrt_pars])
        g = vjp(cv)
        for ref, v in zip(outs[:n_dr], g[:n_dr]):
            ref[...] = v.astype(ref.dtype)
        if wrt_pars:
            @pl.when(pl.program_id(1) == 0)
            def _():
                for ref in outs[n_dr:]:
                    ref[...] = jnp.zeros_like(ref)

            for ref, v in zip(outs[n_dr:], g[n_dr:]):
                ref[...] += v

    return pl.pallas_call(
        body, name=name, grid=lay.grid,
        in_specs=[r[2] for r in rr] + [lay.par(p) for p in params] + [c[2] for c in cc],
        out_specs=[o[1] for o in dr_full] + [lay.par(params[i]) for i in wrt_pars],
        out_shape=[jax.ShapeDtypeStruct(o[0], dt) for o, dt in zip(dr_full, dr_dtypes)]
        + [jax.ShapeDtypeStruct(params[i].shape, F32) for i in wrt_pars],
        compiler_params=_cparams(("parallel", "arbitrary")))(
            *[r[0] for r in rr], *params, *[c[0] for c in cc])


def _sigmoid(x):
    return 1.0 / (1.0 + jnp.exp(-x))


def _softplus(x):
    return jnp.maximum(x, 0.0) + jnp.log(1.0 + jnp.exp(-jnp.abs(x)))


def _layer_norm(x, g, b, eps):
    mu = jnp.mean(x, -1, keepdims=True)
    xc = x - mu
    var = jnp.mean(xc * xc, -1, keepdims=True)
    return xc * lax.rsqrt(var + eps) * g + b


def _as_bf16(fn):
    return lambda *args: tuple(o.astype(BF16) for o in fn(*args))


def fn_ln(x, g, b):
    return (_layer_norm(x, g, b, LN_EPS),)


def fn_ln_res(h, t, g, b):
    return (_layer_norm(ALPHA * h + t, g, b, LN_EPS),)


def fn_ln_res_both(h, t, g, b):
    y, = fn_ln_res(h, t, g, b)
    return y, y.astype(BF16)


def make_fn_lora(r_decay, r_iclr, r_gate):
    def fn(z):
        lane = lax.broadcasted_iota(jnp.int32, z.shape, 1)
        out = jnp.where(lane < r_decay, jnp.tanh(z), z)
        out = jnp.where(lane >= r_decay + r_iclr, _sigmoid(z), out)
        return (jnp.where(lane < r_decay + r_iclr + r_gate, out, 0.0),)
    return fn


def fn_rwkv_pre(k, wl, al, w0, a0, k_k, k_a):
    w = -_softplus(-(w0 + wl)) - 0.5
    lw = -jnp.exp(w)
    a = _sigmoid(a0 + al)
    kk = k * k_k
    kk = kk / jnp.maximum(jnp.sqrt(jnp.sum(kk * kk, -1, keepdims=True)), 1e-12)
    k2 = k * (1.0 + (a - 1.0) * k_a)
    return lw, k2, -kk, kk * a


def fn_rwkv_post(o, r, k2, v, g, gn_g, gn_b, r_k):
    mu = jnp.mean(o, -1, keepdims=True)
    oc = o - mu
    var = jnp.mean(oc * oc, -1, keepdims=True)
    y = oc * lax.rsqrt(var + GN_EPS) * gn_g + gn_b
    y = y + jnp.sum(r * k2 * r_k, -1, keepdims=True) * v
    return (y * g,)


def fn_glu(zu, zg):
    return (zu * _sigmoid(zg),)


def fn_ln_silu(y, g, b):
    n = _layer_norm(y, g, b, LN_EPS)
    return (n * _sigmoid(n),)


def fn_merge(zgr, zgc, pr, pc):
    return (_sigmoid(zgr) * pr + _sigmoid(zgc) * pc,)


def fn_relu2(f):
    r = jnp.maximum(f, 0.0)
    return (r * r,)


def _dot_nn(a, b):
    return lax.dot_general(a.astype(BF16), b.astype(BF16), (((1,), (0,)), ((), ())), preferred_element_type=F32)


def _dot_nt(a, b):
    return lax.dot_general(a.astype(BF16), b.astype(BF16), (((1,), (1,)), ((), ())), preferred_element_type=F32)


def _dot_tn(a, b):
    return lax.dot_general(a.astype(BF16), b.astype(BF16), (((0,), (0,)), ((), ())), preferred_element_type=F32)


def _softmax_rows(s):
    s = s - jnp.max(s, -1, keepdims=True)
    e = jnp.exp(s)
    return e / jnp.sum(e, -1, keepdims=True)


def attn_fwd(q, kx, vx, tm, name):
    S, D = q.shape
    M = kx.shape[0]
    dh = D // XATTN_HEADS
    scale = dh ** -0.5

    def body(q_ref, k_ref, v_ref, o_ref):
        p = _softmax_rows(_dot_nt(q_ref[...], k_ref[...]) * scale)
        o_ref[...] = _dot_nn(p, v_ref[...]).astype(BF16)

    row = pl.BlockSpec((tm, dh), lambda h, i: (i, h))
    kv = pl.BlockSpec((M, dh), lambda h, i: (0, h))
    return pl.pallas_call(body, name=name, grid=(XATTN_HEADS, S // tm), in_specs=[row, kv, kv], out_specs=row,
                          out_shape=jax.ShapeDtypeStruct((S, D), BF16),
                          compiler_params=_cparams(("parallel", "parallel")))(q, kx, vx)


def attn_bwd(q, kx, vx, do, tm, name):
    S, D = q.shape
    M = kx.shape[0]
    dh = D // XATTN_HEADS
    scale = dh ** -0.5

    def body(q_ref, k_ref, v_ref, do_ref, dq_ref, dk_ref, dv_ref):
        qb, kb, dob = q_ref[...], k_ref[...], do_ref[...]
        p = _softmax_rows(_dot_nt(qb, kb) * scale)
        dp = _dot_nt(dob, v_ref[...])
        ds = p * (dp - jnp.sum(dp * p, -1, keepdims=True)) * scale
        dq_ref[...] = _dot_nn(ds, kb).astype(BF16)

        @pl.when(pl.program_id(1) == 0)
        def _():
            dk_ref[...] = jnp.zeros_like(dk_ref)
            dv_ref[...] = jnp.zeros_like(dv_ref)

        dk_ref[...] += _dot_tn(ds, qb)
        dv_ref[...] += _dot_tn(p, dob)

    row = pl.BlockSpec((tm, dh), lambda h, i: (i, h))
    kv = pl.BlockSpec((M, dh), lambda h, i: (0, h))
    return pl.pallas_call(
        body, name=name, grid=(XATTN_HEADS, S // tm), in_specs=[row, kv, kv, row], out_specs=[row, kv, kv],
        out_shape=[jax.ShapeDtypeStruct((S, D), BF16), jax.ShapeDtypeStruct((M, D), F32), jax.ShapeDtypeStruct((M, D), F32)],
        compiler_params=_cparams(("parallel", "arbitrary")))(q, kx, vx, do)


def _shift_down(blk, halo_last_row, first_block):
    rolled = pltpu.roll(blk, 1, 0)
    row = lax.broadcasted_iota(jnp.int32, blk.shape, 0)
    top = jnp.where(first_block, 0.0, halo_last_row)
    return jnp.where(row == 0, top, rolled)


def _shift_up(blk, halo_first_row, last_block):
    n = blk.shape[0]
    rolled = pltpu.roll(blk, n - 1, 0)
    row = lax.broadcasted_iota(jnp.int32, blk.shape, 0)
    bot = jnp.where(last_block, 0.0, halo_first_row)
    return jnp.where(row == n - 1, bot, rolled)


def _zcol(j, cols):
    n_first, first, second = cols
    return jnp.where(j < n_first, first + j, second + j - n_first)


def token_shift_fwd(z, cols, cw, width, mu, tm, name):
    S = z.shape[0]
    hb = tm // SUBLANES

    def body(z_ref, halo_ref, mu_ref, o_ref):
        zb = z_ref[...]
        prev = _shift_down(zb, halo_ref[SUBLANES - 1:SUBLANES, :], pl.program_id(1) == 0)
        o_ref[...] = zb + (prev - zb) * mu_ref[...]

    return pl.pallas_call(
        body, name=name, grid=(width // cw, S // tm),
        in_specs=[pl.BlockSpec((tm, cw), lambda j, i: (i, _zcol(j, cols))),
                  pl.BlockSpec((SUBLANES, cw), lambda j, i: (jnp.maximum(i * hb - 1, 0), _zcol(j, cols))),
                  pl.BlockSpec((1, cw), lambda j, i: (0, j))],
        out_specs=pl.BlockSpec((tm, cw), lambda j, i: (i, j)),
        out_shape=jax.ShapeDtypeStruct((S, width), F32),
        compiler_params=_cparams(("parallel", "parallel")))(z, z, mu)


def token_shift_bwd(z, dzs, cols, cw, width, mu, tm, name):
    S = z.shape[0]
    hb = tm // SUBLANES
    nblk = S // tm
    last8 = S // SUBLANES - 1

    def body(z_ref, zh_ref, d_ref, dh_ref, mu_ref, dz_ref, dmu_ref):
        i = pl.program_id(1)
        zb, db, m = z_ref[...], d_ref[...], mu_ref[...]
        prev = _shift_down(zb, zh_ref[SUBLANES - 1:SUBLANES, :], i == 0)
        dm = db * m
        nxt = _shift_up(dm, dh_ref[0:1, :] * m, i == nblk - 1)
        dz_ref[...] = db - dm + nxt

        @pl.when(i == 0)
        def _():
            dmu_ref[...] = jnp.zeros_like(dmu_ref)

        dmu_ref[...] += jnp.sum(db * (prev - zb), 0, keepdims=True)

    return pl.pallas_call(
        body, name=name, grid=(width // cw, nblk),
        in_specs=[pl.BlockSpec((tm, cw), lambda j, i: (i, _zcol(j, cols))),
                  pl.BlockSpec((SUBLANES, cw), lambda j, i: (jnp.maximum(i * hb - 1, 0), _zcol(j, cols))),
                  pl.BlockSpec((tm, cw), lambda j, i: (i, j)),
                  pl.BlockSpec((SUBLANES, cw), lambda j, i: (jnp.minimum((i + 1) * hb, last8), j)),
                  pl.BlockSpec((1, cw), lambda j, i: (0, j))],
        out_specs=[pl.BlockSpec((tm, cw), lambda j, i: (i, j)), pl.BlockSpec((1, cw), lambda j, i: (0, j))],
        out_shape=[jax.ShapeDtypeStruct((S, width), F32), jax.ShapeDtypeStruct((1, width), F32)],
        compiler_params=_cparams(("parallel", "arbitrary")))(z, z, dzs, dzs, mu)


def conv_fwd(u, w, b, width, tr, cb, name):
    S, C = u.shape
    hb = tr // CONV_HALO

    def body(u_ref, h_ref, w_ref, b_ref, y_ref):
        i = pl.program_id(1)
        halo = jnp.where(i == 0, 0.0, h_ref[...])
        win = jnp.concatenate([halo, u_ref[...]], axis=0)
        acc = jnp.zeros((tr, cb), F32) + b_ref[...]
        for d in range(width):
            sh = win if d == 0 else pltpu.roll(win, d, 0)
            acc = acc + sh[CONV_HALO:, :] * w_ref[width - 1 - d:width - d, :]
        y_ref[...] = acc

    return pl.pallas_call(
        body, name=name, grid=(C // cb, S // tr),
        in_specs=[pl.BlockSpec((tr, cb), lambda j, i: (i, j)),
                  pl.BlockSpec((CONV_HALO, cb), lambda j, i: (jnp.maximum(i * hb - 1, 0), j)),
                  pl.BlockSpec((CONV_HALO, cb), lambda j, i: (0, j)),
                  pl.BlockSpec((1, cb), lambda j, i: (0, j))],
        out_specs=pl.BlockSpec((tr, cb), lambda j, i: (i, j)),
        out_shape=jax.ShapeDtypeStruct((S, C), F32), compiler_params=_cparams(("parallel", "parallel")))(u, u, w, b)


def conv_bwd(u, dy, w, width, tr, cb, name):
    S, C = u.shape
    hb = tr // CONV_HALO
    nblk = S // tr
    last = S // CONV_HALO - 1

    def body(u_ref, uh_ref, d_ref, dh_ref, w_ref, du_ref, dw_ref, db_ref):
        i = pl.program_id(1)
        dyb = d_ref[...]
        uwin = jnp.concatenate([jnp.where(i == 0, 0.0, uh_ref[...]), u_ref[...]], axis=0)
        dwin = jnp.concatenate([dyb, jnp.where(i == nblk - 1, 0.0, dh_ref[...])], axis=0)

        @pl.when(i == 0)
        def _():
            dw_ref[...] = jnp.zeros_like(dw_ref)
            db_ref[...] = jnp.zeros_like(db_ref)

        acc = jnp.zeros((tr, cb), F32)
        for d in range(width):
            tap = width - 1 - d
            dsh = dwin if d == 0 else pltpu.roll(dwin, tr + CONV_HALO - d, 0)
            acc = acc + dsh[:tr, :] * w_ref[tap:tap + 1, :]
            ush = uwin if d == 0 else pltpu.roll(uwin, d, 0)
            dw_ref[tap:tap + 1, :] += jnp.sum(ush[CONV_HALO:, :] * dyb, 0, keepdims=True)
        du_ref[...] = acc
        db_ref[...] += jnp.sum(dyb, 0, keepdims=True)

    return pl.pallas_call(
        body, name=name, grid=(C // cb, nblk),
        in_specs=[pl.BlockSpec((tr, cb), lambda j, i: (i, j)),
                  pl.BlockSpec((CONV_HALO, cb), lambda j, i: (jnp.maximum(i * hb - 1, 0), j)),
                  pl.BlockSpec((tr, cb), lambda j, i: (i, j)),
                  pl.BlockSpec((CONV_HALO, cb), lambda j, i: (jnp.minimum((i + 1) * hb, last), j)),
                  pl.BlockSpec((CONV_HALO, cb), lambda j, i: (0, j))],
        out_specs=[pl.BlockSpec((tr, cb), lambda j, i: (i, j)),
                   pl.BlockSpec((CONV_HALO, cb), lambda j, i: (0, j)),
                   pl.BlockSpec((1, cb), lambda j, i: (0, j))],
        out_shape=[jax.ShapeDtypeStruct((S, C), F32), jax.ShapeDtypeStruct((CONV_HALO, C), F32),
                   jax.ShapeDtypeStruct((1, C), F32)],
        compiler_params=_cparams(("parallel", "arbitrary")))(u, u, dy, dy, w)


_HI = lax.Precision.HIGHEST


@jax.custom_vjp
def _bnn(a, b):
    return lax.dot_general(a, b, (((2,), (1,)), ((0,), (0,))), precision=_HI, preferred_element_type=F32)


@jax.custom_vjp
def _bnt(a, b):
    return lax.dot_general(a, b, (((2,), (2,)), ((0,), (0,))), precision=_HI, preferred_element_type=F32)


@jax.custom_vjp
def _btn(a, b):
    return lax.dot_general(a, b, (((1,), (1,)), ((0,), (0,))), precision=_HI, preferred_element_type=F32)


_bnn.defvjp(lambda a, b: (_bnn(a, b), (a, b)), lambda res, g: (_bnt(g, res[1]), _btn(res[0], g)))
_bnt.defvjp(lambda a, b: (_bnt(a, b), (a, b)), lambda res, g: (_bnn(g, res[1]), _btn(g, res[0])))
_btn.defvjp(lambda a, b: (_btn(a, b), (a, b)), lambda res, g: (_bnt(res[1], g), _bnn(res[0], g)))


def scan_chunk(st0, r, lw, k, v, a, b):
    h, c, n = r.shape
    ti = lax.broadcasted_iota(jnp.int32, (h, c, c), 1)
    si = lax.broadcasted_iota(jnp.int32, (h, c, c), 2)
    incl = si <= ti
    strict = si < ti
    cum = _bnn(incl.astype(F32), lw)
    p = jnp.exp(cum)
    pinv = jnp.exp(-cum)
    at = a * jnp.exp(cum - lw)
    bt, kt, rt = b * pinv, k * pinv, r * p
    a_ab = jnp.where(strict, _bnt(at, bt), 0.0)
    a_ak = jnp.where(strict, _bnt(at, kt), 0.0)
    m_rb = jnp.where(incl, _bnt(rt, bt), 0.0)
    m_rk = jnp.where(incl, _bnt(rt, kt), 0.0)
    tinv = (si == ti).astype(F32) + a_ab
    pw = a_ab
    for _ in range(int(math.log2(c)) - 1):
        pw = _bnn(pw, pw)
        tinv = tinv + _bnn(tinv, pw)
    u = _bnn(tinv, _bnn(at, st0) + _bnn(a_ak, v))
    o = _bnn(rt, st0) + _bnn(m_rb, u) + _bnn(m_rk, v)
    cum_c = jnp.sum(lw, axis=1, keepdims=True)
    tail = jnp.exp(cum_c - cum)
    ki = lax.broadcasted_iota(jnp.int32, (h, n, n), 1)
    kj = lax.broadcasted_iota(jnp.int32, (h, n, n), 2)
    pc_col = jnp.sum(jnp.where(ki == kj, jnp.exp(cum_c), 0.0), axis=2, keepdims=True)
    st = st0 * pc_col + _btn(b * tail, u) + _btn(k * tail, v)
    return o, st


def scan_fwd(r, lw, k, v, a, b, hb, name):
    H, S, N = r.shape
    C = SCAN_CHUNK
    nc = S // C

    def body(r_ref, lw_ref, k_ref, v_ref, a_ref, b_ref, o_ref, ck_ref, st_ref):
        @pl.when(pl.program_id(1) == 0)
        def _():
            st_ref[...] = jnp.zeros_like(st_ref)

        st0 = st_ref[...]
        ck_ref[...] = st0[:, None]
        o, st = scan_chunk(st0, r_ref[...], lw_ref[...], k_ref[...], v_ref[...], a_ref[...], b_ref[...])
        o_ref[...] = o
        st_ref[...] = st

    seq = pl.BlockSpec((hb, C, N), lambda g, c: (g, c, 0))
    return pl.pallas_call(
        body, name=name, grid=(H // hb, nc), in_specs=[seq] * 6,
        out_specs=[seq, pl.BlockSpec((hb, 1, N, N), lambda g, c: (g, c, 0, 0))],
        out_shape=[jax.ShapeDtypeStruct((H, S, N), F32), jax.ShapeDtypeStruct((H, nc, N, N), F32)],
        scratch_shapes=[pltpu.VMEM((hb, N, N), F32)],
        compiler_params=_cparams(("parallel", "arbitrary")))(r, lw, k, v, a, b)


def scan_bwd(r, lw, k, v, a, b, ck, do, dr_add, dk_add, dv_add, hb, name):
    H, S, N = r.shape
    C = SCAN_CHUNK
    nc = S // C

    def body(r_ref, lw_ref, k_ref, v_ref, a_ref, b_ref, ck_ref, do_ref, ra_ref, ka_ref, va_ref,
             dr_ref, dlw_ref, dk_ref, dv_ref, da_ref, db_ref, dst_ref):
        @pl.when(pl.program_id(1) == 0)
        def _():
            dst_ref[...] = jnp.zeros_like(dst_ref)

        st0 = ck_ref[...][:, 0]
        _, vjp = jax.vjp(scan_chunk, st0, r_ref[...], lw_ref[...], k_ref[...], v_ref[...], a_ref[...], b_ref[...])
        dst0, dr, dlw, dk, dv, da, db = vjp((do_ref[...], dst_ref[...]))
        dr_ref[...], dlw_ref[...], dk_ref[...] = dr + ra_ref[...], dlw, dk + ka_ref[...]
        dv_ref[...], da_ref[...], db_ref[...] = dv + va_ref[...], da, db
        dst_ref[...] = dst0

    seq = pl.BlockSpec((hb, C, N), lambda g, c: (g, nc - 1 - c, 0))
    return pl.pallas_call(
        body, name=name, grid=(H // hb, nc),
        in_specs=[seq] * 6 + [pl.BlockSpec((hb, 1, N, N), lambda g, c: (g, nc - 1 - c, 0, 0))] + [seq] * 4,
        out_specs=[seq] * 6, out_shape=[jax.ShapeDtypeStruct((H, S, N), F32)] * 6,
        scratch_shapes=[pltpu.VMEM((hb, N, N), F32)],
        compiler_params=_cparams(("parallel", "arbitrary")))(r, lw, k, v, a, b, ck, do, dr_add, dk_add, dv_add)


def loss_head(y, target, tm, name):
    S, D = y.shape

    def body(y_ref, t_ref, dy_ref, l_ref):
        err = y_ref[...] - t_ref[...]
        dy_ref[...] = err * (1.0 / D)

        @pl.when(pl.program_id(0) == 0)
        def _():
            l_ref[...] = jnp.zeros_like(l_ref)

        l_ref[...] += 0.5 * jnp.sum(jnp.mean(err * err, -1, keepdims=True), 0, keepdims=True)

    row = pl.BlockSpec((tm, D), lambda i: (i, 0))
    return pl.pallas_call(
        body, name=name, grid=(S // tm,), in_specs=[row, row],
        out_specs=[row, pl.BlockSpec((SUBLANES, LANES), lambda i: (0, 0))],
        out_shape=[jax.ShapeDtypeStruct((S, D), F32), jax.ShapeDtypeStruct((SUBLANES, LANES), F32)],
        compiler_params=_cparams(("arbitrary",)))(y, target)


def adamw(parts, w, m, v, name):
    R, C = w.shape
    tr = R
    if R * C > ADAM_BLOCK_ELEMS:
        tr = _tile(R, [t for t in (512, 256, 128, 64, 32, 16) if t * C <= ADAM_BLOCK_ELEMS])
    c1 = 1.0 / (1.0 - ADAM_B1 ** ADAM_STEP)
    c2 = 1.0 / (1.0 - ADAM_B2 ** ADAM_STEP)

    def body(p_ref, w_ref, m_ref, v_ref, g_ref, d_ref, nm_ref, nv_ref):
        g = p_ref[0].astype(F32)
        for j in range(1, N_DEV):
            g = g + p_ref[j].astype(F32)
        nm = ADAM_B1 * m_ref[...] + (1.0 - ADAM_B1) * g
        nv = ADAM_B2 * v_ref[...] + (1.0 - ADAM_B2) * (g * g)
        g_ref[...] = g
        nm_ref[...] = nm
        nv_ref[...] = nv
        d_ref[...] = -ADAM_LR * ((nm * c1) / (jnp.sqrt(nv * c2) + ADAM_EPS) + ADAM_WD * w_ref[...])

    blk = pl.BlockSpec((tr, C), lambda i: (i, 0))
    return pl.pallas_call(
        body, name=name, grid=(R // tr,), in_specs=[pl.BlockSpec((N_DEV, tr, C), lambda i: (0, i, 0)), blk, blk, blk],
        out_specs=[blk] * 4, out_shape=[jax.ShapeDtypeStruct((R, C), F32)] * 4,
        compiler_params=_cparams(("parallel",)))(parts, w, m, v)


def exchange(src, gather, name):
    shape = src.shape if gather else src.shape[1:]

    def body(src_ref, out_ref, send_sems, recv_sems, local_sem):
        x, y, c = lax.axis_index("x"), lax.axis_index("y"), lax.axis_index("c")
        me = 4 * x + 2 * y + c
        mine = pltpu.make_async_copy(src_ref if gather else src_ref.at[me], out_ref.at[me], local_sem)
        mine.start()
        copies = []
        for k in range(1, N_DEV):
            px, py, pc = x ^ ((k >> 2) & 1), y ^ ((k >> 1) & 1), c ^ (k & 1)
            peer = 4 * px + 2 * py + pc
            copies.append(pltpu.make_async_remote_copy(
                src_ref=src_ref if gather else src_ref.at[peer], dst_ref=out_ref.at[me],
                send_sem=send_sems.at[k - 1], recv_sem=recv_sems.at[k - 1],
                device_id=(px, py, pc), device_id_type=pl.DeviceIdType.MESH))
        for cp in copies:
            cp.start()
        for cp in copies:
            cp.wait()
        mine.wait()

    return pl.pallas_call(
        body, name=name, in_specs=[pl.BlockSpec(memory_space=pl.ANY)], out_specs=pl.BlockSpec(memory_space=pl.ANY),
        out_shape=jax.ShapeDtypeStruct((N_DEV,) + tuple(shape), src.dtype),
        scratch_shapes=[pltpu.SemaphoreType.DMA((N_DEV - 1,)), pltpu.SemaphoreType.DMA((N_DEV - 1,)),
                        pltpu.SemaphoreType.DMA])(src)


def _to_heads(a2d):
    s, d = a2d.shape
    return a2d.reshape(s, d // RWKV_HEAD, RWKV_HEAD).transpose(1, 0, 2)


def _from_heads(a3d):
    h, s, n = a3d.shape
    return a3d.transpose(1, 0, 2).reshape(s, h * n)


def _cols_joined(g):
    return g.transpose(1, 0, 2).reshape(g.shape[1], N_DEV * g.shape[2])


def _cols_split(w):
    return w.reshape(w.shape[0], N_DEV, w.shape[1] // N_DEV).transpose(1, 0, 2)


def _round_up(n, m):
    return -(-n // m) * m


def _step(a):
    x, mem, target = a['x'][0], a['mem'][0], a['loss_target'][0]
    S, D = x.shape
    M = mem.shape[0]
    DR = a['rwkv_w0'].shape[1]
    H = DR // RWKV_HEAD
    DC = a['conv_b'].shape[1]
    r_decay, r_iclr, r_gate = a['rwkv_w_up'].shape[1], a['rwkv_a_up'].shape[1], a['rwkv_g_up'].shape[1]
    n_lora = r_decay + r_iclr + r_gate
    lora_w = _round_up(n_lora, LANES)
    n_rwkv = 3 * DR + n_lora
    zr_w = 3 * DR + lora_w
    pad_cols = zr_w - n_rwkv
    conv_taps = a['conv_w'].shape[1]
    assert conv_taps - 1 <= CONV_HALO and S % SCAN_CHUNK == 0

    def shard(n, pre=''):
        w = a[pre + n]
        if n == 'conv_w':
            return jnp.pad(w.reshape(conv_taps, w.shape[-1]), ((0, CONV_HALO - conv_taps), (0, 0)))
        return w.reshape(w.shape[1:])

    Wg = {n: exchange(shard(n).astype(BF16), True, "gather_" + n)
          for n in BIG if n not in ('rwkv_w_up', 'rwkv_a_up', 'rwkv_g_up', 'conv_w')}
    lora_src = jnp.concatenate([shard('rwkv_w_up'), shard('rwkv_a_up'), shard('rwkv_g_up')], axis=0).astype(BF16)
    lora_all = exchange(lora_src, True, "gather_lora")
    conv_w = _cols_joined(exchange(shard('conv_w'), True, "gather_conv_w"))
    W = {n: Wg[n].reshape(N_DEV * Wg[n].shape[1], Wg[n].shape[2]) for n in ROW_SHARDED}

    w_in = _cols_joined(Wg['w_in'])
    o_conv, o_gate = n_rwkv, n_rwkv + 2 * DC
    w_in_p = jnp.concatenate([w_in[:, o_gate:], w_in[:, :3 * DR], w_in[:, o_conv:o_gate], w_in[:, 3 * DR:n_rwkv],
                              jnp.zeros((D, pad_cols), BF16)], axis=1)
    c_gr, c_gc, c_rkv, c_u = 0, D, 2 * D, 2 * D + 3 * DR
    c_cg, c_lora = c_u + DC, c_u + 2 * DC
    assert c_rkv % lora_w == 0 and c_lora % lora_w == 0 and (3 * DR) % lora_w == 0 and c_u % DC == 0 and c_rkv % DR == 0
    shift_cols = (3 * DR // lora_w, c_rkv // lora_w, c_lora // lora_w)
    lora_full = _cols_joined(lora_all)
    w_lora = jnp.zeros((lora_w, 3 * DR), BF16)
    w_lora = w_lora.at[:r_decay, :DR].set(lora_full[:r_decay])
    w_lora = w_lora.at[r_decay:r_decay + r_iclr, DR:2 * DR].set(lora_full[r_decay:r_decay + r_iclr])
    w_lora = w_lora.at[r_decay + r_iclr:n_lora, 2 * DR:].set(lora_full[r_decay + r_iclr:])
    mu_p = jnp.pad(a['rwkv_shift_mix'], ((0, 0), (0, pad_cols)))
    x_bf = x.astype(BF16)

    hp = lambda n: a[n].reshape(H, 1, RWKV_HEAD)
    w0_h, a0_h, kk_h, ka_h, gng_h, gnb_h = (hp(n) for n in ('rwkv_w0', 'rwkv_a0', 'rwkv_k_k', 'rwkv_k_a',
                                                             'rwkv_gn_g', 'rwkv_gn_b'))
    rk_h = a['rwkv_r_k'].reshape(H, 1, RWKV_HEAD)
    ln_mem_g, ln_mem_b = a['ln_mem_g'].reshape(1, D), a['ln_mem_b'].reshape(1, D)

    tm_d = _tile(S, (64, 32, 16, 8))
    tm_a = _tile(S, (512, 256, 128, 64))
    rows = _Rows2D(S, tm_d)
    rows_mem = _Rows2D(M, _tile(M, (64, 32, 16, 8)))
    rows_ff = _Rows2D(S, _tile(S, (32, 16, 8)))
    hb = _tile(H, (4, 2, 1))
    heads = _RowsHeads(H, S, hb, _tile(S, (256, 128, 64)))
    tr_conv = _tile(S, (512, 256, 128, 64, 32))
    cb_conv = _tile(DC, (256, 128))
    fn_lora = make_fn_lora(r_decay, r_iclr, r_gate)
    lane_blk = lambda off, width: off // width

    mem_n, = ew_fwd(rows_mem, _as_bf16(fn_ln), [mem], [ln_mem_g, ln_mem_b], "ln_mem")
    z = matmul(x_bf, w_in_p, 'nn', "mm_z")
    zs = token_shift_fwd(z, shift_cols, lora_w, zr_w, mu_p, tm_a, "token_shift")
    lora, = ew_fwd(rows, _as_bf16(fn_lora), [(zs, lora_w, lane_blk(3 * DR, lora_w))], [], "lora_act")
    up = matmul(lora, w_lora, 'nn', "mm_lora_up")
    r_h, k_h, v_h = (_to_heads(zs[:, i * DR:(i + 1) * DR]) for i in range(3))
    wl_h, al_h, g_h = (_to_heads(up[:, i * DR:(i + 1) * DR]) for i in range(3))
    pre_rows, pre_pars = [k_h, wl_h, al_h], [w0_h, a0_h, kk_h, ka_h]
    lw_h, k2_h, na_h, b_h = ew_fwd(heads, fn_rwkv_pre, pre_rows, pre_pars, "rwkv_pre")
    o_h, ckpt = scan_fwd(r_h, lw_h, k2_h, v_h, na_h, b_h, hb, "scan_fwd")
    post_rows, post_pars = [o_h, r_h, k2_h, v_h, g_h], [gng_h, gnb_h, rk_h]
    or_h, = ew_fwd(heads, _as_bf16(fn_rwkv_post), post_rows, post_pars, "rwkv_post")
    o_r = _from_heads(or_h)

    glu_rows = [(z, DC, lane_blk(c_u, DC)), (z, DC, lane_blk(c_cg, DC))]
    u, = ew_fwd(rows, fn_glu, glu_rows, [], "glu")
    yc = conv_fwd(u, conv_w, a['conv_b'], conv_taps, tr_conv, cb_conv, "conv")
    cln = [a['conv_ln_g'], a['conv_ln_b']]
    o_c, = ew_fwd(rows, _as_bf16(fn_ln_silu), [yc], cln, "conv_ln_silu")

    G = {}
    rows_split = lambda g: g.reshape(N_DEV, g.shape[0] // N_DEV, g.shape[1])
    pr = matmul(o_r, Wg['proj_rwkv'], 'nn', "mm_proj_rwkv", b_dev=True)
    pc = matmul(o_c, Wg['proj_conv'], 'nn', "mm_proj_conv", b_dev=True)
    merge_rows = [(z, D, lane_blk(c_gr, D)), (z, D, lane_blk(c_gc, D)), pr, pc]
    merged, = ew_fwd(rows, _as_bf16(fn_merge), merge_rows, [], "merge")
    t1 = matmul(merged, W['w_out'], 'nn', "mm_w_out")
    ln1 = [a['ln1_g'], a['ln1_b']]
    h1, h1_bf = ew_fwd(rows, fn_ln_res_both, [x, t1], ln1, "ln1")

    q = matmul(h1_bf, W['xattn_wq'], 'nn', "mm_q")
    kx = matmul(mem_n, W['xattn_wk'], 'nn', "mm_k")
    vx = matmul(mem_n, W['xattn_wv'], 'nn', "mm_v")
    oa = attn_fwd(q, kx, vx, tm_a, "attn")
    ca = matmul(oa, W['xattn_wo'], 'nn', "mm_wo")
    ln2 = [a['ln2_g'], a['ln2_b']]
    h2, h2_bf = ew_fwd(rows, fn_ln_res_both, [h1, ca], ln2, "ln2")

    f1 = matmul(h2_bf, Wg['mlp_w1'], 'nn', "mm_w1", b_dev=True)
    act, = ew_fwd(rows_ff, _as_bf16(fn_relu2), [f1], [], "relu2")
    ff = matmul(act, W['mlp_w2'], 'nn', "mm_w2")
    ln3 = [a['ln3_g'], a['ln3_b']]
    h3, = ew_fwd(rows, fn_ln_res, [h2, ff], ln3, "ln3")
    dh3, loss_rows = loss_head(h3, target, tm_d, "loss")

    dh2a, dff, G['ln3_g'], G['ln3_b'] = ew_bwd(rows, fn_ln_res, [h2, ff], ln3, [dh3], (0, 1), (0, 1), "ln3_bwd",
                                               dr_dtypes=[F32, BF16])
    dact = matmul(dff, W['mlp_w2'], 'nt', "mm_w2_dx")
    G['mlp_w2'] = rows_split(matmul(act, dff, 'tn', "mm_w2_dw", out_dtype=BF16))
    df1, = ew_bwd(rows_ff, fn_relu2, [f1], [], [dact], (0,), (), "relu2_bwd", dr_dtypes=[BF16])
    dh2 = matmul(df1, Wg['mlp_w1'], 'nt', "mm_w1_dx", add=dh2a, b_dev=True)
    G['mlp_w1'] = matmul(h2_bf, df1, 'tn', "mm_w1_dw", out_dtype=BF16, out_dev=True)

    dh1a, dca, G['ln2_g'], G['ln2_b'] = ew_bwd(rows, fn_ln_res, [h1, ca], ln2, [dh2], (0, 1), (0, 1), "ln2_bwd",
                                               dr_dtypes=[F32, BF16])
    doa = matmul(dca, W['xattn_wo'], 'nt', "mm_wo_dx")
    G['xattn_wo'] = rows_split(matmul(oa, dca, 'tn', "mm_wo_dw", out_dtype=BF16))
    dq, dkx, dvx = attn_bwd(q, kx, vx, doa, tm_a, "attn_bwd")
    dh1 = matmul(dq, W['xattn_wq'], 'nt', "mm_q_dx", add=dh1a)
    G['xattn_wq'] = rows_split(matmul(h1_bf, dq, 'tn', "mm_q_dw", out_dtype=BF16))
    G['xattn_wk'] = rows_split(matmul(mem_n, dkx, 'tn', "mm_k_dw", out_dtype=BF16))
    G['xattn_wv'] = rows_split(matmul(mem_n, dvx, 'tn', "mm_v_dw", out_dtype=BF16))
    dmem_k = matmul(dkx, W['xattn_wk'], 'nt', "mm_k_dx")
    dmem_n = matmul(dvx, W['xattn_wv'], 'nt', "mm_v_dx", add=dmem_k)
    G['ln_mem_g'], G['ln_mem_b'] = ew_bwd(rows_mem, fn_ln, [mem], [ln_mem_g, ln_mem_b], [dmem_n], (), (0, 1),
                                          "ln_mem_bwd")

    dxa, dt1, G['ln1_g'], G['ln1_b'] = ew_bwd(rows, fn_ln_res, [x, t1], ln1, [dh1], (0, 1), (0, 1), "ln1_bwd",
                                              dr_dtypes=[F32, BF16])
    dmerged = matmul(dt1, W['w_out'], 'nt', "mm_w_out_dx")
    G['w_out'] = rows_split(matmul(merged, dt1, 'tn', "mm_w_out_dw", out_dtype=BF16))
    dzgr, dzgc, dpr, dpc = ew_bwd(rows, fn_merge, merge_rows, [], [dmerged], (0, 1, 2, 3), (), "merge_bwd",
                                  dr_dtypes=[BF16] * 4)
    do_r = matmul(dpr, Wg['proj_rwkv'], 'nt', "mm_proj_rwkv_dx", b_dev=True)
    G['proj_rwkv'] = matmul(o_r, dpr, 'tn', "mm_proj_rwkv_dw", out_dtype=BF16, out_dev=True)
    do_c = matmul(dpc, Wg['proj_conv'], 'nt', "mm_proj_conv_dx", b_dev=True)
    G['proj_conv'] = matmul(o_c, dpc, 'tn', "mm_proj_conv_dw", out_dtype=BF16, out_dev=True)

    dyc, G['conv_ln_g'], G['conv_ln_b'] = ew_bwd(rows, fn_ln_silu, [yc], cln, [do_c], (0,), (0, 1), "conv_ln_silu_bwd")
    du, dconv_w, G['conv_b'] = conv_bwd(u, dyc, conv_w, conv_taps, tr_conv, cb_conv, "conv_bwd")
    dzu, dzcg = ew_bwd(rows, fn_glu, glu_rows, [], [du], (0, 1), (), "glu_bwd", dr_dtypes=[BF16] * 2)

    do_r_h = _to_heads(do_r)
    do_h, dr1_h, dk2a_h, dv1_h, dg_h, dgng, dgnb, drk = ew_bwd(
        heads, fn_rwkv_post, post_rows, post_pars, [do_r_h], (0, 1, 2, 3, 4), (0, 1, 2), "rwkv_post_bwd",
        dr_dtypes=[F32, F32, F32, F32, BF16])
    dr_h, dlw_h, dk2_h, dv_h, dna_h, db_h = scan_bwd(r_h, lw_h, k2_h, v_h, na_h, b_h, ckpt, do_h, dr1_h, dk2a_h, dv1_h,
                                                     hb, "scan_bwd")
    dk_h, dwl_h, dal_h, dw0, da0, dkk, dka = ew_bwd(
        heads, fn_rwkv_pre, pre_rows, pre_pars, [dlw_h, dk2_h, dna_h, db_h], (0, 1, 2), (0, 1, 2, 3), "rwkv_pre_bwd",
        dr_dtypes=[F32, BF16, BF16])
    G['rwkv_w0'], G['rwkv_a0'], G['rwkv_k_k'], G['rwkv_k_a'] = (t.reshape(1, DR) for t in (dw0, da0, dkk, dka))
    G['rwkv_gn_g'], G['rwkv_gn_b'] = dgng.reshape(1, DR), dgnb.reshape(1, DR)
    G['rwkv_r_k'] = drk.reshape(1, H, RWKV_HEAD)
    dup = jnp.concatenate([_from_heads(dwl_h), _from_heads(dal_h), _from_heads(dg_h)], axis=1)
    dlora = matmul(dup, w_lora, 'nt', "mm_lora_up_dx")
    dw_lora = matmul(lora, dup, 'tn', "mm_lora_up_dw", out_dtype=BF16)
    d_lora_stack = jnp.concatenate([dw_lora[:r_decay, :DR], dw_lora[r_decay:r_decay + r_iclr, DR:2 * DR],
                                    dw_lora[r_decay + r_iclr:n_lora, 2 * DR:]], axis=0)
    dzs_lora, = ew_bwd(rows, fn_lora, [(zs, lora_w, lane_blk(3 * DR, lora_w))], [], [dlora], (0,), (), "lora_act_bwd")
    dzs = jnp.concatenate([_from_heads(dr_h), _from_heads(dk_h), _from_heads(dv_h), dzs_lora], axis=1)
    dzr, dmu = token_shift_bwd(z, dzs, shift_cols, lora_w, zr_w, mu_p, tm_a, "token_shift_bwd")
    G['rwkv_shift_mix'] = dmu[:, :n_rwkv]
    dz = jnp.concatenate([dzgr, dzgc, dzr[:, :3 * DR].astype(BF16), dzu, dzcg, dzr[:, 3 * DR:].astype(BF16)], axis=1)
    grad_x = matmul(dz, w_in_p, 'nt', "mm_z_dx", add=dxa)
    dw_in_p = matmul(x_bf, dz, 'tn', "mm_z_dw", out_dtype=BF16)
    G['w_in'] = _cols_split(jnp.concatenate([dw_in_p[:, c_rkv:c_u], dw_in_p[:, c_lora:c_lora + n_lora],
                                             dw_in_p[:, c_u:c_lora], dw_in_p[:, :c_rkv]], axis=1))

    out = {}

    def update(n, parts, w, m, v):
        res = adamw(parts, w, m, v, "adamw_" + n)
        out[n] = [t[:conv_taps].reshape(a[n].shape) if n == 'conv_w' else t.reshape(a[n].shape) for t in res]

    for n in BIG:
        if n not in ('rwkv_w_up', 'rwkv_a_up', 'rwkv_g_up', 'conv_w'):
            update(n, exchange(G[n], False, "scatter_" + n), shard(n), shard(n, 'm_'), shard(n, 'v_'))
    recv_lora = exchange(_cols_split(d_lora_stack), False, "scatter_lora")
    lo = 0
    for n, r_ in (('rwkv_w_up', r_decay), ('rwkv_a_up', r_iclr), ('rwkv_g_up', r_gate)):
        update(n, recv_lora[:, lo:lo + r_], shard(n), shard(n, 'm_'), shard(n, 'v_'))
        lo += r_
    recv_conv = exchange(_cols_split(dconv_w.astype(BF16)), False, "scatter_conv_w")
    update('conv_w', recv_conv, shard('conv_w'), shard('conv_w', 'm_'), shard('conv_w', 'v_'))

    small_sizes = [a[n].size for n in SMALL]
    n_small = sum(small_sizes) + 1
    n_pack = _round_up(n_small, SUBLANES * LANES)
    pack = lambda get, last: jnp.pad(jnp.concatenate([get(n).reshape(-1) for n in SMALL] + [last]),
                                     (0, n_pack - n_small)).reshape(n_pack // LANES, LANES)
    zero1 = jnp.zeros((1,), F32)
    g_pack = pack(lambda n: G[n], loss_rows[0, :1])
    parts_small = exchange(g_pack, True, "gather_small")
    res = adamw(parts_small, pack(lambda n: a[n], zero1), pack(lambda n: a['m_' + n], zero1),
                pack(lambda n: a['v_' + n], zero1), "adamw_small")
    res = [t.reshape(-1) for t in res]
    o_ = 0
    for n, sz in zip(SMALL, small_sizes):
        out[n] = [t[o_:o_ + sz].reshape(a[n].shape) for t in res]
        o_ += sz
    loss = res[0][o_]

    return (loss, grad_x[None], *[out[n][0] for n in WEIGHTS], *[out[n][1] for n in WEIGHTS],
            *[out[n][2] for n in WEIGHTS], *[out[n][3] for n in WEIGHTS])


def kernel(x, mem, w_in, rwkv_shift_mix, rwkv_w0, rwkv_w_up, rwkv_a0, rwkv_a_up, rwkv_g_up, rwkv_k_k, rwkv_k_a, rwkv_r_k, rwkv_gn_g, rwkv_gn_b, conv_w, conv_b, conv_ln_g, conv_ln_b, proj_rwkv, proj_conv, w_out, ln1_g, ln1_b, ln_mem_g, ln_mem_b, xattn_wq, xattn_wk, xattn_wv, xattn_wo, ln2_g, ln2_b, mlp_w1, mlp_w2, ln3_g, ln3_b, loss_target, m_w_in, m_rwkv_shift_mix, m_rwkv_w0, m_rwkv_w_up, m_rwkv_a0, m_rwkv_a_up, m_rwkv_g_up, m_rwkv_k_k, m_rwkv_k_a, m_rwkv_r_k, m_rwkv_gn_g, m_rwkv_gn_b, m_conv_w, m_conv_b, m_conv_ln_g, m_conv_ln_b, m_proj_rwkv, m_proj_conv, m_w_out, m_ln1_g, m_ln1_b, m_ln_mem_g, m_ln_mem_b, m_xattn_wq, m_xattn_wk, m_xattn_wv, m_xattn_wo, m_ln2_g, m_ln2_b, m_mlp_w1, m_mlp_w2, m_ln3_g, m_ln3_b, v_w_in, v_rwkv_shift_mix, v_rwkv_w0, v_rwkv_w_up, v_rwkv_a0, v_rwkv_a_up, v_rwkv_g_up, v_rwkv_k_k, v_rwkv_k_a, v_rwkv_r_k, v_rwkv_gn_g, v_rwkv_gn_b, v_conv_w, v_conv_b, v_conv_ln_g, v_conv_ln_b, v_proj_rwkv, v_proj_conv, v_w_out, v_ln1_g, v_ln1_b, v_ln_mem_g, v_ln_mem_b, v_xattn_wq, v_xattn_wk, v_xattn_wv, v_xattn_wo, v_ln2_g, v_ln2_b, v_mlp_w1, v_mlp_w2, v_ln3_g, v_ln3_b):
    return _step(dict(locals()))
```

```python
import functools
import math

import jax
import jax.numpy as jnp
from jax import lax
from jax.experimental import pallas as pl
from jax.experimental.pallas import tpu as pltpu

F32 = jnp.float32
BF16 = jnp.bfloat16

N_DEV = 8
RWKV_HEAD = 64
SCAN_CHUNK = 64
XATTN_HEADS = 4
CONV_HALO = 32
LN_EPS = 1e-5
GN_EPS = 64e-5
ALPHA = float(2.0 ** 0.25)
ADAM_LR, ADAM_B1, ADAM_B2, ADAM_EPS, ADAM_WD, ADAM_STEP = 0.001, 0.9, 0.999, 1e-08, 0.01, 10
LANES = 128
SUBLANES = 8
VMEM_LIMIT = 56 * 1024 * 1024
ADAM_BLOCK_ELEMS = 256 * 1024

WEIGHTS = ['w_in', 'rwkv_shift_mix', 'rwkv_w0', 'rwkv_w_up', 'rwkv_a0', 'rwkv_a_up', 'rwkv_g_up', 'rwkv_k_k',
           'rwkv_k_a', 'rwkv_r_k', 'rwkv_gn_g', 'rwkv_gn_b', 'conv_w', 'conv_b', 'conv_ln_g', 'conv_ln_b',
           'proj_rwkv', 'proj_conv', 'w_out', 'ln1_g', 'ln1_b', 'ln_mem_g', 'ln_mem_b', 'xattn_wq', 'xattn_wk',
           'xattn_wv', 'xattn_wo', 'ln2_g', 'ln2_b', 'mlp_w1', 'mlp_w2', 'ln3_g', 'ln3_b']
COL_SHARDED = ['w_in', 'rwkv_w_up', 'rwkv_a_up', 'rwkv_g_up', 'conv_w', 'proj_rwkv', 'proj_conv', 'mlp_w1']
ROW_SHARDED = ['w_out', 'xattn_wq', 'xattn_wk', 'xattn_wv', 'xattn_wo', 'mlp_w2']
BIG = ['w_in', 'rwkv_w_up', 'rwkv_a_up', 'rwkv_g_up', 'conv_w', 'proj_rwkv', 'proj_conv', 'w_out', 'xattn_wq',
       'xattn_wk', 'xattn_wv', 'xattn_wo', 'mlp_w1', 'mlp_w2']
SMALL = [w for w in WEIGHTS if w not in BIG]


def _cparams(dims):
    return pltpu.CompilerParams(dimension_semantics=dims, vmem_limit_bytes=VMEM_LIMIT)


def _tile(n, cands):
    for c in cands:
        if n % c == 0:
            return c
    return n


def _exchange_copies(e, src_ref, out_ref, gather, send_sems, recv_sems, local_sems):
    x, y, c = lax.axis_index("x"), lax.axis_index("y"), lax.axis_index("c")
    me = 4 * x + 2 * y + c
    copies = [pltpu.make_async_copy(src_ref if gather else src_ref.at[me], out_ref.at[me], local_sems.at[e])]
    for k in range(1, N_DEV):
        px, py, pc = x ^ ((k >> 2) & 1), y ^ ((k >> 1) & 1), c ^ (k & 1)
        peer = 4 * px + 2 * py + pc
        copies.append(pltpu.make_async_remote_copy(
            src_ref=src_ref if gather else src_ref.at[peer], dst_ref=out_ref.at[me],
            send_sem=send_sems.at[e * (N_DEV - 1) + k - 1], recv_sem=recv_sems.at[e * (N_DEV - 1) + k - 1],
            device_id=(px, py, pc), device_id_type=pl.DeviceIdType.MESH))
    return copies


class _Side:
    def __init__(self, items):
        self.items = list(items or [])
        self.n = len(self.items)
        any_spec = pl.BlockSpec(memory_space=pl.ANY)
        self.srcs = [s for s, _ in self.items]
        self.in_specs = [any_spec] * self.n
        self.out_specs = [any_spec] * self.n
        self.out_shapes = [jax.ShapeDtypeStruct((N_DEV,) + tuple(s.shape if g else s.shape[1:]), s.dtype)
                           for s, g in self.items]
        self.scratch = [pltpu.SemaphoreType.DMA((self.n * (N_DEV - 1),)), pltpu.SemaphoreType.DMA((self.n * (N_DEV - 1),)),
                        pltpu.SemaphoreType.DMA((self.n,))] if self.n else []

    def _copies(self, src_refs, out_refs, sems):
        return [cp for e, (_, g) in enumerate(self.items)
                for cp in _exchange_copies(e, src_refs[e], out_refs[e], g, *sems)]

    def run(self, first, last, src_refs, out_refs, sems):
        if not self.n:
            return lambda: None

        @pl.when(first)
        def _():
            for cp in self._copies(src_refs, out_refs, sems):
                cp.start()

        def finish():
            @pl.when(last)
            def _():
                for cp in self._copies(src_refs, out_refs, sems):
                    cp.wait()
        return finish


def exchange(src, gather, name):
    side = _Side([(src, gather)])

    def body(src_ref, out_ref, *sems):
        copies = side._copies([src_ref], [out_ref], sems)
        for cp in copies:
            cp.start()
        for cp in copies:
            cp.wait()

    return pl.pallas_call(body, name=name, in_specs=side.in_specs, out_specs=side.out_specs[0],
                          out_shape=side.out_shapes[0], scratch_shapes=side.scratch)(src)


def matmul(a, b, mode, name, add=None, out_dtype=F32, b_dev=False, out_dev=False, side=None):
    side = _Side(side)
    if b_dev:
        assert mode in ('nn', 'nt') and b.shape[0] == N_DEV
        b_rows, b_cols = b.shape[1], N_DEV * b.shape[2]
    else:
        b_rows, b_cols = b.shape
    if mode == 'nn':
        (M, K), (K2, N) = a.shape, (b_rows, b_cols)
    elif mode == 'nt':
        (M, K), (N, K2) = a.shape, (b_rows, b_cols)
    else:
        (K, M), (K2, N) = a.shape, (b_rows, b_cols)
    assert K == K2, (a.shape, b.shape, mode)
    n_unit = N // N_DEV if (out_dev or (b_dev and mode == 'nn')) else N
    k_unit = K // N_DEV if (b_dev and mode == 'nt') else K
    tm = _tile(M, (1024, 512, 256, 128))
    tn = _tile(n_unit, (1024, 512, 256, 128))
    tk = _tile(k_unit, (2048, 1024, 512, 256, 128))
    nk = K // tk
    nb, kb = n_unit // tn, k_unit // tk
    dims = {'nn': ((1,), (0,)), 'nt': ((1,), (1,)), 'tn': ((0,), (0,))}[mode]

    n_in = 2 + (add is not None)
    grid = (M // tm, N // tn, nk)

    def body(*refs):
        a_ref, b_ref = refs[:2]
        add_ref = refs[2] if add is not None else None
        side_src = refs[n_in:n_in + side.n]
        o_ref = refs[n_in + side.n]
        side_out = refs[n_in + side.n + 1:n_in + 2 * side.n + 1]
        acc_ref = refs[n_in + 2 * side.n + 1]
        sems = refs[n_in + 2 * side.n + 2:]
        i, j, k = pl.program_id(0), pl.program_id(1), pl.program_id(2)
        finish = side.run((i == 0) & (j == 0) & (k == 0), (i == grid[0] - 1) & (j == grid[1] - 1) & (k == nk - 1),
                          side_src, side_out, sems)

        @pl.when(k == 0)
        def _():
            acc_ref[...] = jnp.zeros_like(acc_ref)

        acc_ref[...] += lax.dot_general(a_ref[...].astype(BF16), b_ref[...].astype(BF16), (dims, ((), ())),
                                        preferred_element_type=F32)

        @pl.when(k == nk - 1)
        def _():
            r = acc_ref[...]
            if add is not None:
                r = r + add_ref[...]
            o_ref[...] = r.astype(out_dtype)

        finish()

    if mode == 'nn':
        a_spec = pl.BlockSpec((tm, tk), lambda i, j, k: (i, k))
        b_spec = (pl.BlockSpec((None, tk, tn), lambda i, j, k: (j // nb, k, j % nb)) if b_dev
                  else pl.BlockSpec((tk, tn), lambda i, j, k: (k, j)))
    elif mode == 'nt':
        a_spec = pl.BlockSpec((tm, tk), lambda i, j, k: (i, k))
        b_spec = (pl.BlockSpec((None, tn, tk), lambda i, j, k: (k // kb, j, k % kb)) if b_dev
                  else pl.BlockSpec((tn, tk), lambda i, j, k: (j, k)))
    else:
        a_spec = pl.BlockSpec((tk, tm), lambda i, j, k: (k, i))
        b_spec = pl.BlockSpec((tk, tn), lambda i, j, k: (k, j))
    add_spec = pl.BlockSpec((tm, tn), lambda i, j, k: (i, j))
    if out_dev:
        o_spec = pl.BlockSpec((None, tm, tn), lambda i, j, k: (j // nb, i, j % nb))
        o_shape = (N_DEV, M, N // N_DEV)
    else:
        o_spec, o_shape = add_spec, (M, N)
    in_specs = [a_spec, b_spec] + ([add_spec] if add is not None else [])
    ops = (a, b) + ((add,) if add is not None else ())
    sem = ("arbitrary",) * 3 if side.n else ("parallel", "parallel", "arbitrary")
    res = pl.pallas_call(
        body, name=name, grid=grid, in_specs=in_specs + side.in_specs, out_specs=[o_spec] + side.out_specs,
        out_shape=[jax.ShapeDtypeStruct(o_shape, out_dtype)] + side.out_shapes,
        scratch_shapes=[pltpu.VMEM((tm, tn), F32)] + side.scratch,
        compiler_params=_cparams(sem))(*ops, *side.srcs)
    return tuple(res) if side.n else res[0]


class _Rows2D:
    def __init__(self, n_rows, tm):
        self.n, self.tm = n_rows, tm
        self.grid = (1, n_rows // tm)

    def row(self, e):
        if isinstance(e, tuple):
            arr, width, cb = e
            return arr, (self.tm, width), pl.BlockSpec((self.tm, width), lambda g, i, cb=cb: (i, cb))
        return e, (self.tm, e.shape[1]), pl.BlockSpec((self.tm, e.shape[1]), lambda g, i: (i, 0))

    def par(self, p):
        return pl.BlockSpec(p.shape, lambda g, i: (0,) * p.ndim)

    def out(self, blk):
        return (self.n, blk[1]), pl.BlockSpec((self.tm, blk[1]), lambda g, i: (i, 0))


class _RowsHeads:
    def __init__(self, n_heads, n_rows, hb, ts):
        self.h, self.n, self.hb, self.ts = n_heads, n_rows, hb, ts
        self.grid = (n_heads // hb, n_rows // ts)

    def row(self, e):
        blk = (self.hb, self.ts, e.shape[2])
        return e, blk, pl.BlockSpec(blk, lambda g, i: (g, i, 0))

    def par(self, p):
        return pl.BlockSpec((self.hb, 1, p.shape[2]), lambda g, i: (g, 0, 0))

    def out(self, blk):
        return (self.h, self.n, blk[2]), pl.BlockSpec(blk, lambda g, i: (g, i, 0))


def _par_block(lay, p):
    return lay.par(p).block_shape


def ew_fwd(lay, fn, rows, params, name):
    rr = [lay.row(e) for e in rows]
    arrs = [r[0] for r in rr]
    blk_avals = [jax.ShapeDtypeStruct(r[1], r[0].dtype) for r in rr]
    par_avals = [jax.ShapeDtypeStruct(_par_block(lay, p), p.dtype) for p in params]
    outs = jax.eval_shape(fn, *blk_avals, *par_avals)
    out_full = [lay.out(o.shape) for o in outs]
    nr, npar = len(rows), len(params)

    def body(*refs):
        vals = [r[...] for r in refs[:nr + npar]]
        res = fn(*vals)
        for ref, v in zip(refs[nr + npar:], res):
            ref[...] = v.astype(ref.dtype)

    return pl.pallas_call(
        body, name=name, grid=lay.grid,
        in_specs=[r[2] for r in rr] + [lay.par(p) for p in params],
        out_specs=[o[1] for o in out_full],
        out_shape=[jax.ShapeDtypeStruct(o[0], a.dtype) for o, a in zip(out_full, outs)],
        compiler_params=_cparams(("parallel", "parallel")))(*arrs, *params)


def ew_bwd(lay, fn, rows, params, cots, wrt_rows, wrt_pars, name, dr_dtypes=None):
    rr = [lay.row(e) for e in rows]
    cc = [lay.row(e) for e in cots]
    nr, npar, nc = len(rows), len(params), len(cots)
    n_dr = len(wrt_rows)
    dr_full = [lay.out(rr[i][1]) for i in wrt_rows]
    dr_dtypes = dr_dtypes or [F32] * n_dr

    def body(*refs):
        rv = [r[...] for r in refs[:nr]]
        pv = [r[...] for r in refs[nr:nr + npar]]
        cv = tuple(r[...] for r in refs[nr + npar:nr + npar + nc])
        outs = refs[nr + npar + nc:]

        def f(*wrt):
            r2, p2 = list(rv), list(pv)
            for idx, v in zip(wrt_rows, wrt[:n_dr]):
                r2[idx] = v
            for idx, v in zip(wrt_pars, wrt[n_dr:]):
                p2[idx] = v
            return fn(*r2, *p2)

        _, vjp = jax.vjp(f, *[rv[i] for i in wrt_rows], *[pv[i] for i in wrt_pars])
        g = vjp(cv)
        for ref, v in zip(outs[:n_dr], g[:n_dr]):
            ref[...] = v.astype(ref.dtype)
        if wrt_pars:
            @pl.when(pl.program_id(1) == 0)
            def _():
                for ref in outs[n_dr:]:
                    ref[...] = jnp.zeros_like(ref)

            for ref, v in zip(outs[n_dr:], g[n_dr:]):
                ref[...] += v

    return pl.pallas_call(
        body, name=name, grid=lay.grid,
        in_specs=[r[2] for r in rr] + [lay.par(p) for p in params] + [c[2] for c in cc],
        out_specs=[o[1] for o in dr_full] + [lay.par(params[i]) for i in wrt_pars],
        out_shape=[jax.ShapeDtypeStruct(o[0], dt) for o, dt in zip(dr_full, dr_dtypes)]
        + [jax.ShapeDtypeStruct(params[i].shape, F32) for i in wrt_pars],
        compiler_params=_cparams(("parallel", "arbitrary")))(
            *[r[0] for r in rr], *params, *[c[0] for c in cc])


def _sigmoid(x):
    return 1.0 / (1.0 + jnp.exp(-x))


def _softplus(x):
    return jnp.maximum(x, 0.0) + jnp.log(1.0 + jnp.exp(-jnp.abs(x)))


def _layer_norm(x, g, b, eps):
    mu = jnp.mean(x, -1, keepdims=True)
    xc = x - mu
    var = jnp.mean(xc * xc, -1, keepdims=True)
    return xc * lax.rsqrt(var + eps) * g + b


def _as_bf16(fn):
    return lambda *args: tuple(o.astype(BF16) for o in fn(*args))


def fn_ln(x, g, b):
    return (_layer_norm(x, g, b, LN_EPS),)


def fn_ln_res(h, t, g, b):
    return (_layer_norm(ALPHA * h + t, g, b, LN_EPS),)


def fn_ln_res_both(h, t, g, b):
    y, = fn_ln_res(h, t, g, b)
    return y, y.astype(BF16)


def make_fn_lora(r_decay, r_iclr, r_gate):
    def fn(z):
        lane = lax.broadcasted_iota(jnp.int32, z.shape, 1)
        out = jnp.where(lane < r_decay, jnp.tanh(z), z)
        out = jnp.where(lane >= r_decay + r_iclr, _sigmoid(z), out)
        return (jnp.where(lane < r_decay + r_iclr + r_gate, out, 0.0),)
    return fn


def fn_rwkv_pre(k, wl, al, w0, a0, k_k, k_a):
    w = -_softplus(-(w0 + wl)) - 0.5
    lw = -jnp.exp(w)
    a = _sigmoid(a0 + al)
    kk = k * k_k
    kk = kk / jnp.maximum(jnp.sqrt(jnp.sum(kk * kk, -1, keepdims=True)), 1e-12)
    k2 = k * (1.0 + (a - 1.0) * k_a)
    return lw, k2, -kk, kk * a


def fn_rwkv_post(o, r, k2, v, g, gn_g, gn_b, r_k):
    mu = jnp.mean(o, -1, keepdims=True)
    oc = o - mu
    var = jnp.mean(oc * oc, -1, keepdims=True)
    y = oc * lax.rsqrt(var + GN_EPS) * gn_g + gn_b
    y = y + jnp.sum(r * k2 * r_k, -1, keepdims=True) * v
    return (y * g,)


def fn_glu(zu, zg):
    return (zu * _sigmoid(zg),)


def fn_ln_silu(y, g, b):
    n = _layer_norm(y, g, b, LN_EPS)
    return (n * _sigmoid(n),)


def fn_merge(zgr, zgc, pr, pc):
    return (_sigmoid(zgr) * pr + _sigmoid(zgc) * pc,)


def fn_relu2(f):
    r = jnp.maximum(f, 0.0)
    return (r * r,)


def _dot_nn(a, b):
    return lax.dot_general(a.astype(BF16), b.astype(BF16), (((1,), (0,)), ((), ())), preferred_element_type=F32)


def _dot_nt(a, b):
    return lax.dot_general(a.astype(BF16), b.astype(BF16), (((1,), (1,)), ((), ())), preferred_element_type=F32)


def _dot_tn(a, b):
    return lax.dot_general(a.astype(BF16), b.astype(BF16), (((0,), (0,)), ((), ())), preferred_element_type=F32)


def _softmax_rows(s):
    s = s - jnp.max(s, -1, keepdims=True)
    e = jnp.exp(s)
    return e / jnp.sum(e, -1, keepdims=True)


def attn_fwd(q, kx, vx, tm, name):
    S, D = q.shape
    M = kx.shape[0]
    dh = D // XATTN_HEADS
    scale = dh ** -0.5

    def body(q_ref, k_ref, v_ref, o_ref):
        p = _softmax_rows(_dot_nt(q_ref[...], k_ref[...]) * scale)
        o_ref[...] = _dot_nn(p, v_ref[...]).astype(BF16)

    row = pl.BlockSpec((tm, dh), lambda h, i: (i, h))
    kv = pl.BlockSpec((M, dh), lambda h, i: (0, h))
    return pl.pallas_call(body, name=name, grid=(XATTN_HEADS, S // tm), in_specs=[row, kv, kv], out_specs=row,
                          out_shape=jax.ShapeDtypeStruct((S, D), BF16),
                          compiler_params=_cparams(("parallel", "parallel")))(q, kx, vx)


def attn_bwd(q, kx, vx, do, tm, name):
    S, D = q.shape
    M = kx.shape[0]
    dh = D // XATTN_HEADS
    scale = dh ** -0.5

    def body(q_ref, k_ref, v_ref, do_ref, dq_ref, dk_ref, dv_ref):
        qb, kb, dob = q_ref[...], k_ref[...], do_ref[...]
        p = _softmax_rows(_dot_nt(qb, kb) * scale)
        dp = _dot_nt(dob, v_ref[...])
        ds = p * (dp - jnp.sum(dp * p, -1, keepdims=True)) * scale
        dq_ref[...] = _dot_nn(ds, kb).astype(BF16)

        @pl.when(pl.program_id(1) == 0)
        def _():
            dk_ref[...] = jnp.zeros_like(dk_ref)
            dv_ref[...] = jnp.zeros_like(dv_ref)

        dk_ref[...] += _dot_tn(ds, qb)
        dv_ref[...] += _dot_tn(p, dob)

    row = pl.BlockSpec((tm, dh), lambda h, i: (i, h))
    kv = pl.BlockSpec((M, dh), lambda h, i: (0, h))
    return pl.pallas_call(
        body, name=name, grid=(XATTN_HEADS, S // tm), in_specs=[row, kv, kv, row], out_specs=[row, kv, kv],
        out_shape=[jax.ShapeDtypeStruct((S, D), BF16), jax.ShapeDtypeStruct((M, D), F32), jax.ShapeDtypeStruct((M, D), F32)],
        compiler_params=_cparams(("parallel", "arbitrary")))(q, kx, vx, do)


def _shift_down(blk, halo_last_row, first_block):
    rolled = pltpu.roll(blk, 1, 0)
    row = lax.broadcasted_iota(jnp.int32, blk.shape, 0)
    top = jnp.where(first_block, 0.0, halo_last_row)
    return jnp.where(row == 0, top, rolled)


def _shift_up(blk, halo_first_row, last_block):
    n = blk.shape[0]
    rolled = pltpu.roll(blk, n - 1, 0)
    row = lax.broadcasted_iota(jnp.int32, blk.shape, 0)
    bot = jnp.where(last_block, 0.0, halo_first_row)
    return jnp.where(row == n - 1, bot, rolled)


def _zcol(j, cols):
    n_first, first, second = cols
    return jnp.where(j < n_first, first + j, second + j - n_first)


def token_shift_fwd(z, cols, cw, width, mu, tm, name):
    S = z.shape[0]
    hb = tm // SUBLANES

    def body(z_ref, halo_ref, mu_ref, o_ref):
        zb = z_ref[...]
        prev = _shift_down(zb, halo_ref[SUBLANES - 1:SUBLANES, :], pl.program_id(1) == 0)
        o_ref[...] = zb + (prev - zb) * mu_ref[...]

    return pl.pallas_call(
        body, name=name, grid=(width // cw, S // tm),
        in_specs=[pl.BlockSpec((tm, cw), lambda j, i: (i, _zcol(j, cols))),
                  pl.BlockSpec((SUBLANES, cw), lambda j, i: (jnp.maximum(i * hb - 1, 0), _zcol(j, cols))),
                  pl.BlockSpec((1, cw), lambda j, i: (0, j))],
        out_specs=pl.BlockSpec((tm, cw), lambda j, i: (i, j)),
        out_shape=jax.ShapeDtypeStruct((S, width), F32),
        compiler_params=_cparams(("parallel", "parallel")))(z, z, mu)


def token_shift_bwd(z, dzs, cols, cw, width, mu, tm, name):
    S = z.shape[0]
    hb = tm // SUBLANES
    nblk = S // tm
    last8 = S // SUBLANES - 1

    def body(z_ref, zh_ref, d_ref, dh_ref, mu_ref, dz_ref, dmu_ref):
        i = pl.program_id(1)
        zb, db, m = z_ref[...], d_ref[...], mu_ref[...]
        prev = _shift_down(zb, zh_ref[SUBLANES - 1:SUBLANES, :], i == 0)
        dm = db * m
        nxt = _shift_up(dm, dh_ref[0:1, :] * m, i == nblk - 1)
        dz_ref[...] = db - dm + nxt

        @pl.when(i == 0)
        def _():
            dmu_ref[...] = jnp.zeros_like(dmu_ref)

        dmu_ref[...] += jnp.sum(db * (prev - zb), 0, keepdims=True)

    return pl.pallas_call(
        body, name=name, grid=(width // cw, nblk),
        in_specs=[pl.BlockSpec((tm, cw), lambda j, i: (i, _zcol(j, cols))),
                  pl.BlockSpec((SUBLANES, cw), lambda j, i: (jnp.maximum(i * hb - 1, 0), _zcol(j, cols))),
                  pl.BlockSpec((tm, cw), lambda j, i: (i, j)),
                  pl.BlockSpec((SUBLANES, cw), lambda j, i: (jnp.minimum((i + 1) * hb, last8), j)),
                  pl.BlockSpec((1, cw), lambda j, i: (0, j))],
        out_specs=[pl.BlockSpec((tm, cw), lambda j, i: (i, j)), pl.BlockSpec((1, cw), lambda j, i: (0, j))],
        out_shape=[jax.ShapeDtypeStruct((S, width), F32), jax.ShapeDtypeStruct((1, width), F32)],
        compiler_params=_cparams(("parallel", "arbitrary")))(z, z, dzs, dzs, mu)


def conv_fwd(u, w, b, width, tr, cb, name):
    S, C = u.shape
    hb = tr // CONV_HALO

    def body(u_ref, h_ref, w_ref, b_ref, y_ref):
        i = pl.program_id(1)
        halo = jnp.where(i == 0, 0.0, h_ref[...])
        win = jnp.concatenate([halo, u_ref[...]], axis=0)
        acc = jnp.zeros((tr, cb), F32) + b_ref[...]
        for d in range(width):
            sh = win if d == 0 else pltpu.roll(win, d, 0)
            acc = acc + sh[CONV_HALO:, :] * w_ref[width - 1 - d:width - d, :]
        y_ref[...] = acc

    return pl.pallas_call(
        body, name=name, grid=(C // cb, S // tr),
        in_specs=[pl.BlockSpec((tr, cb), lambda j, i: (i, j)),
                  pl.BlockSpec((CONV_HALO, cb), lambda j, i: (jnp.maximum(i * hb - 1, 0), j)),
                  pl.BlockSpec((CONV_HALO, cb), lambda j, i: (0, j)),
                  pl.BlockSpec((1, cb), lambda j, i: (0, j))],
        out_specs=pl.BlockSpec((tr, cb), lambda j, i: (i, j)),
        out_shape=jax.ShapeDtypeStruct((S, C), F32), compiler_params=_cparams(("parallel", "parallel")))(u, u, w, b)


def conv_bwd(u, dy, w, width, tr, cb, name):
    S, C = u.shape
    hb = tr // CONV_HALO
    nblk = S // tr
    last = S // CONV_HALO - 1

    def body(u_ref, uh_ref, d_ref, dh_ref, w_ref, du_ref, dw_ref, db_ref):
        i = pl.program_id(1)
        dyb = d_ref[...]
        uwin = jnp.concatenate([jnp.where(i == 0, 0.0, uh_ref[...]), u_ref[...]], axis=0)
        dwin = jnp.concatenate([dyb, jnp.where(i == nblk - 1, 0.0, dh_ref[...])], axis=0)

        @pl.when(i == 0)
        def _():
            dw_ref[...] = jnp.zeros_like(dw_ref)
            db_ref[...] = jnp.zeros_like(db_ref)

        acc = jnp.zeros((tr, cb), F32)
        for d in range(width):
            tap = width - 1 - d
            dsh = dwin if d == 0 else pltpu.roll(dwin, tr + CONV_HALO - d, 0)
            acc = acc + dsh[:tr, :] * w_ref[tap:tap + 1, :]
            ush = uwin if d == 0 else pltpu.roll(uwin, d, 0)
            dw_ref[tap:tap + 1, :] += jnp.sum(ush[CONV_HALO:, :] * dyb, 0, keepdims=True)
        du_ref[...] = acc
        db_ref[...] += jnp.sum(dyb, 0, keepdims=True)

    return pl.pallas_call(
        body, name=name, grid=(C // cb, nblk),
        in_specs=[pl.BlockSpec((tr, cb), lambda j, i: (i, j)),
                  pl.BlockSpec((CONV_HALO, cb), lambda j, i: (jnp.maximum(i * hb - 1, 0), j)),
                  pl.BlockSpec((tr, cb), lambda j, i: (i, j)),
                  pl.BlockSpec((CONV_HALO, cb), lambda j, i: (jnp.minimum((i + 1) * hb, last), j)),
                  pl.BlockSpec((CONV_HALO, cb), lambda j, i: (0, j))],
        out_specs=[pl.BlockSpec((tr, cb), lambda j, i: (i, j)),
                   pl.BlockSpec((CONV_HALO, cb), lambda j, i: (0, j)),
                   pl.BlockSpec((1, cb), lambda j, i: (0, j))],
        out_shape=[jax.ShapeDtypeStruct((S, C), F32), jax.ShapeDtypeStruct((CONV_HALO, C), F32),
                   jax.ShapeDtypeStruct((1, C), F32)],
        compiler_params=_cparams(("parallel", "arbitrary")))(u, u, dy, dy, w)


_HI = lax.Precision.HIGHEST


@jax.custom_vjp
def _bnn(a, b):
    return lax.dot_general(a, b, (((2,), (1,)), ((0,), (0,))), precision=_HI, preferred_element_type=F32)


@jax.custom_vjp
def _bnt(a, b):
    return lax.dot_general(a, b, (((2,), (2,)), ((0,), (0,))), precision=_HI, preferred_element_type=F32)


@jax.custom_vjp
def _btn(a, b):
    return lax.dot_general(a, b, (((1,), (1,)), ((0,), (0,))), precision=_HI, preferred_element_type=F32)


_bnn.defvjp(lambda a, b: (_bnn(a, b), (a, b)), lambda res, g: (_bnt(g, res[1]), _btn(res[0], g)))
_bnt.defvjp(lambda a, b: (_bnt(a, b), (a, b)), lambda res, g: (_bnn(g, res[1]), _btn(g, res[0])))
_btn.defvjp(lambda a, b: (_btn(a, b), (a, b)), lambda res, g: (_bnt(res[1], g), _bnn(res[0], g)))


def scan_chunk(st0, r, lw, k, v, a, b):
    h, c, n = r.shape
    ti = lax.broadcasted_iota(jnp.int32, (h, c, c), 1)
    si = lax.broadcasted_iota(jnp.int32, (h, c, c), 2)
    incl = si <= ti
    strict = si < ti
    cum = _bnn(incl.astype(F32), lw)
    p = jnp.exp(cum)
    pinv = jnp.exp(-cum)
    at = a * jnp.exp(cum - lw)
    bt, kt, rt = b * pinv, k * pinv, r * p
    a_ab = jnp.where(strict, _bnt(at, bt), 0.0)
    a_ak = jnp.where(strict, _bnt(at, kt), 0.0)
    m_rb = jnp.where(incl, _bnt(rt, bt), 0.0)
    m_rk = jnp.where(incl, _bnt(rt, kt), 0.0)
    tinv = (si == ti).astype(F32) + a_ab
    pw = a_ab
    for _ in range(int(math.log2(c)) - 1):
        pw = _bnn(pw, pw)
        tinv = tinv + _bnn(tinv, pw)
    u = _bnn(tinv, _bnn(at, st0) + _bnn(a_ak, v))
    o = _bnn(rt, st0) + _bnn(m_rb, u) + _bnn(m_rk, v)
    cum_c = jnp.sum(lw, axis=1, keepdims=True)
    tail = jnp.exp(cum_c - cum)
    ki = lax.broadcasted_iota(jnp.int32, (h, n, n), 1)
    kj = lax.broadcasted_iota(jnp.int32, (h, n, n), 2)
    pc_col = jnp.sum(jnp.where(ki == kj, jnp.exp(cum_c), 0.0), axis=2, keepdims=True)
    st = st0 * pc_col + _btn(b * tail, u) + _btn(k * tail, v)
    return o, st


def _grid_ends(grid):
    g, c = pl.program_id(0), pl.program_id(1)
    return (g == 0) & (c == 0), (g == grid[0] - 1) & (c == grid[1] - 1)


def scan_fwd(r, lw, k, v, a, b, hb, name, side=None):
    H, S, N = r.shape
    C = SCAN_CHUNK
    nc = S // C
    side = _Side(side)
    grid = (H // hb, nc)

    def body(*refs):
        r_ref, lw_ref, k_ref, v_ref, a_ref, b_ref = refs[:6]
        side_src = refs[6:6 + side.n]
        o_ref, ck_ref = refs[6 + side.n:8 + side.n]
        side_out = refs[8 + side.n:8 + 2 * side.n]
        st_ref = refs[8 + 2 * side.n]
        finish = side.run(*_grid_ends(grid), side_src, side_out, refs[9 + 2 * side.n:])

        @pl.when(pl.program_id(1) == 0)
        def _():
            st_ref[...] = jnp.zeros_like(st_ref)

        st0 = st_ref[...]
        ck_ref[...] = st0[:, None]
        o, st = scan_chunk(st0, r_ref[...], lw_ref[...], k_ref[...], v_ref[...], a_ref[...], b_ref[...])
        o_ref[...] = o
        st_ref[...] = st
        finish()

    seq = pl.BlockSpec((hb, C, N), lambda g, c: (g, c, 0))
    return pl.pallas_call(
        body, name=name, grid=grid, in_specs=[seq] * 6 + side.in_specs,
        out_specs=[seq, pl.BlockSpec((hb, 1, N, N), lambda g, c: (g, c, 0, 0))] + side.out_specs,
        out_shape=[jax.ShapeDtypeStruct((H, S, N), F32), jax.ShapeDtypeStruct((H, nc, N, N), F32)] + side.out_shapes,
        scratch_shapes=[pltpu.VMEM((hb, N, N), F32)] + side.scratch,
        compiler_params=_cparams(("arbitrary", "arbitrary")))(r, lw, k, v, a, b, *side.srcs)


def scan_bwd(r, lw, k, v, a, b, ck, do, dr_add, dk_add, dv_add, hb, name, side=None):
    H, S, N = r.shape
    C = SCAN_CHUNK
    nc = S // C
    side = _Side(side)
    grid = (H // hb, nc)

    def body(*refs):
        r_ref, lw_ref, k_ref, v_ref, a_ref, b_ref, ck_ref, do_ref, ra_ref, ka_ref, va_ref = refs[:11]
        side_src = refs[11:11 + side.n]
        dr_ref, dlw_ref, dk_ref, dv_ref, da_ref, db_ref = refs[11 + side.n:17 + side.n]
        side_out = refs[17 + side.n:17 + 2 * side.n]
        dst_ref = refs[17 + 2 * side.n]
        finish = side.run(*_grid_ends(grid), side_src, side_out, refs[18 + 2 * side.n:])

        @pl.when(pl.program_id(1) == 0)
        def _():
            dst_ref[...] = jnp.zeros_like(dst_ref)

        st0 = ck_ref[...][:, 0]
        _, vjp = jax.vjp(scan_chunk, st0, r_ref[...], lw_ref[...], k_ref[...], v_ref[...], a_ref[...], b_ref[...])
        dst0, dr, dlw, dk, dv, da, db = vjp((do_ref[...], dst_ref[...]))
        dr_ref[...], dlw_ref[...], dk_ref[...] = dr + ra_ref[...], dlw, dk + ka_ref[...]
        dv_ref[...], da_ref[...], db_ref[...] = dv + va_ref[...], da, db
        dst_ref[...] = dst0
        finish()

    seq = pl.BlockSpec((hb, C, N), lambda g, c: (g, nc - 1 - c, 0))
    return pl.pallas_call(
        body, name=name, grid=grid,
        in_specs=[seq] * 6 + [pl.BlockSpec((hb, 1, N, N), lambda g, c: (g, nc - 1 - c, 0, 0))] + [seq] * 4 + side.in_specs,
        out_specs=[seq] * 6 + side.out_specs, out_shape=[jax.ShapeDtypeStruct((H, S, N), F32)] * 6 + side.out_shapes,
        scratch_shapes=[pltpu.VMEM((hb, N, N), F32)] + side.scratch,
        compiler_params=_cparams(("arbitrary", "arbitrary")))(r, lw, k, v, a, b, ck, do, dr_add, dk_add, dv_add,
                                                              *side.srcs)


def loss_head(y, target, tm, name):
    S, D = y.shape

    def body(y_ref, t_ref, dy_ref, l_ref):
        err = y_ref[...] - t_ref[...]
        dy_ref[...] = err * (1.0 / D)

        @pl.when(pl.program_id(0) == 0)
        def _():
            l_ref[...] = jnp.zeros_like(l_ref)

        l_ref[...] += 0.5 * jnp.sum(jnp.mean(err * err, -1, keepdims=True), 0, keepdims=True)

    row = pl.BlockSpec((tm, D), lambda i: (i, 0))
    return pl.pallas_call(
        body, name=name, grid=(S // tm,), in_specs=[row, row],
        out_specs=[row, pl.BlockSpec((SUBLANES, LANES), lambda i: (0, 0))],
        out_shape=[jax.ShapeDtypeStruct((S, D), F32), jax.ShapeDtypeStruct((SUBLANES, LANES), F32)],
        compiler_params=_cparams(("arbitrary",)))(y, target)


def adamw(parts, w, m, v, name):
    R, C = w.shape
    tr = R
    if R * C > ADAM_BLOCK_ELEMS:
        tr = _tile(R, [t for t in (512, 256, 128, 64, 32, 16) if t * C <= ADAM_BLOCK_ELEMS])
    c1 = 1.0 / (1.0 - ADAM_B1 ** ADAM_STEP)
    c2 = 1.0 / (1.0 - ADAM_B2 ** ADAM_STEP)

    def body(p_ref, w_ref, m_ref, v_ref, g_ref, d_ref, nm_ref, nv_ref):
        g = p_ref[0].astype(F32)
        for j in range(1, N_DEV):
            g = g + p_ref[j].astype(F32)
        nm = ADAM_B1 * m_ref[...] + (1.0 - ADAM_B1) * g
        nv = ADAM_B2 * v_ref[...] + (1.0 - ADAM_B2) * (g * g)
        g_ref[...] = g
        nm_ref[...] = nm
        nv_ref[...] = nv
        d_ref[...] = -ADAM_LR * ((nm * c1) / (jnp.sqrt(nv * c2) + ADAM_EPS) + ADAM_WD * w_ref[...])

    blk = pl.BlockSpec((tr, C), lambda i: (i, 0))
    return pl.pallas_call(
        body, name=name, grid=(R // tr,), in_specs=[pl.BlockSpec((N_DEV, tr, C), lambda i: (0, i, 0)), blk, blk, blk],
        out_specs=[blk] * 4, out_shape=[jax.ShapeDtypeStruct((R, C), F32)] * 4,
        compiler_params=_cparams(("parallel",)))(parts, w, m, v)


def _to_heads(a2d):
    s, d = a2d.shape
    return a2d.reshape(s, d // RWKV_HEAD, RWKV_HEAD).transpose(1, 0, 2)


def _from_heads(a3d):
    h, s, n = a3d.shape
    return a3d.transpose(1, 0, 2).reshape(s, h * n)


def _cols_joined(g):
    return g.transpose(1, 0, 2).reshape(g.shape[1], N_DEV * g.shape[2])


def _cols_split(w):
    return w.reshape(w.shape[0], N_DEV, w.shape[1] // N_DEV).transpose(1, 0, 2)


def _round_up(n, m):
    return -(-n // m) * m


def _step(a):
    x, mem, target = a['x'][0], a['mem'][0], a['loss_target'][0]
    S, D = x.shape
    M = mem.shape[0]
    DR = a['rwkv_w0'].shape[1]
    H = DR // RWKV_HEAD
    DC = a['conv_b'].shape[1]
    r_decay, r_iclr, r_gate = a['rwkv_w_up'].shape[1], a['rwkv_a_up'].shape[1], a['rwkv_g_up'].shape[1]
    n_lora = r_decay + r_iclr + r_gate
    lora_w = _round_up(n_lora, LANES)
    n_rwkv = 3 * DR + n_lora
    zr_w = 3 * DR + lora_w
    pad_cols = zr_w - n_rwkv
    conv_taps = a['conv_w'].shape[1]
    assert conv_taps - 1 <= CONV_HALO and S % SCAN_CHUNK == 0

    def shard(n, pre=''):
        w = a[pre + n]
        if n == 'conv_w':
            return jnp.pad(w.reshape(conv_taps, w.shape[-1]), ((0, CONV_HALO - conv_taps), (0, 0)))
        return w.reshape(w.shape[1:])

    Wg, W = {}, {}

    def gathers(names):
        return [(shard(n).astype(BF16), True) for n in names]

    def gathered(names, results):
        for n, g in zip(names, results):
            Wg[n] = g
            if n in ROW_SHARDED:
                W[n] = g.reshape(N_DEV * g.shape[1], g.shape[2])

    with_mm_z = ['proj_rwkv', 'proj_conv', 'w_out', 'xattn_wq']
    with_scan_fwd = ['xattn_wk', 'xattn_wv', 'xattn_wo', 'mlp_w1']
    with_mm_w1 = ['mlp_w2']
    with_scan_bwd = ['mlp_w2', 'mlp_w1', 'xattn_wo', 'xattn_wq', 'xattn_wk', 'xattn_wv', 'w_out', 'proj_rwkv', 'proj_conv']
    gathered(['w_in'], [exchange(shard('w_in').astype(BF16), True, "gather_w_in")])
    lora_src = jnp.concatenate([shard('rwkv_w_up'), shard('rwkv_a_up'), shard('rwkv_g_up')], axis=0).astype(BF16)
    lora_all = exchange(lora_src, True, "gather_lora")
    conv_w = _cols_joined(exchange(shard('conv_w'), True, "gather_conv_w"))

    w_in = _cols_joined(Wg['w_in'])
    o_conv, o_gate = n_rwkv, n_rwkv + 2 * DC
    w_in_p = jnp.concatenate([w_in[:, o_gate:], w_in[:, :3 * DR], w_in[:, o_conv:o_gate], w_in[:, 3 * DR:n_rwkv],
                              jnp.zeros((D, pad_cols), BF16)], axis=1)
    c_gr, c_gc, c_rkv, c_u = 0, D, 2 * D, 2 * D + 3 * DR
    c_cg, c_lora = c_u + DC, c_u + 2 * DC
    assert c_rkv % lora_w == 0 and c_lora % lora_w == 0 and (3 * DR) % lora_w == 0 and c_u % DC == 0 and c_rkv % DR == 0
    shift_cols = (3 * DR // lora_w, c_rkv // lora_w, c_lora // lora_w)
    lora_full = _cols_joined(lora_all)
    w_lora = jnp.zeros((lora_w, 3 * DR), BF16)
    w_lora = w_lora.at[:r_decay, :DR].set(lora_full[:r_decay])
    w_lora = w_lora.at[r_decay:r_decay + r_iclr, DR:2 * DR].set(lora_full[r_decay:r_decay + r_iclr])
    w_lora = w_lora.at[r_decay + r_iclr:n_lora, 2 * DR:].set(lora_full[r_decay + r_iclr:])
    mu_p = jnp.pad(a['rwkv_shift_mix'], ((0, 0), (0, pad_cols)))
    x_bf = x.astype(BF16)

    hp = lambda n: a[n].reshape(H, 1, RWKV_HEAD)
    w0_h, a0_h, kk_h, ka_h, gng_h, gnb_h = (hp(n) for n in ('rwkv_w0', 'rwkv_a0', 'rwkv_k_k', 'rwkv_k_a',
                                                             'rwkv_gn_g', 'rwkv_gn_b'))
    rk_h = a['rwkv_r_k'].reshape(H, 1, RWKV_HEAD)
    ln_mem_g, ln_mem_b = a['ln_mem_g'].reshape(1, D), a['ln_mem_b'].reshape(1, D)

    tm_d = _tile(S, (64, 32, 16, 8))
    tm_a = _tile(S, (512, 256, 128, 64))
    rows = _Rows2D(S, tm_d)
    rows_mem = _Rows2D(M, _tile(M, (64, 32, 16, 8)))
    rows_ff = _Rows2D(S, _tile(S, (32, 16, 8)))
    hb = _tile(H, (4, 2, 1))
    heads = _RowsHeads(H, S, hb, _tile(S, (256, 128, 64)))
    tr_conv = _tile(S, (512, 256, 128, 64, 32))
    cb_conv = _tile(DC, (256, 128))
    fn_lora = make_fn_lora(r_decay, r_iclr, r_gate)
    lane_blk = lambda off, width: off // width

    mem_n, = ew_fwd(rows_mem, _as_bf16(fn_ln), [mem], [ln_mem_g, ln_mem_b], "ln_mem")
    z, *got = matmul(x_bf, w_in_p, 'nn', "mm_z", side=gathers(with_mm_z))
    gathered(with_mm_z, got)
    zs = token_shift_fwd(z, shift_cols, lora_w, zr_w, mu_p, tm_a, "token_shift")
    lora, = ew_fwd(rows, _as_bf16(fn_lora), [(zs, lora_w, lane_blk(3 * DR, lora_w))], [], "lora_act")
    up = matmul(lora, w_lora, 'nn', "mm_lora_up")
    r_h, k_h, v_h = (_to_heads(zs[:, i * DR:(i + 1) * DR]) for i in range(3))
    wl_h, al_h, g_h = (_to_heads(up[:, i * DR:(i + 1) * DR]) for i in range(3))
    pre_rows, pre_pars = [k_h, wl_h, al_h], [w0_h, a0_h, kk_h, ka_h]
    lw_h, k2_h, na_h, b_h = ew_fwd(heads, fn_rwkv_pre, pre_rows, pre_pars, "rwkv_pre")
    o_h, ckpt, *got = scan_fwd(r_h, lw_h, k2_h, v_h, na_h, b_h, hb, "scan_fwd", side=gathers(with_scan_fwd))
    gathered(with_scan_fwd, got)
    post_rows, post_pars = [o_h, r_h, k2_h, v_h, g_h], [gng_h, gnb_h, rk_h]
    or_h, = ew_fwd(heads, _as_bf16(fn_rwkv_post), post_rows, post_pars, "rwkv_post")
    o_r = _from_heads(or_h)

    glu_rows = [(z, DC, lane_blk(c_u, DC)), (z, DC, lane_blk(c_cg, DC))]
    u, = ew_fwd(rows, fn_glu, glu_rows, [], "glu")
    yc = conv_fwd(u, conv_w, a['conv_b'], conv_taps, tr_conv, cb_conv, "conv")
    cln = [a['conv_ln_g'], a['conv_ln_b']]
    o_c, = ew_fwd(rows, _as_bf16(fn_ln_silu), [yc], cln, "conv_ln_silu")

    G = {}
    rows_split = lambda g: g.reshape(N_DEV, g.shape[0] // N_DEV, g.shape[1])
    pr = matmul(o_r, Wg['proj_rwkv'], 'nn', "mm_proj_rwkv", b_dev=True)
    pc = matmul(o_c, Wg['proj_conv'], 'nn', "mm_proj_conv", b_dev=True)
    merge_rows = [(z, D, lane_blk(c_gr, D)), (z, D, lane_blk(c_gc, D)), pr, pc]
    merged, = ew_fwd(rows, _as_bf16(fn_merge), merge_rows, [], "merge")
    t1 = matmul(merged, W['w_out'], 'nn', "mm_w_out")
    ln1 = [a['ln1_g'], a['ln1_b']]
    h1, h1_bf = ew_fwd(rows, fn_ln_res_both, [x, t1], ln1, "ln1")

    q = matmul(h1_bf, W['xattn_wq'], 'nn', "mm_q")
    kx = matmul(mem_n, W['xattn_wk'], 'nn', "mm_k")
    vx = matmul(mem_n, W['xattn_wv'], 'nn', "mm_v")
    oa = attn_fwd(q, kx, vx, tm_a, "attn")
    ca = matmul(oa, W['xattn_wo'], 'nn', "mm_wo")
    ln2 = [a['ln2_g'], a['ln2_b']]
    h2, h2_bf = ew_fwd(rows, fn_ln_res_both, [h1, ca], ln2, "ln2")

    f1, *got = matmul(h2_bf, Wg['mlp_w1'], 'nn', "mm_w1", b_dev=True, side=gathers(with_mm_w1))
    gathered(with_mm_w1, got)
    act, = ew_fwd(rows_ff, _as_bf16(fn_relu2), [f1], [], "relu2")
    ff = matmul(act, W['mlp_w2'], 'nn', "mm_w2")
    ln3 = [a['ln3_g'], a['ln3_b']]
    h3, = ew_fwd(rows, fn_ln_res, [h2, ff], ln3, "ln3")
    dh3, loss_rows = loss_head(h3, target, tm_d, "loss")

    dh2a, dff, G['ln3_g'], G['ln3_b'] = ew_bwd(rows, fn_ln_res, [h2, ff], ln3, [dh3], (0, 1), (0, 1), "ln3_bwd",
                                               dr_dtypes=[F32, BF16])
    dact = matmul(dff, W['mlp_w2'], 'nt', "mm_w2_dx")
    G['mlp_w2'] = rows_split(matmul(act, dff, 'tn', "mm_w2_dw", out_dtype=BF16))
    df1, = ew_bwd(rows_ff, fn_relu2, [f1], [], [dact], (0,), (), "relu2_bwd", dr_dtypes=[BF16])
    dh2 = matmul(df1, Wg['mlp_w1'], 'nt', "mm_w1_dx", add=dh2a, b_dev=True)
    G['mlp_w1'] = matmul(h2_bf, df1, 'tn', "mm_w1_dw", out_dtype=BF16, out_dev=True)

    dh1a, dca, G['ln2_g'], G['ln2_b'] = ew_bwd(rows, fn_ln_res, [h1, ca], ln2, [dh2], (0, 1), (0, 1), "ln2_bwd",
                                               dr_dtypes=[F32, BF16])
    doa = matmul(dca, W['xattn_wo'], 'nt', "mm_wo_dx")
    G['xattn_wo'] = rows_split(matmul(oa, dca, 'tn', "mm_wo_dw", out_dtype=BF16))
    dq, dkx, dvx = attn_bwd(q, kx, vx, doa, tm_a, "attn_bwd")
    dh1 = matmul(dq, W['xattn_wq'], 'nt', "mm_q_dx", add=dh1a)
    G['xattn_wq'] = rows_split(matmul(h1_bf, dq, 'tn', "mm_q_dw", out_dtype=BF16))
    G['xattn_wk'] = rows_split(matmul(mem_n, dkx, 'tn', "mm_k_dw", out_dtype=BF16))
    G['xattn_wv'] = rows_split(matmul(mem_n, dvx, 'tn', "mm_v_dw", out_dtype=BF16))
    dmem_k = matmul(dkx, W['xattn_wk'], 'nt', "mm_k_dx")
    dmem_n = matmul(dvx, W['xattn_wv'], 'nt', "mm_v_dx", add=dmem_k)
    G['ln_mem_g'], G['ln_mem_b'] = ew_bwd(rows_mem, fn_ln, [mem], [ln_mem_g, ln_mem_b], [dmem_n], (), (0, 1),
                                          "ln_mem_bwd")

    dxa, dt1, G['ln1_g'], G['ln1_b'] = ew_bwd(rows, fn_ln_res, [x, t1], ln1, [dh1], (0, 1), (0, 1), "ln1_bwd",
                                              dr_dtypes=[F32, BF16])
    dmerged = matmul(dt1, W['w_out'], 'nt', "mm_w_out_dx")
    G['w_out'] = rows_split(matmul(merged, dt1, 'tn', "mm_w_out_dw", out_dtype=BF16))
    dzgr, dzgc, dpr, dpc = ew_bwd(rows, fn_merge, merge_rows, [], [dmerged], (0, 1, 2, 3), (), "merge_bwd",
                                  dr_dtypes=[BF16] * 4)
    do_r = matmul(dpr, Wg['proj_rwkv'], 'nt', "mm_proj_rwkv_dx", b_dev=True)
    G['proj_rwkv'] = matmul(o_r, dpr, 'tn', "mm_proj_rwkv_dw", out_dtype=BF16, out_dev=True)
    do_c = matmul(dpc, Wg['proj_conv'], 'nt', "mm_proj_conv_dx", b_dev=True)
    G['proj_conv'] = matmul(o_c, dpc, 'tn', "mm_proj_conv_dw", out_dtype=BF16, out_dev=True)

    dyc, G['conv_ln_g'], G['conv_ln_b'] = ew_bwd(rows, fn_ln_silu, [yc], cln, [do_c], (0,), (0, 1), "conv_ln_silu_bwd")
    du, dconv_w, G['conv_b'] = conv_bwd(u, dyc, conv_w, conv_taps, tr_conv, cb_conv, "conv_bwd")
    dzu, dzcg = ew_bwd(rows, fn_glu, glu_rows, [], [du], (0, 1), (), "glu_bwd", dr_dtypes=[BF16] * 2)

    do_r_h = _to_heads(do_r)
    do_h, dr1_h, dk2a_h, dv1_h, dg_h, dgng, dgnb, drk = ew_bwd(
        heads, fn_rwkv_post, post_rows, post_pars, [do_r_h], (0, 1, 2, 3, 4), (0, 1, 2), "rwkv_post_bwd",
        dr_dtypes=[F32, F32, F32, F32, BF16])
    dr_h, dlw_h, dk2_h, dv_h, dna_h, db_h, *got = scan_bwd(
        r_h, lw_h, k2_h, v_h, na_h, b_h, ckpt, do_h, dr1_h, dk2a_h, dv1_h, hb, "scan_bwd",
        side=[(G[n], False) for n in with_scan_bwd])
    recv = dict(zip(with_scan_bwd, got))
    dk_h, dwl_h, dal_h, dw0, da0, dkk, dka = ew_bwd(
        heads, fn_rwkv_pre, pre_rows, pre_pars, [dlw_h, dk2_h, dna_h, db_h], (0, 1, 2), (0, 1, 2, 3), "rwkv_pre_bwd",
        dr_dtypes=[F32, BF16, BF16])
    G['rwkv_w0'], G['rwkv_a0'], G['rwkv_k_k'], G['rwkv_k_a'] = (t.reshape(1, DR) for t in (dw0, da0, dkk, dka))
    G['rwkv_gn_g'], G['rwkv_gn_b'] = dgng.reshape(1, DR), dgnb.reshape(1, DR)
    G['rwkv_r_k'] = drk.reshape(1, H, RWKV_HEAD)
    dup = jnp.concatenate([_from_heads(dwl_h), _from_heads(dal_h), _from_heads(dg_h)], axis=1)
    dlora = matmul(dup, w_lora, 'nt', "mm_lora_up_dx")
    dw_lora = matmul(lora, dup, 'tn', "mm_lora_up_dw", out_dtype=BF16)
    d_lora_stack = jnp.concatenate([dw_lora[:r_decay, :DR], dw_lora[r_decay:r_decay + r_iclr, DR:2 * DR],
                                    dw_lora[r_decay + r_iclr:n_lora, 2 * DR:]], axis=0)
    dzs_lora, = ew_bwd(rows, fn_lora, [(zs, lora_w, lane_blk(3 * DR, lora_w))], [], [dlora], (0,), (), "lora_act_bwd")
    dzs = jnp.concatenate([_from_heads(dr_h), _from_heads(dk_h), _from_heads(dv_h), dzs_lora], axis=1)
    dzr, dmu = token_shift_bwd(z, dzs, shift_cols, lora_w, zr_w, mu_p, tm_a, "token_shift_bwd")
    G['rwkv_shift_mix'] = dmu[:, :n_rwkv]
    dz = jnp.concatenate([dzgr, dzgc, dzr[:, :3 * DR].astype(BF16), dzu, dzcg, dzr[:, 3 * DR:].astype(BF16)], axis=1)
    grad_x = matmul(dz, w_in_p, 'nt', "mm_z_dx", add=dxa)
    dw_in_p = matmul(x_bf, dz, 'tn', "mm_z_dw", out_dtype=BF16)
    G['w_in'] = _cols_split(jnp.concatenate([dw_in_p[:, c_rkv:c_u], dw_in_p[:, c_lora:c_lora + n_lora],
                                             dw_in_p[:, c_u:c_lora], dw_in_p[:, :c_rkv]], axis=1))

    out = {}

    def update(n, parts, w, m, v):
        res = adamw(parts, w, m, v, "adamw_" + n)
        out[n] = [t[:conv_taps].reshape(a[n].shape) if n == 'conv_w' else t.reshape(a[n].shape) for t in res]

    recv['w_in'] = exchange(G['w_in'], False, "scatter_w_in")
    for n in ['w_in'] + with_scan_bwd:
        update(n, recv[n], shard(n), shard(n, 'm_'), shard(n, 'v_'))
    recv_lora = exchange(_cols_split(d_lora_stack), False, "scatter_lora")
    lo = 0
    for n, r_ in (('rwkv_w_up', r_decay), ('rwkv_a_up', r_iclr), ('rwkv_g_up', r_gate)):
        update(n, recv_lora[:, lo:lo + r_], shard(n), shard(n, 'm_'), shard(n, 'v_'))
        lo += r_
    recv_conv = exchange(_cols_split(dconv_w.astype(BF16)), False, "scatter_conv_w")
    update('conv_w', recv_conv, shard('conv_w'), shard('conv_w', 'm_'), shard('conv_w', 'v_'))

    small_sizes = [a[n].size for n in SMALL]
    n_small = sum(small_sizes) + 1
    n_pack = _round_up(n_small, SUBLANES * LANES)
    pack = lambda get, last: jnp.pad(jnp.concatenate([get(n).reshape(-1) for n in SMALL] + [last]),
                                     (0, n_pack - n_small)).reshape(n_pack // LANES, LANES)
    zero1 = jnp.zeros((1,), F32)
    g_pack = pack(lambda n: G[n], loss_rows[0, :1])
    parts_small = exchange(g_pack, True, "gather_small")
    res = adamw(parts_small, pack(lambda n: a[n], zero1), pack(lambda n: a['m_' + n], zero1),
                pack(lambda n: a['v_' + n], zero1), "adamw_small")
    res = [t.reshape(-1) for t in res]
    o_ = 0
    for n, sz in zip(SMALL, small_sizes):
        out[n] = [t[o_:o_ + sz].reshape(a[n].shape) for t in res]
        o_ += sz
    loss = res[0][o_]

    return (loss, grad_x[None], *[out[n][0] for n in WEIGHTS], *[out[n][1] for n in WEIGHTS],
            *[out[n][2] for n in WEIGHTS], *[out[n][3] for n in WEIGHTS])


def kernel(x, mem, w_in, rwkv_shift_mix, rwkv_w0, rwkv_w_up, rwkv_a0, rwkv_a_up, rwkv_g_up, rwkv_k_k, rwkv_k_a, rwkv_r_k, rwkv_gn_g, rwkv_gn_b, conv_w, conv_b, conv_ln_g, conv_ln_b, proj_rwkv, proj_conv, w_out, ln1_g, ln1_b, ln_mem_g, ln_mem_b, xattn_wq, xattn_wk, xattn_wv, xattn_wo, ln2_g, ln2_b, mlp_w1, mlp_w2, ln3_g, ln3_b, loss_target, m_w_in, m_rwkv_shift_mix, m_rwkv_w0, m_rwkv_w_up, m_rwkv_a0, m_rwkv_a_up, m_rwkv_g_up, m_rwkv_k_k, m_rwkv_k_a, m_rwkv_r_k, m_rwkv_gn_g, m_rwkv_gn_b, m_conv_w, m_conv_b, m_conv_ln_g, m_conv_ln_b, m_proj_rwkv, m_proj_conv, m_w_out, m_ln1_g, m_ln1_b, m_ln_mem_g, m_ln_mem_b, m_xattn_wq, m_xattn_wk, m_xattn_wv, m_xattn_wo, m_ln2_g, m_ln2_b, m_mlp_w1, m_mlp_w2, m_ln3_g, m_ln3_b, v_w_in, v_rwkv_shift_mix, v_rwkv_w0, v_rwkv_w_up, v_rwkv_a0, v_rwkv_a_up, v_rwkv_g_up, v_rwkv_k_k, v_rwkv_k_a, v_rwkv_r_k, v_rwkv_gn_g, v_rwkv_gn_b, v_conv_w, v_conv_b, v_conv_ln_g, v_conv_ln_b, v_proj_rwkv, v_proj_conv, v_w_out, v_ln1_g, v_ln1_b, v_ln_mem_g, v_ln_mem_b, v_xattn_wq, v_xattn_wk, v_xattn_wv, v_xattn_wo, v_ln2_g, v_ln2_b, v_mlp_w1, v_mlp_w2, v_ln3_g, v_ln3_b):
    return _step(dict(locals()))
```

```python
import functools
import math

import jax
import jax.numpy as jnp
from jax import lax
from jax.experimental import pallas as pl
from jax.experimental.pallas import tpu as pltpu

F32 = jnp.float32
BF16 = jnp.bfloat16

N_DEV = 8
RWKV_HEAD = 64
SCAN_CHUNK = 64
XATTN_HEADS = 4
CONV_HALO = 32
LN_EPS = 1e-5
GN_EPS = 64e-5
ALPHA = float(2.0 ** 0.25)
ADAM_LR, ADAM_B1, ADAM_B2, ADAM_EPS, ADAM_WD, ADAM_STEP = 0.001, 0.9, 0.999, 1e-08, 0.01, 10
LANES = 128
SUBLANES = 8
VMEM_LIMIT = 56 * 1024 * 1024
ADAM_BLOCK_ELEMS = 256 * 1024

WEIGHTS = ['w_in', 'rwkv_shift_mix', 'rwkv_w0', 'rwkv_w_up', 'rwkv_a0', 'rwkv_a_up', 'rwkv_g_up', 'rwkv_k_k',
           'rwkv_k_a', 'rwkv_r_k', 'rwkv_gn_g', 'rwkv_gn_b', 'conv_w', 'conv_b', 'conv_ln_g', 'conv_ln_b',
           'proj_rwkv', 'proj_conv', 'w_out', 'ln1_g', 'ln1_b', 'ln_mem_g', 'ln_mem_b', 'xattn_wq', 'xattn_wk',
           'xattn_wv', 'xattn_wo', 'ln2_g', 'ln2_b', 'mlp_w1', 'mlp_w2', 'ln3_g', 'ln3_b']
COL_SHARDED = ['w_in', 'rwkv_w_up', 'rwkv_a_up', 'rwkv_g_up', 'conv_w', 'proj_rwkv', 'proj_conv', 'mlp_w1']
ROW_SHARDED = ['w_out', 'xattn_wq', 'xattn_wk', 'xattn_wv', 'xattn_wo', 'mlp_w2']
BIG = ['w_in', 'rwkv_w_up', 'rwkv_a_up', 'rwkv_g_up', 'conv_w', 'proj_rwkv', 'proj_conv', 'w_out', 'xattn_wq',
       'xattn_wk', 'xattn_wv', 'xattn_wo', 'mlp_w1', 'mlp_w2']
SMALL = [w for w in WEIGHTS if w not in BIG]


def _cparams(dims):
    return pltpu.CompilerParams(dimension_semantics=dims, vmem_limit_bytes=VMEM_LIMIT)


def _tile(n, cands):
    for c in cands:
        if n % c == 0:
            return c
    return n


N_CHIP = N_DEV // 2
SEMS_PER_EXCHANGE = N_DEV + 2
CHIP_XORS = (2, 4, 6)


def _exchange_shape(src, kind):
    return {'gather': (N_DEV,) + src.shape, 'gather2': (N_DEV,) + src.shape, 'scatter': src.shape,
            'scatter_sib': (N_CHIP,) + src.shape[1:], 'scatter_chips': src.shape}[kind]


def _exchange_copies(e, src_ref, out_ref, kind, send_sems, recv_sems, local_sems):
    x, y, c = lax.axis_index("x"), lax.axis_index("y"), lax.axis_index("c")
    me, chip = 4 * x + 2 * y + c, 2 * x + y
    sibling = (x, y, 1 - c)
    base = e * SEMS_PER_EXCHANGE

    def remote(src, dst, idx, dev):
        return pltpu.make_async_remote_copy(src_ref=src, dst_ref=dst, send_sem=send_sems.at[base + idx],
                                            recv_sem=recv_sems.at[base + idx], device_id=dev,
                                            device_id_type=pl.DeviceIdType.MESH)

    def peer(k):
        return x ^ ((k >> 2) & 1), y ^ ((k >> 1) & 1), c ^ (k & 1)

    first, second = [], []
    if kind in ('gather', 'gather2'):
        first.append(pltpu.make_async_copy(src_ref, out_ref.at[me], local_sems.at[e]))
        for k in (range(1, N_DEV) if kind == 'gather' else (1,) + CHIP_XORS):
            first.append(remote(src_ref, out_ref.at[me], k - 1, peer(k)))
        if kind == 'gather2':
            for i, k in enumerate(CHIP_XORS):
                second.append(remote(out_ref.at[me ^ k], out_ref.at[me ^ k], N_DEV - 1 + i, sibling))
    elif kind == 'scatter':
        first.append(pltpu.make_async_copy(src_ref.at[me], out_ref.at[me], local_sems.at[e]))
        for k in range(1, N_DEV):
            px, py, pc = peer(k)
            first.append(remote(src_ref.at[4 * px + 2 * py + pc], out_ref.at[me], k - 1, (px, py, pc)))
    elif kind == 'scatter_sib':
        for q in range(N_CHIP):
            first.append(remote(src_ref.at[2 * q + 1 - c], out_ref.at[q], q, sibling))
    else:
        assert kind == 'scatter_chips', kind
        first.append(pltpu.make_async_copy(src_ref.at[chip], out_ref.at[chip], local_sems.at[e]))
        for k in CHIP_XORS:
            first.append(remote(src_ref.at[chip ^ (k >> 1)], out_ref.at[chip], k - 1, peer(k)))
    return first, second


class _Side:
    def __init__(self, items):
        self.items = list(items or [])
        self.n = len(self.items)
        any_spec = pl.BlockSpec(memory_space=pl.ANY)
        self.srcs = [s for s, _ in self.items]
        self.in_specs = [any_spec] * self.n
        self.out_specs = [any_spec] * self.n
        self.out_shapes = [jax.ShapeDtypeStruct(_exchange_shape(s, kind), s.dtype) for s, kind in self.items]
        self.scratch = [pltpu.SemaphoreType.DMA((self.n * SEMS_PER_EXCHANGE,)),
                        pltpu.SemaphoreType.DMA((self.n * SEMS_PER_EXCHANGE,)),
                        pltpu.SemaphoreType.DMA((self.n,))] if self.n else []

    def _copies(self, src_refs, out_refs, sems):
        both = [_exchange_copies(e, src_refs[e], out_refs[e], kind, *sems) for e, (_, kind) in enumerate(self.items)]
        return [cp for f, _ in both for cp in f], [cp for _, s in both for cp in s]

    def start(self, src_refs, out_refs, sems):
        for cp in self._copies(src_refs, out_refs, sems)[0]:
            cp.start()

    def finish(self, src_refs, out_refs, sems):
        first, second = self._copies(src_refs, out_refs, sems)
        for cp in first:
            cp.wait()
        for cp in second:
            cp.start()
        for cp in second:
            cp.wait()

    def run(self, first, last, src_refs, out_refs, sems):
        if not self.n:
            return lambda: None
        pl.when(first)(lambda: self.start(src_refs, out_refs, sems))
        return lambda: pl.when(last)(lambda: self.finish(src_refs, out_refs, sems))


def exchange(src, kind, name):
    side = _Side([(src, kind)])

    def body(src_ref, out_ref, *sems):
        side.start([src_ref], [out_ref], sems)
        side.finish([src_ref], [out_ref], sems)

    return pl.pallas_call(body, name=name, in_specs=side.in_specs, out_specs=side.out_specs[0],
                          out_shape=side.out_shapes[0], scratch_shapes=side.scratch)(src)


def pair_add(parts, landed, name):
    _, R, C = parts.shape
    tr = R
    if R * C > ADAM_BLOCK_ELEMS:
        tr = _tile(R, [t for t in (512, 256, 128, 64, 32, 16) if t * C <= ADAM_BLOCK_ELEMS])

    def body(p_ref, l_ref, o_ref):
        mine = p_ref[lax.axis_index("c")]
        o_ref[...] = (mine.astype(F32) + l_ref[...].astype(F32)).astype(BF16)

    return pl.pallas_call(
        body, name=name, grid=(N_CHIP, R // tr),
        in_specs=[pl.BlockSpec((None, 2, tr, C), lambda q, i: (q, 0, i, 0)), pl.BlockSpec((None, tr, C), lambda q, i: (q, i, 0))],
        out_specs=pl.BlockSpec((None, tr, C), lambda q, i: (q, i, 0)),
        out_shape=jax.ShapeDtypeStruct((N_CHIP, R, C), BF16),
        compiler_params=_cparams(("parallel", "parallel")))(parts.reshape(N_CHIP, 2, R, C), landed)


def matmul(a, b, mode, name, add=None, out_dtype=F32, b_dev=False, out_dev=False, side=None):
    side = _Side(side)
    if b_dev:
        assert mode in ('nn', 'nt') and b.shape[0] == N_DEV
        b_rows, b_cols = b.shape[1], N_DEV * b.shape[2]
    else:
        b_rows, b_cols = b.shape
    if mode == 'nn':
        (M, K), (K2, N) = a.shape, (b_rows, b_cols)
    elif mode == 'nt':
        (M, K), (N, K2) = a.shape, (b_rows, b_cols)
    else:
        (K, M), (K2, N) = a.shape, (b_rows, b_cols)
    assert K == K2, (a.shape, b.shape, mode)
    n_unit = N // N_DEV if (out_dev or (b_dev and mode == 'nn')) else N
    k_unit = K // N_DEV if (b_dev and mode == 'nt') else K
    tm = _tile(M, (1024, 512, 256, 128))
    tn = _tile(n_unit, (1024, 512, 256, 128))
    tk = _tile(k_unit, (2048, 1024, 512, 256, 128))
    nk = K // tk
    nb, kb = n_unit // tn, k_unit // tk
    dims = {'nn': ((1,), (0,)), 'nt': ((1,), (1,)), 'tn': ((0,), (0,))}[mode]

    n_in = 2 + (add is not None)
    grid = (M // tm, N // tn, nk)

    def body(*refs):
        a_ref, b_ref = refs[:2]
        add_ref = refs[2] if add is not None else None
        side_src = refs[n_in:n_in + side.n]
        o_ref = refs[n_in + side.n]
        side_out = refs[n_in + side.n + 1:n_in + 2 * side.n + 1]
        acc_ref = refs[n_in + 2 * side.n + 1]
        sems = refs[n_in + 2 * side.n + 2:]
        i, j, k = pl.program_id(0), pl.program_id(1), pl.program_id(2)
        finish = side.run((i == 0) & (j == 0) & (k == 0), (i == grid[0] - 1) & (j == grid[1] - 1) & (k == nk - 1),
                          side_src, side_out, sems)

        @pl.when(k == 0)
        def _():
            acc_ref[...] = jnp.zeros_like(acc_ref)

        acc_ref[...] += lax.dot_general(a_ref[...].astype(BF16), b_ref[...].astype(BF16), (dims, ((), ())),
                                        preferred_element_type=F32)

        @pl.when(k == nk - 1)
        def _():
            r = acc_ref[...]
            if add is not None:
                r = r + add_ref[...]
            o_ref[...] = r.astype(out_dtype)

        finish()

    if mode == 'nn':
        a_spec = pl.BlockSpec((tm, tk), lambda i, j, k: (i, k))
        b_spec = (pl.BlockSpec((None, tk, tn), lambda i, j, k: (j // nb, k, j % nb)) if b_dev
                  else pl.BlockSpec((tk, tn), lambda i, j, k: (k, j)))
    elif mode == 'nt':
        a_spec = pl.BlockSpec((tm, tk), lambda i, j, k: (i, k))
        b_spec = (pl.BlockSpec((None, tn, tk), lambda i, j, k: (k // kb, j, k % kb)) if b_dev
                  else pl.BlockSpec((tn, tk), lambda i, j, k: (j, k)))
    else:
        a_spec = pl.BlockSpec((tk, tm), lambda i, j, k: (k, i))
        b_spec = pl.BlockSpec((tk, tn), lambda i, j, k: (k, j))
    add_spec = pl.BlockSpec((tm, tn), lambda i, j, k: (i, j))
    if out_dev:
        o_spec = pl.BlockSpec((None, tm, tn), lambda i, j, k: (j // nb, i, j % nb))
        o_shape = (N_DEV, M, N // N_DEV)
    else:
        o_spec, o_shape = add_spec, (M, N)
    in_specs = [a_spec, b_spec] + ([add_spec] if add is not None else [])
    ops = (a, b) + ((add,) if add is not None else ())
    sem = ("arbitrary",) * 3 if side.n else ("parallel", "parallel", "arbitrary")
    res = pl.pallas_call(
        body, name=name, grid=grid, in_specs=in_specs + side.in_specs, out_specs=[o_spec] + side.out_specs,
        out_shape=[jax.ShapeDtypeStruct(o_shape, out_dtype)] + side.out_shapes,
        scratch_shapes=[pltpu.VMEM((tm, tn), F32)] + side.scratch,
        compiler_params=_cparams(sem))(*ops, *side.srcs)
    return tuple(res) if side.n else res[0]


class _Rows2D:
    def __init__(self, n_rows, tm):
        self.n, self.tm = n_rows, tm
        self.grid = (1, n_rows // tm)

    def row(self, e):
        if isinstance(e, tuple):
            arr, width, cb = e
            return arr, (self.tm, width), pl.BlockSpec((self.tm, width), lambda g, i, cb=cb: (i, cb))
        return e, (self.tm, e.shape[1]), pl.BlockSpec((self.tm, e.shape[1]), lambda g, i: (i, 0))

    def par(self, p):
        return pl.BlockSpec(p.shape, lambda g, i: (0,) * p.ndim)

    def out(self, blk):
        return (self.n, blk[1]), pl.BlockSpec((self.tm, blk[1]), lambda g, i: (i, 0))


class _RowsHeads:
    def __init__(self, n_heads, n_rows, hb, ts):
        self.h, self.n, self.hb, self.ts = n_heads, n_rows, hb, ts
        self.grid = (n_heads // hb, n_rows // ts)

    def row(self, e):
        blk = (self.hb, self.ts, e.shape[2])
        return e, blk, pl.BlockSpec(blk, lambda g, i: (g, i, 0))

    def par(self, p):
        return pl.BlockSpec((self.hb, 1, p.shape[2]), lambda g, i: (g, 0, 0))

    def out(self, blk):
        return (self.h, self.n, blk[2]), pl.BlockSpec(blk, lambda g, i: (g, i, 0))


def _par_block(lay, p):
    return lay.par(p).block_shape


def ew_fwd(lay, fn, rows, params, name):
    rr = [lay.row(e) for e in rows]
    arrs = [r[0] for r in rr]
    blk_avals = [jax.ShapeDtypeStruct(r[1], r[0].dtype) for r in rr]
    par_avals = [jax.ShapeDtypeStruct(_par_block(lay, p), p.dtype) for p in params]
    outs = jax.eval_shape(fn, *blk_avals, *par_avals)
    out_full = [lay.out(o.shape) for o in outs]
    nr, npar = len(rows), len(params)

    def body(*refs):
        vals = [r[...] for r in refs[:nr + npar]]
        res = fn(*vals)
        for ref, v in zip(refs[nr + npar:], res):
            ref[...] = v.astype(ref.dtype)

    return pl.pallas_call(
        body, name=name, grid=lay.grid,
        in_specs=[r[2] for r in rr] + [lay.par(p) for p in params],
        out_specs=[o[1] for o in out_full],
        out_shape=[jax.ShapeDtypeStruct(o[0], a.dtype) for o, a in zip(out_full, outs)],
        compiler_params=_cparams(("parallel", "parallel")))(*arrs, *params)


def ew_bwd(lay, fn, rows, params, cots, wrt_rows, wrt_pars, name, dr_dtypes=None):
    rr = [lay.row(e) for e in rows]
    cc = [lay.row(e) for e in cots]
    nr, npar, nc = len(rows), len(params), len(cots)
    n_dr = len(wrt_rows)
    dr_full = [lay.out(rr[i][1]) for i in wrt_rows]
    dr_dtypes = dr_dtypes or [F32] * n_dr

    def body(*refs):
        rv = [r[...] for r in refs[:nr]]
        pv = [r[...] for r in refs[nr:nr + npar]]
        cv = tuple(r[...] for r in refs[nr + npar:nr + npar + nc])
        outs = refs[nr + npar + nc:]

        def f(*wrt):
            r2, p2 = list(rv), list(pv)
            for idx, v in zip(wrt_rows, wrt[:n_dr]):
                r2[idx] = v
            for idx, v in zip(wrt_pars, wrt[n_dr:]):
                p2[idx] = v
            return fn(*r2, *p2)

        _, vjp = jax.vjp(f, *[rv[i] for i in wrt_rows], *[pv[i] for i in wrt_pars])
        g = vjp(cv)
        for ref, v in zip(outs[:n_dr], g[:n_dr]):
            ref[...] = v.astype(ref.dtype)
        if wrt_pars:
            @pl.when(pl.program_id(1) == 0)
            def _():
                for ref in outs[n_dr:]:
                    ref[...] = jnp.zeros_like(ref)

            for ref, v in zip(outs[n_dr:], g[n_dr:]):
                ref[...] += v

    return pl.pallas_call(
        body, name=name, grid=lay.grid,
        in_specs=[r[2] for r in rr] + [lay.par(p) for p in params] + [c[2] for c in cc],
        out_specs=[o[1] for o in dr_full] + [lay.par(params[i]) for i in wrt_pars],
        out_shape=[jax.ShapeDtypeStruct(o[0], dt) for o, dt in zip(dr_full, dr_dtypes)]
        + [jax.ShapeDtypeStruct(params[i].shape, F32) for i in wrt_pars],
        compiler_params=_cparams(("parallel", "arbitrary")))(
            *[r[0] for r in rr], *params, *[c[0] for c in cc])


def _sigmoid(x):
    return 1.0 / (1.0 + jnp.exp(-x))


def _softplus(x):
    return jnp.maximum(x, 0.0) + jnp.log(1.0 + jnp.exp(-jnp.abs(x)))


def _layer_norm(x, g, b, eps):
    mu = jnp.mean(x, -1, keepdims=True)
    xc = x - mu
    var = jnp.mean(xc * xc, -1, keepdims=True)
    return xc * lax.rsqrt(var + eps) * g + b


def _as_bf16(fn):
    return lambda *args: tuple(o.astype(BF16) for o in fn(*args))


def fn_ln(x, g, b):
    return (_layer_norm(x, g, b, LN_EPS),)


def fn_ln_res(h, t, g, b):
    return (_layer_norm(ALPHA * h + t, g, b, LN_EPS),)


def fn_ln_res_both(h, t, g, b):
    y, = fn_ln_res(h, t, g, b)
    return y, y.astype(BF16)


def make_fn_lora(r_decay, r_iclr, r_gate):
    def fn(z):
        lane = lax.broadcasted_iota(jnp.int32, z.shape, 1)
        out = jnp.where(lane < r_decay, jnp.tanh(z), z)
        out = jnp.where(lane >= r_decay + r_iclr, _sigmoid(z), out)
        return (jnp.where(lane < r_decay + r_iclr + r_gate, out, 0.0),)
    return fn


def fn_rwkv_pre(k, wl, al, w0, a0, k_k, k_a):
    w = -_softplus(-(w0 + wl)) - 0.5
    lw = -jnp.exp(w)
    a = _sigmoid(a0 + al)
    kk = k * k_k
    kk = kk / jnp.maximum(jnp.sqrt(jnp.sum(kk * kk, -1, keepdims=True)), 1e-12)
    k2 = k * (1.0 + (a - 1.0) * k_a)
    return lw, k2, -kk, kk * a


def fn_rwkv_post(o, r, k2, v, g, gn_g, gn_b, r_k):
    mu = jnp.mean(o, -1, keepdims=True)
    oc = o - mu
    var = jnp.mean(oc * oc, -1, keepdims=True)
    y = oc * lax.rsqrt(var + GN_EPS) * gn_g + gn_b
    y = y + jnp.sum(r * k2 * r_k, -1, keepdims=True) * v
    return (y * g,)


def fn_glu(zu, zg):
    return (zu * _sigmoid(zg),)


def fn_ln_silu(y, g, b):
    n = _layer_norm(y, g, b, LN_EPS)
    return (n * _sigmoid(n),)


def fn_merge(zgr, zgc, pr, pc):
    return (_sigmoid(zgr) * pr + _sigmoid(zgc) * pc,)


def fn_relu2(f):
    r = jnp.maximum(f, 0.0)
    return (r * r,)


def _dot_nn(a, b):
    return lax.dot_general(a.astype(BF16), b.astype(BF16), (((1,), (0,)), ((), ())), preferred_element_type=F32)


def _dot_nt(a, b):
    return lax.dot_general(a.astype(BF16), b.astype(BF16), (((1,), (1,)), ((), ())), preferred_element_type=F32)


def _dot_tn(a, b):
    return lax.dot_general(a.astype(BF16), b.astype(BF16), (((0,), (0,)), ((), ())), preferred_element_type=F32)


def _softmax_rows(s):
    s = s - jnp.max(s, -1, keepdims=True)
    e = jnp.exp(s)
    return e / jnp.sum(e, -1, keepdims=True)


def attn_fwd(q, kx, vx, tm, name):
    S, D = q.shape
    M = kx.shape[0]
    dh = D // XATTN_HEADS
    scale = dh ** -0.5

    def body(q_ref, k_ref, v_ref, o_ref):
        p = _softmax_rows(_dot_nt(q_ref[...], k_ref[...]) * scale)
        o_ref[...] = _dot_nn(p, v_ref[...]).astype(BF16)

    row = pl.BlockSpec((tm, dh), lambda h, i: (i, h))
    kv = pl.BlockSpec((M, dh), lambda h, i: (0, h))
    return pl.pallas_call(body, name=name, grid=(XATTN_HEADS, S // tm), in_specs=[row, kv, kv], out_specs=row,
                          out_shape=jax.ShapeDtypeStruct((S, D), BF16),
                          compiler_params=_cparams(("parallel", "parallel")))(q, kx, vx)


def attn_bwd(q, kx, vx, do, tm, name):
    S, D = q.shape
    M = kx.shape[0]
    dh = D // XATTN_HEADS
    scale = dh ** -0.5

    def body(q_ref, k_ref, v_ref, do_ref, dq_ref, dk_ref, dv_ref):
        qb, kb, dob = q_ref[...], k_ref[...], do_ref[...]
        p = _softmax_rows(_dot_nt(qb, kb) * scale)
        dp = _dot_nt(dob, v_ref[...])
        ds = p * (dp - jnp.sum(dp * p, -1, keepdims=True)) * scale
        dq_ref[...] = _dot_nn(ds, kb).astype(BF16)

        @pl.when(pl.program_id(1) == 0)
        def _():
            dk_ref[...] = jnp.zeros_like(dk_ref)
            dv_ref[...] = jnp.zeros_like(dv_ref)

        dk_ref[...] += _dot_tn(ds, qb)
        dv_ref[...] += _dot_tn(p, dob)

    row = pl.BlockSpec((tm, dh), lambda h, i: (i, h))
    kv = pl.BlockSpec((M, dh), lambda h, i: (0, h))
    return pl.pallas_call(
        body, name=name, grid=(XATTN_HEADS, S // tm), in_specs=[row, kv, kv, row], out_specs=[row, kv, kv],
        out_shape=[jax.ShapeDtypeStruct((S, D), BF16), jax.ShapeDtypeStruct((M, D), F32), jax.ShapeDtypeStruct((M, D), F32)],
        compiler_params=_cparams(("parallel", "arbitrary")))(q, kx, vx, do)


def _shift_down(blk, halo_last_row, first_block):
    rolled = pltpu.roll(blk, 1, 0)
    row = lax.broadcasted_iota(jnp.int32, blk.shape, 0)
    top = jnp.where(first_block, 0.0, halo_last_row)
    return jnp.where(row == 0, top, rolled)


def _shift_up(blk, halo_first_row, last_block):
    n = blk.shape[0]
    rolled = pltpu.roll(blk, n - 1, 0)
    row = lax.broadcasted_iota(jnp.int32, blk.shape, 0)
    bot = jnp.where(last_block, 0.0, halo_first_row)
    return jnp.where(row == n - 1, bot, rolled)


def _zcol(j, cols):
    n_first, first, second = cols
    return jnp.where(j < n_first, first + j, second + j - n_first)


def token_shift_fwd(z, cols, cw, width, mu, tm, name):
    S = z.shape[0]
    hb = tm // SUBLANES

    def body(z_ref, halo_ref, mu_ref, o_ref):
        zb = z_ref[...]
        prev = _shift_down(zb, halo_ref[SUBLANES - 1:SUBLANES, :], pl.program_id(1) == 0)
        o_ref[...] = zb + (prev - zb) * mu_ref[...]

    return pl.pallas_call(
        body, name=name, grid=(width // cw, S // tm),
        in_specs=[pl.BlockSpec((tm, cw), lambda j, i: (i, _zcol(j, cols))),
                  pl.BlockSpec((SUBLANES, cw), lambda j, i: (jnp.maximum(i * hb - 1, 0), _zcol(j, cols))),
                  pl.BlockSpec((1, cw), lambda j, i: (0, j))],
        out_specs=pl.BlockSpec((tm, cw), lambda j, i: (i, j)),
        out_shape=jax.ShapeDtypeStruct((S, width), F32),
        compiler_params=_cparams(("parallel", "parallel")))(z, z, mu)


def token_shift_bwd(z, dzs, cols, cw, width, mu, tm, name):
    S = z.shape[0]
    hb = tm // SUBLANES
    nblk = S // tm
    last8 = S // SUBLANES - 1

    def body(z_ref, zh_ref, d_ref, dh_ref, mu_ref, dz_ref, dmu_ref):
        i = pl.program_id(1)
        zb, db, m = z_ref[...], d_ref[...], mu_ref[...]
        prev = _shift_down(zb, zh_ref[SUBLANES - 1:SUBLANES, :], i == 0)
        dm = db * m
        nxt = _shift_up(dm, dh_ref[0:1, :] * m, i == nblk - 1)
        dz_ref[...] = db - dm + nxt

        @pl.when(i == 0)
        def _():
            dmu_ref[...] = jnp.zeros_like(dmu_ref)

        dmu_ref[...] += jnp.sum(db * (prev - zb), 0, keepdims=True)

    return pl.pallas_call(
        body, name=name, grid=(width // cw, nblk),
        in_specs=[pl.BlockSpec((tm, cw), lambda j, i: (i, _zcol(j, cols))),
                  pl.BlockSpec((SUBLANES, cw), lambda j, i: (jnp.maximum(i * hb - 1, 0), _zcol(j, cols))),
                  pl.BlockSpec((tm, cw), lambda j, i: (i, j)),
                  pl.BlockSpec((SUBLANES, cw), lambda j, i: (jnp.minimum((i + 1) * hb, last8), j)),
                  pl.BlockSpec((1, cw), lambda j, i: (0, j))],
        out_specs=[pl.BlockSpec((tm, cw), lambda j, i: (i, j)), pl.BlockSpec((1, cw), lambda j, i: (0, j))],
        out_shape=[jax.ShapeDtypeStruct((S, width), F32), jax.ShapeDtypeStruct((1, width), F32)],
        compiler_params=_cparams(("parallel", "arbitrary")))(z, z, dzs, dzs, mu)


def conv_fwd(u, w, b, width, tr, cb, name):
    S, C = u.shape
    hb = tr // CONV_HALO

    def body(u_ref, h_ref, w_ref, b_ref, y_ref):
        i = pl.program_id(1)
        halo = jnp.where(i == 0, 0.0, h_ref[...])
        win = jnp.concatenate([halo, u_ref[...]], axis=0)
        acc = jnp.zeros((tr, cb), F32) + b_ref[...]
        for d in range(width):
            sh = win if d == 0 else pltpu.roll(win, d, 0)
            acc = acc + sh[CONV_HALO:, :] * w_ref[width - 1 - d:width - d, :]
        y_ref[...] = acc

    return pl.pallas_call(
        body, name=name, grid=(C // cb, S // tr),
        in_specs=[pl.BlockSpec((tr, cb), lambda j, i: (i, j)),
                  pl.BlockSpec((CONV_HALO, cb), lambda j, i: (jnp.maximum(i * hb - 1, 0), j)),
                  pl.BlockSpec((CONV_HALO, cb), lambda j, i: (0, j)),
                  pl.BlockSpec((1, cb), lambda j, i: (0, j))],
        out_specs=pl.BlockSpec((tr, cb), lambda j, i: (i, j)),
        out_shape=jax.ShapeDtypeStruct((S, C), F32), compiler_params=_cparams(("parallel", "parallel")))(u, u, w, b)


def conv_bwd(u, dy, w, width, tr, cb, name):
    S, C = u.shape
    hb = tr // CONV_HALO
    nblk = S // tr
    last = S // CONV_HALO - 1

    def body(u_ref, uh_ref, d_ref, dh_ref, w_ref, du_ref, dw_ref, db_ref):
        i = pl.program_id(1)
        dyb = d_ref[...]
        uwin = jnp.concatenate([jnp.where(i == 0, 0.0, uh_ref[...]), u_ref[...]], axis=0)
        dwin = jnp.concatenate([dyb, jnp.where(i == nblk - 1, 0.0, dh_ref[...])], axis=0)

        @pl.when(i == 0)
        def _():
            dw_ref[...] = jnp.zeros_like(dw_ref)
            db_ref[...] = jnp.zeros_like(db_ref)

        acc = jnp.zeros((tr, cb), F32)
        for d in range(width):
            tap = width - 1 - d
            dsh = dwin if d == 0 else pltpu.roll(dwin, tr + CONV_HALO - d, 0)
            acc = acc + dsh[:tr, :] * w_ref[tap:tap + 1, :]
            ush = uwin if d == 0 else pltpu.roll(uwin, d, 0)
            dw_ref[tap:tap + 1, :] += jnp.sum(ush[CONV_HALO:, :] * dyb, 0, keepdims=True)
        du_ref[...] = acc
        db_ref[...] += jnp.sum(dyb, 0, keepdims=True)

    return pl.pallas_call(
        body, name=name, grid=(C // cb, nblk),
        in_specs=[pl.BlockSpec((tr, cb), lambda j, i: (i, j)),
                  pl.BlockSpec((CONV_HALO, cb), lambda j, i: (jnp.maximum(i * hb - 1, 0), j)),
                  pl.BlockSpec((tr, cb), lambda j, i: (i, j)),
                  pl.BlockSpec((CONV_HALO, cb), lambda j, i: (jnp.minimum((i + 1) * hb, last), j)),
                  pl.BlockSpec((CONV_HALO, cb), lambda j, i: (0, j))],
        out_specs=[pl.BlockSpec((tr, cb), lambda j, i: (i, j)),
                   pl.BlockSpec((CONV_HALO, cb), lambda j, i: (0, j)),
                   pl.BlockSpec((1, cb), lambda j, i: (0, j))],
        out_shape=[jax.ShapeDtypeStruct((S, C), F32), jax.ShapeDtypeStruct((CONV_HALO, C), F32),
                   jax.ShapeDtypeStruct((1, C), F32)],
        compiler_params=_cparams(("parallel", "arbitrary")))(u, u, dy, dy, w)


def _split2(x):
    hi = x.astype(BF16)
    return hi, (x - hi.astype(F32)).astype(BF16)


def _dot3(a, b, dims):
    ah, al = _split2(a)
    bh, bl = _split2(b)
    d = lambda p, q: lax.dot_general(p, q, dims, preferred_element_type=F32)
    return d(ah, bh) + (d(ah, bl) + d(al, bh))


@jax.custom_vjp
def _bnn(a, b):
    return _dot3(a, b, (((2,), (1,)), ((0,), (0,))))


@jax.custom_vjp
def _bnt(a, b):
    return _dot3(a, b, (((2,), (2,)), ((0,), (0,))))


@jax.custom_vjp
def _btn(a, b):
    return _dot3(a, b, (((1,), (1,)), ((0,), (0,))))


def _tri_sum(x, lower):
    h, c, _ = x.shape
    ti = lax.broadcasted_iota(jnp.int32, (h, c, c), 1)
    si = lax.broadcasted_iota(jnp.int32, (h, c, c), 2)
    m = (si <= ti if lower else si >= ti).astype(BF16)
    x1 = x.astype(BF16)
    r1 = x - x1.astype(F32)
    x2 = r1.astype(BF16)
    x3 = (r1 - x2.astype(F32)).astype(BF16)
    d = lambda q: lax.dot_general(m, q, (((2,), (1,)), ((0,), (0,))), preferred_element_type=F32)
    return d(x1) + (d(x2) + d(x3))


@jax.custom_vjp
def _cumsum_rows(x):
    return _tri_sum(x, True)


@jax.custom_vjp
def _rev_cumsum_rows(x):
    return _tri_sum(x, False)


_cumsum_rows.defvjp(lambda x: (_cumsum_rows(x), None), lambda _, g: (_rev_cumsum_rows(g),))
_rev_cumsum_rows.defvjp(lambda x: (_rev_cumsum_rows(x), None), lambda _, g: (_cumsum_rows(g),))


_bnn.defvjp(lambda a, b: (_bnn(a, b), (a, b)), lambda res, g: (_bnt(g, res[1]), _btn(res[0], g)))
_bnt.defvjp(lambda a, b: (_bnt(a, b), (a, b)), lambda res, g: (_bnn(g, res[1]), _btn(g, res[0])))
_btn.defvjp(lambda a, b: (_btn(a, b), (a, b)), lambda res, g: (_bnt(res[1], g), _bnn(res[0], g)))


def scan_chunk(st0, r, lw, k, v, a, b):
    h, c, n = r.shape
    ti = lax.broadcasted_iota(jnp.int32, (h, c, c), 1)
    si = lax.broadcasted_iota(jnp.int32, (h, c, c), 2)
    incl = si <= ti
    strict = si < ti
    cum = _cumsum_rows(lw)
    p = jnp.exp(cum)
    pinv = jnp.exp(-cum)
    at = a * jnp.exp(cum - lw)
    bt, kt, rt = b * pinv, k * pinv, r * p
    a_ab = jnp.where(strict, _bnt(at, bt), 0.0)
    a_ak = jnp.where(strict, _bnt(at, kt), 0.0)
    m_rb = jnp.where(incl, _bnt(rt, bt), 0.0)
    m_rk = jnp.where(incl, _bnt(rt, kt), 0.0)
    tinv = (si == ti).astype(F32) + a_ab
    pw = a_ab
    for _ in range(int(math.log2(c)) - 1):
        pw = _bnn(pw, pw)
        tinv = tinv + _bnn(tinv, pw)
    u = _bnn(tinv, _bnn(at, st0) + _bnn(a_ak, v))
    o = _bnn(rt, st0) + _bnn(m_rb, u) + _bnn(m_rk, v)
    cum_c = jnp.sum(lw, axis=1, keepdims=True)
    tail = jnp.exp(cum_c - cum)
    ki = lax.broadcasted_iota(jnp.int32, (h, n, n), 1)
    kj = lax.broadcasted_iota(jnp.int32, (h, n, n), 2)
    pc_col = jnp.sum(jnp.where(ki == kj, jnp.exp(cum_c), 0.0), axis=2, keepdims=True)
    st = st0 * pc_col + _btn(b * tail, u) + _btn(k * tail, v)
    return o, st


def _grid_ends(grid):
    g, c = pl.program_id(0), pl.program_id(1)
    return (g == 0) & (c == 0), (g == grid[0] - 1) & (c == grid[1] - 1)


def scan_fwd(r, lw, k, v, a, b, hb, name, side=None):
    H, S, N = r.shape
    C = SCAN_CHUNK
    nc = S // C
    side = _Side(side)
    grid = (H // hb, nc)

    def body(*refs):
        r_ref, lw_ref, k_ref, v_ref, a_ref, b_ref = refs[:6]
        side_src = refs[6:6 + side.n]
        o_ref, ck_ref = refs[6 + side.n:8 + side.n]
        side_out = refs[8 + side.n:8 + 2 * side.n]
        st_ref = refs[8 + 2 * side.n]
        finish = side.run(*_grid_ends(grid), side_src, side_out, refs[9 + 2 * side.n:])

        @pl.when(pl.program_id(1) == 0)
        def _():
            st_ref[...] = jnp.zeros_like(st_ref)

        st0 = st_ref[...]
        ck_ref[...] = st0[:, None]
        o, st = scan_chunk(st0, r_ref[...], lw_ref[...], k_ref[...], v_ref[...], a_ref[...], b_ref[...])
        o_ref[...] = o
        st_ref[...] = st
        finish()

    seq = pl.BlockSpec((hb, C, N), lambda g, c: (g, c, 0))
    return pl.pallas_call(
        body, name=name, grid=grid, in_specs=[seq] * 6 + side.in_specs,
        out_specs=[seq, pl.BlockSpec((hb, 1, N, N), lambda g, c: (g, c, 0, 0))] + side.out_specs,
        out_shape=[jax.ShapeDtypeStruct((H, S, N), F32), jax.ShapeDtypeStruct((H, nc, N, N), F32)] + side.out_shapes,
        scratch_shapes=[pltpu.VMEM((hb, N, N), F32)] + side.scratch,
        compiler_params=_cparams(("arbitrary", "arbitrary")))(r, lw, k, v, a, b, *side.srcs)


def scan_bwd(r, lw, k, v, a, b, ck, do, dr_add, dk_add, dv_add, hb, name, side=None):
    H, S, N = r.shape
    C = SCAN_CHUNK
    nc = S // C
    side = _Side(side)
    grid = (H // hb, nc)

    def body(*refs):
        r_ref, lw_ref, k_ref, v_ref, a_ref, b_ref, ck_ref, do_ref, ra_ref, ka_ref, va_ref = refs[:11]
        side_src = refs[11:11 + side.n]
        dr_ref, dlw_ref, dk_ref, dv_ref, da_ref, db_ref = refs[11 + side.n:17 + side.n]
        side_out = refs[17 + side.n:17 + 2 * side.n]
        dst_ref = refs[17 + 2 * side.n]
        finish = side.run(*_grid_ends(grid), side_src, side_out, refs[18 + 2 * side.n:])

        @pl.when(pl.program_id(1) == 0)
        def _():
            dst_ref[...] = jnp.zeros_like(dst_ref)

        st0 = ck_ref[...][:, 0]
        _, vjp = jax.vjp(scan_chunk, st0, r_ref[...], lw_ref[...], k_ref[...], v_ref[...], a_ref[...], b_ref[...])
        dst0, dr, dlw, dk, dv, da, db = vjp((do_ref[...], dst_ref[...]))
        dr_ref[...], dlw_ref[...], dk_ref[...] = dr + ra_ref[...], dlw, dk + ka_ref[...]
        dv_ref[...], da_ref[...], db_ref[...] = dv + va_ref[...], da, db
        dst_ref[...] = dst0
        finish()

    seq = pl.BlockSpec((hb, C, N), lambda g, c: (g, nc - 1 - c, 0))
    return pl.pallas_call(
        body, name=name, grid=grid,
        in_specs=[seq] * 6 + [pl.BlockSpec((hb, 1, N, N), lambda g, c: (g, nc - 1 - c, 0, 0))] + [seq] * 4 + side.in_specs,
        out_specs=[seq] * 6 + side.out_specs, out_shape=[jax.ShapeDtypeStruct((H, S, N), F32)] * 6 + side.out_shapes,
        scratch_shapes=[pltpu.VMEM((hb, N, N), F32)] + side.scratch,
        compiler_params=_cparams(("arbitrary", "arbitrary")))(r, lw, k, v, a, b, ck, do, dr_add, dk_add, dv_add,
                                                              *side.srcs)


def loss_head(y, target, tm, name):
    S, D = y.shape

    def body(y_ref, t_ref, dy_ref, l_ref):
        err = y_ref[...] - t_ref[...]
        dy_ref[...] = err * (1.0 / D)

        @pl.when(pl.program_id(0) == 0)
        def _():
            l_ref[...] = jnp.zeros_like(l_ref)

        l_ref[...] += 0.5 * jnp.sum(jnp.mean(err * err, -1, keepdims=True), 0, keepdims=True)

    row = pl.BlockSpec((tm, D), lambda i: (i, 0))
    return pl.pallas_call(
        body, name=name, grid=(S // tm,), in_specs=[row, row],
        out_specs=[row, pl.BlockSpec((SUBLANES, LANES), lambda i: (0, 0))],
        out_shape=[jax.ShapeDtypeStruct((S, D), F32), jax.ShapeDtypeStruct((SUBLANES, LANES), F32)],
        compiler_params=_cparams(("arbitrary",)))(y, target)


def adamw(parts, w, m, v, name):
    R, C = w.shape
    n_parts = parts.shape[0]
    tr = R
    if R * C > ADAM_BLOCK_ELEMS:
        tr = _tile(R, [t for t in (512, 256, 128, 64, 32, 16) if t * C <= ADAM_BLOCK_ELEMS])
    c1 = 1.0 / (1.0 - ADAM_B1 ** ADAM_STEP)
    c2 = 1.0 / (1.0 - ADAM_B2 ** ADAM_STEP)

    def body(p_ref, w_ref, m_ref, v_ref, g_ref, d_ref, nm_ref, nv_ref):
        g = p_ref[0].astype(F32)
        for j in range(1, n_parts):
            g = g + p_ref[j].astype(F32)
        nm = ADAM_B1 * m_ref[...] + (1.0 - ADAM_B1) * g
        nv = ADAM_B2 * v_ref[...] + (1.0 - ADAM_B2) * (g * g)
        g_ref[...] = g
        nm_ref[...] = nm
        nv_ref[...] = nv
        d_ref[...] = -ADAM_LR * ((nm * c1) / (jnp.sqrt(nv * c2) + ADAM_EPS) + ADAM_WD * w_ref[...])

    blk = pl.BlockSpec((tr, C), lambda i: (i, 0))
    return pl.pallas_call(
        body, name=name, grid=(R // tr,), in_specs=[pl.BlockSpec((n_parts, tr, C), lambda i: (0, i, 0)), blk, blk, blk],
        out_specs=[blk] * 4, out_shape=[jax.ShapeDtypeStruct((R, C), F32)] * 4,
        compiler_params=_cparams(("parallel",)))(parts, w, m, v)


def _to_heads(a2d):
    s, d = a2d.shape
    return a2d.reshape(s, d // RWKV_HEAD, RWKV_HEAD).transpose(1, 0, 2)


def _from_heads(a3d):
    h, s, n = a3d.shape
    return a3d.transpose(1, 0, 2).reshape(s, h * n)


def _cols_joined(g):
    return g.transpose(1, 0, 2).reshape(g.shape[1], N_DEV * g.shape[2])


def _cols_split(w):
    return w.reshape(w.shape[0], N_DEV, w.shape[1] // N_DEV).transpose(1, 0, 2)


def _round_up(n, m):
    return -(-n // m) * m


def _step(a):
    x, mem, target = a['x'][0], a['mem'][0], a['loss_target'][0]
    S, D = x.shape
    M = mem.shape[0]
    DR = a['rwkv_w0'].shape[1]
    H = DR // RWKV_HEAD
    DC = a['conv_b'].shape[1]
    r_decay, r_iclr, r_gate = a['rwkv_w_up'].shape[1], a['rwkv_a_up'].shape[1], a['rwkv_g_up'].shape[1]
    n_lora = r_decay + r_iclr + r_gate
    lora_w = _round_up(n_lora, LANES)
    n_rwkv = 3 * DR + n_lora
    zr_w = 3 * DR + lora_w
    pad_cols = zr_w - n_rwkv
    conv_taps = a['conv_w'].shape[1]
    assert conv_taps - 1 <= CONV_HALO and S % SCAN_CHUNK == 0

    def shard(n, pre=''):
        w = a[pre + n]
        if n == 'conv_w':
            return jnp.pad(w.reshape(conv_taps, w.shape[-1]), ((0, CONV_HALO - conv_taps), (0, 0)))
        return w.reshape(w.shape[1:])

    Wg, W = {}, {}

    def gathers(names):
        return [(shard(n).astype(BF16), 'gather2') for n in names]

    def gathered(names, results):
        for n, g in zip(names, results):
            Wg[n] = g
            if n in ROW_SHARDED:
                W[n] = g.reshape(N_DEV * g.shape[1], g.shape[2])

    with_mm_z = ['proj_rwkv', 'proj_conv', 'w_out', 'xattn_wq']
    with_scan_fwd = ['xattn_wk', 'xattn_wv', 'xattn_wo', 'mlp_w1']
    with_mm_w1 = ['mlp_w2']
    with_scan_bwd = ['mlp_w2', 'mlp_w1', 'xattn_wo', 'xattn_wq', 'xattn_wk', 'xattn_wv', 'w_out', 'proj_rwkv', 'proj_conv']
    gathered(['w_in'], [exchange(shard('w_in').astype(BF16), 'gather2', "gather_w_in")])
    lora_src = jnp.concatenate([shard('rwkv_w_up'), shard('rwkv_a_up'), shard('rwkv_g_up')], axis=0).astype(BF16)
    lora_all = exchange(lora_src, 'gather', "gather_lora")
    conv_w = _cols_joined(exchange(shard('conv_w'), 'gather', "gather_conv_w"))

    w_in = _cols_joined(Wg['w_in'])
    o_conv, o_gate = n_rwkv, n_rwkv + 2 * DC
    w_in_p = jnp.concatenate([w_in[:, o_gate:], w_in[:, :3 * DR], w_in[:, o_conv:o_gate], w_in[:, 3 * DR:n_rwkv],
                              jnp.zeros((D, pad_cols), BF16)], axis=1)
    c_gr, c_gc, c_rkv, c_u = 0, D, 2 * D, 2 * D + 3 * DR
    c_cg, c_lora = c_u + DC, c_u + 2 * DC
    assert c_rkv % lora_w == 0 and c_lora % lora_w == 0 and (3 * DR) % lora_w == 0 and c_u % DC == 0 and c_rkv % DR == 0
    shift_cols = (3 * DR // lora_w, c_rkv // lora_w, c_lora // lora_w)
    lora_full = _cols_joined(lora_all)
    w_lora = jnp.zeros((lora_w, 3 * DR), BF16)
    w_lora = w_lora.at[:r_decay, :DR].set(lora_full[:r_decay])
    w_lora = w_lora.at[r_decay:r_decay + r_iclr, DR:2 * DR].set(lora_full[r_decay:r_decay + r_iclr])
    w_lora = w_lora.at[r_decay + r_iclr:n_lora, 2 * DR:].set(lora_full[r_decay + r_iclr:])
    mu_p = jnp.pad(a['rwkv_shift_mix'], ((0, 0), (0, pad_cols)))
    x_bf = x.astype(BF16)

    hp = lambda n: a[n].reshape(H, 1, RWKV_HEAD)
    w0_h, a0_h, kk_h, ka_h, gng_h, gnb_h = (hp(n) for n in ('rwkv_w0', 'rwkv_a0', 'rwkv_k_k', 'rwkv_k_a',
                                                             'rwkv_gn_g', 'rwkv_gn_b'))
    rk_h = a['rwkv_r_k'].reshape(H, 1, RWKV_HEAD)
    ln_mem_g, ln_mem_b = a['ln_mem_g'].reshape(1, D), a['ln_mem_b'].reshape(1, D)

    tm_d = _tile(S, (64, 32, 16, 8))
    tm_a = _tile(S, (512, 256, 128, 64))
    rows = _Rows2D(S, tm_d)
    rows_mem = _Rows2D(M, _tile(M, (64, 32, 16, 8)))
    rows_ff = _Rows2D(S, _tile(S, (32, 16, 8)))
    hb = _tile(H, (8, 4, 2, 1))
    heads = _RowsHeads(H, S, _tile(H, (4, 2, 1)), _tile(S, (256, 128, 64)))
    tr_conv = _tile(S, (512, 256, 128, 64, 32))
    cb_conv = _tile(DC, (256, 128))
    fn_lora = make_fn_lora(r_decay, r_iclr, r_gate)
    lane_blk = lambda off, width: off // width

    mem_n, = ew_fwd(rows_mem, _as_bf16(fn_ln), [mem], [ln_mem_g, ln_mem_b], "ln_mem")
    z, *got = matmul(x_bf, w_in_p, 'nn', "mm_z", side=gathers(with_mm_z))
    gathered(with_mm_z, got)
    zs = token_shift_fwd(z, shift_cols, lora_w, zr_w, mu_p, tm_a, "token_shift")
    lora, = ew_fwd(rows, _as_bf16(fn_lora), [(zs, lora_w, lane_blk(3 * DR, lora_w))], [], "lora_act")
    up = matmul(lora, w_lora, 'nn', "mm_lora_up")
    r_h, k_h, v_h = (_to_heads(zs[:, i * DR:(i + 1) * DR]) for i in range(3))
    wl_h, al_h, g_h = (_to_heads(up[:, i * DR:(i + 1) * DR]) for i in range(3))
    pre_rows, pre_pars = [k_h, wl_h, al_h], [w0_h, a0_h, kk_h, ka_h]
    lw_h, k2_h, na_h, b_h = ew_fwd(heads, fn_rwkv_pre, pre_rows, pre_pars, "rwkv_pre")
    o_h, ckpt, *got = scan_fwd(r_h, lw_h, k2_h, v_h, na_h, b_h, hb, "scan_fwd", side=gathers(with_scan_fwd))
    gathered(with_scan_fwd, got)
    post_rows, post_pars = [o_h, r_h, k2_h, v_h, g_h], [gng_h, gnb_h, rk_h]
    or_h, = ew_fwd(heads, _as_bf16(fn_rwkv_post), post_rows, post_pars, "rwkv_post")
    o_r = _from_heads(or_h)

    glu_rows = [(z, DC, lane_blk(c_u, DC)), (z, DC, lane_blk(c_cg, DC))]
    u, = ew_fwd(rows, fn_glu, glu_rows, [], "glu")
    yc = conv_fwd(u, conv_w, a['conv_b'], conv_taps, tr_conv, cb_conv, "conv")
    cln = [a['conv_ln_g'], a['conv_ln_b']]
    o_c, = ew_fwd(rows, _as_bf16(fn_ln_silu), [yc], cln, "conv_ln_silu")

    G = {}
    rows_split = lambda g: g.reshape(N_DEV, g.shape[0] // N_DEV, g.shape[1])

    def chip_partials(n):
        return pair_add(G[n], exchange(G[n], 'scatter_sib', "scatter_sib_" + n), "pair_add_" + n)
    pr = matmul(o_r, Wg['proj_rwkv'], 'nn', "mm_proj_rwkv", b_dev=True)
    pc = matmul(o_c, Wg['proj_conv'], 'nn', "mm_proj_conv", b_dev=True)
    merge_rows = [(z, D, lane_blk(c_gr, D)), (z, D, lane_blk(c_gc, D)), pr, pc]
    merged, = ew_fwd(rows, _as_bf16(fn_merge), merge_rows, [], "merge")
    t1 = matmul(merged, W['w_out'], 'nn', "mm_w_out")
    ln1 = [a['ln1_g'], a['ln1_b']]
    h1, h1_bf = ew_fwd(rows, fn_ln_res_both, [x, t1], ln1, "ln1")

    q = matmul(h1_bf, W['xattn_wq'], 'nn', "mm_q")
    kx = matmul(mem_n, W['xattn_wk'], 'nn', "mm_k")
    vx = matmul(mem_n, W['xattn_wv'], 'nn', "mm_v")
    oa = attn_fwd(q, kx, vx, tm_a, "attn")
    ca = matmul(oa, W['xattn_wo'], 'nn', "mm_wo")
    ln2 = [a['ln2_g'], a['ln2_b']]
    h2, h2_bf = ew_fwd(rows, fn_ln_res_both, [h1, ca], ln2, "ln2")

    f1, *got = matmul(h2_bf, Wg['mlp_w1'], 'nn', "mm_w1", b_dev=True, side=gathers(with_mm_w1))
    gathered(with_mm_w1, got)
    act, = ew_fwd(rows_ff, _as_bf16(fn_relu2), [f1], [], "relu2")
    ff = matmul(act, W['mlp_w2'], 'nn', "mm_w2")
    ln3 = [a['ln3_g'], a['ln3_b']]
    h3, = ew_fwd(rows, fn_ln_res, [h2, ff], ln3, "ln3")
    dh3, loss_rows = loss_head(h3, target, tm_d, "loss")

    dh2a, dff, G['ln3_g'], G['ln3_b'] = ew_bwd(rows, fn_ln_res, [h2, ff], ln3, [dh3], (0, 1), (0, 1), "ln3_bwd",
                                               dr_dtypes=[F32, BF16])
    dact = matmul(dff, W['mlp_w2'], 'nt', "mm_w2_dx")
    G['mlp_w2'] = rows_split(matmul(act, dff, 'tn', "mm_w2_dw", out_dtype=BF16))
    df1, = ew_bwd(rows_ff, fn_relu2, [f1], [], [dact], (0,), (), "relu2_bwd", dr_dtypes=[BF16])
    dh2 = matmul(df1, Wg['mlp_w1'], 'nt', "mm_w1_dx", add=dh2a, b_dev=True)
    G['mlp_w1'] = matmul(h2_bf, df1, 'tn', "mm_w1_dw", out_dtype=BF16, out_dev=True)

    dh1a, dca, G['ln2_g'], G['ln2_b'] = ew_bwd(rows, fn_ln_res, [h1, ca], ln2, [dh2], (0, 1), (0, 1), "ln2_bwd",
                                               dr_dtypes=[F32, BF16])
    doa = matmul(dca, W['xattn_wo'], 'nt', "mm_wo_dx")
    G['xattn_wo'] = rows_split(matmul(oa, dca, 'tn', "mm_wo_dw", out_dtype=BF16))
    dq, dkx, dvx = attn_bwd(q, kx, vx, doa, tm_a, "attn_bwd")
    dh1 = matmul(dq, W['xattn_wq'], 'nt', "mm_q_dx", add=dh1a)
    G['xattn_wq'] = rows_split(matmul(h1_bf, dq, 'tn', "mm_q_dw", out_dtype=BF16))
    G['xattn_wk'] = rows_split(matmul(mem_n, dkx, 'tn', "mm_k_dw", out_dtype=BF16))
    G['xattn_wv'] = rows_split(matmul(mem_n, dvx, 'tn', "mm_v_dw", out_dtype=BF16))
    dmem_k = matmul(dkx, W['xattn_wk'], 'nt', "mm_k_dx")
    dmem_n = matmul(dvx, W['xattn_wv'], 'nt', "mm_v_dx", add=dmem_k)
    G['ln_mem_g'], G['ln_mem_b'] = ew_bwd(rows_mem, fn_ln, [mem], [ln_mem_g, ln_mem_b], [dmem_n], (), (0, 1),
                                          "ln_mem_bwd")

    dxa, dt1, G['ln1_g'], G['ln1_b'] = ew_bwd(rows, fn_ln_res, [x, t1], ln1, [dh1], (0, 1), (0, 1), "ln1_bwd",
                                              dr_dtypes=[F32, BF16])
    dmerged = matmul(dt1, W['w_out'], 'nt', "mm_w_out_dx")
    G['w_out'] = rows_split(matmul(merged, dt1, 'tn', "mm_w_out_dw", out_dtype=BF16))
    dzgr, dzgc, dpr, dpc = ew_bwd(rows, fn_merge, merge_rows, [], [dmerged], (0, 1, 2, 3), (), "merge_bwd",
                                  dr_dtypes=[BF16] * 4)
    do_r = matmul(dpr, Wg['proj_rwkv'], 'nt', "mm_proj_rwkv_dx", b_dev=True)
    G['proj_rwkv'] = matmul(o_r, dpr, 'tn', "mm_proj_rwkv_dw", out_dtype=BF16, out_dev=True)
    do_c = matmul(dpc, Wg['proj_conv'], 'nt', "mm_proj_conv_dx", b_dev=True)
    G['proj_conv'] = matmul(o_c, dpc, 'tn', "mm_proj_conv_dw", out_dtype=BF16, out_dev=True)

    dyc, G['conv_ln_g'], G['conv_ln_b'] = ew_bwd(rows, fn_ln_silu, [yc], cln, [do_c], (0,), (0, 1), "conv_ln_silu_bwd")
    du, dconv_w, G['conv_b'] = conv_bwd(u, dyc, conv_w, conv_taps, tr_conv, cb_conv, "conv_bwd")
    dzu, dzcg = ew_bwd(rows, fn_glu, glu_rows, [], [du], (0, 1), (), "glu_bwd", dr_dtypes=[BF16] * 2)

    do_r_h = _to_heads(do_r)
    do_h, dr1_h, dk2a_h, dv1_h, dg_h, dgng, dgnb, drk = ew_bwd(
        heads, fn_rwkv_post, post_rows, post_pars, [do_r_h], (0, 1, 2, 3, 4), (0, 1, 2), "rwkv_post_bwd",
        dr_dtypes=[F32, F32, F32, F32, BF16])
    dr_h, dlw_h, dk2_h, dv_h, dna_h, db_h, *got = scan_bwd(
        r_h, lw_h, k2_h, v_h, na_h, b_h, ckpt, do_h, dr1_h, dk2a_h, dv1_h, hb, "scan_bwd",
        side=[(chip_partials(n), 'scatter_chips') for n in with_scan_bwd])
    recv = dict(zip(with_scan_bwd, got))
    dk_h, dwl_h, dal_h, dw0, da0, dkk, dka = ew_bwd(
        heads, fn_rwkv_pre, pre_rows, pre_pars, [dlw_h, dk2_h, dna_h, db_h], (0, 1, 2), (0, 1, 2, 3), "rwkv_pre_bwd",
        dr_dtypes=[F32, BF16, BF16])
    G['rwkv_w0'], G['rwkv_a0'], G['rwkv_k_k'], G['rwkv_k_a'] = (t.reshape(1, DR) for t in (dw0, da0, dkk, dka))
    G['rwkv_gn_g'], G['rwkv_gn_b'] = dgng.reshape(1, DR), dgnb.reshape(1, DR)
    G['rwkv_r_k'] = drk.reshape(1, H, RWKV_HEAD)
    dup = jnp.concatenate([_from_heads(dwl_h), _from_heads(dal_h), _from_heads(dg_h)], axis=1)
    dlora = matmul(dup, w_lora, 'nt', "mm_lora_up_dx")
    dw_lora = matmul(lora, dup, 'tn', "mm_lora_up_dw", out_dtype=BF16)
    d_lora_stack = jnp.concatenate([dw_lora[:r_decay, :DR], dw_lora[r_decay:r_decay + r_iclr, DR:2 * DR],
                                    dw_lora[r_decay + r_iclr:n_lora, 2 * DR:]], axis=0)
    dzs_lora, = ew_bwd(rows, fn_lora, [(zs, lora_w, lane_blk(3 * DR, lora_w))], [], [dlora], (0,), (), "lora_act_bwd")
    dzs = jnp.concatenate([_from_heads(dr_h), _from_heads(dk_h), _from_heads(dv_h), dzs_lora], axis=1)
    dzr, dmu = token_shift_bwd(z, dzs, shift_cols, lora_w, zr_w, mu_p, tm_a, "token_shift_bwd")
    G['rwkv_shift_mix'] = dmu[:, :n_rwkv]
    dz = jnp.concatenate([dzgr, dzgc, dzr[:, :3 * DR].astype(BF16), dzu, dzcg, dzr[:, 3 * DR:].astype(BF16)], axis=1)
    dw_in_p = matmul(x_bf, dz, 'tn', "mm_z_dw", out_dtype=BF16)
    G['w_in'] = _cols_split(jnp.concatenate([dw_in_p[:, c_rkv:c_u], dw_in_p[:, c_lora:c_lora + n_lora],
                                             dw_in_p[:, c_u:c_lora], dw_in_p[:, :c_rkv]], axis=1))
    grad_x, recv['w_in'] = matmul(dz, w_in_p, 'nt', "mm_z_dx", add=dxa, side=[(chip_partials('w_in'), 'scatter_chips')])

    out = {}

    def update(n, parts, w, m, v):
        res = adamw(parts, w, m, v, "adamw_" + n)
        out[n] = [t[:conv_taps].reshape(a[n].shape) if n == 'conv_w' else t.reshape(a[n].shape) for t in res]

    for n in ['w_in'] + with_scan_bwd:
        update(n, recv[n], shard(n), shard(n, 'm_'), shard(n, 'v_'))
    recv_lora = exchange(_cols_split(d_lora_stack), 'scatter', "scatter_lora")
    lo = 0
    for n, r_ in (('rwkv_w_up', r_decay), ('rwkv_a_up', r_iclr), ('rwkv_g_up', r_gate)):
        update(n, recv_lora[:, lo:lo + r_], shard(n), shard(n, 'm_'), shard(n, 'v_'))
        lo += r_
    recv_conv = exchange(_cols_split(dconv_w.astype(BF16)), 'scatter', "scatter_conv_w")
    update('conv_w', recv_conv, shard('conv_w'), shard('conv_w', 'm_'), shard('conv_w', 'v_'))

    small_sizes = [a[n].size for n in SMALL]
    n_small = sum(small_sizes) + 1
    n_pack = _round_up(n_small, SUBLANES * LANES)
    pack = lambda get, last: jnp.pad(jnp.concatenate([get(n).reshape(-1) for n in SMALL] + [last]),
                                     (0, n_pack - n_small)).reshape(n_pack // LANES, LANES)
    zero1 = jnp.zeros((1,), F32)
    g_pack = pack(lambda n: G[n], loss_rows[0, :1])
    parts_small = exchange(g_pack, 'gather', "gather_small")
    res = adamw(parts_small, pack(lambda n: a[n], zero1), pack(lambda n: a['m_' + n], zero1),
                pack(lambda n: a['v_' + n], zero1), "adamw_small")
    res = [t.reshape(-1) for t in res]
    o_ = 0
    for n, sz in zip(SMALL, small_sizes):
        out[n] = [t[o_:o_ + sz].reshape(a[n].shape) for t in res]
        o_ += sz
    loss = res[0][o_]

    return (loss, grad_x[None], *[out[n][0] for n in WEIGHTS], *[out[n][1] for n in WEIGHTS],
            *[out[n][2] for n in WEIGHTS], *[out[n][3] for n in WEIGHTS])


def kernel(x, mem, w_in, rwkv_shift_mix, rwkv_w0, rwkv_w_up, rwkv_a0, rwkv_a_up, rwkv_g_up, rwkv_k_k, rwkv_k_a, rwkv_r_k, rwkv_gn_g, rwkv_gn_b, conv_w, conv_b, conv_ln_g, conv_ln_b, proj_rwkv, proj_conv, w_out, ln1_g, ln1_b, ln_mem_g, ln_mem_b, xattn_wq, xattn_wk, xattn_wv, xattn_wo, ln2_g, ln2_b, mlp_w1, mlp_w2, ln3_g, ln3_b, loss_target, m_w_in, m_rwkv_shift_mix, m_rwkv_w0, m_rwkv_w_up, m_rwkv_a0, m_rwkv_a_up, m_rwkv_g_up, m_rwkv_k_k, m_rwkv_k_a, m_rwkv_r_k, m_rwkv_gn_g, m_rwkv_gn_b, m_conv_w, m_conv_b, m_conv_ln_g, m_conv_ln_b, m_proj_rwkv, m_proj_conv, m_w_out, m_ln1_g, m_ln1_b, m_ln_mem_g, m_ln_mem_b, m_xattn_wq, m_xattn_wk, m_xattn_wv, m_xattn_wo, m_ln2_g, m_ln2_b, m_mlp_w1, m_mlp_w2, m_ln3_g, m_ln3_b, v_w_in, v_rwkv_shift_mix, v_rwkv_w0, v_rwkv_w_up, v_rwkv_a0, v_rwkv_a_up, v_rwkv_g_up, v_rwkv_k_k, v_rwkv_k_a, v_rwkv_r_k, v_rwkv_gn_g, v_rwkv_gn_b, v_conv_w, v_conv_b, v_conv_ln_g, v_conv_ln_b, v_proj_rwkv, v_proj_conv, v_w_out, v_ln1_g, v_ln1_b, v_ln_mem_g, v_ln_mem_b, v_xattn_wq, v_xattn_wk, v_xattn_wv, v_xattn_wo, v_ln2_g, v_ln2_b, v_mlp_w1, v_mlp_w2, v_ln3_g, v_ln3_b):
    return _step(dict(locals()))
```

```python
import functools
import math

import jax
import jax.numpy as jnp
from jax import lax
from jax.experimental import pallas as pl
from jax.experimental.pallas import tpu as pltpu

F32 = jnp.float32
BF16 = jnp.bfloat16

N_DEV = 8
RWKV_HEAD = 64
SCAN_CHUNK = 64
XATTN_HEADS = 4
CONV_HALO = 32
LN_EPS = 1e-5
GN_EPS = 64e-5
ALPHA = float(2.0 ** 0.25)
ADAM_LR, ADAM_B1, ADAM_B2, ADAM_EPS, ADAM_WD, ADAM_STEP = 0.001, 0.9, 0.999, 1e-08, 0.01, 10
LANES = 128
SUBLANES = 8
VMEM_LIMIT = 56 * 1024 * 1024
ADAM_BLOCK_ELEMS = 256 * 1024

WEIGHTS = ['w_in', 'rwkv_shift_mix', 'rwkv_w0', 'rwkv_w_up', 'rwkv_a0', 'rwkv_a_up', 'rwkv_g_up', 'rwkv_k_k',
           'rwkv_k_a', 'rwkv_r_k', 'rwkv_gn_g', 'rwkv_gn_b', 'conv_w', 'conv_b', 'conv_ln_g', 'conv_ln_b',
           'proj_rwkv', 'proj_conv', 'w_out', 'ln1_g', 'ln1_b', 'ln_mem_g', 'ln_mem_b', 'xattn_wq', 'xattn_wk',
           'xattn_wv', 'xattn_wo', 'ln2_g', 'ln2_b', 'mlp_w1', 'mlp_w2', 'ln3_g', 'ln3_b']
COL_SHARDED = ['w_in', 'rwkv_w_up', 'rwkv_a_up', 'rwkv_g_up', 'conv_w', 'proj_rwkv', 'proj_conv', 'mlp_w1']
ROW_SHARDED = ['w_out', 'xattn_wq', 'xattn_wk', 'xattn_wv', 'xattn_wo', 'mlp_w2']
BIG = ['w_in', 'rwkv_w_up', 'rwkv_a_up', 'rwkv_g_up', 'conv_w', 'proj_rwkv', 'proj_conv', 'w_out', 'xattn_wq',
       'xattn_wk', 'xattn_wv', 'xattn_wo', 'mlp_w1', 'mlp_w2']
SMALL = [w for w in WEIGHTS if w not in BIG]


def _cparams(dims):
    return pltpu.CompilerParams(dimension_semantics=dims, vmem_limit_bytes=VMEM_LIMIT)


def _tile(n, cands):
    for c in cands:
        if n % c == 0:
            return c
    return n


N_CHIP = N_DEV // 2
SEMS_PER_EXCHANGE = N_DEV + 2
CHIP_XORS = (2, 4, 6)


def _exchange_shape(src, kind):
    return {'gather': (N_DEV,) + src.shape, 'gather2': (N_DEV,) + src.shape, 'scatter': src.shape,
            'scatter_sib': (N_CHIP,) + src.shape[1:], 'scatter_chips': src.shape}[kind]


def _exchange_copies(e, src_ref, out_ref, kind, send_sems, recv_sems, local_sems):
    x, y, c = lax.axis_index("x"), lax.axis_index("y"), lax.axis_index("c")
    me, chip = 4 * x + 2 * y + c, 2 * x + y
    sibling = (x, y, 1 - c)
    base = e * SEMS_PER_EXCHANGE

    def remote(src, dst, idx, dev):
        return pltpu.make_async_remote_copy(src_ref=src, dst_ref=dst, send_sem=send_sems.at[base + idx],
                                            recv_sem=recv_sems.at[base + idx], device_id=dev,
                                            device_id_type=pl.DeviceIdType.MESH)

    def peer(k):
        return x ^ ((k >> 2) & 1), y ^ ((k >> 1) & 1), c ^ (k & 1)

    first, second = [], []
    if kind in ('gather', 'gather2'):
        first.append(pltpu.make_async_copy(src_ref, out_ref.at[me], local_sems.at[e]))
        for k in (range(1, N_DEV) if kind == 'gather' else (1,) + CHIP_XORS):
            first.append(remote(src_ref, out_ref.at[me], k - 1, peer(k)))
        if kind == 'gather2':
            for i, k in enumerate(CHIP_XORS):
                second.append(remote(out_ref.at[me ^ k], out_ref.at[me ^ k], N_DEV - 1 + i, sibling))
    elif kind == 'scatter':
        first.append(pltpu.make_async_copy(src_ref.at[me], out_ref.at[me], local_sems.at[e]))
        for k in range(1, N_DEV):
            px, py, pc = peer(k)
            first.append(remote(src_ref.at[4 * px + 2 * py + pc], out_ref.at[me], k - 1, (px, py, pc)))
    elif kind == 'scatter_sib':
        for q in range(N_CHIP):
            first.append(remote(src_ref.at[2 * q + 1 - c], out_ref.at[q], q, sibling))
    else:
        assert kind == 'scatter_chips', kind
        first.append(pltpu.make_async_copy(src_ref.at[chip], out_ref.at[chip], local_sems.at[e]))
        for k in CHIP_XORS:
            first.append(remote(src_ref.at[chip ^ (k >> 1)], out_ref.at[chip], k - 1, peer(k)))
    return first, second


class _Side:
    def __init__(self, items):
        self.items = list(items or [])
        self.n = len(self.items)
        any_spec = pl.BlockSpec(memory_space=pl.ANY)
        self.srcs = [s for s, _ in self.items]
        self.in_specs = [any_spec] * self.n
        self.out_specs = [any_spec] * self.n
        self.out_shapes = [jax.ShapeDtypeStruct(_exchange_shape(s, kind), s.dtype) for s, kind in self.items]
        self.scratch = [pltpu.SemaphoreType.DMA((self.n * SEMS_PER_EXCHANGE,)),
                        pltpu.SemaphoreType.DMA((self.n * SEMS_PER_EXCHANGE,)),
                        pltpu.SemaphoreType.DMA((self.n,))] if self.n else []

    def _copies(self, src_refs, out_refs, sems):
        both = [_exchange_copies(e, src_refs[e], out_refs[e], kind, *sems) for e, (_, kind) in enumerate(self.items)]
        return [cp for f, _ in both for cp in f], [cp for _, s in both for cp in s]

    def start(self, src_refs, out_refs, sems):
        for cp in self._copies(src_refs, out_refs, sems)[0]:
            cp.start()

    def finish(self, src_refs, out_refs, sems):
        first, second = self._copies(src_refs, out_refs, sems)
        for cp in first:
            cp.wait()
        for cp in second:
            cp.start()
        for cp in second:
            cp.wait()

    def run(self, first, last, src_refs, out_refs, sems):
        if not self.n:
            return lambda: None
        pl.when(first)(lambda: self.start(src_refs, out_refs, sems))
        return lambda: pl.when(last)(lambda: self.finish(src_refs, out_refs, sems))


def exchange(src, kind, name):
    side = _Side([(src, kind)])

    def body(src_ref, out_ref, *sems):
        side.start([src_ref], [out_ref], sems)
        side.finish([src_ref], [out_ref], sems)

    return pl.pallas_call(body, name=name, in_specs=side.in_specs, out_specs=side.out_specs[0],
                          out_shape=side.out_shapes[0], scratch_shapes=side.scratch)(src)


def pair_add(parts, landed, name):
    _, R, C = parts.shape
    tr = R
    if R * C > ADAM_BLOCK_ELEMS:
        tr = _tile(R, [t for t in (512, 256, 128, 64, 32, 16) if t * C <= ADAM_BLOCK_ELEMS])

    def body(p_ref, l_ref, o_ref):
        mine = p_ref[lax.axis_index("c")]
        o_ref[...] = (mine.astype(F32) + l_ref[...].astype(F32)).astype(BF16)

    return pl.pallas_call(
        body, name=name, grid=(N_CHIP, R // tr),
        in_specs=[pl.BlockSpec((None, 2, tr, C), lambda q, i: (q, 0, i, 0)), pl.BlockSpec((None, tr, C), lambda q, i: (q, i, 0))],
        out_specs=pl.BlockSpec((None, tr, C), lambda q, i: (q, i, 0)),
        out_shape=jax.ShapeDtypeStruct((N_CHIP, R, C), BF16),
        compiler_params=_cparams(("parallel", "parallel")))(parts.reshape(N_CHIP, 2, R, C), landed)


def matmul(a, b, mode, name, add=None, out_dtype=F32, b_dev=False, out_dev=False, side=None):
    side = _Side(side)
    if b_dev:
        assert mode in ('nn', 'nt') and b.shape[0] == N_DEV
        b_rows, b_cols = b.shape[1], N_DEV * b.shape[2]
    else:
        b_rows, b_cols = b.shape
    if mode == 'nn':
        (M, K), (K2, N) = a.shape, (b_rows, b_cols)
    elif mode == 'nt':
        (M, K), (N, K2) = a.shape, (b_rows, b_cols)
    else:
        (K, M), (K2, N) = a.shape, (b_rows, b_cols)
    assert K == K2, (a.shape, b.shape, mode)
    n_unit = N // N_DEV if (out_dev or (b_dev and mode == 'nn')) else N
    k_unit = K // N_DEV if (b_dev and mode == 'nt') else K
    tn = _tile(n_unit, (1024, 512, 256, 128))
    tm = _tile(M, (1024, 512, 256, 128) if tn >= 1024 else (2048, 1024, 512, 256, 128))
    tk = _tile(k_unit, (2048, 1024, 512, 256, 128))
    nk = K // tk
    nb, kb = n_unit // tn, k_unit // tk
    dims = {'nn': ((1,), (0,)), 'nt': ((1,), (1,)), 'tn': ((0,), (0,))}[mode]

    n_in = 2 + (add is not None)
    grid = (M // tm, N // tn, nk)

    def body(*refs):
        a_ref, b_ref = refs[:2]
        add_ref = refs[2] if add is not None else None
        side_src = refs[n_in:n_in + side.n]
        o_ref = refs[n_in + side.n]
        side_out = refs[n_in + side.n + 1:n_in + 2 * side.n + 1]
        acc_ref = refs[n_in + 2 * side.n + 1]
        sems = refs[n_in + 2 * side.n + 2:]
        i, j, k = pl.program_id(0), pl.program_id(1), pl.program_id(2)
        finish = side.run((i == 0) & (j == 0) & (k == 0), (i == grid[0] - 1) & (j == grid[1] - 1) & (k == nk - 1),
                          side_src, side_out, sems)

        @pl.when(k == 0)
        def _():
            acc_ref[...] = jnp.zeros_like(acc_ref)

        acc_ref[...] += lax.dot_general(a_ref[...].astype(BF16), b_ref[...].astype(BF16), (dims, ((), ())),
                                        preferred_element_type=F32)

        @pl.when(k == nk - 1)
        def _():
            r = acc_ref[...]
            if add is not None:
                r = r + add_ref[...]
            o_ref[...] = r.astype(out_dtype)

        finish()

    if mode == 'nn':
        a_spec = pl.BlockSpec((tm, tk), lambda i, j, k: (i, k))
        b_spec = (pl.BlockSpec((None, tk, tn), lambda i, j, k: (j // nb, k, j % nb)) if b_dev
                  else pl.BlockSpec((tk, tn), lambda i, j, k: (k, j)))
    elif mode == 'nt':
        a_spec = pl.BlockSpec((tm, tk), lambda i, j, k: (i, k))
        b_spec = (pl.BlockSpec((None, tn, tk), lambda i, j, k: (k // kb, j, k % kb)) if b_dev
                  else pl.BlockSpec((tn, tk), lambda i, j, k: (j, k)))
    else:
        a_spec = pl.BlockSpec((tk, tm), lambda i, j, k: (k, i))
        b_spec = pl.BlockSpec((tk, tn), lambda i, j, k: (k, j))
    add_spec = pl.BlockSpec((tm, tn), lambda i, j, k: (i, j))
    if out_dev:
        o_spec = pl.BlockSpec((None, tm, tn), lambda i, j, k: (j // nb, i, j % nb))
        o_shape = (N_DEV, M, N // N_DEV)
    else:
        o_spec, o_shape = add_spec, (M, N)
    in_specs = [a_spec, b_spec] + ([add_spec] if add is not None else [])
    ops = (a, b) + ((add,) if add is not None else ())
    sem = ("arbitrary",) * 3 if side.n else ("parallel", "parallel", "arbitrary")
    res = pl.pallas_call(
        body, name=name, grid=grid, in_specs=in_specs + side.in_specs, out_specs=[o_spec] + side.out_specs,
        out_shape=[jax.ShapeDtypeStruct(o_shape, out_dtype)] + side.out_shapes,
        scratch_shapes=[pltpu.VMEM((tm, tn), F32)] + side.scratch,
        compiler_params=_cparams(sem))(*ops, *side.srcs)
    return tuple(res) if side.n else res[0]


class _Rows2D:
    def __init__(self, n_rows, tm):
        self.n, self.tm = n_rows, tm
        self.grid = (1, n_rows // tm)

    def row(self, e):
        if isinstance(e, tuple):
            arr, width, cb = e
            return arr, (self.tm, width), pl.BlockSpec((self.tm, width), lambda g, i, cb=cb: (i, cb))
        return e, (self.tm, e.shape[1]), pl.BlockSpec((self.tm, e.shape[1]), lambda g, i: (i, 0))

    def par(self, p):
        return pl.BlockSpec(p.shape, lambda g, i: (0,) * p.ndim)

    def out(self, blk):
        return (self.n, blk[1]), pl.BlockSpec((self.tm, blk[1]), lambda g, i: (i, 0))


class _RowsHeads:
    def __init__(self, n_heads, n_rows, hb, ts):
        self.h, self.n, self.hb, self.ts = n_heads, n_rows, hb, ts
        self.grid = (n_heads // hb, n_rows // ts)

    def row(self, e):
        blk = (self.hb, self.ts, e.shape[2])
        return e, blk, pl.BlockSpec(blk, lambda g, i: (g, i, 0))

    def par(self, p):
        return pl.BlockSpec((self.hb, 1, p.shape[2]), lambda g, i: (g, 0, 0))

    def out(self, blk):
        return (self.h, self.n, blk[2]), pl.BlockSpec(blk, lambda g, i: (g, i, 0))


def _par_block(lay, p):
    return lay.par(p).block_shape


def ew_fwd(lay, fn, rows, params, name):
    rr = [lay.row(e) for e in rows]
    arrs = [r[0] for r in rr]
    blk_avals = [jax.ShapeDtypeStruct(r[1], r[0].dtype) for r in rr]
    par_avals = [jax.ShapeDtypeStruct(_par_block(lay, p), p.dtype) for p in params]
    outs = jax.eval_shape(fn, *blk_avals, *par_avals)
    out_full = [lay.out(o.shape) for o in outs]
    nr, npar = len(rows), len(params)

    def body(*refs):
        vals = [r[...] for r in refs[:nr + npar]]
        res = fn(*vals)
        for ref, v in zip(refs[nr + npar:], res):
            ref[...] = v.astype(ref.dtype)

    return pl.pallas_call(
        body, name=name, grid=lay.grid,
        in_specs=[r[2] for r in rr] + [lay.par(p) for p in params],
        out_specs=[o[1] for o in out_full],
        out_shape=[jax.ShapeDtypeStruct(o[0], a.dtype) for o, a in zip(out_full, outs)],
        compiler_params=_cparams(("parallel", "parallel")))(*arrs, *params)


def ew_bwd(lay, fn, rows, params, cots, wrt_rows, wrt_pars, name, dr_dtypes=None):
    rr = [lay.row(e) for e in rows]
    cc = [lay.row(e) for e in cots]
    nr, npar, nc = len(rows), len(params), len(cots)
    n_dr = len(wrt_rows)
    dr_full = [lay.out(rr[i][1]) for i in wrt_rows]
    dr_dtypes = dr_dtypes or [F32] * n_dr

    def body(*refs):
        rv = [r[...] for r in refs[:nr]]
        pv = [r[...] for r in refs[nr:nr + npar]]
        cv = tuple(r[...] for r in refs[nr + npar:nr + npar + nc])
        outs = refs[nr + npar + nc:]

        def f(*wrt):
            r2, p2 = list(rv), list(pv)
            for idx, v in zip(wrt_rows, wrt[:n_dr]):
                r2[idx] = v
            for idx, v in zip(wrt_pars, wrt[n_dr:]):
                p2[idx] = v
            return fn(*r2, *p2)

        _, vjp = jax.vjp(f, *[rv[i] for i in wrt_rows], *[pv[i] for i in wrt_pars])
        g = vjp(cv)
        for ref, v in zip(outs[:n_dr], g[:n_dr]):
            ref[...] = v.astype(ref.dtype)
        if wrt_pars:
            @pl.when(pl.program_id(1) == 0)
            def _():
                for ref in outs[n_dr:]:
                    ref[...] = jnp.zeros_like(ref)

            for ref, v in zip(outs[n_dr:], g[n_dr:]):
                ref[...] += v

    return pl.pallas_call(
        body, name=name, grid=lay.grid,
        in_specs=[r[2] for r in rr] + [lay.par(p) for p in params] + [c[2] for c in cc],
        out_specs=[o[1] for o in dr_full] + [lay.par(params[i]) for i in wrt_pars],
        out_shape=[jax.ShapeDtypeStruct(o[0], dt) for o, dt in zip(dr_full, dr_dtypes)]
        + [jax.ShapeDtypeStruct(params[i].shape, F32) for i in wrt_pars],
        compiler_params=_cparams(("parallel", "arbitrary")))(
            *[r[0] for r in rr], *params, *[c[0] for c in cc])


def _sigmoid(x):
    return 1.0 / (1.0 + jnp.exp(-x))


def _softplus(x):
    return jnp.maximum(x, 0.0) + jnp.log(1.0 + jnp.exp(-jnp.abs(x)))


def _layer_norm(x, g, b, eps):
    mu = jnp.mean(x, -1, keepdims=True)
    xc = x - mu
    var = jnp.mean(xc * xc, -1, keepdims=True)
    return xc * lax.rsqrt(var + eps) * g + b


def _as_bf16(fn):
    return lambda *args: tuple(o.astype(BF16) for o in fn(*args))


def fn_ln(x, g, b):
    return (_layer_norm(x, g, b, LN_EPS),)


def fn_ln_res(h, t, g, b):
    return (_layer_norm(ALPHA * h + t, g, b, LN_EPS),)


def fn_ln_res_both(h, t, g, b):
    y, = fn_ln_res(h, t, g, b)
    return y, y.astype(BF16)


def make_fn_lora(r_decay, r_iclr, r_gate):
    def fn(z):
        lane = lax.broadcasted_iota(jnp.int32, z.shape, 1)
        out = jnp.where(lane < r_decay, jnp.tanh(z), z)
        out = jnp.where(lane >= r_decay + r_iclr, _sigmoid(z), out)
        return (jnp.where(lane < r_decay + r_iclr + r_gate, out, 0.0),)
    return fn


def fn_rwkv_pre(k, wl, al, w0, a0, k_k, k_a):
    w = -_softplus(-(w0 + wl)) - 0.5
    lw = -jnp.exp(w)
    a = _sigmoid(a0 + al)
    kk = k * k_k
    kk = kk / jnp.maximum(jnp.sqrt(jnp.sum(kk * kk, -1, keepdims=True)), 1e-12)
    k2 = k * (1.0 + (a - 1.0) * k_a)
    return lw, k2, -kk, kk * a


def fn_rwkv_post(o, r, k2, v, g, gn_g, gn_b, r_k):
    mu = jnp.mean(o, -1, keepdims=True)
    oc = o - mu
    var = jnp.mean(oc * oc, -1, keepdims=True)
    y = oc * lax.rsqrt(var + GN_EPS) * gn_g + gn_b
    y = y + jnp.sum(r * k2 * r_k, -1, keepdims=True) * v
    return (y * g,)


def fn_glu(zu, zg):
    return (zu * _sigmoid(zg),)


def fn_ln_silu(y, g, b):
    n = _layer_norm(y, g, b, LN_EPS)
    return (n * _sigmoid(n),)


def fn_merge(zgr, zgc, pr, pc):
    return (_sigmoid(zgr) * pr + _sigmoid(zgc) * pc,)


def fn_relu2(f):
    r = jnp.maximum(f, 0.0)
    return (r * r,)


def _dot_nn(a, b):
    return lax.dot_general(a.astype(BF16), b.astype(BF16), (((1,), (0,)), ((), ())), preferred_element_type=F32)


def _dot_nt(a, b):
    return lax.dot_general(a.astype(BF16), b.astype(BF16), (((1,), (1,)), ((), ())), preferred_element_type=F32)


def _dot_tn(a, b):
    return lax.dot_general(a.astype(BF16), b.astype(BF16), (((0,), (0,)), ((), ())), preferred_element_type=F32)


def _softmax_rows(s):
    s = s - jnp.max(s, -1, keepdims=True)
    e = jnp.exp(s)
    return e / jnp.sum(e, -1, keepdims=True)


def attn_fwd(q, kx, vx, tm, name):
    S, D = q.shape
    M = kx.shape[0]
    dh = D // XATTN_HEADS
    scale = dh ** -0.5

    def body(q_ref, k_ref, v_ref, o_ref):
        p = _softmax_rows(_dot_nt(q_ref[...], k_ref[...]) * scale)
        o_ref[...] = _dot_nn(p, v_ref[...]).astype(BF16)

    row = pl.BlockSpec((tm, dh), lambda h, i: (i, h))
    kv = pl.BlockSpec((M, dh), lambda h, i: (0, h))
    return pl.pallas_call(body, name=name, grid=(XATTN_HEADS, S // tm), in_specs=[row, kv, kv], out_specs=row,
                          out_shape=jax.ShapeDtypeStruct((S, D), BF16),
                          compiler_params=_cparams(("parallel", "parallel")))(q, kx, vx)


def attn_bwd(q, kx, vx, do, tm, name):
    S, D = q.shape
    M = kx.shape[0]
    dh = D // XATTN_HEADS
    scale = dh ** -0.5

    def body(q_ref, k_ref, v_ref, do_ref, dq_ref, dk_ref, dv_ref):
        qb, kb, dob = q_ref[...], k_ref[...], do_ref[...]
        p = _softmax_rows(_dot_nt(qb, kb) * scale)
        dp = _dot_nt(dob, v_ref[...])
        ds = p * (dp - jnp.sum(dp * p, -1, keepdims=True)) * scale
        dq_ref[...] = _dot_nn(ds, kb).astype(BF16)

        @pl.when(pl.program_id(1) == 0)
        def _():
            dk_ref[...] = jnp.zeros_like(dk_ref)
            dv_ref[...] = jnp.zeros_like(dv_ref)

        dk_ref[...] += _dot_tn(ds, qb)
        dv_ref[...] += _dot_tn(p, dob)

    row = pl.BlockSpec((tm, dh), lambda h, i: (i, h))
    kv = pl.BlockSpec((M, dh), lambda h, i: (0, h))
    return pl.pallas_call(
        body, name=name, grid=(XATTN_HEADS, S // tm), in_specs=[row, kv, kv, row], out_specs=[row, kv, kv],
        out_shape=[jax.ShapeDtypeStruct((S, D), BF16), jax.ShapeDtypeStruct((M, D), F32), jax.ShapeDtypeStruct((M, D), F32)],
        compiler_params=_cparams(("parallel", "arbitrary")))(q, kx, vx, do)


def _shift_down(blk, halo_last_row, first_block):
    rolled = pltpu.roll(blk, 1, 0)
    row = lax.broadcasted_iota(jnp.int32, blk.shape, 0)
    top = jnp.where(first_block, 0.0, halo_last_row)
    return jnp.where(row == 0, top, rolled)


def _shift_up(blk, halo_first_row, last_block):
    n = blk.shape[0]
    rolled = pltpu.roll(blk, n - 1, 0)
    row = lax.broadcasted_iota(jnp.int32, blk.shape, 0)
    bot = jnp.where(last_block, 0.0, halo_first_row)
    return jnp.where(row == n - 1, bot, rolled)


def _zcol(j, cols):
    n_first, first, second = cols
    return jnp.where(j < n_first, first + j, second + j - n_first)


def token_shift_fwd(z, cols, cw, width, mu, tm, name):
    S = z.shape[0]
    hb = tm // SUBLANES

    def body(z_ref, halo_ref, mu_ref, o_ref):
        zb = z_ref[...]
        prev = _shift_down(zb, halo_ref[SUBLANES - 1:SUBLANES, :], pl.program_id(1) == 0)
        o_ref[...] = zb + (prev - zb) * mu_ref[...]

    return pl.pallas_call(
        body, name=name, grid=(width // cw, S // tm),
        in_specs=[pl.BlockSpec((tm, cw), lambda j, i: (i, _zcol(j, cols))),
                  pl.BlockSpec((SUBLANES, cw), lambda j, i: (jnp.maximum(i * hb - 1, 0), _zcol(j, cols))),
                  pl.BlockSpec((1, cw), lambda j, i: (0, j))],
        out_specs=pl.BlockSpec((tm, cw), lambda j, i: (i, j)),
        out_shape=jax.ShapeDtypeStruct((S, width), F32),
        compiler_params=_cparams(("parallel", "parallel")))(z, z, mu)


def token_shift_bwd(z, dzs, cols, cw, width, mu, tm, name):
    S = z.shape[0]
    hb = tm // SUBLANES
    nblk = S // tm
    last8 = S // SUBLANES - 1

    def body(z_ref, zh_ref, d_ref, dh_ref, mu_ref, dz_ref, dmu_ref):
        i = pl.program_id(1)
        zb, db, m = z_ref[...], d_ref[...], mu_ref[...]
        prev = _shift_down(zb, zh_ref[SUBLANES - 1:SUBLANES, :], i == 0)
        dm = db * m
        nxt = _shift_up(dm, dh_ref[0:1, :] * m, i == nblk - 1)
        dz_ref[...] = db - dm + nxt

        @pl.when(i == 0)
        def _():
            dmu_ref[...] = jnp.zeros_like(dmu_ref)

        dmu_ref[...] += jnp.sum(db * (prev - zb), 0, keepdims=True)

    return pl.pallas_call(
        body, name=name, grid=(width // cw, nblk),
        in_specs=[pl.BlockSpec((tm, cw), lambda j, i: (i, _zcol(j, cols))),
                  pl.BlockSpec((SUBLANES, cw), lambda j, i: (jnp.maximum(i * hb - 1, 0), _zcol(j, cols))),
                  pl.BlockSpec((tm, cw), lambda j, i: (i, j)),
                  pl.BlockSpec((SUBLANES, cw), lambda j, i: (jnp.minimum((i + 1) * hb, last8), j)),
                  pl.BlockSpec((1, cw), lambda j, i: (0, j))],
        out_specs=[pl.BlockSpec((tm, cw), lambda j, i: (i, j)), pl.BlockSpec((1, cw), lambda j, i: (0, j))],
        out_shape=[jax.ShapeDtypeStruct((S, width), F32), jax.ShapeDtypeStruct((1, width), F32)],
        compiler_params=_cparams(("parallel", "arbitrary")))(z, z, dzs, dzs, mu)


def conv_fwd(u, w, b, width, tr, cb, name):
    S, C = u.shape
    hb = tr // CONV_HALO

    def body(u_ref, h_ref, w_ref, b_ref, y_ref):
        i = pl.program_id(1)
        halo = jnp.where(i == 0, 0.0, h_ref[...])
        win = jnp.concatenate([halo, u_ref[...]], axis=0)
        acc = jnp.zeros((tr, cb), F32) + b_ref[...]
        for d in range(width):
            sh = win if d == 0 else pltpu.roll(win, d, 0)
            acc = acc + sh[CONV_HALO:, :] * w_ref[width - 1 - d:width - d, :]
        y_ref[...] = acc

    return pl.pallas_call(
        body, name=name, grid=(C // cb, S // tr),
        in_specs=[pl.BlockSpec((tr, cb), lambda j, i: (i, j)),
                  pl.BlockSpec((CONV_HALO, cb), lambda j, i: (jnp.maximum(i * hb - 1, 0), j)),
                  pl.BlockSpec((CONV_HALO, cb), lambda j, i: (0, j)),
                  pl.BlockSpec((1, cb), lambda j, i: (0, j))],
        out_specs=pl.BlockSpec((tr, cb), lambda j, i: (i, j)),
        out_shape=jax.ShapeDtypeStruct((S, C), F32), compiler_params=_cparams(("parallel", "parallel")))(u, u, w, b)


def conv_bwd(u, dy, w, width, tr, cb, name):
    S, C = u.shape
    hb = tr // CONV_HALO
    nblk = S // tr
    last = S // CONV_HALO - 1

    def body(u_ref, uh_ref, d_ref, dh_ref, w_ref, du_ref, dw_ref, db_ref):
        i = pl.program_id(1)
        dyb = d_ref[...]
        uwin = jnp.concatenate([jnp.where(i == 0, 0.0, uh_ref[...]), u_ref[...]], axis=0)
        dwin = jnp.concatenate([dyb, jnp.where(i == nblk - 1, 0.0, dh_ref[...])], axis=0)

        @pl.when(i == 0)
        def _():
            dw_ref[...] = jnp.zeros_like(dw_ref)
            db_ref[...] = jnp.zeros_like(db_ref)

        acc = jnp.zeros((tr, cb), F32)
        for d in range(width):
            tap = width - 1 - d
            dsh = dwin if d == 0 else pltpu.roll(dwin, tr + CONV_HALO - d, 0)
            acc = acc + dsh[:tr, :] * w_ref[tap:tap + 1, :]
            ush = uwin if d == 0 else pltpu.roll(uwin, d, 0)
            dw_ref[tap:tap + 1, :] += jnp.sum(ush[CONV_HALO:, :] * dyb, 0, keepdims=True)
        du_ref[...] = acc
        db_ref[...] += jnp.sum(dyb, 0, keepdims=True)

    return pl.pallas_call(
        body, name=name, grid=(C // cb, nblk),
        in_specs=[pl.BlockSpec((tr, cb), lambda j, i: (i, j)),
                  pl.BlockSpec((CONV_HALO, cb), lambda j, i: (jnp.maximum(i * hb - 1, 0), j)),
                  pl.BlockSpec((tr, cb), lambda j, i: (i, j)),
                  pl.BlockSpec((CONV_HALO, cb), lambda j, i: (jnp.minimum((i + 1) * hb, last), j)),
                  pl.BlockSpec((CONV_HALO, cb), lambda j, i: (0, j))],
        out_specs=[pl.BlockSpec((tr, cb), lambda j, i: (i, j)),
                   pl.BlockSpec((CONV_HALO, cb), lambda j, i: (0, j)),
                   pl.BlockSpec((1, cb), lambda j, i: (0, j))],
        out_shape=[jax.ShapeDtypeStruct((S, C), F32), jax.ShapeDtypeStruct((CONV_HALO, C), F32),
                   jax.ShapeDtypeStruct((1, C), F32)],
        compiler_params=_cparams(("parallel", "arbitrary")))(u, u, dy, dy, w)


def _split2(x):
    hi = x.astype(BF16)
    return hi, (x - hi.astype(F32)).astype(BF16)


def _dot3(a, b, dims):
    ah, al = _split2(a)
    bh, bl = _split2(b)
    d = lambda p, q: lax.dot_general(p, q, dims, preferred_element_type=F32)
    return d(ah, bh) + (d(ah, bl) + d(al, bh))


@jax.custom_vjp
def _bnn(a, b):
    return _dot3(a, b, (((2,), (1,)), ((0,), (0,))))


@jax.custom_vjp
def _bnt(a, b):
    return _dot3(a, b, (((2,), (2,)), ((0,), (0,))))


@jax.custom_vjp
def _btn(a, b):
    return _dot3(a, b, (((1,), (1,)), ((0,), (0,))))


def _tri_sum(x, lower):
    h, c, _ = x.shape
    ti = lax.broadcasted_iota(jnp.int32, (h, c, c), 1)
    si = lax.broadcasted_iota(jnp.int32, (h, c, c), 2)
    m = (si <= ti if lower else si >= ti).astype(BF16)
    x1 = x.astype(BF16)
    r1 = x - x1.astype(F32)
    x2 = r1.astype(BF16)
    x3 = (r1 - x2.astype(F32)).astype(BF16)
    d = lambda q: lax.dot_general(m, q, (((2,), (1,)), ((0,), (0,))), preferred_element_type=F32)
    return d(x1) + (d(x2) + d(x3))


@jax.custom_vjp
def _cumsum_rows(x):
    return _tri_sum(x, True)


@jax.custom_vjp
def _rev_cumsum_rows(x):
    return _tri_sum(x, False)


_cumsum_rows.defvjp(lambda x: (_cumsum_rows(x), None), lambda _, g: (_rev_cumsum_rows(g),))
_rev_cumsum_rows.defvjp(lambda x: (_rev_cumsum_rows(x), None), lambda _, g: (_cumsum_rows(g),))


@jax.custom_vjp
def _unit_lower_inverse(a):
    h, c, _ = a.shape
    eye = (lax.broadcasted_iota(jnp.int32, (h, c, c), 1) == lax.broadcasted_iota(jnp.int32, (h, c, c), 2)).astype(F32)
    t, pw = eye + a, a
    for _ in range(int(math.log2(c)) - 1):
        pw = _bnn(pw, pw)
        t = t + _bnn(t, pw)
    return t


def _unit_lower_inverse_fwd(a):
    t = _unit_lower_inverse(a)
    return t, t


_unit_lower_inverse.defvjp(_unit_lower_inverse_fwd, lambda t, g: (_btn(t, _bnt(g, t)),))


_bnn.defvjp(lambda a, b: (_bnn(a, b), (a, b)), lambda res, g: (_bnt(g, res[1]), _btn(res[0], g)))
_bnt.defvjp(lambda a, b: (_bnt(a, b), (a, b)), lambda res, g: (_bnn(g, res[1]), _btn(g, res[0])))
_btn.defvjp(lambda a, b: (_btn(a, b), (a, b)), lambda res, g: (_bnt(res[1], g), _bnn(res[0], g)))


def scan_chunk(st0, r, lw, k, v, a, b):
    h, c, n = r.shape
    ti = lax.broadcasted_iota(jnp.int32, (h, c, c), 1)
    si = lax.broadcasted_iota(jnp.int32, (h, c, c), 2)
    incl = si <= ti
    strict = si < ti
    cum = _cumsum_rows(lw)
    p = jnp.exp(cum)
    pinv = jnp.exp(-cum)
    at = a * jnp.exp(cum - lw)
    bt, kt, rt = b * pinv, k * pinv, r * p
    a_ab = jnp.where(strict, _bnt(at, bt), 0.0)
    a_ak = jnp.where(strict, _bnt(at, kt), 0.0)
    m_rb = jnp.where(incl, _bnt(rt, bt), 0.0)
    m_rk = jnp.where(incl, _bnt(rt, kt), 0.0)
    u = _bnn(_unit_lower_inverse(a_ab), _bnn(at, st0) + _bnn(a_ak, v))
    o = _bnn(rt, st0) + _bnn(m_rb, u) + _bnn(m_rk, v)
    cum_c = jnp.sum(lw, axis=1, keepdims=True)
    tail = jnp.exp(cum_c - cum)
    ki = lax.broadcasted_iota(jnp.int32, (h, n, n), 1)
    kj = lax.broadcasted_iota(jnp.int32, (h, n, n), 2)
    pc_col = jnp.sum(jnp.where(ki == kj, jnp.exp(cum_c), 0.0), axis=2, keepdims=True)
    st = st0 * pc_col + _btn(b * tail, u) + _btn(k * tail, v)
    return o, st


def _grid_ends(grid):
    g, c = pl.program_id(0), pl.program_id(1)
    return (g == 0) & (c == 0), (g == grid[0] - 1) & (c == grid[1] - 1)


def scan_fwd(r, lw, k, v, a, b, hb, name, side=None):
    H, S, N = r.shape
    C = SCAN_CHUNK
    nc = S // C
    side = _Side(side)
    grid = (H // hb, nc)

    def body(*refs):
        r_ref, lw_ref, k_ref, v_ref, a_ref, b_ref = refs[:6]
        side_src = refs[6:6 + side.n]
        o_ref, ck_ref = refs[6 + side.n:8 + side.n]
        side_out = refs[8 + side.n:8 + 2 * side.n]
        st_ref = refs[8 + 2 * side.n]
        finish = side.run(*_grid_ends(grid), side_src, side_out, refs[9 + 2 * side.n:])

        @pl.when(pl.program_id(1) == 0)
        def _():
            st_ref[...] = jnp.zeros_like(st_ref)

        st0 = st_ref[...]
        ck_ref[...] = st0[:, None]
        o, st = scan_chunk(st0, r_ref[...], lw_ref[...], k_ref[...], v_ref[...], a_ref[...], b_ref[...])
        o_ref[...] = o
        st_ref[...] = st
        finish()

    seq = pl.BlockSpec((hb, C, N), lambda g, c: (g, c, 0))
    return pl.pallas_call(
        body, name=name, grid=grid, in_specs=[seq] * 6 + side.in_specs,
        out_specs=[seq, pl.BlockSpec((hb, 1, N, N), lambda g, c: (g, c, 0, 0))] + side.out_specs,
        out_shape=[jax.ShapeDtypeStruct((H, S, N), F32), jax.ShapeDtypeStruct((H, nc, N, N), F32)] + side.out_shapes,
        scratch_shapes=[pltpu.VMEM((hb, N, N), F32)] + side.scratch,
        compiler_params=_cparams(("arbitrary", "arbitrary")))(r, lw, k, v, a, b, *side.srcs)


def scan_bwd(r, lw, k, v, a, b, ck, do, dr_add, dk_add, dv_add, hb, name, side=None):
    H, S, N = r.shape
    C = SCAN_CHUNK
    nc = S // C
    side = _Side(side)
    grid = (H // hb, nc)

    def body(*refs):
        r_ref, lw_ref, k_ref, v_ref, a_ref, b_ref, ck_ref, do_ref, ra_ref, ka_ref, va_ref = refs[:11]
        side_src = refs[11:11 + side.n]
        dr_ref, dlw_ref, dk_ref, dv_ref, da_ref, db_ref = refs[11 + side.n:17 + side.n]
        side_out = refs[17 + side.n:17 + 2 * side.n]
        dst_ref = refs[17 + 2 * side.n]
        finish = side.run(*_grid_ends(grid), side_src, side_out, refs[18 + 2 * side.n:])

        @pl.when(pl.program_id(1) == 0)
        def _():
            dst_ref[...] = jnp.zeros_like(dst_ref)

        st0 = ck_ref[...][:, 0]
        _, vjp = jax.vjp(scan_chunk, st0, r_ref[...], lw_ref[...], k_ref[...], v_ref[...], a_ref[...], b_ref[...])
        dst0, dr, dlw, dk, dv, da, db = vjp((do_ref[...], dst_ref[...]))
        dr_ref[...], dlw_ref[...], dk_ref[...] = dr + ra_ref[...], dlw, dk + ka_ref[...]
        dv_ref[...], da_ref[...], db_ref[...] = dv + va_ref[...], da, db
        dst_ref[...] = dst0
        finish()

    seq = pl.BlockSpec((hb, C, N), lambda g, c: (g, nc - 1 - c, 0))
    return pl.pallas_call(
        body, name=name, grid=grid,
        in_specs=[seq] * 6 + [pl.BlockSpec((hb, 1, N, N), lambda g, c: (g, nc - 1 - c, 0, 0))] + [seq] * 4 + side.in_specs,
        out_specs=[seq] * 6 + side.out_specs, out_shape=[jax.ShapeDtypeStruct((H, S, N), F32)] * 6 + side.out_shapes,
        scratch_shapes=[pltpu.VMEM((hb, N, N), F32)] + side.scratch,
        compiler_params=_cparams(("arbitrary", "arbitrary")))(r, lw, k, v, a, b, ck, do, dr_add, dk_add, dv_add,
                                                              *side.srcs)


def loss_head(y, target, tm, name):
    S, D = y.shape

    def body(y_ref, t_ref, dy_ref, l_ref):
        err = y_ref[...] - t_ref[...]
        dy_ref[...] = err * (1.0 / D)

        @pl.when(pl.program_id(0) == 0)
        def _():
            l_ref[...] = jnp.zeros_like(l_ref)

        l_ref[...] += 0.5 * jnp.sum(jnp.mean(err * err, -1, keepdims=True), 0, keepdims=True)

    row = pl.BlockSpec((tm, D), lambda i: (i, 0))
    return pl.pallas_call(
        body, name=name, grid=(S // tm,), in_specs=[row, row],
        out_specs=[row, pl.BlockSpec((SUBLANES, LANES), lambda i: (0, 0))],
        out_shape=[jax.ShapeDtypeStruct((S, D), F32), jax.ShapeDtypeStruct((SUBLANES, LANES), F32)],
        compiler_params=_cparams(("arbitrary",)))(y, target)


def adamw(parts, w, m, v, name):
    R, C = w.shape
    n_parts = parts.shape[0]
    tr = R
    if R * C > ADAM_BLOCK_ELEMS:
        tr = _tile(R, [t for t in (512, 256, 128, 64, 32, 16) if t * C <= ADAM_BLOCK_ELEMS])
    c1 = 1.0 / (1.0 - ADAM_B1 ** ADAM_STEP)
    c2 = 1.0 / (1.0 - ADAM_B2 ** ADAM_STEP)

    def body(p_ref, w_ref, m_ref, v_ref, g_ref, d_ref, nm_ref, nv_ref):
        g = p_ref[0].astype(F32)
        for j in range(1, n_parts):
            g = g + p_ref[j].astype(F32)
        nm = ADAM_B1 * m_ref[...] + (1.0 - ADAM_B1) * g
        nv = ADAM_B2 * v_ref[...] + (1.0 - ADAM_B2) * (g * g)
        g_ref[...] = g
        nm_ref[...] = nm
        nv_ref[...] = nv
        d_ref[...] = -ADAM_LR * ((nm * c1) / (jnp.sqrt(nv * c2) + ADAM_EPS) + ADAM_WD * w_ref[...])

    blk = pl.BlockSpec((tr, C), lambda i: (i, 0))
    return pl.pallas_call(
        body, name=name, grid=(R // tr,), in_specs=[pl.BlockSpec((n_parts, tr, C), lambda i: (0, i, 0)), blk, blk, blk],
        out_specs=[blk] * 4, out_shape=[jax.ShapeDtypeStruct((R, C), F32)] * 4,
        compiler_params=_cparams(("parallel",)))(parts, w, m, v)


def _to_heads(a2d):
    s, d = a2d.shape
    return a2d.reshape(s, d // RWKV_HEAD, RWKV_HEAD).transpose(1, 0, 2)


def _from_heads(a3d):
    h, s, n = a3d.shape
    return a3d.transpose(1, 0, 2).reshape(s, h * n)


def _cols_joined(g):
    return g.transpose(1, 0, 2).reshape(g.shape[1], N_DEV * g.shape[2])


def _cols_split(w):
    return w.reshape(w.shape[0], N_DEV, w.shape[1] // N_DEV).transpose(1, 0, 2)


def _round_up(n, m):
    return -(-n // m) * m


def _step(a):
    x, mem, target = a['x'][0], a['mem'][0], a['loss_target'][0]
    S, D = x.shape
    M = mem.shape[0]
    DR = a['rwkv_w0'].shape[1]
    H = DR // RWKV_HEAD
    DC = a['conv_b'].shape[1]
    r_decay, r_iclr, r_gate = a['rwkv_w_up'].shape[1], a['rwkv_a_up'].shape[1], a['rwkv_g_up'].shape[1]
    n_lora = r_decay + r_iclr + r_gate
    lora_w = _round_up(n_lora, LANES)
    n_rwkv = 3 * DR + n_lora
    zr_w = 3 * DR + lora_w
    pad_cols = zr_w - n_rwkv
    conv_taps = a['conv_w'].shape[1]
    assert conv_taps - 1 <= CONV_HALO and S % SCAN_CHUNK == 0

    def shard(n, pre=''):
        w = a[pre + n]
        if n == 'conv_w':
            return jnp.pad(w.reshape(conv_taps, w.shape[-1]), ((0, CONV_HALO - conv_taps), (0, 0)))
        return w.reshape(w.shape[1:])

    Wg, W = {}, {}

    def gathers(names):
        return [(shard(n).astype(BF16), 'gather2') for n in names]

    def gathered(names, results):
        for n, g in zip(names, results):
            Wg[n] = g
            if n in ROW_SHARDED:
                W[n] = g.reshape(N_DEV * g.shape[1], g.shape[2])

    with_mm_z = ['proj_rwkv', 'proj_conv', 'w_out', 'xattn_wq']
    with_scan_fwd = ['xattn_wk', 'xattn_wv', 'xattn_wo', 'mlp_w1']
    with_mm_w1 = ['mlp_w2']
    with_scan_bwd = ['mlp_w1', 'xattn_wo', 'xattn_wq', 'xattn_wk', 'xattn_wv', 'w_out', 'proj_rwkv', 'proj_conv']
    gathered(['w_in'], [exchange(shard('w_in').astype(BF16), 'gather2', "gather_w_in")])
    lora_src = jnp.concatenate([shard('rwkv_w_up'), shard('rwkv_a_up'), shard('rwkv_g_up')], axis=0).astype(BF16)
    lora_all = exchange(lora_src, 'gather', "gather_lora")
    conv_w = _cols_joined(exchange(shard('conv_w'), 'gather', "gather_conv_w"))

    w_in = _cols_joined(Wg['w_in'])
    o_conv, o_gate = n_rwkv, n_rwkv + 2 * DC
    w_in_p = jnp.concatenate([w_in[:, o_gate:], w_in[:, :3 * DR], w_in[:, o_conv:o_gate], w_in[:, 3 * DR:n_rwkv],
                              jnp.zeros((D, pad_cols), BF16)], axis=1)
    c_gr, c_gc, c_rkv, c_u = 0, D, 2 * D, 2 * D + 3 * DR
    c_cg, c_lora = c_u + DC, c_u + 2 * DC
    assert c_rkv % lora_w == 0 and c_lora % lora_w == 0 and (3 * DR) % lora_w == 0 and c_u % DC == 0 and c_rkv % DR == 0
    shift_cols = (3 * DR // lora_w, c_rkv // lora_w, c_lora // lora_w)
    lora_full = _cols_joined(lora_all)
    w_lora = jnp.zeros((lora_w, 3 * DR), BF16)
    w_lora = w_lora.at[:r_decay, :DR].set(lora_full[:r_decay])
    w_lora = w_lora.at[r_decay:r_decay + r_iclr, DR:2 * DR].set(lora_full[r_decay:r_decay + r_iclr])
    w_lora = w_lora.at[r_decay + r_iclr:n_lora, 2 * DR:].set(lora_full[r_decay + r_iclr:])
    mu_p = jnp.pad(a['rwkv_shift_mix'], ((0, 0), (0, pad_cols)))
    x_bf = x.astype(BF16)

    hp = lambda n: a[n].reshape(H, 1, RWKV_HEAD)
    w0_h, a0_h, kk_h, ka_h, gng_h, gnb_h = (hp(n) for n in ('rwkv_w0', 'rwkv_a0', 'rwkv_k_k', 'rwkv_k_a',
                                                             'rwkv_gn_g', 'rwkv_gn_b'))
    rk_h = a['rwkv_r_k'].reshape(H, 1, RWKV_HEAD)
    ln_mem_g, ln_mem_b = a['ln_mem_g'].reshape(1, D), a['ln_mem_b'].reshape(1, D)

    tm_d = _tile(S, (64, 32, 16, 8))
    tm_a = _tile(S, (512, 256, 128, 64))
    rows = _Rows2D(S, tm_d)
    rows_mem = _Rows2D(M, _tile(M, (64, 32, 16, 8)))
    rows_ff = _Rows2D(S, _tile(S, (32, 16, 8)))
    hb = _tile(H, (8, 4, 2, 1))
    heads = _RowsHeads(H, S, _tile(H, (4, 2, 1)), _tile(S, (256, 128, 64)))
    tr_conv = _tile(S, (512, 256, 128, 64, 32))
    cb_conv = _tile(DC, (256, 128))
    fn_lora = make_fn_lora(r_decay, r_iclr, r_gate)
    lane_blk = lambda off, width: off // width

    mem_n, = ew_fwd(rows_mem, _as_bf16(fn_ln), [mem], [ln_mem_g, ln_mem_b], "ln_mem")
    z, *got = matmul(x_bf, w_in_p, 'nn', "mm_z", side=gathers(with_mm_z))
    gathered(with_mm_z, got)
    zs = token_shift_fwd(z, shift_cols, lora_w, zr_w, mu_p, tm_a, "token_shift")
    lora, = ew_fwd(rows, _as_bf16(fn_lora), [(zs, lora_w, lane_blk(3 * DR, lora_w))], [], "lora_act")
    up = matmul(lora, w_lora, 'nn', "mm_lora_up")
    r_h, k_h, v_h = (_to_heads(zs[:, i * DR:(i + 1) * DR]) for i in range(3))
    wl_h, al_h, g_h = (_to_heads(up[:, i * DR:(i + 1) * DR]) for i in range(3))
    pre_rows, pre_pars = [k_h, wl_h, al_h], [w0_h, a0_h, kk_h, ka_h]
    lw_h, k2_h, na_h, b_h = ew_fwd(heads, fn_rwkv_pre, pre_rows, pre_pars, "rwkv_pre")
    o_h, ckpt, *got = scan_fwd(r_h, lw_h, k2_h, v_h, na_h, b_h, hb, "scan_fwd", side=gathers(with_scan_fwd))
    gathered(with_scan_fwd, got)
    post_rows, post_pars = [o_h, r_h, k2_h, v_h, g_h], [gng_h, gnb_h, rk_h]
    or_h, = ew_fwd(heads, _as_bf16(fn_rwkv_post), post_rows, post_pars, "rwkv_post")
    o_r = _from_heads(or_h)

    glu_rows = [(z, DC, lane_blk(c_u, DC)), (z, DC, lane_blk(c_cg, DC))]
    u, = ew_fwd(rows, fn_glu, glu_rows, [], "glu")
    yc = conv_fwd(u, conv_w, a['conv_b'], conv_taps, tr_conv, cb_conv, "conv")
    cln = [a['conv_ln_g'], a['conv_ln_b']]
    o_c, = ew_fwd(rows, _as_bf16(fn_ln_silu), [yc], cln, "conv_ln_silu")

    G = {}
    rows_split = lambda g: g.reshape(N_DEV, g.shape[0] // N_DEV, g.shape[1])

    Sx, recv = {}, {}

    def to_sibling(names):
        return [(G[n], 'scatter_sib') for n in names]

    def pair_sums(names, landed):
        for n, l in zip(names, landed):
            Sx[n] = pair_add(G[n], l, "pair_add_" + n)

    pr = matmul(o_r, Wg['proj_rwkv'], 'nn', "mm_proj_rwkv", b_dev=True)
    pc = matmul(o_c, Wg['proj_conv'], 'nn', "mm_proj_conv", b_dev=True)
    merge_rows = [(z, D, lane_blk(c_gr, D)), (z, D, lane_blk(c_gc, D)), pr, pc]
    merged, = ew_fwd(rows, _as_bf16(fn_merge), merge_rows, [], "merge")
    t1 = matmul(merged, W['w_out'], 'nn', "mm_w_out")
    ln1 = [a['ln1_g'], a['ln1_b']]
    h1, h1_bf = ew_fwd(rows, fn_ln_res_both, [x, t1], ln1, "ln1")

    q = matmul(h1_bf, W['xattn_wq'], 'nn', "mm_q")
    kx = matmul(mem_n, W['xattn_wk'], 'nn', "mm_k")
    vx = matmul(mem_n, W['xattn_wv'], 'nn', "mm_v")
    oa = attn_fwd(q, kx, vx, tm_a, "attn")
    ca = matmul(oa, W['xattn_wo'], 'nn', "mm_wo")
    ln2 = [a['ln2_g'], a['ln2_b']]
    h2, h2_bf = ew_fwd(rows, fn_ln_res_both, [h1, ca], ln2, "ln2")

    f1, *got = matmul(h2_bf, Wg['mlp_w1'], 'nn', "mm_w1", b_dev=True, side=gathers(with_mm_w1))
    gathered(with_mm_w1, got)
    act, = ew_fwd(rows_ff, _as_bf16(fn_relu2), [f1], [], "relu2")
    ff = matmul(act, W['mlp_w2'], 'nn', "mm_w2")
    ln3 = [a['ln3_g'], a['ln3_b']]
    h3, = ew_fwd(rows, fn_ln_res, [h2, ff], ln3, "ln3")
    dh3, loss_rows = loss_head(h3, target, tm_d, "loss")

    dh2a, dff, G['ln3_g'], G['ln3_b'] = ew_bwd(rows, fn_ln_res, [h2, ff], ln3, [dh3], (0, 1), (0, 1), "ln3_bwd",
                                               dr_dtypes=[F32, BF16])
    dact = matmul(dff, W['mlp_w2'], 'nt', "mm_w2_dx")
    G['mlp_w2'] = rows_split(matmul(act, dff, 'tn', "mm_w2_dw", out_dtype=BF16))
    df1, = ew_bwd(rows_ff, fn_relu2, [f1], [], [dact], (0,), (), "relu2_bwd", dr_dtypes=[BF16])
    dh2, *got = matmul(df1, Wg['mlp_w1'], 'nt', "mm_w1_dx", add=dh2a, b_dev=True, side=to_sibling(['mlp_w2']))
    pair_sums(['mlp_w2'], got)
    G['mlp_w1'], recv['mlp_w2'] = matmul(h2_bf, df1, 'tn', "mm_w1_dw", out_dtype=BF16, out_dev=True,
                                         side=[(Sx['mlp_w2'], 'scatter_chips')])

    dh1a, dca, G['ln2_g'], G['ln2_b'] = ew_bwd(rows, fn_ln_res, [h1, ca], ln2, [dh2], (0, 1), (0, 1), "ln2_bwd",
                                               dr_dtypes=[F32, BF16])
    doa, *got = matmul(dca, W['xattn_wo'], 'nt', "mm_wo_dx", side=to_sibling(['mlp_w1']))
    pair_sums(['mlp_w1'], got)
    G['xattn_wo'] = rows_split(matmul(oa, dca, 'tn', "mm_wo_dw", out_dtype=BF16))
    dq, dkx, dvx = attn_bwd(q, kx, vx, doa, tm_a, "attn_bwd")
    dh1, *got = matmul(dq, W['xattn_wq'], 'nt', "mm_q_dx", add=dh1a, side=to_sibling(['xattn_wo']))
    pair_sums(['xattn_wo'], got)
    G['xattn_wq'] = rows_split(matmul(h1_bf, dq, 'tn', "mm_q_dw", out_dtype=BF16))
    G['xattn_wk'] = rows_split(matmul(mem_n, dkx, 'tn', "mm_k_dw", out_dtype=BF16))
    G['xattn_wv'] = rows_split(matmul(mem_n, dvx, 'tn', "mm_v_dw", out_dtype=BF16))
    dmem_k = matmul(dkx, W['xattn_wk'], 'nt', "mm_k_dx")
    dmem_n = matmul(dvx, W['xattn_wv'], 'nt', "mm_v_dx", add=dmem_k)
    G['ln_mem_g'], G['ln_mem_b'] = ew_bwd(rows_mem, fn_ln, [mem], [ln_mem_g, ln_mem_b], [dmem_n], (), (0, 1),
                                          "ln_mem_bwd")

    dxa, dt1, G['ln1_g'], G['ln1_b'] = ew_bwd(rows, fn_ln_res, [x, t1], ln1, [dh1], (0, 1), (0, 1), "ln1_bwd",
                                              dr_dtypes=[F32, BF16])
    xattn_qkv = ['xattn_wq', 'xattn_wk', 'xattn_wv']
    dmerged, *got = matmul(dt1, W['w_out'], 'nt', "mm_w_out_dx", side=to_sibling(xattn_qkv))
    pair_sums(xattn_qkv, got)
    G['w_out'] = rows_split(matmul(merged, dt1, 'tn', "mm_w_out_dw", out_dtype=BF16))
    dzgr, dzgc, dpr, dpc = ew_bwd(rows, fn_merge, merge_rows, [], [dmerged], (0, 1, 2, 3), (), "merge_bwd",
                                  dr_dtypes=[BF16] * 4)
    do_r, *got = matmul(dpr, Wg['proj_rwkv'], 'nt', "mm_proj_rwkv_dx", b_dev=True, side=to_sibling(['w_out']))
    pair_sums(['w_out'], got)
    G['proj_rwkv'] = matmul(o_r, dpr, 'tn', "mm_proj_rwkv_dw", out_dtype=BF16, out_dev=True)
    do_c, *got = matmul(dpc, Wg['proj_conv'], 'nt', "mm_proj_conv_dx", b_dev=True, side=to_sibling(['proj_rwkv']))
    pair_sums(['proj_rwkv'], got)
    G['proj_conv'] = matmul(o_c, dpc, 'tn', "mm_proj_conv_dw", out_dtype=BF16, out_dev=True)
    pair_sums(['proj_conv'], [exchange(G['proj_conv'], 'scatter_sib', "scatter_sib_proj_conv")])

    dyc, G['conv_ln_g'], G['conv_ln_b'] = ew_bwd(rows, fn_ln_silu, [yc], cln, [do_c], (0,), (0, 1), "conv_ln_silu_bwd")
    du, dconv_w, G['conv_b'] = conv_bwd(u, dyc, conv_w, conv_taps, tr_conv, cb_conv, "conv_bwd")
    dzu, dzcg = ew_bwd(rows, fn_glu, glu_rows, [], [du], (0, 1), (), "glu_bwd", dr_dtypes=[BF16] * 2)

    do_r_h = _to_heads(do_r)
    do_h, dr1_h, dk2a_h, dv1_h, dg_h, dgng, dgnb, drk = ew_bwd(
        heads, fn_rwkv_post, post_rows, post_pars, [do_r_h], (0, 1, 2, 3, 4), (0, 1, 2), "rwkv_post_bwd",
        dr_dtypes=[F32, F32, F32, F32, BF16])
    dr_h, dlw_h, dk2_h, dv_h, dna_h, db_h, *got = scan_bwd(
        r_h, lw_h, k2_h, v_h, na_h, b_h, ckpt, do_h, dr1_h, dk2a_h, dv1_h, hb, "scan_bwd",
        side=[(Sx[n], 'scatter_chips') for n in with_scan_bwd])
    recv.update(zip(with_scan_bwd, got))
    dk_h, dwl_h, dal_h, dw0, da0, dkk, dka = ew_bwd(
        heads, fn_rwkv_pre, pre_rows, pre_pars, [dlw_h, dk2_h, dna_h, db_h], (0, 1, 2), (0, 1, 2, 3), "rwkv_pre_bwd",
        dr_dtypes=[F32, BF16, BF16])
    G['rwkv_w0'], G['rwkv_a0'], G['rwkv_k_k'], G['rwkv_k_a'] = (t.reshape(1, DR) for t in (dw0, da0, dkk, dka))
    G['rwkv_gn_g'], G['rwkv_gn_b'] = dgng.reshape(1, DR), dgnb.reshape(1, DR)
    G['rwkv_r_k'] = drk.reshape(1, H, RWKV_HEAD)
    dup = jnp.concatenate([_from_heads(dwl_h), _from_heads(dal_h), _from_heads(dg_h)], axis=1)
    dlora = matmul(dup, w_lora, 'nt', "mm_lora_up_dx")
    dw_lora = matmul(lora, dup, 'tn', "mm_lora_up_dw", out_dtype=BF16)
    d_lora_stack = jnp.concatenate([dw_lora[:r_decay, :DR], dw_lora[r_decay:r_decay + r_iclr, DR:2 * DR],
                                    dw_lora[r_decay + r_iclr:n_lora, 2 * DR:]], axis=0)
    dzs_lora, = ew_bwd(rows, fn_lora, [(zs, lora_w, lane_blk(3 * DR, lora_w))], [], [dlora], (0,), (), "lora_act_bwd")
    dzs = jnp.concatenate([_from_heads(dr_h), _from_heads(dk_h), _from_heads(dv_h), dzs_lora], axis=1)
    dzr, dmu = token_shift_bwd(z, dzs, shift_cols, lora_w, zr_w, mu_p, tm_a, "token_shift_bwd")
    G['rwkv_shift_mix'] = dmu[:, :n_rwkv]
    dz = jnp.concatenate([dzgr, dzgc, dzr[:, :3 * DR].astype(BF16), dzu, dzcg, dzr[:, 3 * DR:].astype(BF16)], axis=1)
    dw_in_p = matmul(x_bf, dz, 'tn', "mm_z_dw", out_dtype=BF16)
    G['w_in'] = _cols_split(jnp.concatenate([dw_in_p[:, c_rkv:c_u], dw_in_p[:, c_lora:c_lora + n_lora],
                                             dw_in_p[:, c_u:c_lora], dw_in_p[:, :c_rkv]], axis=1))
    pair_sums(['w_in'], [exchange(G['w_in'], 'scatter_sib', "scatter_sib_w_in")])
    grad_x, recv['w_in'] = matmul(dz, w_in_p, 'nt', "mm_z_dx", add=dxa, side=[(Sx['w_in'], 'scatter_chips')])

    out = {}

    def update(n, parts, w, m, v):
        res = adamw(parts, w, m, v, "adamw_" + n)
        out[n] = [t[:conv_taps].reshape(a[n].shape) if n == 'conv_w' else t.reshape(a[n].shape) for t in res]

    for n in ['w_in', 'mlp_w2'] + with_scan_bwd:
        update(n, recv[n], shard(n), shard(n, 'm_'), shard(n, 'v_'))
    recv_lora = exchange(_cols_split(d_lora_stack), 'scatter', "scatter_lora")
    lo = 0
    for n, r_ in (('rwkv_w_up', r_decay), ('rwkv_a_up', r_iclr), ('rwkv_g_up', r_gate)):
        update(n, recv_lora[:, lo:lo + r_], shard(n), shard(n, 'm_'), shard(n, 'v_'))
        lo += r_
    recv_conv = exchange(_cols_split(dconv_w.astype(BF16)), 'scatter', "scatter_conv_w")
    update('conv_w', recv_conv, shard('conv_w'), shard('conv_w', 'm_'), shard('conv_w', 'v_'))

    small_sizes = [a[n].size for n in SMALL]
    n_small = sum(small_sizes) + 1
    n_pack = _round_up(n_small, SUBLANES * LANES)
    pack = lambda get, last: jnp.pad(jnp.concatenate([get(n).reshape(-1) for n in SMALL] + [last]),
                                     (0, n_pack - n_small)).reshape(n_pack // LANES, LANES)
    zero1 = jnp.zeros((1,), F32)
    g_pack = pack(lambda n: G[n], loss_rows[0, :1])
    parts_small = exchange(g_pack, 'gather', "gather_small")
    res = adamw(parts_small, pack(lambda n: a[n], zero1), pack(lambda n: a['m_' + n], zero1),
                pack(lambda n: a['v_' + n], zero1), "adamw_small")
    res = [t.reshape(-1) for t in res]
    o_ = 0
    for n, sz in zip(SMALL, small_sizes):
        out[n] = [t[o_:o_ + sz].reshape(a[n].shape) for t in res]
        o_ += sz
    loss = res[0][o_]

    return (loss, grad_x[None], *[out[n][0] for n in WEIGHTS], *[out[n][1] for n in WEIGHTS],
            *[out[n][2] for n in WEIGHTS], *[out[n][3] for n in WEIGHTS])


def kernel(x, mem, w_in, rwkv_shift_mix, rwkv_w0, rwkv_w_up, rwkv_a0, rwkv_a_up, rwkv_g_up, rwkv_k_k, rwkv_k_a, rwkv_r_k, rwkv_gn_g, rwkv_gn_b, conv_w, conv_b, conv_ln_g, conv_ln_b, proj_rwkv, proj_conv, w_out, ln1_g, ln1_b, ln_mem_g, ln_mem_b, xattn_wq, xattn_wk, xattn_wv, xattn_wo, ln2_g, ln2_b, mlp_w1, mlp_w2, ln3_g, ln3_b, loss_target, m_w_in, m_rwkv_shift_mix, m_rwkv_w0, m_rwkv_w_up, m_rwkv_a0, m_rwkv_a_up, m_rwkv_g_up, m_rwkv_k_k, m_rwkv_k_a, m_rwkv_r_k, m_rwkv_gn_g, m_rwkv_gn_b, m_conv_w, m_conv_b, m_conv_ln_g, m_conv_ln_b, m_proj_rwkv, m_proj_conv, m_w_out, m_ln1_g, m_ln1_b, m_ln_mem_g, m_ln_mem_b, m_xattn_wq, m_xattn_wk, m_xattn_wv, m_xattn_wo, m_ln2_g, m_ln2_b, m_mlp_w1, m_mlp_w2, m_ln3_g, m_ln3_b, v_w_in, v_rwkv_shift_mix, v_rwkv_w0, v_rwkv_w_up, v_rwkv_a0, v_rwkv_a_up, v_rwkv_g_up, v_rwkv_k_k, v_rwkv_k_a, v_rwkv_r_k, v_rwkv_gn_g, v_rwkv_gn_b, v_conv_w, v_conv_b, v_conv_ln_g, v_conv_ln_b, v_proj_rwkv, v_proj_conv, v_w_out, v_ln1_g, v_ln1_b, v_ln_mem_g, v_ln_mem_b, v_xattn_wq, v_xattn_wk, v_xattn_wv, v_xattn_wo, v_ln2_g, v_ln2_b, v_mlp_w1, v_mlp_w2, v_ln3_g, v_ln3_b):
    return _step(dict(locals()))
```

```python
import functools
import math

import jax
import jax.numpy as jnp
from jax import lax
from jax.experimental import pallas as pl
from jax.experimental.pallas import tpu as pltpu

F32 = jnp.float32
BF16 = jnp.bfloat16

N_DEV = 8
RWKV_HEAD = 64
SCAN_CHUNK = 64
XATTN_HEADS = 4
CONV_HALO = 32
LN_EPS = 1e-5
GN_EPS = 64e-5
ALPHA = float(2.0 ** 0.25)
ADAM_LR, ADAM_B1, ADAM_B2, ADAM_EPS, ADAM_WD, ADAM_STEP = 0.001, 0.9, 0.999, 1e-08, 0.01, 10
LANES = 128
SUBLANES = 8
VMEM_LIMIT = 56 * 1024 * 1024
ADAM_BLOCK_ELEMS = 256 * 1024

WEIGHTS = ['w_in', 'rwkv_shift_mix', 'rwkv_w0', 'rwkv_w_up', 'rwkv_a0', 'rwkv_a_up', 'rwkv_g_up', 'rwkv_k_k',
           'rwkv_k_a', 'rwkv_r_k', 'rwkv_gn_g', 'rwkv_gn_b', 'conv_w', 'conv_b', 'conv_ln_g', 'conv_ln_b',
           'proj_rwkv', 'proj_conv', 'w_out', 'ln1_g', 'ln1_b', 'ln_mem_g', 'ln_mem_b', 'xattn_wq', 'xattn_wk',
           'xattn_wv', 'xattn_wo', 'ln2_g', 'ln2_b', 'mlp_w1', 'mlp_w2', 'ln3_g', 'ln3_b']
COL_SHARDED = ['w_in', 'rwkv_w_up', 'rwkv_a_up', 'rwkv_g_up', 'conv_w', 'proj_rwkv', 'proj_conv', 'mlp_w1']
ROW_SHARDED = ['w_out', 'xattn_wq', 'xattn_wk', 'xattn_wv', 'xattn_wo', 'mlp_w2']
BIG = ['w_in', 'rwkv_w_up', 'rwkv_a_up', 'rwkv_g_up', 'conv_w', 'proj_rwkv', 'proj_conv', 'w_out', 'xattn_wq',
       'xattn_wk', 'xattn_wv', 'xattn_wo', 'mlp_w1', 'mlp_w2']
SMALL = [w for w in WEIGHTS if w not in BIG]


def _cparams(dims):
    return pltpu.CompilerParams(dimension_semantics=dims, vmem_limit_bytes=VMEM_LIMIT)


def _tile(n, cands):
    for c in cands:
        if n % c == 0:
            return c
    return n


N_CHIP = N_DEV // 2
SEMS_PER_EXCHANGE = N_DEV + 2
CHIP_XORS = (2, 4, 6)


def _exchange_shape(src, kind):
    return {'gather': (N_DEV,) + src.shape, 'gather2': (N_DEV,) + src.shape, 'scatter': src.shape,
            'scatter_sib': (N_CHIP,) + src.shape[1:], 'scatter_chips': src.shape}[kind]


def _exchange_copies(e, src_ref, out_ref, kind, send_sems, recv_sems, local_sems):
    x, y, c = lax.axis_index("x"), lax.axis_index("y"), lax.axis_index("c")
    me, chip = 4 * x + 2 * y + c, 2 * x + y
    sibling = (x, y, 1 - c)
    base = e * SEMS_PER_EXCHANGE

    def remote(src, dst, idx, dev):
        return pltpu.make_async_remote_copy(src_ref=src, dst_ref=dst, send_sem=send_sems.at[base + idx],
                                            recv_sem=recv_sems.at[base + idx], device_id=dev,
                                            device_id_type=pl.DeviceIdType.MESH)

    def peer(k):
        return x ^ ((k >> 2) & 1), y ^ ((k >> 1) & 1), c ^ (k & 1)

    first, second = [], []
    if kind in ('gather', 'gather2'):
        first.append(pltpu.make_async_copy(src_ref, out_ref.at[me], local_sems.at[e]))
        for k in (range(1, N_DEV) if kind == 'gather' else (1,) + CHIP_XORS):
            first.append(remote(src_ref, out_ref.at[me], k - 1, peer(k)))
        if kind == 'gather2':
            for i, k in enumerate(CHIP_XORS):
                second.append(remote(out_ref.at[me ^ k], out_ref.at[me ^ k], N_DEV - 1 + i, sibling))
    elif kind == 'scatter':
        first.append(pltpu.make_async_copy(src_ref.at[me], out_ref.at[me], local_sems.at[e]))
        for k in range(1, N_DEV):
            px, py, pc = peer(k)
            first.append(remote(src_ref.at[4 * px + 2 * py + pc], out_ref.at[me], k - 1, (px, py, pc)))
    elif kind == 'scatter_sib':
        for q in range(N_CHIP):
            first.append(remote(src_ref.at[2 * q + 1 - c], out_ref.at[q], q, sibling))
    else:
        assert kind == 'scatter_chips', kind
        first.append(pltpu.make_async_copy(src_ref.at[chip], out_ref.at[chip], local_sems.at[e]))
        for k in CHIP_XORS:
            first.append(remote(src_ref.at[chip ^ (k >> 1)], out_ref.at[chip], k - 1, peer(k)))
    return first, second


class _Side:
    def __init__(self, items):
        self.items = list(items or [])
        self.n = len(self.items)
        any_spec = pl.BlockSpec(memory_space=pl.ANY)
        self.srcs = [s for s, _ in self.items]
        self.in_specs = [any_spec] * self.n
        self.out_specs = [any_spec] * self.n
        self.out_shapes = [jax.ShapeDtypeStruct(_exchange_shape(s, kind), s.dtype) for s, kind in self.items]
        self.scratch = [pltpu.SemaphoreType.DMA((self.n * SEMS_PER_EXCHANGE,)),
                        pltpu.SemaphoreType.DMA((self.n * SEMS_PER_EXCHANGE,)),
                        pltpu.SemaphoreType.DMA((self.n,))] if self.n else []

    def _copies(self, src_refs, out_refs, sems):
        both = [_exchange_copies(e, src_refs[e], out_refs[e], kind, *sems) for e, (_, kind) in enumerate(self.items)]
        return [cp for f, _ in both for cp in f], [cp for _, s in both for cp in s]

    def start(self, src_refs, out_refs, sems):
        for cp in self._copies(src_refs, out_refs, sems)[0]:
            cp.start()

    def finish(self, src_refs, out_refs, sems):
        first, second = self._copies(src_refs, out_refs, sems)
        for cp in first:
            cp.wait()
        for cp in second:
            cp.start()
        for cp in second:
            cp.wait()

    def run(self, first, last, src_refs, out_refs, sems):
        if not self.n:
            return lambda: None
        pl.when(first)(lambda: self.start(src_refs, out_refs, sems))
        return lambda: pl.when(last)(lambda: self.finish(src_refs, out_refs, sems))


def exchange(src, kind, name):
    side = _Side([(src, kind)])

    def body(src_ref, out_ref, *sems):
        side.start([src_ref], [out_ref], sems)
        side.finish([src_ref], [out_ref], sems)

    return pl.pallas_call(body, name=name, in_specs=side.in_specs, out_specs=side.out_specs[0],
                          out_shape=side.out_shapes[0], scratch_shapes=side.scratch)(src)


def pair_add(parts, landed, name):
    _, R, C = parts.shape
    tr = R
    if R * C > ADAM_BLOCK_ELEMS:
        tr = _tile(R, [t for t in (512, 256, 128, 64, 32, 16) if t * C <= ADAM_BLOCK_ELEMS])

    def body(core_ref, p_ref, l_ref, o_ref):
        o_ref[...] = (p_ref[...].astype(F32) + l_ref[...].astype(F32)).astype(BF16)

    blk = pl.BlockSpec((None, tr, C), lambda q, i, core_ref: (q, i, 0))
    mine = pl.BlockSpec((None, None, tr, C), lambda q, i, core_ref: (q, core_ref[0], i, 0))
    return pl.pallas_call(
        body, name=name,
        grid_spec=pltpu.PrefetchScalarGridSpec(num_scalar_prefetch=1, grid=(N_CHIP, R // tr), in_specs=[mine, blk],
                                               out_specs=blk),
        out_shape=jax.ShapeDtypeStruct((N_CHIP, R, C), BF16),
        compiler_params=_cparams(("parallel", "parallel")))(
            lax.axis_index("c").astype(jnp.int32).reshape(1), parts.reshape(N_CHIP, 2, R, C), landed)


def matmul(a, b, mode, name, add=None, out_dtype=F32, b_dev=False, out_dev=False, side=None):
    side = _Side(side)
    if b_dev:
        assert mode in ('nn', 'nt') and b.shape[0] == N_DEV
        b_rows, b_cols = b.shape[1], N_DEV * b.shape[2]
    else:
        b_rows, b_cols = b.shape
    if mode == 'nn':
        (M, K), (K2, N) = a.shape, (b_rows, b_cols)
    elif mode == 'nt':
        (M, K), (N, K2) = a.shape, (b_rows, b_cols)
    else:
        (K, M), (K2, N) = a.shape, (b_rows, b_cols)
    assert K == K2, (a.shape, b.shape, mode)
    n_unit = N // N_DEV if (out_dev or (b_dev and mode == 'nn')) else N
    k_unit = K // N_DEV if (b_dev and mode == 'nt') else K
    tn = _tile(n_unit, (1024, 512, 256, 128))
    tm = _tile(M, (1024, 512, 256, 128) if tn >= 1024 else (2048, 1024, 512, 256, 128))
    tk = _tile(k_unit, (2048, 1024, 512, 256, 128))
    nk = K // tk
    nb, kb = n_unit // tn, k_unit // tk
    dims = {'nn': ((1,), (0,)), 'nt': ((1,), (1,)), 'tn': ((0,), (0,))}[mode]

    n_in = 2 + (add is not None)
    grid = (M // tm, N // tn, nk)

    def body(*refs):
        a_ref, b_ref = refs[:2]
        add_ref = refs[2] if add is not None else None
        side_src = refs[n_in:n_in + side.n]
        o_ref = refs[n_in + side.n]
        side_out = refs[n_in + side.n + 1:n_in + 2 * side.n + 1]
        acc_ref = refs[n_in + 2 * side.n + 1]
        sems = refs[n_in + 2 * side.n + 2:]
        i, j, k = pl.program_id(0), pl.program_id(1), pl.program_id(2)
        finish = side.run((i == 0) & (j == 0) & (k == 0), (i == grid[0] - 1) & (j == grid[1] - 1) & (k == nk - 1),
                          side_src, side_out, sems)

        @pl.when(k == 0)
        def _():
            acc_ref[...] = jnp.zeros_like(acc_ref)

        acc_ref[...] += lax.dot_general(a_ref[...].astype(BF16), b_ref[...].astype(BF16), (dims, ((), ())),
                                        preferred_element_type=F32)

        @pl.when(k == nk - 1)
        def _():
            r = acc_ref[...]
            if add is not None:
                r = r + add_ref[...]
            o_ref[...] = r.astype(out_dtype)

        finish()

    if mode == 'nn':
        a_spec = pl.BlockSpec((tm, tk), lambda i, j, k: (i, k))
        b_spec = (pl.BlockSpec((None, tk, tn), lambda i, j, k: (j // nb, k, j % nb)) if b_dev
                  else pl.BlockSpec((tk, tn), lambda i, j, k: (k, j)))
    elif mode == 'nt':
        a_spec = pl.BlockSpec((tm, tk), lambda i, j, k: (i, k))
        b_spec = (pl.BlockSpec((None, tn, tk), lambda i, j, k: (k // kb, j, k % kb)) if b_dev
                  else pl.BlockSpec((tn, tk), lambda i, j, k: (j, k)))
    else:
        a_spec = pl.BlockSpec((tk, tm), lambda i, j, k: (k, i))
        b_spec = pl.BlockSpec((tk, tn), lambda i, j, k: (k, j))
    add_spec = pl.BlockSpec((tm, tn), lambda i, j, k: (i, j))
    if out_dev:
        o_spec = pl.BlockSpec((None, tm, tn), lambda i, j, k: (j // nb, i, j % nb))
        o_shape = (N_DEV, M, N // N_DEV)
    else:
        o_spec, o_shape = add_spec, (M, N)
    in_specs = [a_spec, b_spec] + ([add_spec] if add is not None else [])
    ops = (a, b) + ((add,) if add is not None else ())
    sem = ("arbitrary",) * 3 if side.n else ("parallel", "parallel", "arbitrary")
    res = pl.pallas_call(
        body, name=name, grid=grid, in_specs=in_specs + side.in_specs, out_specs=[o_spec] + side.out_specs,
        out_shape=[jax.ShapeDtypeStruct(o_shape, out_dtype)] + side.out_shapes,
        scratch_shapes=[pltpu.VMEM((tm, tn), F32)] + side.scratch,
        compiler_params=_cparams(sem))(*ops, *side.srcs)
    return tuple(res) if side.n else res[0]


class _Rows2D:
    def __init__(self, n_rows, tm):
        self.n, self.tm = n_rows, tm
        self.grid = (1, n_rows // tm)

    def row(self, e):
        if isinstance(e, tuple):
            arr, width, cb = e
            return arr, (self.tm, width), pl.BlockSpec((self.tm, width), lambda g, i, cb=cb: (i, cb))
        return e, (self.tm, e.shape[1]), pl.BlockSpec((self.tm, e.shape[1]), lambda g, i: (i, 0))

    def par(self, p):
        return pl.BlockSpec(p.shape, lambda g, i: (0,) * p.ndim)

    def out(self, blk):
        return (self.n, blk[1]), pl.BlockSpec((self.tm, blk[1]), lambda g, i: (i, 0))


class _RowsHeads:
    def __init__(self, n_heads, n_rows, hb, ts):
        self.h, self.n, self.hb, self.ts = n_heads, n_rows, hb, ts
        self.grid = (n_heads // hb, n_rows // ts)

    def row(self, e):
        blk = (self.hb, self.ts, e.shape[2])
        return e, blk, pl.BlockSpec(blk, lambda g, i: (g, i, 0))

    def par(self, p):
        return pl.BlockSpec((self.hb, 1, p.shape[2]), lambda g, i: (g, 0, 0))

    def out(self, blk):
        return (self.h, self.n, blk[2]), pl.BlockSpec(blk, lambda g, i: (g, i, 0))


def _par_block(lay, p):
    return lay.par(p).block_shape


def ew_fwd(lay, fn, rows, params, name):
    rr = [lay.row(e) for e in rows]
    arrs = [r[0] for r in rr]
    blk_avals = [jax.ShapeDtypeStruct(r[1], r[0].dtype) for r in rr]
    par_avals = [jax.ShapeDtypeStruct(_par_block(lay, p), p.dtype) for p in params]
    outs = jax.eval_shape(fn, *blk_avals, *par_avals)
    out_full = [lay.out(o.shape) for o in outs]
    nr, npar = len(rows), len(params)

    def body(*refs):
        vals = [r[...] for r in refs[:nr + npar]]
        res = fn(*vals)
        for ref, v in zip(refs[nr + npar:], res):
            ref[...] = v.astype(ref.dtype)

    return pl.pallas_call(
        body, name=name, grid=lay.grid,
        in_specs=[r[2] for r in rr] + [lay.par(p) for p in params],
        out_specs=[o[1] for o in out_full],
        out_shape=[jax.ShapeDtypeStruct(o[0], a.dtype) for o, a in zip(out_full, outs)],
        compiler_params=_cparams(("parallel", "parallel")))(*arrs, *params)


def ew_bwd(lay, fn, rows, params, cots, wrt_rows, wrt_pars, name, dr_dtypes=None):
    rr = [lay.row(e) for e in rows]
    cc = [lay.row(e) for e in cots]
    nr, npar, nc = len(rows), len(params), len(cots)
    n_dr = len(wrt_rows)
    dr_full = [lay.out(rr[i][1]) for i in wrt_rows]
    dr_dtypes = dr_dtypes or [F32] * n_dr

    def body(*refs):
        rv = [r[...] for r in refs[:nr]]
        pv = [r[...] for r in refs[nr:nr + npar]]
        cv = tuple(r[...] for r in refs[nr + npar:nr + npar + nc])
        outs = refs[nr + npar + nc:]

        def f(*wrt):
            r2, p2 = list(rv), list(pv)
            for idx, v in zip(wrt_rows, wrt[:n_dr]):
                r2[idx] = v
            for idx, v in zip(wrt_pars, wrt[n_dr:]):
                p2[idx] = v
            return fn(*r2, *p2)

        _, vjp = jax.vjp(f, *[rv[i] for i in wrt_rows], *[pv[i] for i in wrt_pars])
        g = vjp(cv)
        for ref, v in zip(outs[:n_dr], g[:n_dr]):
            ref[...] = v.astype(ref.dtype)
        if wrt_pars:
            @pl.when(pl.program_id(1) == 0)
            def _():
                for ref in outs[n_dr:]:
                    ref[...] = jnp.zeros_like(ref)

            for ref, v in zip(outs[n_dr:], g[n_dr:]):
                ref[...] += v

    return pl.pallas_call(
        body, name=name, grid=lay.grid,
        in_specs=[r[2] for r in rr] + [lay.par(p) for p in params] + [c[2] for c in cc],
        out_specs=[o[1] for o in dr_full] + [lay.par(params[i]) for i in wrt_pars],
        out_shape=[jax.ShapeDtypeStruct(o[0], dt) for o, dt in zip(dr_full, dr_dtypes)]
        + [jax.ShapeDtypeStruct(params[i].shape, F32) for i in wrt_pars],
        compiler_params=_cparams(("parallel", "arbitrary")))(
            *[r[0] for r in rr], *params, *[c[0] for c in cc])


def _sigmoid(x):
    return 1.0 / (1.0 + jnp.exp(-x))


def _softplus(x):
    return jnp.maximum(x, 0.0) + jnp.log(1.0 + jnp.exp(-jnp.abs(x)))


def _layer_norm(x, g, b, eps):
    mu = jnp.mean(x, -1, keepdims=True)
    xc = x - mu
    var = jnp.mean(xc * xc, -1, keepdims=True)
    return xc * lax.rsqrt(var + eps) * g + b


def _as_bf16(fn):
    return lambda *args: tuple(o.astype(BF16) for o in fn(*args))


def fn_ln(x, g, b):
    return (_layer_norm(x, g, b, LN_EPS),)


def fn_ln_res(h, t, g, b):
    return (_layer_norm(ALPHA * h + t, g, b, LN_EPS),)


def fn_ln_res_both(h, t, g, b):
    y, = fn_ln_res(h, t, g, b)
    return y, y.astype(BF16)


def make_fn_lora(r_decay, r_iclr, r_gate):
    def fn(z):
        lane = lax.broadcasted_iota(jnp.int32, z.shape, 1)
        out = jnp.where(lane < r_decay, jnp.tanh(z), z)
        out = jnp.where(lane >= r_decay + r_iclr, _sigmoid(z), out)
        return (jnp.where(lane < r_decay + r_iclr + r_gate, out, 0.0),)
    return fn


def fn_rwkv_pre(k, wl, al, w0, a0, k_k, k_a):
    w = -_softplus(-(w0 + wl)) - 0.5
    lw = -jnp.exp(w)
    a = _sigmoid(a0 + al)
    kk = k * k_k
    kk = kk / jnp.maximum(jnp.sqrt(jnp.sum(kk * kk, -1, keepdims=True)), 1e-12)
    k2 = k * (1.0 + (a - 1.0) * k_a)
    return lw, k2, -kk, kk * a


def fn_rwkv_post(o, r, k2, v, g, gn_g, gn_b, r_k):
    mu = jnp.mean(o, -1, keepdims=True)
    oc = o - mu
    var = jnp.mean(oc * oc, -1, keepdims=True)
    y = oc * lax.rsqrt(var + GN_EPS) * gn_g + gn_b
    y = y + jnp.sum(r * k2 * r_k, -1, keepdims=True) * v
    return (y * g,)


def fn_glu(zu, zg):
    return (zu * _sigmoid(zg),)


def fn_ln_silu(y, g, b):
    n = _layer_norm(y, g, b, LN_EPS)
    return (n * _sigmoid(n),)


def fn_merge(zgr, zgc, pr, pc):
    return (_sigmoid(zgr) * pr + _sigmoid(zgc) * pc,)


def fn_relu2(f):
    r = jnp.maximum(f, 0.0)
    return (r * r,)


def _dot_nn(a, b):
    return lax.dot_general(a.astype(BF16), b.astype(BF16), (((1,), (0,)), ((), ())), preferred_element_type=F32)


def _dot_nt(a, b):
    return lax.dot_general(a.astype(BF16), b.astype(BF16), (((1,), (1,)), ((), ())), preferred_element_type=F32)


def _dot_tn(a, b):
    return lax.dot_general(a.astype(BF16), b.astype(BF16), (((0,), (0,)), ((), ())), preferred_element_type=F32)


def _softmax_rows(s):
    s = s - jnp.max(s, -1, keepdims=True)
    e = jnp.exp(s)
    return e / jnp.sum(e, -1, keepdims=True)


def attn_fwd(q, kx, vx, tm, name):
    S, D = q.shape
    M = kx.shape[0]
    dh = D // XATTN_HEADS
    scale = dh ** -0.5

    def body(q_ref, k_ref, v_ref, o_ref):
        p = _softmax_rows(_dot_nt(q_ref[...], k_ref[...]) * scale)
        o_ref[...] = _dot_nn(p, v_ref[...]).astype(BF16)

    row = pl.BlockSpec((tm, dh), lambda h, i: (i, h))
    kv = pl.BlockSpec((M, dh), lambda h, i: (0, h))
    return pl.pallas_call(body, name=name, grid=(XATTN_HEADS, S // tm), in_specs=[row, kv, kv], out_specs=row,
                          out_shape=jax.ShapeDtypeStruct((S, D), BF16),
                          compiler_params=_cparams(("parallel", "parallel")))(q, kx, vx)


def attn_bwd(q, kx, vx, do, tm, name):
    S, D = q.shape
    M = kx.shape[0]
    dh = D // XATTN_HEADS
    scale = dh ** -0.5

    def body(q_ref, k_ref, v_ref, do_ref, dq_ref, dk_ref, dv_ref):
        qb, kb, dob = q_ref[...], k_ref[...], do_ref[...]
        p = _softmax_rows(_dot_nt(qb, kb) * scale)
        dp = _dot_nt(dob, v_ref[...])
        ds = p * (dp - jnp.sum(dp * p, -1, keepdims=True)) * scale
        dq_ref[...] = _dot_nn(ds, kb).astype(BF16)

        @pl.when(pl.program_id(1) == 0)
        def _():
            dk_ref[...] = jnp.zeros_like(dk_ref)
            dv_ref[...] = jnp.zeros_like(dv_ref)

        dk_ref[...] += _dot_tn(ds, qb)
        dv_ref[...] += _dot_tn(p, dob)

    row = pl.BlockSpec((tm, dh), lambda h, i: (i, h))
    kv = pl.BlockSpec((M, dh), lambda h, i: (0, h))
    return pl.pallas_call(
        body, name=name, grid=(XATTN_HEADS, S // tm), in_specs=[row, kv, kv, row], out_specs=[row, kv, kv],
        out_shape=[jax.ShapeDtypeStruct((S, D), BF16), jax.ShapeDtypeStruct((M, D), F32), jax.ShapeDtypeStruct((M, D), F32)],
        compiler_params=_cparams(("parallel", "arbitrary")))(q, kx, vx, do)


def _shift_down(blk, halo_last_row, first_block):
    rolled = pltpu.roll(blk, 1, 0)
    row = lax.broadcasted_iota(jnp.int32, blk.shape, 0)
    top = jnp.where(first_block, 0.0, halo_last_row)
    return jnp.where(row == 0, top, rolled)


def _shift_up(blk, halo_first_row, last_block):
    n = blk.shape[0]
    rolled = pltpu.roll(blk, n - 1, 0)
    row = lax.broadcasted_iota(jnp.int32, blk.shape, 0)
    bot = jnp.where(last_block, 0.0, halo_first_row)
    return jnp.where(row == n - 1, bot, rolled)


def _zcol(j, cols):
    n_first, first, second = cols
    return jnp.where(j < n_first, first + j, second + j - n_first)


def token_shift_fwd(z, cols, cw, width, mu, tm, name):
    S = z.shape[0]
    hb = tm // SUBLANES

    def body(z_ref, halo_ref, mu_ref, o_ref):
        zb = z_ref[...]
        prev = _shift_down(zb, halo_ref[SUBLANES - 1:SUBLANES, :], pl.program_id(1) == 0)
        o_ref[...] = zb + (prev - zb) * mu_ref[...]

    return pl.pallas_call(
        body, name=name, grid=(width // cw, S // tm),
        in_specs=[pl.BlockSpec((tm, cw), lambda j, i: (i, _zcol(j, cols))),
                  pl.BlockSpec((SUBLANES, cw), lambda j, i: (jnp.maximum(i * hb - 1, 0), _zcol(j, cols))),
                  pl.BlockSpec((1, cw), lambda j, i: (0, j))],
        out_specs=pl.BlockSpec((tm, cw), lambda j, i: (i, j)),
        out_shape=jax.ShapeDtypeStruct((S, width), F32),
        compiler_params=_cparams(("parallel", "parallel")))(z, z, mu)


def token_shift_bwd(z, dzs, cols, cw, width, mu, tm, name):
    S = z.shape[0]
    hb = tm // SUBLANES
    nblk = S // tm
    last8 = S // SUBLANES - 1

    def body(z_ref, zh_ref, d_ref, dh_ref, mu_ref, dz_ref, dmu_ref):
        i = pl.program_id(1)
        zb, db, m = z_ref[...], d_ref[...], mu_ref[...]
        prev = _shift_down(zb, zh_ref[SUBLANES - 1:SUBLANES, :], i == 0)
        dm = db * m
        nxt = _shift_up(dm, dh_ref[0:1, :] * m, i == nblk - 1)
        dz_ref[...] = db - dm + nxt

        @pl.when(i == 0)
        def _():
            dmu_ref[...] = jnp.zeros_like(dmu_ref)

        dmu_ref[...] += jnp.sum(db * (prev - zb), 0, keepdims=True)

    return pl.pallas_call(
        body, name=name, grid=(width // cw, nblk),
        in_specs=[pl.BlockSpec((tm, cw), lambda j, i: (i, _zcol(j, cols))),
                  pl.BlockSpec((SUBLANES, cw), lambda j, i: (jnp.maximum(i * hb - 1, 0), _zcol(j, cols))),
                  pl.BlockSpec((tm, cw), lambda j, i: (i, j)),
                  pl.BlockSpec((SUBLANES, cw), lambda j, i: (jnp.minimum((i + 1) * hb, last8), j)),
                  pl.BlockSpec((1, cw), lambda j, i: (0, j))],
        out_specs=[pl.BlockSpec((tm, cw), lambda j, i: (i, j)), pl.BlockSpec((1, cw), lambda j, i: (0, j))],
        out_shape=[jax.ShapeDtypeStruct((S, width), F32), jax.ShapeDtypeStruct((1, width), F32)],
        compiler_params=_cparams(("parallel", "arbitrary")))(z, z, dzs, dzs, mu)


def conv_fwd(u, w, b, width, tr, cb, name):
    S, C = u.shape
    hb = tr // CONV_HALO

    def body(u_ref, h_ref, w_ref, b_ref, y_ref):
        i = pl.program_id(1)
        halo = jnp.where(i == 0, 0.0, h_ref[...])
        win = jnp.concatenate([halo, u_ref[...]], axis=0)
        acc = jnp.zeros((tr, cb), F32) + b_ref[...]
        for d in range(width):
            sh = win if d == 0 else pltpu.roll(win, d, 0)
            acc = acc + sh[CONV_HALO:, :] * w_ref[width - 1 - d:width - d, :]
        y_ref[...] = acc

    return pl.pallas_call(
        body, name=name, grid=(C // cb, S // tr),
        in_specs=[pl.BlockSpec((tr, cb), lambda j, i: (i, j)),
                  pl.BlockSpec((CONV_HALO, cb), lambda j, i: (jnp.maximum(i * hb - 1, 0), j)),
                  pl.BlockSpec((CONV_HALO, cb), lambda j, i: (0, j)),
                  pl.BlockSpec((1, cb), lambda j, i: (0, j))],
        out_specs=pl.BlockSpec((tr, cb), lambda j, i: (i, j)),
        out_shape=jax.ShapeDtypeStruct((S, C), F32), compiler_params=_cparams(("parallel", "parallel")))(u, u, w, b)


def conv_bwd(u, dy, w, width, tr, cb, name):
    S, C = u.shape
    hb = tr // CONV_HALO
    nblk = S // tr
    last = S // CONV_HALO - 1

    def body(u_ref, uh_ref, d_ref, dh_ref, w_ref, du_ref, dw_ref, db_ref):
        i = pl.program_id(1)
        dyb = d_ref[...]
        uwin = jnp.concatenate([jnp.where(i == 0, 0.0, uh_ref[...]), u_ref[...]], axis=0)
        dwin = jnp.concatenate([dyb, jnp.where(i == nblk - 1, 0.0, dh_ref[...])], axis=0)

        @pl.when(i == 0)
        def _():
            dw_ref[...] = jnp.zeros_like(dw_ref)
            db_ref[...] = jnp.zeros_like(db_ref)

        acc = jnp.zeros((tr, cb), F32)
        for d in range(width):
            tap = width - 1 - d
            dsh = dwin if d == 0 else pltpu.roll(dwin, tr + CONV_HALO - d, 0)
            acc = acc + dsh[:tr, :] * w_ref[tap:tap + 1, :]
            ush = uwin if d == 0 else pltpu.roll(uwin, d, 0)
            dw_ref[tap:tap + 1, :] += jnp.sum(ush[CONV_HALO:, :] * dyb, 0, keepdims=True)
        du_ref[...] = acc
        db_ref[...] += jnp.sum(dyb, 0, keepdims=True)

    return pl.pallas_call(
        body, name=name, grid=(C // cb, nblk),
        in_specs=[pl.BlockSpec((tr, cb), lambda j, i: (i, j)),
                  pl.BlockSpec((CONV_HALO, cb), lambda j, i: (jnp.maximum(i * hb - 1, 0), j)),
                  pl.BlockSpec((tr, cb), lambda j, i: (i, j)),
                  pl.BlockSpec((CONV_HALO, cb), lambda j, i: (jnp.minimum((i + 1) * hb, last), j)),
                  pl.BlockSpec((CONV_HALO, cb), lambda j, i: (0, j))],
        out_specs=[pl.BlockSpec((tr, cb), lambda j, i: (i, j)),
                   pl.BlockSpec((CONV_HALO, cb), lambda j, i: (0, j)),
                   pl.BlockSpec((1, cb), lambda j, i: (0, j))],
        out_shape=[jax.ShapeDtypeStruct((S, C), F32), jax.ShapeDtypeStruct((CONV_HALO, C), F32),
                   jax.ShapeDtypeStruct((1, C), F32)],
        compiler_params=_cparams(("parallel", "arbitrary")))(u, u, dy, dy, w)


def _split2(x):
    hi = x.astype(BF16)
    return hi, (x - hi.astype(F32)).astype(BF16)


def _dot3(a, b, dims):
    ah, al = _split2(a)
    bh, bl = _split2(b)
    d = lambda p, q: lax.dot_general(p, q, dims, preferred_element_type=F32)
    return d(ah, bh) + (d(ah, bl) + d(al, bh))


@jax.custom_vjp
def _bnn(a, b):
    return _dot3(a, b, (((2,), (1,)), ((0,), (0,))))


@jax.custom_vjp
def _bnt(a, b):
    return _dot3(a, b, (((2,), (2,)), ((0,), (0,))))


@jax.custom_vjp
def _btn(a, b):
    return _dot3(a, b, (((1,), (1,)), ((0,), (0,))))


def _tri_sum(x, lower):
    h, c, _ = x.shape
    ti = lax.broadcasted_iota(jnp.int32, (h, c, c), 1)
    si = lax.broadcasted_iota(jnp.int32, (h, c, c), 2)
    m = (si <= ti if lower else si >= ti).astype(BF16)
    x1 = x.astype(BF16)
    r1 = x - x1.astype(F32)
    x2 = r1.astype(BF16)
    x3 = (r1 - x2.astype(F32)).astype(BF16)
    d = lambda q: lax.dot_general(m, q, (((2,), (1,)), ((0,), (0,))), preferred_element_type=F32)
    return d(x1) + (d(x2) + d(x3))


@jax.custom_vjp
def _cumsum_rows(x):
    return _tri_sum(x, True)


@jax.custom_vjp
def _rev_cumsum_rows(x):
    return _tri_sum(x, False)


_cumsum_rows.defvjp(lambda x: (_cumsum_rows(x), None), lambda _, g: (_rev_cumsum_rows(g),))
_rev_cumsum_rows.defvjp(lambda x: (_rev_cumsum_rows(x), None), lambda _, g: (_cumsum_rows(g),))


@jax.custom_vjp
def _unit_lower_inverse(a):
    h, c, _ = a.shape
    eye = (lax.broadcasted_iota(jnp.int32, (h, c, c), 1) == lax.broadcasted_iota(jnp.int32, (h, c, c), 2)).astype(F32)
    t, pw = eye + a, a
    for _ in range(int(math.log2(c)) - 1):
        pw = _bnn(pw, pw)
        t = t + _bnn(t, pw)
    return t


def _unit_lower_inverse_fwd(a):
    t = _unit_lower_inverse(a)
    return t, t


_unit_lower_inverse.defvjp(_unit_lower_inverse_fwd, lambda t, g: (_btn(t, _bnt(g, t)),))


_bnn.defvjp(lambda a, b: (_bnn(a, b), (a, b)), lambda res, g: (_bnt(g, res[1]), _btn(res[0], g)))
_bnt.defvjp(lambda a, b: (_bnt(a, b), (a, b)), lambda res, g: (_bnn(g, res[1]), _btn(g, res[0])))
_btn.defvjp(lambda a, b: (_btn(a, b), (a, b)), lambda res, g: (_bnt(res[1], g), _bnn(res[0], g)))


@jax.custom_vjp
def _known_inverse(a, t):
    return t


_known_inverse.defvjp(lambda a, t: (t, t), lambda t, g: (_btn(t, _bnt(g, t)), jnp.zeros_like(t)))


def scan_chunk(st0, r, lw, k, v, a, b, tinv=None):
    h, c, n = r.shape
    ti = lax.broadcasted_iota(jnp.int32, (h, c, c), 1)
    si = lax.broadcasted_iota(jnp.int32, (h, c, c), 2)
    incl = si <= ti
    strict = si < ti
    cum = _cumsum_rows(lw)
    p = jnp.exp(cum)
    pinv = jnp.exp(-cum)
    at = a * jnp.exp(cum - lw)
    bt, kt, rt = b * pinv, k * pinv, r * p
    a_ab = jnp.where(strict, _bnt(at, bt), 0.0)
    a_ak = jnp.where(strict, _bnt(at, kt), 0.0)
    m_rb = jnp.where(incl, _bnt(rt, bt), 0.0)
    m_rk = jnp.where(incl, _bnt(rt, kt), 0.0)
    tinv = _unit_lower_inverse(a_ab) if tinv is None else _known_inverse(a_ab, tinv)
    u = _bnn(tinv, _bnn(at, st0) + _bnn(a_ak, v))
    o = _bnn(rt, st0) + _bnn(m_rb, u) + _bnn(m_rk, v)
    cum_c = jnp.sum(lw, axis=1, keepdims=True)
    tail = jnp.exp(cum_c - cum)
    ki = lax.broadcasted_iota(jnp.int32, (h, n, n), 1)
    kj = lax.broadcasted_iota(jnp.int32, (h, n, n), 2)
    pc_col = jnp.sum(jnp.where(ki == kj, jnp.exp(cum_c), 0.0), axis=2, keepdims=True)
    st = st0 * pc_col + _btn(b * tail, u) + _btn(k * tail, v)
    return o, st, tinv


def _grid_ends(grid):
    g, c = pl.program_id(0), pl.program_id(1)
    return (g == 0) & (c == 0), (g == grid[0] - 1) & (c == grid[1] - 1)


def scan_fwd(r, lw, k, v, a, b, hb, name, side=None):
    H, S, N = r.shape
    C = SCAN_CHUNK
    nc = S // C
    side = _Side(side)
    grid = (H // hb, nc)

    def body(*refs):
        r_ref, lw_ref, k_ref, v_ref, a_ref, b_ref = refs[:6]
        side_src = refs[6:6 + side.n]
        o_ref, ck_ref, ti_ref = refs[6 + side.n:9 + side.n]
        side_out = refs[9 + side.n:9 + 2 * side.n]
        st_ref = refs[9 + 2 * side.n]
        finish = side.run(*_grid_ends(grid), side_src, side_out, refs[10 + 2 * side.n:])

        @pl.when(pl.program_id(1) == 0)
        def _():
            st_ref[...] = jnp.zeros_like(st_ref)

        st0 = st_ref[...]
        ck_ref[...] = st0[:, None]
        o, st, tinv = scan_chunk(st0, r_ref[...], lw_ref[...], k_ref[...], v_ref[...], a_ref[...], b_ref[...])
        o_ref[...] = o
        ti_ref[...] = tinv[:, None]
        st_ref[...] = st
        finish()

    seq = pl.BlockSpec((hb, C, N), lambda g, c: (g, c, 0))
    return pl.pallas_call(
        body, name=name, grid=grid, in_specs=[seq] * 6 + side.in_specs,
        out_specs=[seq, pl.BlockSpec((hb, 1, N, N), lambda g, c: (g, c, 0, 0)),
                   pl.BlockSpec((hb, 1, C, C), lambda g, c: (g, c, 0, 0))] + side.out_specs,
        out_shape=[jax.ShapeDtypeStruct((H, S, N), F32), jax.ShapeDtypeStruct((H, nc, N, N), F32),
                   jax.ShapeDtypeStruct((H, nc, C, C), F32)] + side.out_shapes,
        scratch_shapes=[pltpu.VMEM((hb, N, N), F32)] + side.scratch,
        compiler_params=_cparams(("arbitrary", "arbitrary")))(r, lw, k, v, a, b, *side.srcs)


def scan_bwd(r, lw, k, v, a, b, ck, ti, do, dr_add, dk_add, dv_add, hb, name, side=None):
    H, S, N = r.shape
    C = SCAN_CHUNK
    nc = S // C
    side = _Side(side)
    grid = (H // hb, nc)

    def body(*refs):
        r_ref, lw_ref, k_ref, v_ref, a_ref, b_ref, ck_ref, ti_ref, do_ref, ra_ref, ka_ref, va_ref = refs[:12]
        side_src = refs[12:12 + side.n]
        dr_ref, dlw_ref, dk_ref, dv_ref, da_ref, db_ref = refs[12 + side.n:18 + side.n]
        side_out = refs[18 + side.n:18 + 2 * side.n]
        dst_ref = refs[18 + 2 * side.n]
        finish = side.run(*_grid_ends(grid), side_src, side_out, refs[19 + 2 * side.n:])

        @pl.when(pl.program_id(1) == 0)
        def _():
            dst_ref[...] = jnp.zeros_like(dst_ref)

        st0, tinv = ck_ref[...][:, 0], ti_ref[...][:, 0]
        chunk = lambda *args: scan_chunk(*args, tinv=tinv)[:2]
        _, vjp = jax.vjp(chunk, st0, r_ref[...], lw_ref[...], k_ref[...], v_ref[...], a_ref[...], b_ref[...])
        dst0, dr, dlw, dk, dv, da, db = vjp((do_ref[...], dst_ref[...]))
        dr_ref[...], dlw_ref[...], dk_ref[...] = dr + ra_ref[...], dlw, dk + ka_ref[...]
        dv_ref[...], da_ref[...], db_ref[...] = dv + va_ref[...], da, db
        dst_ref[...] = dst0
        finish()

    seq = pl.BlockSpec((hb, C, N), lambda g, c: (g, nc - 1 - c, 0))
    per_chunk = lambda n: pl.BlockSpec((hb, 1, n, n), lambda g, c: (g, nc - 1 - c, 0, 0))
    return pl.pallas_call(
        body, name=name, grid=grid,
        in_specs=[seq] * 6 + [per_chunk(N), per_chunk(C)] + [seq] * 4 + side.in_specs,
        out_specs=[seq] * 6 + side.out_specs, out_shape=[jax.ShapeDtypeStruct((H, S, N), F32)] * 6 + side.out_shapes,
        scratch_shapes=[pltpu.VMEM((hb, N, N), F32)] + side.scratch,
        compiler_params=_cparams(("arbitrary", "arbitrary")))(r, lw, k, v, a, b, ck, ti, do, dr_add, dk_add, dv_add,
                                                              *side.srcs)


def loss_head(y, target, tm, name):
    S, D = y.shape

    def body(y_ref, t_ref, dy_ref, l_ref):
        err = y_ref[...] - t_ref[...]
        dy_ref[...] = err * (1.0 / D)

        @pl.when(pl.program_id(0) == 0)
        def _():
            l_ref[...] = jnp.zeros_like(l_ref)

        l_ref[...] += 0.5 * jnp.sum(jnp.mean(err * err, -1, keepdims=True), 0, keepdims=True)

    row = pl.BlockSpec((tm, D), lambda i: (i, 0))
    return pl.pallas_call(
        body, name=name, grid=(S // tm,), in_specs=[row, row],
        out_specs=[row, pl.BlockSpec((SUBLANES, LANES), lambda i: (0, 0))],
        out_shape=[jax.ShapeDtypeStruct((S, D), F32), jax.ShapeDtypeStruct((SUBLANES, LANES), F32)],
        compiler_params=_cparams(("arbitrary",)))(y, target)


def adamw(parts, w, m, v, name):
    R, C = w.shape
    n_parts = parts.shape[0]
    tr = R
    if R * C > ADAM_BLOCK_ELEMS:
        tr = _tile(R, [t for t in (512, 256, 128, 64, 32, 16) if t * C <= ADAM_BLOCK_ELEMS])
    c1 = 1.0 / (1.0 - ADAM_B1 ** ADAM_STEP)
    c2 = 1.0 / (1.0 - ADAM_B2 ** ADAM_STEP)

    def body(p_ref, w_ref, m_ref, v_ref, g_ref, d_ref, nm_ref, nv_ref):
        g = p_ref[0].astype(F32)
        for j in range(1, n_parts):
            g = g + p_ref[j].astype(F32)
        nm = ADAM_B1 * m_ref[...] + (1.0 - ADAM_B1) * g
        nv = ADAM_B2 * v_ref[...] + (1.0 - ADAM_B2) * (g * g)
        g_ref[...] = g
        nm_ref[...] = nm
        nv_ref[...] = nv
        d_ref[...] = -ADAM_LR * ((nm * c1) / (jnp.sqrt(nv * c2) + ADAM_EPS) + ADAM_WD * w_ref[...])

    blk = pl.BlockSpec((tr, C), lambda i: (i, 0))
    return pl.pallas_call(
        body, name=name, grid=(R // tr,), in_specs=[pl.BlockSpec((n_parts, tr, C), lambda i: (0, i, 0)), blk, blk, blk],
        out_specs=[blk] * 4, out_shape=[jax.ShapeDtypeStruct((R, C), F32)] * 4,
        compiler_params=_cparams(("parallel",)))(parts, w, m, v)


def _to_heads(a2d):
    s, d = a2d.shape
    return a2d.reshape(s, d // RWKV_HEAD, RWKV_HEAD).transpose(1, 0, 2)


def _from_heads(a3d):
    h, s, n = a3d.shape
    return a3d.transpose(1, 0, 2).reshape(s, h * n)


def _cols_joined(g):
    return g.transpose(1, 0, 2).reshape(g.shape[1], N_DEV * g.shape[2])


def _cols_split(w):
    return w.reshape(w.shape[0], N_DEV, w.shape[1] // N_DEV).transpose(1, 0, 2)


def _round_up(n, m):
    return -(-n // m) * m


def _step(a):
    x, mem, target = a['x'][0], a['mem'][0], a['loss_target'][0]
    S, D = x.shape
    M = mem.shape[0]
    DR = a['rwkv_w0'].shape[1]
    H = DR // RWKV_HEAD
    DC = a['conv_b'].shape[1]
    r_decay, r_iclr, r_gate = a['rwkv_w_up'].shape[1], a['rwkv_a_up'].shape[1], a['rwkv_g_up'].shape[1]
    n_lora = r_decay + r_iclr + r_gate
    lora_w = _round_up(n_lora, LANES)
    n_rwkv = 3 * DR + n_lora
    zr_w = 3 * DR + lora_w
    pad_cols = zr_w - n_rwkv
    conv_taps = a['conv_w'].shape[1]
    assert conv_taps - 1 <= CONV_HALO and S % SCAN_CHUNK == 0

    def shard(n, pre=''):
        w = a[pre + n]
        if n == 'conv_w':
            return jnp.pad(w.reshape(conv_taps, w.shape[-1]), ((0, CONV_HALO - conv_taps), (0, 0)))
        return w.reshape(w.shape[1:])

    Wg, W = {}, {}

    def gathers(names):
        return [(shard(n).astype(BF16), 'gather2') for n in names]

    def gathered(names, results):
        for n, g in zip(names, results):
            Wg[n] = g
            if n in ROW_SHARDED:
                W[n] = g.reshape(N_DEV * g.shape[1], g.shape[2])

    with_mm_z = ['proj_rwkv', 'proj_conv', 'w_out', 'xattn_wq']
    with_scan_fwd = ['xattn_wk', 'xattn_wv', 'xattn_wo', 'mlp_w1']
    with_mm_w1 = ['mlp_w2']
    with_scan_bwd = ['xattn_wo', 'xattn_wq', 'xattn_wk', 'xattn_wv', 'w_out', 'proj_rwkv', 'proj_conv']
    gathered(['w_in'], [exchange(shard('w_in').astype(BF16), 'gather2', "gather_w_in")])
    lora_src = jnp.concatenate([shard('rwkv_w_up'), shard('rwkv_a_up'), shard('rwkv_g_up')], axis=0).astype(BF16)
    lora_all = exchange(lora_src, 'gather', "gather_lora")
    conv_w = _cols_joined(exchange(shard('conv_w'), 'gather', "gather_conv_w"))

    w_in = _cols_joined(Wg['w_in'])
    o_conv, o_gate = n_rwkv, n_rwkv + 2 * DC
    w_in_p = jnp.concatenate([w_in[:, o_gate:], w_in[:, :3 * DR], w_in[:, o_conv:o_gate], w_in[:, 3 * DR:n_rwkv],
                              jnp.zeros((D, pad_cols), BF16)], axis=1)
    c_gr, c_gc, c_rkv, c_u = 0, D, 2 * D, 2 * D + 3 * DR
    c_cg, c_lora = c_u + DC, c_u + 2 * DC
    assert c_rkv % lora_w == 0 and c_lora % lora_w == 0 and (3 * DR) % lora_w == 0 and c_u % DC == 0 and c_rkv % DR == 0
    shift_cols = (3 * DR // lora_w, c_rkv // lora_w, c_lora // lora_w)
    lora_full = _cols_joined(lora_all)
    w_lora = jnp.zeros((lora_w, 3 * DR), BF16)
    w_lora = w_lora.at[:r_decay, :DR].set(lora_full[:r_decay])
    w_lora = w_lora.at[r_decay:r_decay + r_iclr, DR:2 * DR].set(lora_full[r_decay:r_decay + r_iclr])
    w_lora = w_lora.at[r_decay + r_iclr:n_lora, 2 * DR:].set(lora_full[r_decay + r_iclr:])
    mu_p = jnp.pad(a['rwkv_shift_mix'], ((0, 0), (0, pad_cols)))
    x_bf = x.astype(BF16)

    hp = lambda n: a[n].reshape(H, 1, RWKV_HEAD)
    w0_h, a0_h, kk_h, ka_h, gng_h, gnb_h = (hp(n) for n in ('rwkv_w0', 'rwkv_a0', 'rwkv_k_k', 'rwkv_k_a',
                                                             'rwkv_gn_g', 'rwkv_gn_b'))
    rk_h = a['rwkv_r_k'].reshape(H, 1, RWKV_HEAD)
    ln_mem_g, ln_mem_b = a['ln_mem_g'].reshape(1, D), a['ln_mem_b'].reshape(1, D)

    tm_d = _tile(S, (64, 32, 16, 8))
    tm_a = _tile(S, (512, 256, 128, 64))
    rows = _Rows2D(S, tm_d)
    rows_mem = _Rows2D(M, _tile(M, (64, 32, 16, 8)))
    rows_ff = _Rows2D(S, _tile(S, (32, 16, 8)))
    hb = _tile(H, (16, 8, 4, 2, 1))
    heads = _RowsHeads(H, S, _tile(H, (4, 2, 1)), _tile(S, (256, 128, 64)))
    tr_conv = _tile(S, (512, 256, 128, 64, 32))
    cb_conv = _tile(DC, (256, 128))
    fn_lora = make_fn_lora(r_decay, r_iclr, r_gate)
    lane_blk = lambda off, width: off // width

    mem_n, = ew_fwd(rows_mem, _as_bf16(fn_ln), [mem], [ln_mem_g, ln_mem_b], "ln_mem")
    z, *got = matmul(x_bf, w_in_p, 'nn', "mm_z", side=gathers(with_mm_z))
    gathered(with_mm_z, got)
    zs = token_shift_fwd(z, shift_cols, lora_w, zr_w, mu_p, tm_a, "token_shift")
    lora, = ew_fwd(rows, _as_bf16(fn_lora), [(zs, lora_w, lane_blk(3 * DR, lora_w))], [], "lora_act")
    up = matmul(lora, w_lora, 'nn', "mm_lora_up")
    r_h, k_h, v_h = (_to_heads(zs[:, i * DR:(i + 1) * DR]) for i in range(3))
    wl_h, al_h, g_h = (_to_heads(up[:, i * DR:(i + 1) * DR]) for i in range(3))
    pre_rows, pre_pars = [k_h, wl_h, al_h], [w0_h, a0_h, kk_h, ka_h]
    lw_h, k2_h, na_h, b_h = ew_fwd(heads, fn_rwkv_pre, pre_rows, pre_pars, "rwkv_pre")
    o_h, ckpt, tinvs, *got = scan_fwd(r_h, lw_h, k2_h, v_h, na_h, b_h, hb, "scan_fwd", side=gathers(with_scan_fwd))
    gathered(with_scan_fwd, got)
    post_rows, post_pars = [o_h, r_h, k2_h, v_h, g_h], [gng_h, gnb_h, rk_h]
    or_h, = ew_fwd(heads, _as_bf16(fn_rwkv_post), post_rows, post_pars, "rwkv_post")
    o_r = _from_heads(or_h)

    glu_rows = [(z, DC, lane_blk(c_u, DC)), (z, DC, lane_blk(c_cg, DC))]
    u, = ew_fwd(rows, fn_glu, glu_rows, [], "glu")
    yc = conv_fwd(u, conv_w, a['conv_b'], conv_taps, tr_conv, cb_conv, "conv")
    cln = [a['conv_ln_g'], a['conv_ln_b']]
    o_c, = ew_fwd(rows, _as_bf16(fn_ln_silu), [yc], cln, "conv_ln_silu")

    G = {}
    rows_split = lambda g: g.reshape(N_DEV, g.shape[0] // N_DEV, g.shape[1])

    Sx, recv = {}, {}

    def to_sibling(names):
        return [(G[n], 'scatter_sib') for n in names]

    def pair_sums(names, landed):
        for n, l in zip(names, landed):
            Sx[n] = pair_add(G[n], l, "pair_add_" + n)

    pr = matmul(o_r, Wg['proj_rwkv'], 'nn', "mm_proj_rwkv", b_dev=True)
    pc = matmul(o_c, Wg['proj_conv'], 'nn', "mm_proj_conv", b_dev=True)
    merge_rows = [(z, D, lane_blk(c_gr, D)), (z, D, lane_blk(c_gc, D)), pr, pc]
    merged, = ew_fwd(rows, _as_bf16(fn_merge), merge_rows, [], "merge")
    t1 = matmul(merged, W['w_out'], 'nn', "mm_w_out")
    ln1 = [a['ln1_g'], a['ln1_b']]
    h1, h1_bf = ew_fwd(rows, fn_ln_res_both, [x, t1], ln1, "ln1")

    q = matmul(h1_bf, W['xattn_wq'], 'nn', "mm_q")
    kx = matmul(mem_n, W['xattn_wk'], 'nn', "mm_k")
    vx = matmul(mem_n, W['xattn_wv'], 'nn', "mm_v")
    oa = attn_fwd(q, kx, vx, tm_a, "attn")
    ca = matmul(oa, W['xattn_wo'], 'nn', "mm_wo")
    ln2 = [a['ln2_g'], a['ln2_b']]
    h2, h2_bf = ew_fwd(rows, fn_ln_res_both, [h1, ca], ln2, "ln2")

    f1, *got = matmul(h2_bf, Wg['mlp_w1'], 'nn', "mm_w1", b_dev=True, side=gathers(with_mm_w1))
    gathered(with_mm_w1, got)
    act, = ew_fwd(rows_ff, _as_bf16(fn_relu2), [f1], [], "relu2")
    ff = matmul(act, W['mlp_w2'], 'nn', "mm_w2")
    ln3 = [a['ln3_g'], a['ln3_b']]
    h3, = ew_fwd(rows, fn_ln_res, [h2, ff], ln3, "ln3")
    dh3, loss_rows = loss_head(h3, target, tm_d, "loss")

    dh2a, dff, G['ln3_g'], G['ln3_b'] = ew_bwd(rows, fn_ln_res, [h2, ff], ln3, [dh3], (0, 1), (0, 1), "ln3_bwd",
                                               dr_dtypes=[F32, BF16])
    dact = matmul(dff, W['mlp_w2'], 'nt', "mm_w2_dx")
    G['mlp_w2'] = rows_split(matmul(act, dff, 'tn', "mm_w2_dw", out_dtype=BF16))
    df1, = ew_bwd(rows_ff, fn_relu2, [f1], [], [dact], (0,), (), "relu2_bwd", dr_dtypes=[BF16])
    dh2, *got = matmul(df1, Wg['mlp_w1'], 'nt', "mm_w1_dx", add=dh2a, b_dev=True, side=to_sibling(['mlp_w2']))
    pair_sums(['mlp_w2'], got)
    G['mlp_w1'], recv['mlp_w2'] = matmul(h2_bf, df1, 'tn', "mm_w1_dw", out_dtype=BF16, out_dev=True,
                                         side=[(Sx['mlp_w2'], 'scatter_chips')])

    dh1a, dca, G['ln2_g'], G['ln2_b'] = ew_bwd(rows, fn_ln_res, [h1, ca], ln2, [dh2], (0, 1), (0, 1), "ln2_bwd",
                                               dr_dtypes=[F32, BF16])
    doa, *got = matmul(dca, W['xattn_wo'], 'nt', "mm_wo_dx", side=to_sibling(['mlp_w1']))
    pair_sums(['mlp_w1'], got)
    G['xattn_wo'] = rows_split(matmul(oa, dca, 'tn', "mm_wo_dw", out_dtype=BF16))
    dq, dkx, dvx = attn_bwd(q, kx, vx, doa, tm_a, "attn_bwd")
    dh1, *got = matmul(dq, W['xattn_wq'], 'nt', "mm_q_dx", add=dh1a, side=to_sibling(['xattn_wo']))
    pair_sums(['xattn_wo'], got)
    G['xattn_wq'] = rows_split(matmul(h1_bf, dq, 'tn', "mm_q_dw", out_dtype=BF16))
    G['xattn_wk'] = rows_split(matmul(mem_n, dkx, 'tn', "mm_k_dw", out_dtype=BF16))
    G['xattn_wv'] = rows_split(matmul(mem_n, dvx, 'tn', "mm_v_dw", out_dtype=BF16))
    dmem_k = matmul(dkx, W['xattn_wk'], 'nt', "mm_k_dx")
    dmem_n = matmul(dvx, W['xattn_wv'], 'nt', "mm_v_dx", add=dmem_k)
    G['ln_mem_g'], G['ln_mem_b'] = ew_bwd(rows_mem, fn_ln, [mem], [ln_mem_g, ln_mem_b], [dmem_n], (), (0, 1),
                                          "ln_mem_bwd")

    dxa, dt1, G['ln1_g'], G['ln1_b'] = ew_bwd(rows, fn_ln_res, [x, t1], ln1, [dh1], (0, 1), (0, 1), "ln1_bwd",
                                              dr_dtypes=[F32, BF16])
    xattn_qkv = ['xattn_wq', 'xattn_wk', 'xattn_wv']
    dmerged, *got = matmul(dt1, W['w_out'], 'nt', "mm_w_out_dx", side=to_sibling(xattn_qkv))
    pair_sums(xattn_qkv, got)
    G['w_out'] = rows_split(matmul(merged, dt1, 'tn', "mm_w_out_dw", out_dtype=BF16))
    dzgr, dzgc, dpr, dpc = ew_bwd(rows, fn_merge, merge_rows, [], [dmerged], (0, 1, 2, 3), (), "merge_bwd",
                                  dr_dtypes=[BF16] * 4)
    do_r, *got = matmul(dpr, Wg['proj_rwkv'], 'nt', "mm_proj_rwkv_dx", b_dev=True, side=to_sibling(['w_out']))
    pair_sums(['w_out'], got)
    G['proj_rwkv'] = matmul(o_r, dpr, 'tn', "mm_proj_rwkv_dw", out_dtype=BF16, out_dev=True)
    do_c, *got = matmul(dpc, Wg['proj_conv'], 'nt', "mm_proj_conv_dx", b_dev=True, side=to_sibling(['proj_rwkv']))
    pair_sums(['proj_rwkv'], got)
    G['proj_conv'] = matmul(o_c, dpc, 'tn', "mm_proj_conv_dw", out_dtype=BF16, out_dev=True)
    pair_sums(['proj_conv'], [exchange(G['proj_conv'], 'scatter_sib', "scatter_sib_proj_conv")])

    dyc, G['conv_ln_g'], G['conv_ln_b'] = ew_bwd(rows, fn_ln_silu, [yc], cln, [do_c], (0,), (0, 1), "conv_ln_silu_bwd")
    du, dconv_w, G['conv_b'] = conv_bwd(u, dyc, conv_w, conv_taps, tr_conv, cb_conv, "conv_bwd")
    dzu, dzcg = ew_bwd(rows, fn_glu, glu_rows, [], [du], (0, 1), (), "glu_bwd", dr_dtypes=[BF16] * 2)

    do_r_h = _to_heads(do_r)
    do_h, dr1_h, dk2a_h, dv1_h, dg_h, dgng, dgnb, drk = ew_bwd(
        heads, fn_rwkv_post, post_rows, post_pars, [do_r_h], (0, 1, 2, 3, 4), (0, 1, 2), "rwkv_post_bwd",
        dr_dtypes=[F32, F32, F32, F32, BF16])
    dr_h, dlw_h, dk2_h, dv_h, dna_h, db_h, *got = scan_bwd(
        r_h, lw_h, k2_h, v_h, na_h, b_h, ckpt, tinvs, do_h, dr1_h, dk2a_h, dv1_h, hb, "scan_bwd",
        side=[(Sx[n], 'scatter_chips') for n in with_scan_bwd])
    recv.update(zip(with_scan_bwd, got))
    dk_h, dwl_h, dal_h, dw0, da0, dkk, dka = ew_bwd(
        heads, fn_rwkv_pre, pre_rows, pre_pars, [dlw_h, dk2_h, dna_h, db_h], (0, 1, 2), (0, 1, 2, 3), "rwkv_pre_bwd",
        dr_dtypes=[F32, BF16, BF16])
    G['rwkv_w0'], G['rwkv_a0'], G['rwkv_k_k'], G['rwkv_k_a'] = (t.reshape(1, DR) for t in (dw0, da0, dkk, dka))
    G['rwkv_gn_g'], G['rwkv_gn_b'] = dgng.reshape(1, DR), dgnb.reshape(1, DR)
    G['rwkv_r_k'] = drk.reshape(1, H, RWKV_HEAD)
    dup = jnp.concatenate([_from_heads(dwl_h), _from_heads(dal_h), _from_heads(dg_h)], axis=1)
    dlora = matmul(dup, w_lora, 'nt', "mm_lora_up_dx")
    dw_lora = matmul(lora, dup, 'tn', "mm_lora_up_dw", out_dtype=BF16)
    d_lora_stack = jnp.concatenate([dw_lora[:r_decay, :DR], dw_lora[r_decay:r_decay + r_iclr, DR:2 * DR],
                                    dw_lora[r_decay + r_iclr:n_lora, 2 * DR:]], axis=0)
    dzs_lora, = ew_bwd(rows, fn_lora, [(zs, lora_w, lane_blk(3 * DR, lora_w))], [], [dlora], (0,), (), "lora_act_bwd")
    dzs = jnp.concatenate([_from_heads(dr_h), _from_heads(dk_h), _from_heads(dv_h), dzs_lora], axis=1)
    dzr, dmu = token_shift_bwd(z, dzs, shift_cols, lora_w, zr_w, mu_p, tm_a, "token_shift_bwd")
    G['rwkv_shift_mix'] = dmu[:, :n_rwkv]
    dz = jnp.concatenate([dzgr, dzgc, dzr[:, :3 * DR].astype(BF16), dzu, dzcg, dzr[:, 3 * DR:].astype(BF16)], axis=1)
    dw_in_p, recv['mlp_w1'] = matmul(x_bf, dz, 'tn', "mm_z_dw", out_dtype=BF16, side=[(Sx['mlp_w1'], 'scatter_chips')])
    G['w_in'] = _cols_split(jnp.concatenate([dw_in_p[:, c_rkv:c_u], dw_in_p[:, c_lora:c_lora + n_lora],
                                             dw_in_p[:, c_u:c_lora], dw_in_p[:, :c_rkv]], axis=1))
    pair_sums(['w_in'], [exchange(G['w_in'], 'scatter_sib', "scatter_sib_w_in")])
    grad_x, recv['w_in'] = matmul(dz, w_in_p, 'nt', "mm_z_dx", add=dxa, side=[(Sx['w_in'], 'scatter_chips')])

    out = {}

    def update(n, parts, w, m, v):
        res = adamw(parts, w, m, v, "adamw_" + n)
        out[n] = [t[:conv_taps].reshape(a[n].shape) if n == 'conv_w' else t.reshape(a[n].shape) for t in res]

    for n in ['w_in', 'mlp_w2', 'mlp_w1'] + with_scan_bwd:
        update(n, recv[n], shard(n), shard(n, 'm_'), shard(n, 'v_'))
    recv_lora = exchange(_cols_split(d_lora_stack), 'scatter', "scatter_lora")
    lo = 0
    for n, r_ in (('rwkv_w_up', r_decay), ('rwkv_a_up', r_iclr), ('rwkv_g_up', r_gate)):
        update(n, recv_lora[:, lo:lo + r_], shard(n), shard(n, 'm_'), shard(n, 'v_'))
        lo += r_
    recv_conv = exchange(_cols_split(dconv_w.astype(BF16)), 'scatter', "scatter_conv_w")
    update('conv_w', recv_conv, shard('conv_w'), shard('conv_w', 'm_'), shard('conv_w', 'v_'))

    small_sizes = [a[n].size for n in SMALL]
    n_small = sum(small_sizes) + 1
    n_pack = _round_up(n_small, SUBLANES * LANES)
    pack = lambda get, last: jnp.pad(jnp.concatenate([get(n).reshape(-1) for n in SMALL] + [last]),
                                     (0, n_pack - n_small)).reshape(n_pack // LANES, LANES)
    zero1 = jnp.zeros((1,), F32)
    g_pack = pack(lambda n: G[n], loss_rows[0, :1])
    parts_small = exchange(g_pack, 'gather', "gather_small")
    res = adamw(parts_small, pack(lambda n: a[n], zero1), pack(lambda n: a['m_' + n], zero1),
                pack(lambda n: a['v_' + n], zero1), "adamw_small")
    res = [t.reshape(-1) for t in res]
    o_ = 0
    for n, sz in zip(SMALL, small_sizes):
        out[n] = [t[o_:o_ + sz].reshape(a[n].shape) for t in res]
        o_ += sz
    loss = res[0][o_]

    return (loss, grad_x[None], *[out[n][0] for n in WEIGHTS], *[out[n][1] for n in WEIGHTS],
            *[out[n][2] for n in WEIGHTS], *[out[n][3] for n in WEIGHTS])


def kernel(x, mem, w_in, rwkv_shift_mix, rwkv_w0, rwkv_w_up, rwkv_a0, rwkv_a_up, rwkv_g_up, rwkv_k_k, rwkv_k_a, rwkv_r_k, rwkv_gn_g, rwkv_gn_b, conv_w, conv_b, conv_ln_g, conv_ln_b, proj_rwkv, proj_conv, w_out, ln1_g, ln1_b, ln_mem_g, ln_mem_b, xattn_wq, xattn_wk, xattn_wv, xattn_wo, ln2_g, ln2_b, mlp_w1, mlp_w2, ln3_g, ln3_b, loss_target, m_w_in, m_rwkv_shift_mix, m_rwkv_w0, m_rwkv_w_up, m_rwkv_a0, m_rwkv_a_up, m_rwkv_g_up, m_rwkv_k_k, m_rwkv_k_a, m_rwkv_r_k, m_rwkv_gn_g, m_rwkv_gn_b, m_conv_w, m_conv_b, m_conv_ln_g, m_conv_ln_b, m_proj_rwkv, m_proj_conv, m_w_out, m_ln1_g, m_ln1_b, m_ln_mem_g, m_ln_mem_b, m_xattn_wq, m_xattn_wk, m_xattn_wv, m_xattn_wo, m_ln2_g, m_ln2_b, m_mlp_w1, m_mlp_w2, m_ln3_g, m_ln3_b, v_w_in, v_rwkv_shift_mix, v_rwkv_w0, v_rwkv_w_up, v_rwkv_a0, v_rwkv_a_up, v_rwkv_g_up, v_rwkv_k_k, v_rwkv_k_a, v_rwkv_r_k, v_rwkv_gn_g, v_rwkv_gn_b, v_conv_w, v_conv_b, v_conv_ln_g, v_conv_ln_b, v_proj_rwkv, v_proj_conv, v_w_out, v_ln1_g, v_ln1_b, v_ln_mem_g, v_ln_mem_b, v_xattn_wq, v_xattn_wk, v_xattn_wv, v_xattn_wo, v_ln2_g, v_ln2_b, v_mlp_w1, v_mlp_w2, v_ln3_g, v_ln3_b):
    return _step(dict(locals()))
```

```python
import functools
import math

import jax
import jax.numpy as jnp
from jax import lax
from jax.experimental import pallas as pl
from jax.experimental.pallas import tpu as pltpu

F32 = jnp.float32
BF16 = jnp.bfloat16

N_DEV = 8
RWKV_HEAD = 64
SCAN_CHUNK = 64
XATTN_HEADS = 4
CONV_HALO = 32
LN_EPS = 1e-5
GN_EPS = 64e-5
ALPHA = float(2.0 ** 0.25)
ADAM_LR, ADAM_B1, ADAM_B2, ADAM_EPS, ADAM_WD, ADAM_STEP = 0.001, 0.9, 0.999, 1e-08, 0.01, 10
LANES = 128
SUBLANES = 8
VMEM_LIMIT = 56 * 1024 * 1024
ADAM_BLOCK_ELEMS = 256 * 1024

WEIGHTS = ['w_in', 'rwkv_shift_mix', 'rwkv_w0', 'rwkv_w_up', 'rwkv_a0', 'rwkv_a_up', 'rwkv_g_up', 'rwkv_k_k',
           'rwkv_k_a', 'rwkv_r_k', 'rwkv_gn_g', 'rwkv_gn_b', 'conv_w', 'conv_b', 'conv_ln_g', 'conv_ln_b',
           'proj_rwkv', 'proj_conv', 'w_out', 'ln1_g', 'ln1_b', 'ln_mem_g', 'ln_mem_b', 'xattn_wq', 'xattn_wk',
           'xattn_wv', 'xattn_wo', 'ln2_g', 'ln2_b', 'mlp_w1', 'mlp_w2', 'ln3_g', 'ln3_b']
COL_SHARDED = ['w_in', 'rwkv_w_up', 'rwkv_a_up', 'rwkv_g_up', 'conv_w', 'proj_rwkv', 'proj_conv', 'mlp_w1']
ROW_SHARDED = ['w_out', 'xattn_wq', 'xattn_wk', 'xattn_wv', 'xattn_wo', 'mlp_w2']
BIG = ['w_in', 'rwkv_w_up', 'rwkv_a_up', 'rwkv_g_up', 'conv_w', 'proj_rwkv', 'proj_conv', 'w_out', 'xattn_wq',
       'xattn_wk', 'xattn_wv', 'xattn_wo', 'mlp_w1', 'mlp_w2']
SMALL = [w for w in WEIGHTS if w not in BIG]


def _cparams(dims):
    return pltpu.CompilerParams(dimension_semantics=dims, vmem_limit_bytes=VMEM_LIMIT)


def _tile(n, cands):
    for c in cands:
        if n % c == 0:
            return c
    return n


N_CHIP = N_DEV // 2
SEMS_PER_EXCHANGE = N_DEV + 2
CHIP_XORS = (2, 4, 6)


def _exchange_shape(src, kind):
    return {'gather': (N_DEV,) + src.shape, 'gather2': (N_DEV,) + src.shape, 'scatter': src.shape,
            'scatter_sib': (N_CHIP,) + src.shape[1:], 'scatter_chips': src.shape}[kind]


def _exchange_copies(e, src_ref, out_ref, kind, send_sems, recv_sems, local_sems):
    x, y, c = lax.axis_index("x"), lax.axis_index("y"), lax.axis_index("c")
    me, chip = 4 * x + 2 * y + c, 2 * x + y
    sibling = (x, y, 1 - c)
    base = e * SEMS_PER_EXCHANGE

    def remote(src, dst, idx, dev):
        return pltpu.make_async_remote_copy(src_ref=src, dst_ref=dst, send_sem=send_sems.at[base + idx],
                                            recv_sem=recv_sems.at[base + idx], device_id=dev,
                                            device_id_type=pl.DeviceIdType.MESH)

    def peer(k):
        return x ^ ((k >> 2) & 1), y ^ ((k >> 1) & 1), c ^ (k & 1)

    first, second = [], []
    if kind in ('gather', 'gather2'):
        first.append(pltpu.make_async_copy(src_ref, out_ref.at[me], local_sems.at[e]))
        for k in (range(1, N_DEV) if kind == 'gather' else (1,) + CHIP_XORS):
            first.append(remote(src_ref, out_ref.at[me], k - 1, peer(k)))
        if kind == 'gather2':
            for i, k in enumerate(CHIP_XORS):
                second.append(remote(out_ref.at[me ^ k], out_ref.at[me ^ k], N_DEV - 1 + i, sibling))
    elif kind == 'scatter':
        first.append(pltpu.make_async_copy(src_ref.at[me], out_ref.at[me], local_sems.at[e]))
        for k in range(1, N_DEV):
            px, py, pc = peer(k)
            first.append(remote(src_ref.at[4 * px + 2 * py + pc], out_ref.at[me], k - 1, (px, py, pc)))
    elif kind == 'scatter_sib':
        for q in range(N_CHIP):
            first.append(remote(src_ref.at[2 * q + 1 - c], out_ref.at[q], q, sibling))
    else:
        assert kind == 'scatter_chips', kind
        first.append(pltpu.make_async_copy(src_ref.at[chip], out_ref.at[chip], local_sems.at[e]))
        for k in CHIP_XORS:
            first.append(remote(src_ref.at[chip ^ (k >> 1)], out_ref.at[chip], k - 1, peer(k)))
    return first, second


class _Side:
    def __init__(self, items):
        self.items = list(items or [])
        self.n = len(self.items)
        any_spec = pl.BlockSpec(memory_space=pl.ANY)
        self.srcs = [s for s, _ in self.items]
        self.in_specs = [any_spec] * self.n
        self.out_specs = [any_spec] * self.n
        self.out_shapes = [jax.ShapeDtypeStruct(_exchange_shape(s, kind), s.dtype) for s, kind in self.items]
        self.scratch = [pltpu.SemaphoreType.DMA((self.n * SEMS_PER_EXCHANGE,)),
                        pltpu.SemaphoreType.DMA((self.n * SEMS_PER_EXCHANGE,)),
                        pltpu.SemaphoreType.DMA((self.n,))] if self.n else []

    def _copies(self, src_refs, out_refs, sems):
        both = [_exchange_copies(e, src_refs[e], out_refs[e], kind, *sems) for e, (_, kind) in enumerate(self.items)]
        return [cp for f, _ in both for cp in f], [cp for _, s in both for cp in s]

    def start(self, src_refs, out_refs, sems):
        for cp in self._copies(src_refs, out_refs, sems)[0]:
            cp.start()

    def finish(self, src_refs, out_refs, sems):
        first, second = self._copies(src_refs, out_refs, sems)
        for cp in first:
            cp.wait()
        for cp in second:
            cp.start()
        for cp in second:
            cp.wait()

    def run(self, first, last, src_refs, out_refs, sems):
        if not self.n:
            return lambda: None
        pl.when(first)(lambda: self.start(src_refs, out_refs, sems))
        return lambda: pl.when(last)(lambda: self.finish(src_refs, out_refs, sems))


def exchange(src, kind, name):
    side = _Side([(src, kind)])

    def body(src_ref, out_ref, *sems):
        side.start([src_ref], [out_ref], sems)
        side.finish([src_ref], [out_ref], sems)

    return pl.pallas_call(body, name=name, in_specs=side.in_specs, out_specs=side.out_specs[0],
                          out_shape=side.out_shapes[0], scratch_shapes=side.scratch)(src)


def pair_add(parts, landed, name):
    _, R, C = parts.shape
    tr = R
    if R * C > ADAM_BLOCK_ELEMS:
        tr = _tile(R, [t for t in (512, 256, 128, 64, 32, 16) if t * C <= ADAM_BLOCK_ELEMS])

    def body(core_ref, p_ref, l_ref, o_ref):
        o_ref[...] = (p_ref[...].astype(F32) + l_ref[...].astype(F32)).astype(BF16)

    blk = pl.BlockSpec((None, tr, C), lambda q, i, core_ref: (q, i, 0))
    mine = pl.BlockSpec((None, None, tr, C), lambda q, i, core_ref: (q, core_ref[0], i, 0))
    return pl.pallas_call(
        body, name=name,
        grid_spec=pltpu.PrefetchScalarGridSpec(num_scalar_prefetch=1, grid=(N_CHIP, R // tr), in_specs=[mine, blk],
                                               out_specs=blk),
        out_shape=jax.ShapeDtypeStruct((N_CHIP, R, C), BF16),
        compiler_params=_cparams(("parallel", "parallel")))(
            lax.axis_index("c").astype(jnp.int32).reshape(1), parts.reshape(N_CHIP, 2, R, C), landed)


def matmul(a, b, mode, name, add=None, out_dtype=F32, b_dev=False, out_dev=False, side=None):
    side = _Side(side)
    if b_dev:
        assert mode in ('nn', 'nt') and b.shape[0] == N_DEV
        b_rows, b_cols = b.shape[1], N_DEV * b.shape[2]
    else:
        b_rows, b_cols = b.shape
    if mode == 'nn':
        (M, K), (K2, N) = a.shape, (b_rows, b_cols)
    elif mode == 'nt':
        (M, K), (N, K2) = a.shape, (b_rows, b_cols)
    else:
        (K, M), (K2, N) = a.shape, (b_rows, b_cols)
    assert K == K2, (a.shape, b.shape, mode)
    n_unit = N // N_DEV if (out_dev or (b_dev and mode == 'nn')) else N
    k_unit = K // N_DEV if (b_dev and mode == 'nt') else K
    tn = _tile(n_unit, (1024, 512, 256, 128))
    tm = _tile(M, (1024, 512, 256, 128) if tn >= 1024 else (2048, 1024, 512, 256, 128))
    tk = _tile(k_unit, (2048, 1024, 512, 256, 128))
    nk = K // tk
    nb, kb = n_unit // tn, k_unit // tk
    dims = {'nn': ((1,), (0,)), 'nt': ((1,), (1,)), 'tn': ((0,), (0,))}[mode]

    n_in = 2 + (add is not None)
    grid = (M // tm, N // tn, nk)

    def body(*refs):
        a_ref, b_ref = refs[:2]
        add_ref = refs[2] if add is not None else None
        side_src = refs[n_in:n_in + side.n]
        o_ref = refs[n_in + side.n]
        side_out = refs[n_in + side.n + 1:n_in + 2 * side.n + 1]
        acc_ref = refs[n_in + 2 * side.n + 1]
        sems = refs[n_in + 2 * side.n + 2:]
        i, j, k = pl.program_id(0), pl.program_id(1), pl.program_id(2)
        finish = side.run((i == 0) & (j == 0) & (k == 0), (i == grid[0] - 1) & (j == grid[1] - 1) & (k == nk - 1),
                          side_src, side_out, sems)

        @pl.when(k == 0)
        def _():
            acc_ref[...] = jnp.zeros_like(acc_ref)

        acc_ref[...] += lax.dot_general(a_ref[...].astype(BF16), b_ref[...].astype(BF16), (dims, ((), ())),
                                        preferred_element_type=F32)

        @pl.when(k == nk - 1)
        def _():
            r = acc_ref[...]
            if add is not None:
                r = r + add_ref[...]
            o_ref[...] = r.astype(out_dtype)

        finish()

    if mode == 'nn':
        a_spec = pl.BlockSpec((tm, tk), lambda i, j, k: (i, k))
        b_spec = (pl.BlockSpec((None, tk, tn), lambda i, j, k: (j // nb, k, j % nb)) if b_dev
                  else pl.BlockSpec((tk, tn), lambda i, j, k: (k, j)))
    elif mode == 'nt':
        a_spec = pl.BlockSpec((tm, tk), lambda i, j, k: (i, k))
        b_spec = (pl.BlockSpec((None, tn, tk), lambda i, j, k: (k // kb, j, k % kb)) if b_dev
                  else pl.BlockSpec((tn, tk), lambda i, j, k: (j, k)))
    else:
        a_spec = pl.BlockSpec((tk, tm), lambda i, j, k: (k, i))
        b_spec = pl.BlockSpec((tk, tn), lambda i, j, k: (k, j))
    add_spec = pl.BlockSpec((tm, tn), lambda i, j, k: (i, j))
    if out_dev:
        o_spec = pl.BlockSpec((None, tm, tn), lambda i, j, k: (j // nb, i, j % nb))
        o_shape = (N_DEV, M, N // N_DEV)
    else:
        o_spec, o_shape = add_spec, (M, N)
    in_specs = [a_spec, b_spec] + ([add_spec] if add is not None else [])
    ops = (a, b) + ((add,) if add is not None else ())
    sem = ("arbitrary",) * 3 if side.n else ("parallel", "parallel", "arbitrary")
    res = pl.pallas_call(
        body, name=name, grid=grid, in_specs=in_specs + side.in_specs, out_specs=[o_spec] + side.out_specs,
        out_shape=[jax.ShapeDtypeStruct(o_shape, out_dtype)] + side.out_shapes,
        scratch_shapes=[pltpu.VMEM((tm, tn), F32)] + side.scratch,
        compiler_params=_cparams(sem))(*ops, *side.srcs)
    return tuple(res) if side.n else res[0]


class _Rows2D:
    def __init__(self, n_rows, tm):
        self.n, self.tm = n_rows, tm
        self.grid = (1, n_rows // tm)

    def row(self, e):
        if isinstance(e, tuple):
            arr, width, cb = e
            return arr, (self.tm, width), pl.BlockSpec((self.tm, width), lambda g, i, cb=cb: (i, cb))
        return e, (self.tm, e.shape[1]), pl.BlockSpec((self.tm, e.shape[1]), lambda g, i: (i, 0))

    def par(self, p):
        return pl.BlockSpec(p.shape, lambda g, i: (0,) * p.ndim)

    def out(self, blk):
        return (self.n, blk[1]), pl.BlockSpec((self.tm, blk[1]), lambda g, i: (i, 0))


class _RowsHeads:
    def __init__(self, n_heads, n_rows, hb, ts):
        self.h, self.n, self.hb, self.ts = n_heads, n_rows, hb, ts
        self.grid = (n_heads // hb, n_rows // ts)

    def row(self, e):
        blk = (self.hb, self.ts, e.shape[2])
        return e, blk, pl.BlockSpec(blk, lambda g, i: (g, i, 0))

    def par(self, p):
        return pl.BlockSpec((self.hb, 1, p.shape[2]), lambda g, i: (g, 0, 0))

    def out(self, blk):
        return (self.h, self.n, blk[2]), pl.BlockSpec(blk, lambda g, i: (g, i, 0))


def _par_block(lay, p):
    return lay.par(p).block_shape


def ew_fwd(lay, fn, rows, params, name):
    rr = [lay.row(e) for e in rows]
    arrs = [r[0] for r in rr]
    blk_avals = [jax.ShapeDtypeStruct(r[1], r[0].dtype) for r in rr]
    par_avals = [jax.ShapeDtypeStruct(_par_block(lay, p), p.dtype) for p in params]
    outs = jax.eval_shape(fn, *blk_avals, *par_avals)
    out_full = [lay.out(o.shape) for o in outs]
    nr, npar = len(rows), len(params)

    def body(*refs):
        vals = [r[...] for r in refs[:nr + npar]]
        res = fn(*vals)
        for ref, v in zip(refs[nr + npar:], res):
            ref[...] = v.astype(ref.dtype)

    return pl.pallas_call(
        body, name=name, grid=lay.grid,
        in_specs=[r[2] for r in rr] + [lay.par(p) for p in params],
        out_specs=[o[1] for o in out_full],
        out_shape=[jax.ShapeDtypeStruct(o[0], a.dtype) for o, a in zip(out_full, outs)],
        compiler_params=_cparams(("parallel", "parallel")))(*arrs, *params)


def ew_bwd(lay, fn, rows, params, cots, wrt_rows, wrt_pars, name, dr_dtypes=None):
    rr = [lay.row(e) for e in rows]
    cc = [lay.row(e) for e in cots]
    nr, npar, nc = len(rows), len(params), len(cots)
    n_dr = len(wrt_rows)
    dr_full = [lay.out(rr[i][1]) for i in wrt_rows]
    dr_dtypes = dr_dtypes or [F32] * n_dr

    def body(*refs):
        rv = [r[...] for r in refs[:nr]]
        pv = [r[...] for r in refs[nr:nr + npar]]
        cv = tuple(r[...] for r in refs[nr + npar:nr + npar + nc])
        outs = refs[nr + npar + nc:]

        def f(*wrt):
            r2, p2 = list(rv), list(pv)
            for idx, v in zip(wrt_rows, wrt[:n_dr]):
                r2[idx] = v
            for idx, v in zip(wrt_pars, wrt[n_dr:]):
                p2[idx] = v
            return fn(*r2, *p2)

        _, vjp = jax.vjp(f, *[rv[i] for i in wrt_rows], *[pv[i] for i in wrt_pars])
        g = vjp(cv)
        for ref, v in zip(outs[:n_dr], g[:n_dr]):
            ref[...] = v.astype(ref.dtype)
        if wrt_pars:
            @pl.when(pl.program_id(1) == 0)
            def _():
                for ref in outs[n_dr:]:
                    ref[...] = jnp.zeros_like(ref)

            for ref, v in zip(outs[n_dr:], g[n_dr:]):
                ref[...] += v

    return pl.pallas_call(
        body, name=name, grid=lay.grid,
        in_specs=[r[2] for r in rr] + [lay.par(p) for p in params] + [c[2] for c in cc],
        out_specs=[o[1] for o in dr_full] + [lay.par(params[i]) for i in wrt_pars],
        out_shape=[jax.ShapeDtypeStruct(o[0], dt) for o, dt in zip(dr_full, dr_dtypes)]
        + [jax.ShapeDtypeStruct(params[i].shape, F32) for i in wrt_pars],
        compiler_params=_cparams(("parallel", "arbitrary")))(
            *[r[0] for r in rr], *params, *[c[0] for c in cc])


def _sigmoid(x):
    return 1.0 / (1.0 + jnp.exp(-x))


def _softplus(x):
    return jnp.maximum(x, 0.0) + jnp.log(1.0 + jnp.exp(-jnp.abs(x)))


def _layer_norm(x, g, b, eps):
    mu = jnp.mean(x, -1, keepdims=True)
    xc = x - mu
    var = jnp.mean(xc * xc, -1, keepdims=True)
    return xc * lax.rsqrt(var + eps) * g + b


def _as_bf16(fn):
    return lambda *args: tuple(o.astype(BF16) for o in fn(*args))


def fn_ln(x, g, b):
    return (_layer_norm(x, g, b, LN_EPS),)


def fn_ln_res(h, t, g, b):
    return (_layer_norm(ALPHA * h + t, g, b, LN_EPS),)


def fn_ln_res_both(h, t, g, b):
    y, = fn_ln_res(h, t, g, b)
    return y, y.astype(BF16)


def make_fn_lora(r_decay, r_iclr, r_gate):
    def fn(z):
        lane = lax.broadcasted_iota(jnp.int32, z.shape, 1)
        out = jnp.where(lane < r_decay, jnp.tanh(z), z)
        out = jnp.where(lane >= r_decay + r_iclr, _sigmoid(z), out)
        return (jnp.where(lane < r_decay + r_iclr + r_gate, out, 0.0),)
    return fn


def fn_rwkv_pre(k, wl, al, w0, a0, k_k, k_a):
    w = -_softplus(-(w0 + wl)) - 0.5
    lw = -jnp.exp(w)
    a = _sigmoid(a0 + al)
    kk = k * k_k
    kk = kk / jnp.maximum(jnp.sqrt(jnp.sum(kk * kk, -1, keepdims=True)), 1e-12)
    k2 = k * (1.0 + (a - 1.0) * k_a)
    return lw, k2, -kk, kk * a


def fn_rwkv_post(o, r, k2, v, g, gn_g, gn_b, r_k):
    mu = jnp.mean(o, -1, keepdims=True)
    oc = o - mu
    var = jnp.mean(oc * oc, -1, keepdims=True)
    y = oc * lax.rsqrt(var + GN_EPS) * gn_g + gn_b
    y = y + jnp.sum(r * k2 * r_k, -1, keepdims=True) * v
    return (y * g,)


def fn_glu(zu, zg):
    return (zu * _sigmoid(zg),)


def fn_ln_silu(y, g, b):
    n = _layer_norm(y, g, b, LN_EPS)
    return (n * _sigmoid(n),)


def fn_merge(zgr, zgc, pr, pc):
    return (_sigmoid(zgr) * pr + _sigmoid(zgc) * pc,)


def fn_relu2(f):
    r = jnp.maximum(f, 0.0)
    return (r * r,)


def _dot_nn(a, b):
    return lax.dot_general(a.astype(BF16), b.astype(BF16), (((1,), (0,)), ((), ())), preferred_element_type=F32)


def _dot_nt(a, b):
    return lax.dot_general(a.astype(BF16), b.astype(BF16), (((1,), (1,)), ((), ())), preferred_element_type=F32)


def _dot_tn(a, b):
    return lax.dot_general(a.astype(BF16), b.astype(BF16), (((0,), (0,)), ((), ())), preferred_element_type=F32)


def _softmax_rows(s):
    s = s - jnp.max(s, -1, keepdims=True)
    e = jnp.exp(s)
    return e / jnp.sum(e, -1, keepdims=True)


def attn_fwd(q, kx, vx, tm, name):
    S, D = q.shape
    M = kx.shape[0]
    dh = D // XATTN_HEADS
    scale = dh ** -0.5

    def body(q_ref, k_ref, v_ref, o_ref):
        p = _softmax_rows(_dot_nt(q_ref[...], k_ref[...]) * scale)
        o_ref[...] = _dot_nn(p, v_ref[...]).astype(BF16)

    row = pl.BlockSpec((tm, dh), lambda h, i: (i, h))
    kv = pl.BlockSpec((M, dh), lambda h, i: (0, h))
    return pl.pallas_call(body, name=name, grid=(XATTN_HEADS, S // tm), in_specs=[row, kv, kv], out_specs=row,
                          out_shape=jax.ShapeDtypeStruct((S, D), BF16),
                          compiler_params=_cparams(("parallel", "parallel")))(q, kx, vx)


def attn_bwd(q, kx, vx, do, tm, name):
    S, D = q.shape
    M = kx.shape[0]
    dh = D // XATTN_HEADS
    scale = dh ** -0.5

    def body(q_ref, k_ref, v_ref, do_ref, dq_ref, dk_ref, dv_ref):
        qb, kb, dob = q_ref[...], k_ref[...], do_ref[...]
        p = _softmax_rows(_dot_nt(qb, kb) * scale)
        dp = _dot_nt(dob, v_ref[...])
        ds = p * (dp - jnp.sum(dp * p, -1, keepdims=True)) * scale
        dq_ref[...] = _dot_nn(ds, kb).astype(BF16)

        @pl.when(pl.program_id(1) == 0)
        def _():
            dk_ref[...] = jnp.zeros_like(dk_ref)
            dv_ref[...] = jnp.zeros_like(dv_ref)

        dk_ref[...] += _dot_tn(ds, qb)
        dv_ref[...] += _dot_tn(p, dob)

    row = pl.BlockSpec((tm, dh), lambda h, i: (i, h))
    kv = pl.BlockSpec((M, dh), lambda h, i: (0, h))
    return pl.pallas_call(
        body, name=name, grid=(XATTN_HEADS, S // tm), in_specs=[row, kv, kv, row], out_specs=[row, kv, kv],
        out_shape=[jax.ShapeDtypeStruct((S, D), BF16), jax.ShapeDtypeStruct((M, D), F32), jax.ShapeDtypeStruct((M, D), F32)],
        compiler_params=_cparams(("parallel", "arbitrary")))(q, kx, vx, do)


def _shift_down(blk, halo_last_row, first_block):
    rolled = pltpu.roll(blk, 1, 0)
    row = lax.broadcasted_iota(jnp.int32, blk.shape, 0)
    top = jnp.where(first_block, 0.0, halo_last_row)
    return jnp.where(row == 0, top, rolled)


def _shift_up(blk, halo_first_row, last_block):
    n = blk.shape[0]
    rolled = pltpu.roll(blk, n - 1, 0)
    row = lax.broadcasted_iota(jnp.int32, blk.shape, 0)
    bot = jnp.where(last_block, 0.0, halo_first_row)
    return jnp.where(row == n - 1, bot, rolled)


def _zcol(j, cols):
    n_first, first, second = cols
    return jnp.where(j < n_first, first + j, second + j - n_first)


def token_shift_fwd(z, cols, cw, width, mu, tm, name):
    S = z.shape[0]
    hb = tm // SUBLANES

    def body(z_ref, halo_ref, mu_ref, o_ref):
        zb = z_ref[...]
        prev = _shift_down(zb, halo_ref[SUBLANES - 1:SUBLANES, :], pl.program_id(1) == 0)
        o_ref[...] = zb + (prev - zb) * mu_ref[...]

    return pl.pallas_call(
        body, name=name, grid=(width // cw, S // tm),
        in_specs=[pl.BlockSpec((tm, cw), lambda j, i: (i, _zcol(j, cols))),
                  pl.BlockSpec((SUBLANES, cw), lambda j, i: (jnp.maximum(i * hb - 1, 0), _zcol(j, cols))),
                  pl.BlockSpec((1, cw), lambda j, i: (0, j))],
        out_specs=pl.BlockSpec((tm, cw), lambda j, i: (i, j)),
        out_shape=jax.ShapeDtypeStruct((S, width), F32),
        compiler_params=_cparams(("parallel", "parallel")))(z, z, mu)


def token_shift_bwd(z, dzs, cols, cw, width, mu, tm, name):
    S = z.shape[0]
    hb = tm // SUBLANES
    nblk = S // tm
    last8 = S // SUBLANES - 1

    def body(z_ref, zh_ref, d_ref, dh_ref, mu_ref, dz_ref, dmu_ref):
        i = pl.program_id(1)
        zb, db, m = z_ref[...], d_ref[...], mu_ref[...]
        prev = _shift_down(zb, zh_ref[SUBLANES - 1:SUBLANES, :], i == 0)
        dm = db * m
        nxt = _shift_up(dm, dh_ref[0:1, :] * m, i == nblk - 1)
        dz_ref[...] = db - dm + nxt

        @pl.when(i == 0)
        def _():
            dmu_ref[...] = jnp.zeros_like(dmu_ref)

        dmu_ref[...] += jnp.sum(db * (prev - zb), 0, keepdims=True)

    return pl.pallas_call(
        body, name=name, grid=(width // cw, nblk),
        in_specs=[pl.BlockSpec((tm, cw), lambda j, i: (i, _zcol(j, cols))),
                  pl.BlockSpec((SUBLANES, cw), lambda j, i: (jnp.maximum(i * hb - 1, 0), _zcol(j, cols))),
                  pl.BlockSpec((tm, cw), lambda j, i: (i, j)),
                  pl.BlockSpec((SUBLANES, cw), lambda j, i: (jnp.minimum((i + 1) * hb, last8), j)),
                  pl.BlockSpec((1, cw), lambda j, i: (0, j))],
        out_specs=[pl.BlockSpec((tm, cw), lambda j, i: (i, j)), pl.BlockSpec((1, cw), lambda j, i: (0, j))],
        out_shape=[jax.ShapeDtypeStruct((S, width), F32), jax.ShapeDtypeStruct((1, width), F32)],
        compiler_params=_cparams(("parallel", "arbitrary")))(z, z, dzs, dzs, mu)


def conv_fwd(u, w, b, width, tr, cb, name):
    S, C = u.shape
    hb = tr // CONV_HALO

    def body(u_ref, h_ref, w_ref, b_ref, y_ref):
        i = pl.program_id(1)
        halo = jnp.where(i == 0, 0.0, h_ref[...])
        win = jnp.concatenate([halo, u_ref[...]], axis=0)
        acc = jnp.zeros((tr, cb), F32) + b_ref[...]
        for d in range(width):
            sh = win if d == 0 else pltpu.roll(win, d, 0)
            acc = acc + sh[CONV_HALO:, :] * w_ref[width - 1 - d:width - d, :]
        y_ref[...] = acc

    return pl.pallas_call(
        body, name=name, grid=(C // cb, S // tr),
        in_specs=[pl.BlockSpec((tr, cb), lambda j, i: (i, j)),
                  pl.BlockSpec((CONV_HALO, cb), lambda j, i: (jnp.maximum(i * hb - 1, 0), j)),
                  pl.BlockSpec((CONV_HALO, cb), lambda j, i: (0, j)),
                  pl.BlockSpec((1, cb), lambda j, i: (0, j))],
        out_specs=pl.BlockSpec((tr, cb), lambda j, i: (i, j)),
        out_shape=jax.ShapeDtypeStruct((S, C), F32), compiler_params=_cparams(("parallel", "parallel")))(u, u, w, b)


def conv_bwd(u, dy, w, width, tr, cb, name):
    S, C = u.shape
    hb = tr // CONV_HALO
    nblk = S // tr
    last = S // CONV_HALO - 1

    def body(u_ref, uh_ref, d_ref, dh_ref, w_ref, du_ref, dw_ref, db_ref):
        i = pl.program_id(1)
        dyb = d_ref[...]
        uwin = jnp.concatenate([jnp.where(i == 0, 0.0, uh_ref[...]), u_ref[...]], axis=0)
        dwin = jnp.concatenate([dyb, jnp.where(i == nblk - 1, 0.0, dh_ref[...])], axis=0)

        @pl.when(i == 0)
        def _():
            dw_ref[...] = jnp.zeros_like(dw_ref)
            db_ref[...] = jnp.zeros_like(db_ref)

        acc = jnp.zeros((tr, cb), F32)
        for d in range(width):
            tap = width - 1 - d
            dsh = dwin if d == 0 else pltpu.roll(dwin, tr + CONV_HALO - d, 0)
            acc = acc + dsh[:tr, :] * w_ref[tap:tap + 1, :]
            ush = uwin if d == 0 else pltpu.roll(uwin, d, 0)
            dw_ref[tap:tap + 1, :] += jnp.sum(ush[CONV_HALO:, :] * dyb, 0, keepdims=True)
        du_ref[...] = acc
        db_ref[...] += jnp.sum(dyb, 0, keepdims=True)

    return pl.pallas_call(
        body, name=name, grid=(C // cb, nblk),
        in_specs=[pl.BlockSpec((tr, cb), lambda j, i: (i, j)),
                  pl.BlockSpec((CONV_HALO, cb), lambda j, i: (jnp.maximum(i * hb - 1, 0), j)),
                  pl.BlockSpec((tr, cb), lambda j, i: (i, j)),
                  pl.BlockSpec((CONV_HALO, cb), lambda j, i: (jnp.minimum((i + 1) * hb, last), j)),
                  pl.BlockSpec((CONV_HALO, cb), lambda j, i: (0, j))],
        out_specs=[pl.BlockSpec((tr, cb), lambda j, i: (i, j)),
                   pl.BlockSpec((CONV_HALO, cb), lambda j, i: (0, j)),
                   pl.BlockSpec((1, cb), lambda j, i: (0, j))],
        out_shape=[jax.ShapeDtypeStruct((S, C), F32), jax.ShapeDtypeStruct((CONV_HALO, C), F32),
                   jax.ShapeDtypeStruct((1, C), F32)],
        compiler_params=_cparams(("parallel", "arbitrary")))(u, u, dy, dy, w)


def _split2(x):
    hi = x.astype(BF16)
    return hi, (x - hi.astype(F32)).astype(BF16)


def _dot3(a, b, dims):
    ah, al = _split2(a)
    bh, bl = _split2(b)
    d = lambda p, q: lax.dot_general(p, q, dims, preferred_element_type=F32)
    return d(ah, bh) + (d(ah, bl) + d(al, bh))


@jax.custom_vjp
def _bnn(a, b):
    return _dot3(a, b, (((2,), (1,)), ((0,), (0,))))


@jax.custom_vjp
def _bnt(a, b):
    return _dot3(a, b, (((2,), (2,)), ((0,), (0,))))


@jax.custom_vjp
def _btn(a, b):
    return _dot3(a, b, (((1,), (1,)), ((0,), (0,))))


def _tri_sum(x, lower):
    h, c, _ = x.shape
    ti = lax.broadcasted_iota(jnp.int32, (h, c, c), 1)
    si = lax.broadcasted_iota(jnp.int32, (h, c, c), 2)
    m = (si <= ti if lower else si >= ti).astype(BF16)
    x1 = x.astype(BF16)
    r1 = x - x1.astype(F32)
    x2 = r1.astype(BF16)
    x3 = (r1 - x2.astype(F32)).astype(BF16)
    d = lambda q: lax.dot_general(m, q, (((2,), (1,)), ((0,), (0,))), preferred_element_type=F32)
    return d(x1) + (d(x2) + d(x3))


@jax.custom_vjp
def _cumsum_rows(x):
    return _tri_sum(x, True)


@jax.custom_vjp
def _rev_cumsum_rows(x):
    return _tri_sum(x, False)


_cumsum_rows.defvjp(lambda x: (_cumsum_rows(x), None), lambda _, g: (_rev_cumsum_rows(g),))
_rev_cumsum_rows.defvjp(lambda x: (_rev_cumsum_rows(x), None), lambda _, g: (_cumsum_rows(g),))


@jax.custom_vjp
def _unit_lower_inverse(a):
    h, c, _ = a.shape
    eye = (lax.broadcasted_iota(jnp.int32, (h, c, c), 1) == lax.broadcasted_iota(jnp.int32, (h, c, c), 2)).astype(F32)
    t, pw = eye + a, a
    for _ in range(int(math.log2(c)) - 1):
        pw = _bnn(pw, pw)
        t = t + _bnn(t, pw)
    return t


def _unit_lower_inverse_fwd(a):
    t = _unit_lower_inverse(a)
    return t, t


_unit_lower_inverse.defvjp(_unit_lower_inverse_fwd, lambda t, g: (_btn(t, _bnt(g, t)),))


_bnn.defvjp(lambda a, b: (_bnn(a, b), (a, b)), lambda res, g: (_bnt(g, res[1]), _btn(res[0], g)))
_bnt.defvjp(lambda a, b: (_bnt(a, b), (a, b)), lambda res, g: (_bnn(g, res[1]), _btn(g, res[0])))
_btn.defvjp(lambda a, b: (_btn(a, b), (a, b)), lambda res, g: (_bnt(res[1], g), _bnn(res[0], g)))


@jax.custom_vjp
def _known_inverse(a, t):
    return t


_known_inverse.defvjp(lambda a, t: (t, t), lambda t, g: (_btn(t, _bnt(g, t)), jnp.zeros_like(t)))


def scan_chunk(st0, r, lw, k, v, a, b, tinv=None):
    h, c, n = r.shape
    ti = lax.broadcasted_iota(jnp.int32, (h, c, c), 1)
    si = lax.broadcasted_iota(jnp.int32, (h, c, c), 2)
    incl = si <= ti
    strict = si < ti
    cum = _cumsum_rows(lw)
    p = jnp.exp(cum)
    pinv = jnp.exp(-cum)
    at = a * jnp.exp(cum - lw)
    bt, kt, rt = b * pinv, k * pinv, r * p
    a_ab = jnp.where(strict, _bnt(at, bt), 0.0)
    a_ak = jnp.where(strict, _bnt(at, kt), 0.0)
    m_rb = jnp.where(incl, _bnt(rt, bt), 0.0)
    m_rk = jnp.where(incl, _bnt(rt, kt), 0.0)
    tinv = _unit_lower_inverse(a_ab) if tinv is None else _known_inverse(a_ab, tinv)
    u = _bnn(tinv, _bnn(at, st0) + _bnn(a_ak, v))
    o = _bnn(rt, st0) + _bnn(m_rb, u) + _bnn(m_rk, v)
    cum_c = jnp.sum(lw, axis=1, keepdims=True)
    tail = jnp.exp(cum_c - cum)
    ki = lax.broadcasted_iota(jnp.int32, (h, n, n), 1)
    kj = lax.broadcasted_iota(jnp.int32, (h, n, n), 2)
    pc_col = jnp.sum(jnp.where(ki == kj, jnp.exp(cum_c), 0.0), axis=2, keepdims=True)
    st = st0 * pc_col + _btn(b * tail, u) + _btn(k * tail, v)
    return o, st, tinv


def _grid_ends(grid):
    g, c = pl.program_id(0), pl.program_id(1)
    return (g == 0) & (c == 0), (g == grid[0] - 1) & (c == grid[1] - 1)


def scan_fwd(r, lw, k, v, a, b, hb, name, side=None):
    H, S, N = r.shape
    C = SCAN_CHUNK
    nc = S // C
    side = _Side(side)
    grid = (H // hb, nc)

    def body(*refs):
        r_ref, lw_ref, k_ref, v_ref, a_ref, b_ref = refs[:6]
        side_src = refs[6:6 + side.n]
        o_ref, ck_ref, ti_ref = refs[6 + side.n:9 + side.n]
        side_out = refs[9 + side.n:9 + 2 * side.n]
        st_ref = refs[9 + 2 * side.n]
        finish = side.run(*_grid_ends(grid), side_src, side_out, refs[10 + 2 * side.n:])

        @pl.when(pl.program_id(1) == 0)
        def _():
            st_ref[...] = jnp.zeros_like(st_ref)

        st0 = st_ref[...]
        ck_ref[...] = st0[:, None]
        o, st, tinv = scan_chunk(st0, r_ref[...], lw_ref[...], k_ref[...], v_ref[...], a_ref[...], b_ref[...])
        o_ref[...] = o
        ti_ref[...] = tinv[:, None]
        st_ref[...] = st
        finish()

    seq = pl.BlockSpec((hb, C, N), lambda g, c: (g, c, 0))
    return pl.pallas_call(
        body, name=name, grid=grid, in_specs=[seq] * 6 + side.in_specs,
        out_specs=[seq, pl.BlockSpec((hb, 1, N, N), lambda g, c: (g, c, 0, 0)),
                   pl.BlockSpec((hb, 1, C, C), lambda g, c: (g, c, 0, 0))] + side.out_specs,
        out_shape=[jax.ShapeDtypeStruct((H, S, N), F32), jax.ShapeDtypeStruct((H, nc, N, N), F32),
                   jax.ShapeDtypeStruct((H, nc, C, C), F32)] + side.out_shapes,
        scratch_shapes=[pltpu.VMEM((hb, N, N), F32)] + side.scratch,
        compiler_params=_cparams(("arbitrary", "arbitrary")))(r, lw, k, v, a, b, *side.srcs)


def scan_bwd(r, lw, k, v, a, b, ck, ti, do, dr_add, dk_add, dv_add, hb, name, side=None):
    H, S, N = r.shape
    C = SCAN_CHUNK
    nc = S // C
    side = _Side(side)
    grid = (H // hb, nc)

    def body(*refs):
        r_ref, lw_ref, k_ref, v_ref, a_ref, b_ref, ck_ref, ti_ref, do_ref, ra_ref, ka_ref, va_ref = refs[:12]
        side_src = refs[12:12 + side.n]
        dr_ref, dlw_ref, dk_ref, dv_ref, da_ref, db_ref = refs[12 + side.n:18 + side.n]
        side_out = refs[18 + side.n:18 + 2 * side.n]
        dst_ref = refs[18 + 2 * side.n]
        finish = side.run(*_grid_ends(grid), side_src, side_out, refs[19 + 2 * side.n:])

        @pl.when(pl.program_id(1) == 0)
        def _():
            dst_ref[...] = jnp.zeros_like(dst_ref)

        st0, tinv = ck_ref[...][:, 0], ti_ref[...][:, 0]
        chunk = lambda *args: scan_chunk(*args, tinv=tinv)[:2]
        _, vjp = jax.vjp(chunk, st0, r_ref[...], lw_ref[...], k_ref[...], v_ref[...], a_ref[...], b_ref[...])
        dst0, dr, dlw, dk, dv, da, db = vjp((do_ref[...], dst_ref[...]))
        dr_ref[...], dlw_ref[...], dk_ref[...] = dr + ra_ref[...], dlw, dk + ka_ref[...]
        dv_ref[...], da_ref[...], db_ref[...] = dv + va_ref[...], da, db
        dst_ref[...] = dst0
        finish()

    seq = pl.BlockSpec((hb, C, N), lambda g, c: (g, nc - 1 - c, 0))
    per_chunk = lambda n: pl.BlockSpec((hb, 1, n, n), lambda g, c: (g, nc - 1 - c, 0, 0))
    return pl.pallas_call(
        body, name=name, grid=grid,
        in_specs=[seq] * 6 + [per_chunk(N), per_chunk(C)] + [seq] * 4 + side.in_specs,
        out_specs=[seq] * 6 + side.out_specs, out_shape=[jax.ShapeDtypeStruct((H, S, N), F32)] * 6 + side.out_shapes,
        scratch_shapes=[pltpu.VMEM((hb, N, N), F32)] + side.scratch,
        compiler_params=_cparams(("arbitrary", "arbitrary")))(r, lw, k, v, a, b, ck, ti, do, dr_add, dk_add, dv_add,
                                                              *side.srcs)


def loss_head(y, target, tm, name):
    S, D = y.shape

    def body(y_ref, t_ref, dy_ref, l_ref):
        err = y_ref[...] - t_ref[...]
        dy_ref[...] = err * (1.0 / D)

        @pl.when(pl.program_id(0) == 0)
        def _():
            l_ref[...] = jnp.zeros_like(l_ref)

        l_ref[...] += 0.5 * jnp.sum(jnp.mean(err * err, -1, keepdims=True), 0, keepdims=True)

    row = pl.BlockSpec((tm, D), lambda i: (i, 0))
    return pl.pallas_call(
        body, name=name, grid=(S // tm,), in_specs=[row, row],
        out_specs=[row, pl.BlockSpec((SUBLANES, LANES), lambda i: (0, 0))],
        out_shape=[jax.ShapeDtypeStruct((S, D), F32), jax.ShapeDtypeStruct((SUBLANES, LANES), F32)],
        compiler_params=_cparams(("arbitrary",)))(y, target)


def adamw(parts, w, m, v, name):
    R, C = w.shape
    n_parts = parts.shape[0]
    tr = R
    if R * C > ADAM_BLOCK_ELEMS:
        tr = _tile(R, [t for t in (512, 256, 128, 64, 32, 16) if t * C <= ADAM_BLOCK_ELEMS])
    c1 = 1.0 / (1.0 - ADAM_B1 ** ADAM_STEP)
    c2 = 1.0 / (1.0 - ADAM_B2 ** ADAM_STEP)

    def body(p_ref, w_ref, m_ref, v_ref, g_ref, d_ref, nm_ref, nv_ref):
        g = p_ref[0].astype(F32)
        for j in range(1, n_parts):
            g = g + p_ref[j].astype(F32)
        nm = ADAM_B1 * m_ref[...] + (1.0 - ADAM_B1) * g
        nv = ADAM_B2 * v_ref[...] + (1.0 - ADAM_B2) * (g * g)
        g_ref[...] = g
        nm_ref[...] = nm
        nv_ref[...] = nv
        d_ref[...] = -ADAM_LR * ((nm * c1) / (jnp.sqrt(nv * c2) + ADAM_EPS) + ADAM_WD * w_ref[...])

    blk = pl.BlockSpec((tr, C), lambda i: (i, 0))
    return pl.pallas_call(
        body, name=name, grid=(R // tr,), in_specs=[pl.BlockSpec((n_parts, tr, C), lambda i: (0, i, 0)), blk, blk, blk],
        out_specs=[blk] * 4, out_shape=[jax.ShapeDtypeStruct((R, C), F32)] * 4,
        compiler_params=_cparams(("parallel",)))(parts, w, m, v)


def _to_heads(a2d):
    s, d = a2d.shape
    return a2d.reshape(s, d // RWKV_HEAD, RWKV_HEAD).transpose(1, 0, 2)


def _from_heads(a3d):
    h, s, n = a3d.shape
    return a3d.transpose(1, 0, 2).reshape(s, h * n)


def _cols_joined(g):
    return g.transpose(1, 0, 2).reshape(g.shape[1], N_DEV * g.shape[2])


def _cols_split(w):
    return w.reshape(w.shape[0], N_DEV, w.shape[1] // N_DEV).transpose(1, 0, 2)


def _round_up(n, m):
    return -(-n // m) * m


def _step(a):
    x, mem, target = a['x'][0], a['mem'][0], a['loss_target'][0]
    S, D = x.shape
    M = mem.shape[0]
    DR = a['rwkv_w0'].shape[1]
    H = DR // RWKV_HEAD
    DC = a['conv_b'].shape[1]
    r_decay, r_iclr, r_gate = a['rwkv_w_up'].shape[1], a['rwkv_a_up'].shape[1], a['rwkv_g_up'].shape[1]
    n_lora = r_decay + r_iclr + r_gate
    lora_w = _round_up(n_lora, LANES)
    n_rwkv = 3 * DR + n_lora
    zr_w = 3 * DR + lora_w
    pad_cols = zr_w - n_rwkv
    conv_taps = a['conv_w'].shape[1]
    assert conv_taps - 1 <= CONV_HALO and S % SCAN_CHUNK == 0

    def shard(n, pre=''):
        w = a[pre + n]
        if n == 'conv_w':
            return jnp.pad(w.reshape(conv_taps, w.shape[-1]), ((0, CONV_HALO - conv_taps), (0, 0)))
        return w.reshape(w.shape[1:])

    Wg, W = {}, {}

    def gathers(names):
        return [(shard(n).astype(BF16), 'gather2') for n in names]

    def gathered(names, results):
        for n, g in zip(names, results):
            Wg[n] = g
            if n in ROW_SHARDED:
                W[n] = g.reshape(N_DEV * g.shape[1], g.shape[2])

    with_mm_z = ['mlp_w1']
    with_scan_fwd = ['proj_rwkv', 'proj_conv', 'w_out']
    with_mm_w1 = ['mlp_w2']
    with_scan_bwd = ['xattn_wo', 'xattn_wq', 'xattn_wk', 'xattn_wv', 'w_out', 'proj_rwkv', 'proj_conv']
    gathered(['w_in'], [exchange(shard('w_in').astype(BF16), 'gather2', "gather_w_in")])
    lora_src = jnp.concatenate([shard('rwkv_w_up'), shard('rwkv_a_up'), shard('rwkv_g_up')], axis=0).astype(BF16)
    lora_all = exchange(lora_src, 'gather', "gather_lora")
    conv_w = _cols_joined(exchange(shard('conv_w'), 'gather', "gather_conv_w"))

    w_in = _cols_joined(Wg['w_in'])
    o_conv, o_gate = n_rwkv, n_rwkv + 2 * DC
    w_in_p = jnp.concatenate([w_in[:, o_gate:], w_in[:, :3 * DR], w_in[:, o_conv:o_gate], w_in[:, 3 * DR:n_rwkv],
                              jnp.zeros((D, pad_cols), BF16)], axis=1)
    c_gr, c_gc, c_rkv, c_u = 0, D, 2 * D, 2 * D + 3 * DR
    c_cg, c_lora = c_u + DC, c_u + 2 * DC
    assert c_rkv % lora_w == 0 and c_lora % lora_w == 0 and (3 * DR) % lora_w == 0 and c_u % DC == 0 and c_rkv % DR == 0
    shift_cols = (3 * DR // lora_w, c_rkv // lora_w, c_lora // lora_w)
    lora_full = _cols_joined(lora_all)
    w_lora = jnp.zeros((lora_w, 3 * DR), BF16)
    w_lora = w_lora.at[:r_decay, :DR].set(lora_full[:r_decay])
    w_lora = w_lora.at[r_decay:r_decay + r_iclr, DR:2 * DR].set(lora_full[r_decay:r_decay + r_iclr])
    w_lora = w_lora.at[r_decay + r_iclr:n_lora, 2 * DR:].set(lora_full[r_decay + r_iclr:])
    mu_p = jnp.pad(a['rwkv_shift_mix'], ((0, 0), (0, pad_cols)))
    x_bf = x.astype(BF16)

    hp = lambda n: a[n].reshape(H, 1, RWKV_HEAD)
    w0_h, a0_h, kk_h, ka_h, gng_h, gnb_h = (hp(n) for n in ('rwkv_w0', 'rwkv_a0', 'rwkv_k_k', 'rwkv_k_a',
                                                             'rwkv_gn_g', 'rwkv_gn_b'))
    rk_h = a['rwkv_r_k'].reshape(H, 1, RWKV_HEAD)
    ln_mem_g, ln_mem_b = a['ln_mem_g'].reshape(1, D), a['ln_mem_b'].reshape(1, D)

    tm_d = _tile(S, (128, 64, 32, 16, 8))
    rows_many = _Rows2D(S, _tile(S, (64, 32, 16, 8)))
    tm_a = _tile(S, (512, 256, 128, 64))
    rows = _Rows2D(S, tm_d)
    rows_mem = _Rows2D(M, _tile(M, (64, 32, 16, 8)))
    rows_ff = _Rows2D(S, _tile(S, (64, 32, 16, 8)))
    hb = _tile(H, (16, 8, 4, 2, 1))
    heads = _RowsHeads(H, S, _tile(H, (4, 2, 1)), _tile(S, (256, 128, 64)))
    tr_conv = _tile(S, (512, 256, 128, 64, 32))
    cb_conv = _tile(DC, (256, 128))
    fn_lora = make_fn_lora(r_decay, r_iclr, r_gate)
    lane_blk = lambda off, width: off // width

    mem_n, = ew_fwd(rows_mem, _as_bf16(fn_ln), [mem], [ln_mem_g, ln_mem_b], "ln_mem")
    z, *got = matmul(x_bf, w_in_p, 'nn', "mm_z", side=gathers(with_mm_z))
    gathered(with_mm_z, got)
    zs = token_shift_fwd(z, shift_cols, lora_w, zr_w, mu_p, tm_a, "token_shift")
    lora, = ew_fwd(rows, _as_bf16(fn_lora), [(zs, lora_w, lane_blk(3 * DR, lora_w))], [], "lora_act")
    up = matmul(lora, w_lora, 'nn', "mm_lora_up")
    r_h, k_h, v_h = (_to_heads(zs[:, i * DR:(i + 1) * DR]) for i in range(3))
    wl_h, al_h, g_h = (_to_heads(up[:, i * DR:(i + 1) * DR]) for i in range(3))
    pre_rows, pre_pars = [k_h, wl_h, al_h], [w0_h, a0_h, kk_h, ka_h]
    lw_h, k2_h, na_h, b_h = ew_fwd(heads, fn_rwkv_pre, pre_rows, pre_pars, "rwkv_pre")
    o_h, ckpt, tinvs, *got = scan_fwd(r_h, lw_h, k2_h, v_h, na_h, b_h, hb, "scan_fwd", side=gathers(with_scan_fwd))
    gathered(with_scan_fwd, got)
    post_rows, post_pars = [o_h, r_h, k2_h, v_h, g_h], [gng_h, gnb_h, rk_h]
    or_h, = ew_fwd(heads, _as_bf16(fn_rwkv_post), post_rows, post_pars, "rwkv_post")
    o_r = _from_heads(or_h)

    glu_rows = [(z, DC, lane_blk(c_u, DC)), (z, DC, lane_blk(c_cg, DC))]
    u, = ew_fwd(rows, fn_glu, glu_rows, [], "glu")
    yc = conv_fwd(u, conv_w, a['conv_b'], conv_taps, tr_conv, cb_conv, "conv")
    cln = [a['conv_ln_g'], a['conv_ln_b']]
    o_c, = ew_fwd(rows, _as_bf16(fn_ln_silu), [yc], cln, "conv_ln_silu")

    G = {}
    rows_split = lambda g: g.reshape(N_DEV, g.shape[0] // N_DEV, g.shape[1])

    Sx, recv = {}, {}

    def to_sibling(names):
        return [(G[n], 'scatter_sib') for n in names]

    def pair_sums(names, landed):
        for n, l in zip(names, landed):
            Sx[n] = pair_add(G[n], l, "pair_add_" + n)

    pr, *got = matmul(o_r, Wg['proj_rwkv'], 'nn', "mm_proj_rwkv", b_dev=True, side=gathers(['xattn_wq']))
    gathered(['xattn_wq'], got)
    pc, *got = matmul(o_c, Wg['proj_conv'], 'nn', "mm_proj_conv", b_dev=True, side=gathers(['xattn_wk']))
    gathered(['xattn_wk'], got)
    merge_rows = [(z, D, lane_blk(c_gr, D)), (z, D, lane_blk(c_gc, D)), pr, pc]
    merged, = ew_fwd(rows_many, _as_bf16(fn_merge), merge_rows, [], "merge")
    t1, *got = matmul(merged, W['w_out'], 'nn', "mm_w_out", side=gathers(['xattn_wv']))
    gathered(['xattn_wv'], got)
    ln1 = [a['ln1_g'], a['ln1_b']]
    h1, h1_bf = ew_fwd(rows, fn_ln_res_both, [x, t1], ln1, "ln1")

    q, *got = matmul(h1_bf, W['xattn_wq'], 'nn', "mm_q", side=gathers(['xattn_wo']))
    gathered(['xattn_wo'], got)
    kx = matmul(mem_n, W['xattn_wk'], 'nn', "mm_k")
    vx = matmul(mem_n, W['xattn_wv'], 'nn', "mm_v")
    oa = attn_fwd(q, kx, vx, tm_a, "attn")
    ca = matmul(oa, W['xattn_wo'], 'nn', "mm_wo")
    ln2 = [a['ln2_g'], a['ln2_b']]
    h2, h2_bf = ew_fwd(rows, fn_ln_res_both, [h1, ca], ln2, "ln2")

    f1, *got = matmul(h2_bf, Wg['mlp_w1'], 'nn', "mm_w1", b_dev=True, side=gathers(with_mm_w1))
    gathered(with_mm_w1, got)
    act, = ew_fwd(rows_ff, _as_bf16(fn_relu2), [f1], [], "relu2")
    ff = matmul(act, W['mlp_w2'], 'nn', "mm_w2")
    ln3 = [a['ln3_g'], a['ln3_b']]
    h3, = ew_fwd(rows, fn_ln_res, [h2, ff], ln3, "ln3")
    dh3, loss_rows = loss_head(h3, target, tm_d, "loss")

    dh2a, dff, G['ln3_g'], G['ln3_b'] = ew_bwd(rows, fn_ln_res, [h2, ff], ln3, [dh3], (0, 1), (0, 1), "ln3_bwd",
                                               dr_dtypes=[F32, BF16])
    dact = matmul(dff, W['mlp_w2'], 'nt', "mm_w2_dx")
    G['mlp_w2'] = rows_split(matmul(act, dff, 'tn', "mm_w2_dw", out_dtype=BF16))
    df1, = ew_bwd(rows_ff, fn_relu2, [f1], [], [dact], (0,), (), "relu2_bwd", dr_dtypes=[BF16])
    dh2, *got = matmul(df1, Wg['mlp_w1'], 'nt', "mm_w1_dx", add=dh2a, b_dev=True, side=to_sibling(['mlp_w2']))
    pair_sums(['mlp_w2'], got)
    G['mlp_w1'], recv['mlp_w2'] = matmul(h2_bf, df1, 'tn', "mm_w1_dw", out_dtype=BF16, out_dev=True,
                                         side=[(Sx['mlp_w2'], 'scatter_chips')])

    dh1a, dca, G['ln2_g'], G['ln2_b'] = ew_bwd(rows, fn_ln_res, [h1, ca], ln2, [dh2], (0, 1), (0, 1), "ln2_bwd",
                                               dr_dtypes=[F32, BF16])
    doa, *got = matmul(dca, W['xattn_wo'], 'nt', "mm_wo_dx", side=to_sibling(['mlp_w1']))
    pair_sums(['mlp_w1'], got)
    G['xattn_wo'] = rows_split(matmul(oa, dca, 'tn', "mm_wo_dw", out_dtype=BF16))
    dq, dkx, dvx = attn_bwd(q, kx, vx, doa, tm_a, "attn_bwd")
    dh1, *got = matmul(dq, W['xattn_wq'], 'nt', "mm_q_dx", add=dh1a, side=to_sibling(['xattn_wo']))
    pair_sums(['xattn_wo'], got)
    G['xattn_wq'] = rows_split(matmul(h1_bf, dq, 'tn', "mm_q_dw", out_dtype=BF16))
    G['xattn_wk'] = rows_split(matmul(mem_n, dkx, 'tn', "mm_k_dw", out_dtype=BF16))
    G['xattn_wv'] = rows_split(matmul(mem_n, dvx, 'tn', "mm_v_dw", out_dtype=BF16))
    dmem_k = matmul(dkx, W['xattn_wk'], 'nt', "mm_k_dx")
    dmem_n = matmul(dvx, W['xattn_wv'], 'nt', "mm_v_dx", add=dmem_k)
    G['ln_mem_g'], G['ln_mem_b'] = ew_bwd(rows_mem, fn_ln, [mem], [ln_mem_g, ln_mem_b], [dmem_n], (), (0, 1),
                                          "ln_mem_bwd")

    dxa, dt1, G['ln1_g'], G['ln1_b'] = ew_bwd(rows, fn_ln_res, [x, t1], ln1, [dh1], (0, 1), (0, 1), "ln1_bwd",
                                              dr_dtypes=[F32, BF16])
    xattn_qkv = ['xattn_wq', 'xattn_wk', 'xattn_wv']
    dmerged, *got = matmul(dt1, W['w_out'], 'nt', "mm_w_out_dx", side=to_sibling(xattn_qkv))
    pair_sums(xattn_qkv, got)
    G['w_out'] = rows_split(matmul(merged, dt1, 'tn', "mm_w_out_dw", out_dtype=BF16))
    dzgr, dzgc, dpr, dpc = ew_bwd(rows_many, fn_merge, merge_rows, [], [dmerged], (0, 1, 2, 3), (), "merge_bwd",
                                  dr_dtypes=[BF16] * 4)
    do_r, *got = matmul(dpr, Wg['proj_rwkv'], 'nt', "mm_proj_rwkv_dx", b_dev=True, side=to_sibling(['w_out']))
    pair_sums(['w_out'], got)
    G['proj_rwkv'] = matmul(o_r, dpr, 'tn', "mm_proj_rwkv_dw", out_dtype=BF16, out_dev=True)
    do_c, *got = matmul(dpc, Wg['proj_conv'], 'nt', "mm_proj_conv_dx", b_dev=True, side=to_sibling(['proj_rwkv']))
    pair_sums(['proj_rwkv'], got)
    G['proj_conv'] = matmul(o_c, dpc, 'tn', "mm_proj_conv_dw", out_dtype=BF16, out_dev=True)
    pair_sums(['proj_conv'], [exchange(G['proj_conv'], 'scatter_sib', "scatter_sib_proj_conv")])

    dyc, G['conv_ln_g'], G['conv_ln_b'] = ew_bwd(rows, fn_ln_silu, [yc], cln, [do_c], (0,), (0, 1), "conv_ln_silu_bwd")
    du, dconv_w, G['conv_b'] = conv_bwd(u, dyc, conv_w, conv_taps, tr_conv, cb_conv, "conv_bwd")
    dzu, dzcg = ew_bwd(rows, fn_glu, glu_rows, [], [du], (0, 1), (), "glu_bwd", dr_dtypes=[BF16] * 2)

    do_r_h = _to_heads(do_r)
    do_h, dr1_h, dk2a_h, dv1_h, dg_h, dgng, dgnb, drk = ew_bwd(
        heads, fn_rwkv_post, post_rows, post_pars, [do_r_h], (0, 1, 2, 3, 4), (0, 1, 2), "rwkv_post_bwd",
        dr_dtypes=[F32, F32, F32, F32, BF16])
    dr_h, dlw_h, dk2_h, dv_h, dna_h, db_h, *got = scan_bwd(
        r_h, lw_h, k2_h, v_h, na_h, b_h, ckpt, tinvs, do_h, dr1_h, dk2a_h, dv1_h, hb, "scan_bwd",
        side=[(Sx[n], 'scatter_chips') for n in with_scan_bwd])
    recv.update(zip(with_scan_bwd, got))
    dk_h, dwl_h, dal_h, dw0, da0, dkk, dka = ew_bwd(
        heads, fn_rwkv_pre, pre_rows, pre_pars, [dlw_h, dk2_h, dna_h, db_h], (0, 1, 2), (0, 1, 2, 3), "rwkv_pre_bwd",
        dr_dtypes=[F32, BF16, BF16])
    G['rwkv_w0'], G['rwkv_a0'], G['rwkv_k_k'], G['rwkv_k_a'] = (t.reshape(1, DR) for t in (dw0, da0, dkk, dka))
    G['rwkv_gn_g'], G['rwkv_gn_b'] = dgng.reshape(1, DR), dgnb.reshape(1, DR)
    G['rwkv_r_k'] = drk.reshape(1, H, RWKV_HEAD)
    dup = jnp.concatenate([_from_heads(dwl_h), _from_heads(dal_h), _from_heads(dg_h)], axis=1)
    dlora = matmul(dup, w_lora, 'nt', "mm_lora_up_dx")
    dw_lora = matmul(lora, dup, 'tn', "mm_lora_up_dw", out_dtype=BF16)
    d_lora_stack = jnp.concatenate([dw_lora[:r_decay, :DR], dw_lora[r_decay:r_decay + r_iclr, DR:2 * DR],
                                    dw_lora[r_decay + r_iclr:n_lora, 2 * DR:]], axis=0)
    dzs_lora, = ew_bwd(rows, fn_lora, [(zs, lora_w, lane_blk(3 * DR, lora_w))], [], [dlora], (0,), (), "lora_act_bwd")
    dzs = jnp.concatenate([_from_heads(dr_h), _from_heads(dk_h), _from_heads(dv_h), dzs_lora], axis=1)
    dzr, dmu = token_shift_bwd(z, dzs, shift_cols, lora_w, zr_w, mu_p, tm_a, "token_shift_bwd")
    G['rwkv_shift_mix'] = dmu[:, :n_rwkv]
    dz = jnp.concatenate([dzgr, dzgc, dzr[:, :3 * DR].astype(BF16), dzu, dzcg, dzr[:, 3 * DR:].astype(BF16)], axis=1)
    dw_in_p, recv['mlp_w1'] = matmul(x_bf, dz, 'tn', "mm_z_dw", out_dtype=BF16, side=[(Sx['mlp_w1'], 'scatter_chips')])
    G['w_in'] = _cols_split(jnp.concatenate([dw_in_p[:, c_rkv:c_u], dw_in_p[:, c_lora:c_lora + n_lora],
                                             dw_in_p[:, c_u:c_lora], dw_in_p[:, :c_rkv]], axis=1))
    pair_sums(['w_in'], [exchange(G['w_in'], 'scatter_sib', "scatter_sib_w_in")])
    grad_x, recv['w_in'] = matmul(dz, w_in_p, 'nt', "mm_z_dx", add=dxa, side=[(Sx['w_in'], 'scatter_chips')])

    out = {}

    def update(n, parts, w, m, v):
        res = adamw(parts, w, m, v, "adamw_" + n)
        out[n] = [t[:conv_taps].reshape(a[n].shape) if n == 'conv_w' else t.reshape(a[n].shape) for t in res]

    for n in ['w_in', 'mlp_w2', 'mlp_w1'] + with_scan_bwd:
        update(n, recv[n], shard(n), shard(n, 'm_'), shard(n, 'v_'))
    recv_lora = exchange(_cols_split(d_lora_stack), 'scatter', "scatter_lora")
    lo = 0
    for n, r_ in (('rwkv_w_up', r_decay), ('rwkv_a_up', r_iclr), ('rwkv_g_up', r_gate)):
        update(n, recv_lora[:, lo:lo + r_], shard(n), shard(n, 'm_'), shard(n, 'v_'))
        lo += r_
    recv_conv = exchange(_cols_split(dconv_w.astype(BF16)), 'scatter', "scatter_conv_w")
    update('conv_w', recv_conv, shard('conv_w'), shard('conv_w', 'm_'), shard('conv_w', 'v_'))

    small_sizes = [a[n].size for n in SMALL]
    n_small = sum(small_sizes) + 1
    n_pack = _round_up(n_small, SUBLANES * LANES)
    pack = lambda get, last: jnp.pad(jnp.concatenate([get(n).reshape(-1) for n in SMALL] + [last]),
                                     (0, n_pack - n_small)).reshape(n_pack // LANES, LANES)
    zero1 = jnp.zeros((1,), F32)
    g_pack = pack(lambda n: G[n], loss_rows[0, :1])
    parts_small = exchange(g_pack, 'gather', "gather_small")
    res = adamw(parts_small, pack(lambda n: a[n], zero1), pack(lambda n: a['m_' + n], zero1),
                pack(lambda n: a['v_' + n], zero1), "adamw_small")
    res = [t.reshape(-1) for t in res]
    o_ = 0
    for n, sz in zip(SMALL, small_sizes):
        out[n] = [t[o_:o_ + sz].reshape(a[n].shape) for t in res]
        o_ += sz
    loss = res[0][o_]

    return (loss, grad_x[None], *[out[n][0] for n in WEIGHTS], *[out[n][1] for n in WEIGHTS],
            *[out[n][2] for n in WEIGHTS], *[out[n][3] for n in WEIGHTS])


def kernel(x, mem, w_in, rwkv_shift_mix, rwkv_w0, rwkv_w_up, rwkv_a0, rwkv_a_up, rwkv_g_up, rwkv_k_k, rwkv_k_a, rwkv_r_k, rwkv_gn_g, rwkv_gn_b, conv_w, conv_b, conv_ln_g, conv_ln_b, proj_rwkv, proj_conv, w_out, ln1_g, ln1_b, ln_mem_g, ln_mem_b, xattn_wq, xattn_wk, xattn_wv, xattn_wo, ln2_g, ln2_b, mlp_w1, mlp_w2, ln3_g, ln3_b, loss_target, m_w_in, m_rwkv_shift_mix, m_rwkv_w0, m_rwkv_w_up, m_rwkv_a0, m_rwkv_a_up, m_rwkv_g_up, m_rwkv_k_k, m_rwkv_k_a, m_rwkv_r_k, m_rwkv_gn_g, m_rwkv_gn_b, m_conv_w, m_conv_b, m_conv_ln_g, m_conv_ln_b, m_proj_rwkv, m_proj_conv, m_w_out, m_ln1_g, m_ln1_b, m_ln_mem_g, m_ln_mem_b, m_xattn_wq, m_xattn_wk, m_xattn_wv, m_xattn_wo, m_ln2_g, m_ln2_b, m_mlp_w1, m_mlp_w2, m_ln3_g, m_ln3_b, v_w_in, v_rwkv_shift_mix, v_rwkv_w0, v_rwkv_w_up, v_rwkv_a0, v_rwkv_a_up, v_rwkv_g_up, v_rwkv_k_k, v_rwkv_k_a, v_rwkv_r_k, v_rwkv_gn_g, v_rwkv_gn_b, v_conv_w, v_conv_b, v_conv_ln_g, v_conv_ln_b, v_proj_rwkv, v_proj_conv, v_w_out, v_ln1_g, v_ln1_b, v_ln_mem_g, v_ln_mem_b, v_xattn_wq, v_xattn_wk, v_xattn_wv, v_xattn_wo, v_ln2_g, v_ln2_b, v_mlp_w1, v_mlp_w2, v_ln3_g, v_ln3_b):
    return _step(dict(locals()))
```

```python
import functools
import math

import jax
import jax.numpy as jnp
from jax import lax
from jax.experimental import pallas as pl
from jax.experimental.pallas import tpu as pltpu

F32 = jnp.float32
BF16 = jnp.bfloat16

N_DEV = 8
RWKV_HEAD = 64
SCAN_CHUNK = 64
XATTN_HEADS = 4
CONV_HALO = 32
LN_EPS = 1e-5
GN_EPS = 64e-5
ALPHA = float(2.0 ** 0.25)
ADAM_LR, ADAM_B1, ADAM_B2, ADAM_EPS, ADAM_WD, ADAM_STEP = 0.001, 0.9, 0.999, 1e-08, 0.01, 10
LANES = 128
SUBLANES = 8
VMEM_LIMIT = 56 * 1024 * 1024
ADAM_BLOCK_ELEMS = 256 * 1024
PAIR_BLOCK_ELEMS = 1024 * 1024

WEIGHTS = ['w_in', 'rwkv_shift_mix', 'rwkv_w0', 'rwkv_w_up', 'rwkv_a0', 'rwkv_a_up', 'rwkv_g_up', 'rwkv_k_k',
           'rwkv_k_a', 'rwkv_r_k', 'rwkv_gn_g', 'rwkv_gn_b', 'conv_w', 'conv_b', 'conv_ln_g', 'conv_ln_b',
           'proj_rwkv', 'proj_conv', 'w_out', 'ln1_g', 'ln1_b', 'ln_mem_g', 'ln_mem_b', 'xattn_wq', 'xattn_wk',
           'xattn_wv', 'xattn_wo', 'ln2_g', 'ln2_b', 'mlp_w1', 'mlp_w2', 'ln3_g', 'ln3_b']
COL_SHARDED = ['w_in', 'rwkv_w_up', 'rwkv_a_up', 'rwkv_g_up', 'conv_w', 'proj_rwkv', 'proj_conv', 'mlp_w1']
ROW_SHARDED = ['w_out', 'xattn_wq', 'xattn_wk', 'xattn_wv', 'xattn_wo', 'mlp_w2']
BIG = ['w_in', 'rwkv_w_up', 'rwkv_a_up', 'rwkv_g_up', 'conv_w', 'proj_rwkv', 'proj_conv', 'w_out', 'xattn_wq',
       'xattn_wk', 'xattn_wv', 'xattn_wo', 'mlp_w1', 'mlp_w2']
SMALL = [w for w in WEIGHTS if w not in BIG]


def _cparams(dims):
    return pltpu.CompilerParams(dimension_semantics=dims, vmem_limit_bytes=VMEM_LIMIT)


def _tile(n, cands):
    for c in cands:
        if n % c == 0:
            return c
    return n


N_CHIP = N_DEV // 2
SEMS_PER_EXCHANGE = N_DEV + 2
CHIP_XORS = (2, 4, 6)


def _exchange_shape(src, kind):
    return {'gather': (N_DEV,) + src.shape, 'gather2': (N_DEV,) + src.shape, 'scatter': src.shape,
            'scatter_sib': (N_CHIP,) + src.shape[1:], 'scatter_chips': src.shape}[kind]


def _exchange_copies(e, src_ref, out_ref, kind, send_sems, recv_sems, local_sems):
    x, y, c = lax.axis_index("x"), lax.axis_index("y"), lax.axis_index("c")
    me, chip = 4 * x + 2 * y + c, 2 * x + y
    sibling = (x, y, 1 - c)
    base = e * SEMS_PER_EXCHANGE

    def remote(src, dst, idx, dev):
        return pltpu.make_async_remote_copy(src_ref=src, dst_ref=dst, send_sem=send_sems.at[base + idx],
                                            recv_sem=recv_sems.at[base + idx], device_id=dev,
                                            device_id_type=pl.DeviceIdType.MESH)

    def peer(k):
        return x ^ ((k >> 2) & 1), y ^ ((k >> 1) & 1), c ^ (k & 1)

    first, second = [], []
    if kind in ('gather', 'gather2'):
        first.append(pltpu.make_async_copy(src_ref, out_ref.at[me], local_sems.at[e]))
        for k in (range(1, N_DEV) if kind == 'gather' else (1,) + CHIP_XORS):
            first.append(remote(src_ref, out_ref.at[me], k - 1, peer(k)))
        if kind == 'gather2':
            for i, k in enumerate(CHIP_XORS):
                second.append(remote(out_ref.at[me ^ k], out_ref.at[me ^ k], N_DEV - 1 + i, sibling))
    elif kind == 'scatter':
        first.append(pltpu.make_async_copy(src_ref.at[me], out_ref.at[me], local_sems.at[e]))
        for k in range(1, N_DEV):
            px, py, pc = peer(k)
            first.append(remote(src_ref.at[4 * px + 2 * py + pc], out_ref.at[me], k - 1, (px, py, pc)))
    elif kind == 'scatter_sib':
        for q in range(N_CHIP):
            first.append(remote(src_ref.at[2 * q + 1 - c], out_ref.at[q], q, sibling))
    else:
        assert kind == 'scatter_chips', kind
        first.append(pltpu.make_async_copy(src_ref.at[chip], out_ref.at[chip], local_sems.at[e]))
        for k in CHIP_XORS:
            first.append(remote(src_ref.at[chip ^ (k >> 1)], out_ref.at[chip], k - 1, peer(k)))
    return first, second


class _Side:
    def __init__(self, items):
        self.items = list(items or [])
        self.n = len(self.items)
        any_spec = pl.BlockSpec(memory_space=pl.ANY)
        self.srcs = [s for s, _ in self.items]
        self.in_specs = [any_spec] * self.n
        self.out_specs = [any_spec] * self.n
        self.out_shapes = [jax.ShapeDtypeStruct(_exchange_shape(s, kind), s.dtype) for s, kind in self.items]
        self.scratch = [pltpu.SemaphoreType.DMA((self.n * SEMS_PER_EXCHANGE,)),
                        pltpu.SemaphoreType.DMA((self.n * SEMS_PER_EXCHANGE,)),
                        pltpu.SemaphoreType.DMA((self.n,))] if self.n else []

    def _copies(self, src_refs, out_refs, sems):
        both = [_exchange_copies(e, src_refs[e], out_refs[e], kind, *sems) for e, (_, kind) in enumerate(self.items)]
        return [cp for f, _ in both for cp in f], [cp for _, s in both for cp in s]

    def start(self, src_refs, out_refs, sems):
        for cp in self._copies(src_refs, out_refs, sems)[0]:
            cp.start()

    def finish(self, src_refs, out_refs, sems):
        first, second = self._copies(src_refs, out_refs, sems)
        for cp in first:
            cp.wait()
        for cp in second:
            cp.start()
        for cp in second:
            cp.wait()

    def run(self, first, last, src_refs, out_refs, sems):
        if not self.n:
            return lambda: None
        pl.when(first)(lambda: self.start(src_refs, out_refs, sems))
        return lambda: pl.when(last)(lambda: self.finish(src_refs, out_refs, sems))


def exchange(src, kind, name):
    side = _Side([(src, kind)])

    def body(src_ref, out_ref, *sems):
        side.start([src_ref], [out_ref], sems)
        side.finish([src_ref], [out_ref], sems)

    return pl.pallas_call(body, name=name, in_specs=side.in_specs, out_specs=side.out_specs[0],
                          out_shape=side.out_shapes[0], scratch_shapes=side.scratch)(src)


def pair_add(parts, landed, name):
    _, R, C = parts.shape
    tr = R
    if R * C > PAIR_BLOCK_ELEMS:
        tr = _tile(R, [t for t in (2048, 1024, 512, 256, 128, 64, 32, 16) if t * C <= PAIR_BLOCK_ELEMS])

    def body(core_ref, p_ref, l_ref, o_ref):
        o_ref[...] = (p_ref[...].astype(F32) + l_ref[...].astype(F32)).astype(BF16)

    blk = pl.BlockSpec((None, tr, C), lambda q, i, core_ref: (q, i, 0))
    mine = pl.BlockSpec((None, None, tr, C), lambda q, i, core_ref: (q, core_ref[0], i, 0))
    return pl.pallas_call(
        body, name=name,
        grid_spec=pltpu.PrefetchScalarGridSpec(num_scalar_prefetch=1, grid=(N_CHIP, R // tr), in_specs=[mine, blk],
                                               out_specs=blk),
        out_shape=jax.ShapeDtypeStruct((N_CHIP, R, C), BF16),
        compiler_params=_cparams(("parallel", "parallel")))(
            lax.axis_index("c").astype(jnp.int32).reshape(1), parts.reshape(N_CHIP, 2, R, C), landed)


def matmul(a, b, mode, name, add=None, out_dtype=F32, b_dev=False, out_dev=False, side=None):
    side = _Side(side)
    if b_dev:
        assert mode in ('nn', 'nt') and b.shape[0] == N_DEV
        b_rows, b_cols = b.shape[1], N_DEV * b.shape[2]
    else:
        b_rows, b_cols = b.shape
    if mode == 'nn':
        (M, K), (K2, N) = a.shape, (b_rows, b_cols)
    elif mode == 'nt':
        (M, K), (N, K2) = a.shape, (b_rows, b_cols)
    else:
        (K, M), (K2, N) = a.shape, (b_rows, b_cols)
    assert K == K2, (a.shape, b.shape, mode)
    n_unit = N // N_DEV if (out_dev or (b_dev and mode == 'nn')) else N
    k_unit = K // N_DEV if (b_dev and mode == 'nt') else K
    tn = _tile(n_unit, (1024, 512, 256, 128))
    tm = _tile(M, (1024, 512, 256, 128) if tn >= 1024 else (2048, 1024, 512, 256, 128))
    tk = _tile(k_unit, (2048, 1024, 512, 256, 128))
    nk = K // tk
    nb, kb = n_unit // tn, k_unit // tk
    dims = {'nn': ((1,), (0,)), 'nt': ((1,), (1,)), 'tn': ((0,), (0,))}[mode]

    n_in = 2 + (add is not None)
    grid = (M // tm, N // tn, nk)

    def body(*refs):
        a_ref, b_ref = refs[:2]
        add_ref = refs[2] if add is not None else None
        side_src = refs[n_in:n_in + side.n]
        o_ref = refs[n_in + side.n]
        side_out = refs[n_in + side.n + 1:n_in + 2 * side.n + 1]
        acc_ref = refs[n_in + 2 * side.n + 1]
        sems = refs[n_in + 2 * side.n + 2:]
        i, j, k = pl.program_id(0), pl.program_id(1), pl.program_id(2)
        finish = side.run((i == 0) & (j == 0) & (k == 0), (i == grid[0] - 1) & (j == grid[1] - 1) & (k == nk - 1),
                          side_src, side_out, sems)

        @pl.when(k == 0)
        def _():
            acc_ref[...] = jnp.zeros_like(acc_ref)

        acc_ref[...] += lax.dot_general(a_ref[...].astype(BF16), b_ref[...].astype(BF16), (dims, ((), ())),
                                        preferred_element_type=F32)

        @pl.when(k == nk - 1)
        def _():
            r = acc_ref[...]
            if add is not None:
                r = r + add_ref[...]
            o_ref[...] = r.astype(out_dtype)

        finish()

    if mode == 'nn':
        a_spec = pl.BlockSpec((tm, tk), lambda i, j, k: (i, k))
        b_spec = (pl.BlockSpec((None, tk, tn), lambda i, j, k: (j // nb, k, j % nb)) if b_dev
                  else pl.BlockSpec((tk, tn), lambda i, j, k: (k, j)))
    elif mode == 'nt':
        a_spec = pl.BlockSpec((tm, tk), lambda i, j, k: (i, k))
        b_spec = (pl.BlockSpec((None, tn, tk), lambda i, j, k: (k // kb, j, k % kb)) if b_dev
                  else pl.BlockSpec((tn, tk), lambda i, j, k: (j, k)))
    else:
        a_spec = pl.BlockSpec((tk, tm), lambda i, j, k: (k, i))
        b_spec = pl.BlockSpec((tk, tn), lambda i, j, k: (k, j))
    add_spec = pl.BlockSpec((tm, tn), lambda i, j, k: (i, j))
    if out_dev:
        o_spec = pl.BlockSpec((None, tm, tn), lambda i, j, k: (j // nb, i, j % nb))
        o_shape = (N_DEV, M, N // N_DEV)
    else:
        o_spec, o_shape = add_spec, (M, N)
    in_specs = [a_spec, b_spec] + ([add_spec] if add is not None else [])
    ops = (a, b) + ((add,) if add is not None else ())
    sem = ("arbitrary",) * 3 if side.n else ("parallel", "parallel", "arbitrary")
    res = pl.pallas_call(
        body, name=name, grid=grid, in_specs=in_specs + side.in_specs, out_specs=[o_spec] + side.out_specs,
        out_shape=[jax.ShapeDtypeStruct(o_shape, out_dtype)] + side.out_shapes,
        scratch_shapes=[pltpu.VMEM((tm, tn), F32)] + side.scratch,
        compiler_params=_cparams(sem))(*ops, *side.srcs)
    return tuple(res) if side.n else res[0]


class _Rows2D:
    def __init__(self, n_rows, tm):
        self.n, self.tm = n_rows, tm
        self.grid = (1, n_rows // tm)

    def row(self, e):
        if isinstance(e, tuple):
            arr, width, cb = e
            return arr, (self.tm, width), pl.BlockSpec((self.tm, width), lambda g, i, cb=cb: (i, cb))
        return e, (self.tm, e.shape[1]), pl.BlockSpec((self.tm, e.shape[1]), lambda g, i: (i, 0))

    def par(self, p):
        return pl.BlockSpec(p.shape, lambda g, i: (0,) * p.ndim)

    def out(self, blk):
        return (self.n, blk[1]), pl.BlockSpec((self.tm, blk[1]), lambda g, i: (i, 0))


class _RowsHeads:
    def __init__(self, n_heads, n_rows, hb, ts):
        self.h, self.n, self.hb, self.ts = n_heads, n_rows, hb, ts
        self.grid = (n_heads // hb, n_rows // ts)

    def row(self, e):
        blk = (self.hb, self.ts, e.shape[2])
        return e, blk, pl.BlockSpec(blk, lambda g, i: (g, i, 0))

    def par(self, p):
        return pl.BlockSpec((self.hb, 1, p.shape[2]), lambda g, i: (g, 0, 0))

    def out(self, blk):
        return (self.h, self.n, blk[2]), pl.BlockSpec(blk, lambda g, i: (g, i, 0))


def _par_block(lay, p):
    return lay.par(p).block_shape


def ew_fwd(lay, fn, rows, params, name):
    rr = [lay.row(e) for e in rows]
    arrs = [r[0] for r in rr]
    blk_avals = [jax.ShapeDtypeStruct(r[1], r[0].dtype) for r in rr]
    par_avals = [jax.ShapeDtypeStruct(_par_block(lay, p), p.dtype) for p in params]
    outs = jax.eval_shape(fn, *blk_avals, *par_avals)
    out_full = [lay.out(o.shape) for o in outs]
    nr, npar = len(rows), len(params)

    def body(*refs):
        vals = [r[...] for r in refs[:nr + npar]]
        res = fn(*vals)
        for ref, v in zip(refs[nr + npar:], res):
            ref[...] = v.astype(ref.dtype)

    return pl.pallas_call(
        body, name=name, grid=lay.grid,
        in_specs=[r[2] for r in rr] + [lay.par(p) for p in params],
        out_specs=[o[1] for o in out_full],
        out_shape=[jax.ShapeDtypeStruct(o[0], a.dtype) for o, a in zip(out_full, outs)],
        compiler_params=_cparams(("parallel", "parallel")))(*arrs, *params)


def ew_bwd(lay, fn, rows, params, cots, wrt_rows, wrt_pars, name, dr_dtypes=None):
    rr = [lay.row(e) for e in rows]
    cc = [lay.row(e) for e in cots]
    nr, npar, nc = len(rows), len(params), len(cots)
    n_dr = len(wrt_rows)
    dr_full = [lay.out(rr[i][1]) for i in wrt_rows]
    dr_dtypes = dr_dtypes or [F32] * n_dr

    def body(*refs):
        rv = [r[...] for r in refs[:nr]]
        pv = [r[...] for r in refs[nr:nr + npar]]
        cv = tuple(r[...] for r in refs[nr + npar:nr + npar + nc])
        outs = refs[nr + npar + nc:]

        def f(*wrt):
            r2, p2 = list(rv), list(pv)
            for idx, v in zip(wrt_rows, wrt[:n_dr]):
                r2[idx] = v
            for idx, v in zip(wrt_pars, wrt[n_dr:]):
                p2[idx] = v
            return fn(*r2, *p2)

        _, vjp = jax.vjp(f, *[rv[i] for i in wrt_rows], *[pv[i] for i in wrt_pars])
        g = vjp(cv)
        for ref, v in zip(outs[:n_dr], g[:n_dr]):
            ref[...] = v.astype(ref.dtype)
        if wrt_pars:
            @pl.when(pl.program_id(1) == 0)
            def _():
                for ref in outs[n_dr:]:
                    ref[...] = jnp.zeros_like(ref)

            for ref, v in zip(outs[n_dr:], g[n_dr:]):
                ref[...] += v

    return pl.pallas_call(
        body, name=name, grid=lay.grid,
        in_specs=[r[2] for r in rr] + [lay.par(p) for p in params] + [c[2] for c in cc],
        out_specs=[o[1] for o in dr_full] + [lay.par(params[i]) for i in wrt_pars],
        out_shape=[jax.ShapeDtypeStruct(o[0], dt) for o, dt in zip(dr_full, dr_dtypes)]
        + [jax.ShapeDtypeStruct(params[i].shape, F32) for i in wrt_pars],
        compiler_params=_cparams(("parallel", "arbitrary")))(
            *[r[0] for r in rr], *params, *[c[0] for c in cc])


def _sigmoid(x):
    return 1.0 / (1.0 + jnp.exp(-x))


def _softplus(x):
    return jnp.maximum(x, 0.0) + jnp.log(1.0 + jnp.exp(-jnp.abs(x)))


def _layer_norm(x, g, b, eps):
    mu = jnp.mean(x, -1, keepdims=True)
    xc = x - mu
    var = jnp.mean(xc * xc, -1, keepdims=True)
    return xc * lax.rsqrt(var + eps) * g + b


def _as_bf16(fn):
    return lambda *args: tuple(o.astype(BF16) for o in fn(*args))


def fn_ln(x, g, b):
    return (_layer_norm(x, g, b, LN_EPS),)


def fn_ln_res(h, t, g, b):
    return (_layer_norm(ALPHA * h + t, g, b, LN_EPS),)


def fn_ln_res_both(h, t, g, b):
    y, = fn_ln_res(h, t, g, b)
    return y, y.astype(BF16)


def make_fn_lora(r_decay, r_iclr, r_gate):
    def fn(z):
        lane = lax.broadcasted_iota(jnp.int32, z.shape, 1)
        out = jnp.where(lane < r_decay, jnp.tanh(z), z)
        out = jnp.where(lane >= r_decay + r_iclr, _sigmoid(z), out)
        return (jnp.where(lane < r_decay + r_iclr + r_gate, out, 0.0),)
    return fn


def fn_rwkv_pre(k, wl, al, w0, a0, k_k, k_a):
    w = -_softplus(-(w0 + wl)) - 0.5
    lw = -jnp.exp(w)
    a = _sigmoid(a0 + al)
    kk = k * k_k
    kk = kk / jnp.maximum(jnp.sqrt(jnp.sum(kk * kk, -1, keepdims=True)), 1e-12)
    k2 = k * (1.0 + (a - 1.0) * k_a)
    return lw, k2, -kk, kk * a


def fn_rwkv_post(o, r, k2, v, g, gn_g, gn_b, r_k):
    mu = jnp.mean(o, -1, keepdims=True)
    oc = o - mu
    var = jnp.mean(oc * oc, -1, keepdims=True)
    y = oc * lax.rsqrt(var + GN_EPS) * gn_g + gn_b
    y = y + jnp.sum(r * k2 * r_k, -1, keepdims=True) * v
    return (y * g,)


def fn_glu(zu, zg):
    return (zu * _sigmoid(zg),)


def fn_ln_silu(y, g, b):
    n = _layer_norm(y, g, b, LN_EPS)
    return (n * _sigmoid(n),)


def fn_merge(zgr, zgc, pr, pc):
    return (_sigmoid(zgr) * pr + _sigmoid(zgc) * pc,)


def fn_relu2(f):
    r = jnp.maximum(f, 0.0)
    return (r * r,)


def _dot_nn(a, b):
    return lax.dot_general(a.astype(BF16), b.astype(BF16), (((1,), (0,)), ((), ())), preferred_element_type=F32)


def _dot_nt(a, b):
    return lax.dot_general(a.astype(BF16), b.astype(BF16), (((1,), (1,)), ((), ())), preferred_element_type=F32)


def _dot_tn(a, b):
    return lax.dot_general(a.astype(BF16), b.astype(BF16), (((0,), (0,)), ((), ())), preferred_element_type=F32)


def _softmax_rows(s):
    s = s - jnp.max(s, -1, keepdims=True)
    e = jnp.exp(s)
    return e / jnp.sum(e, -1, keepdims=True)


def attn_fwd(q, kx, vx, tm, name):
    S, D = q.shape
    M = kx.shape[0]
    dh = D // XATTN_HEADS
    scale = dh ** -0.5

    def body(q_ref, k_ref, v_ref, o_ref):
        p = _softmax_rows(_dot_nt(q_ref[...], k_ref[...]) * scale)
        o_ref[...] = _dot_nn(p, v_ref[...]).astype(BF16)

    row = pl.BlockSpec((tm, dh), lambda h, i: (i, h))
    kv = pl.BlockSpec((M, dh), lambda h, i: (0, h))
    return pl.pallas_call(body, name=name, grid=(XATTN_HEADS, S // tm), in_specs=[row, kv, kv], out_specs=row,
                          out_shape=jax.ShapeDtypeStruct((S, D), BF16),
                          compiler_params=_cparams(("parallel", "parallel")))(q, kx, vx)


def attn_bwd(q, kx, vx, do, tm, name):
    S, D = q.shape
    M = kx.shape[0]
    dh = D // XATTN_HEADS
    scale = dh ** -0.5

    def body(q_ref, k_ref, v_ref, do_ref, dq_ref, dk_ref, dv_ref):
        qb, kb, dob = q_ref[...], k_ref[...], do_ref[...]
        p = _softmax_rows(_dot_nt(qb, kb) * scale)
        dp = _dot_nt(dob, v_ref[...])
        ds = p * (dp - jnp.sum(dp * p, -1, keepdims=True)) * scale
        dq_ref[...] = _dot_nn(ds, kb).astype(BF16)

        @pl.when(pl.program_id(1) == 0)
        def _():
            dk_ref[...] = jnp.zeros_like(dk_ref)
            dv_ref[...] = jnp.zeros_like(dv_ref)

        dk_ref[...] += _dot_tn(ds, qb)
        dv_ref[...] += _dot_tn(p, dob)

    row = pl.BlockSpec((tm, dh), lambda h, i: (i, h))
    kv = pl.BlockSpec((M, dh), lambda h, i: (0, h))
    return pl.pallas_call(
        body, name=name, grid=(XATTN_HEADS, S // tm), in_specs=[row, kv, kv, row], out_specs=[row, kv, kv],
        out_shape=[jax.ShapeDtypeStruct((S, D), BF16), jax.ShapeDtypeStruct((M, D), F32), jax.ShapeDtypeStruct((M, D), F32)],
        compiler_params=_cparams(("parallel", "arbitrary")))(q, kx, vx, do)


def _shift_down(blk, halo_last_row, first_block):
    rolled = pltpu.roll(blk, 1, 0)
    row = lax.broadcasted_iota(jnp.int32, blk.shape, 0)
    top = jnp.where(first_block, 0.0, halo_last_row)
    return jnp.where(row == 0, top, rolled)


def _shift_up(blk, halo_first_row, last_block):
    n = blk.shape[0]
    rolled = pltpu.roll(blk, n - 1, 0)
    row = lax.broadcasted_iota(jnp.int32, blk.shape, 0)
    bot = jnp.where(last_block, 0.0, halo_first_row)
    return jnp.where(row == n - 1, bot, rolled)


def _zcol(j, cols):
    n_first, first, second = cols
    return jnp.where(j < n_first, first + j, second + j - n_first)


def token_shift_fwd(z, cols, cw, width, mu, tm, name):
    S = z.shape[0]
    hb = tm // SUBLANES

    def body(z_ref, halo_ref, mu_ref, o_ref):
        zb = z_ref[...]
        prev = _shift_down(zb, halo_ref[SUBLANES - 1:SUBLANES, :], pl.program_id(1) == 0)
        o_ref[...] = zb + (prev - zb) * mu_ref[...]

    return pl.pallas_call(
        body, name=name, grid=(width // cw, S // tm),
        in_specs=[pl.BlockSpec((tm, cw), lambda j, i: (i, _zcol(j, cols))),
                  pl.BlockSpec((SUBLANES, cw), lambda j, i: (jnp.maximum(i * hb - 1, 0), _zcol(j, cols))),
                  pl.BlockSpec((1, cw), lambda j, i: (0, j))],
        out_specs=pl.BlockSpec((tm, cw), lambda j, i: (i, j)),
        out_shape=jax.ShapeDtypeStruct((S, width), F32),
        compiler_params=_cparams(("parallel", "parallel")))(z, z, mu)


def token_shift_bwd(z, dzs, cols, cw, width, mu, tm, name):
    S = z.shape[0]
    hb = tm // SUBLANES
    nblk = S // tm
    last8 = S // SUBLANES - 1

    def body(z_ref, zh_ref, d_ref, dh_ref, mu_ref, dz_ref, dmu_ref):
        i = pl.program_id(1)
        zb, db, m = z_ref[...], d_ref[...], mu_ref[...]
        prev = _shift_down(zb, zh_ref[SUBLANES - 1:SUBLANES, :], i == 0)
        dm = db * m
        nxt = _shift_up(dm, dh_ref[0:1, :] * m, i == nblk - 1)
        dz_ref[...] = db - dm + nxt

        @pl.when(i == 0)
        def _():
            dmu_ref[...] = jnp.zeros_like(dmu_ref)

        dmu_ref[...] += jnp.sum(db * (prev - zb), 0, keepdims=True)

    return pl.pallas_call(
        body, name=name, grid=(width // cw, nblk),
        in_specs=[pl.BlockSpec((tm, cw), lambda j, i: (i, _zcol(j, cols))),
                  pl.BlockSpec((SUBLANES, cw), lambda j, i: (jnp.maximum(i * hb - 1, 0), _zcol(j, cols))),
                  pl.BlockSpec((tm, cw), lambda j, i: (i, j)),
                  pl.BlockSpec((SUBLANES, cw), lambda j, i: (jnp.minimum((i + 1) * hb, last8), j)),
                  pl.BlockSpec((1, cw), lambda j, i: (0, j))],
        out_specs=[pl.BlockSpec((tm, cw), lambda j, i: (i, j)), pl.BlockSpec((1, cw), lambda j, i: (0, j))],
        out_shape=[jax.ShapeDtypeStruct((S, width), F32), jax.ShapeDtypeStruct((1, width), F32)],
        compiler_params=_cparams(("parallel", "arbitrary")))(z, z, dzs, dzs, mu)


def conv_fwd(u, w, b, width, tr, cb, name):
    S, C = u.shape
    hb = tr // CONV_HALO

    def body(u_ref, h_ref, w_ref, b_ref, y_ref):
        i = pl.program_id(1)
        halo = jnp.where(i == 0, 0.0, h_ref[...])
        win = jnp.concatenate([halo, u_ref[...]], axis=0)
        acc = jnp.zeros((tr, cb), F32) + b_ref[...]
        for d in range(width):
            sh = win if d == 0 else pltpu.roll(win, d, 0)
            acc = acc + sh[CONV_HALO:, :] * w_ref[width - 1 - d:width - d, :]
        y_ref[...] = acc

    return pl.pallas_call(
        body, name=name, grid=(C // cb, S // tr),
        in_specs=[pl.BlockSpec((tr, cb), lambda j, i: (i, j)),
                  pl.BlockSpec((CONV_HALO, cb), lambda j, i: (jnp.maximum(i * hb - 1, 0), j)),
                  pl.BlockSpec((CONV_HALO, cb), lambda j, i: (0, j)),
                  pl.BlockSpec((1, cb), lambda j, i: (0, j))],
        out_specs=pl.BlockSpec((tr, cb), lambda j, i: (i, j)),
        out_shape=jax.ShapeDtypeStruct((S, C), F32), compiler_params=_cparams(("parallel", "parallel")))(u, u, w, b)


def conv_bwd(u, dy, w, width, tr, cb, name):
    S, C = u.shape
    hb = tr // CONV_HALO
    nblk = S // tr
    last = S // CONV_HALO - 1

    def body(u_ref, uh_ref, d_ref, dh_ref, w_ref, du_ref, dw_ref, db_ref):
        i = pl.program_id(1)
        dyb = d_ref[...]
        uwin = jnp.concatenate([jnp.where(i == 0, 0.0, uh_ref[...]), u_ref[...]], axis=0)
        dwin = jnp.concatenate([dyb, jnp.where(i == nblk - 1, 0.0, dh_ref[...])], axis=0)

        @pl.when(i == 0)
        def _():
            dw_ref[...] = jnp.zeros_like(dw_ref)
            db_ref[...] = jnp.zeros_like(db_ref)

        acc = jnp.zeros((tr, cb), F32)
        for d in range(width):
            tap = width - 1 - d
            dsh = dwin if d == 0 else pltpu.roll(dwin, tr + CONV_HALO - d, 0)
            acc = acc + dsh[:tr, :] * w_ref[tap:tap + 1, :]
            ush = uwin if d == 0 else pltpu.roll(uwin, d, 0)
            dw_ref[tap:tap + 1, :] += jnp.sum(ush[CONV_HALO:, :] * dyb, 0, keepdims=True)
        du_ref[...] = acc
        db_ref[...] += jnp.sum(dyb, 0, keepdims=True)

    return pl.pallas_call(
        body, name=name, grid=(C // cb, nblk),
        in_specs=[pl.BlockSpec((tr, cb), lambda j, i: (i, j)),
                  pl.BlockSpec((CONV_HALO, cb), lambda j, i: (jnp.maximum(i * hb - 1, 0), j)),
                  pl.BlockSpec((tr, cb), lambda j, i: (i, j)),
                  pl.BlockSpec((CONV_HALO, cb), lambda j, i: (jnp.minimum((i + 1) * hb, last), j)),
                  pl.BlockSpec((CONV_HALO, cb), lambda j, i: (0, j))],
        out_specs=[pl.BlockSpec((tr, cb), lambda j, i: (i, j)),
                   pl.BlockSpec((CONV_HALO, cb), lambda j, i: (0, j)),
                   pl.BlockSpec((1, cb), lambda j, i: (0, j))],
        out_shape=[jax.ShapeDtypeStruct((S, C), F32), jax.ShapeDtypeStruct((CONV_HALO, C), F32),
                   jax.ShapeDtypeStruct((1, C), F32)],
        compiler_params=_cparams(("parallel", "arbitrary")))(u, u, dy, dy, w)


def _split2(x):
    hi = x.astype(BF16)
    return hi, (x - hi.astype(F32)).astype(BF16)


def _dot3(a, b, dims):
    ah, al = _split2(a)
    bh, bl = _split2(b)
    d = lambda p, q: lax.dot_general(p, q, dims, preferred_element_type=F32)
    return d(ah, bh) + (d(ah, bl) + d(al, bh))


@jax.custom_vjp
def _bnn(a, b):
    return _dot3(a, b, (((2,), (1,)), ((0,), (0,))))


@jax.custom_vjp
def _bnt(a, b):
    return _dot3(a, b, (((2,), (2,)), ((0,), (0,))))


@jax.custom_vjp
def _btn(a, b):
    return _dot3(a, b, (((1,), (1,)), ((0,), (0,))))


def _tri_sum(x, lower):
    h, c, _ = x.shape
    ti = lax.broadcasted_iota(jnp.int32, (h, c, c), 1)
    si = lax.broadcasted_iota(jnp.int32, (h, c, c), 2)
    m = (si <= ti if lower else si >= ti).astype(BF16)
    x1 = x.astype(BF16)
    r1 = x - x1.astype(F32)
    x2 = r1.astype(BF16)
    x3 = (r1 - x2.astype(F32)).astype(BF16)
    d = lambda q: lax.dot_general(m, q, (((2,), (1,)), ((0,), (0,))), preferred_element_type=F32)
    return d(x1) + (d(x2) + d(x3))


@jax.custom_vjp
def _cumsum_rows(x):
    return _tri_sum(x, True)


@jax.custom_vjp
def _rev_cumsum_rows(x):
    return _tri_sum(x, False)


_cumsum_rows.defvjp(lambda x: (_cumsum_rows(x), None), lambda _, g: (_rev_cumsum_rows(g),))
_rev_cumsum_rows.defvjp(lambda x: (_rev_cumsum_rows(x), None), lambda _, g: (_cumsum_rows(g),))


@jax.custom_vjp
def _unit_lower_inverse(a):
    h, c, _ = a.shape
    eye = (lax.broadcasted_iota(jnp.int32, (h, c, c), 1) == lax.broadcasted_iota(jnp.int32, (h, c, c), 2)).astype(F32)
    t, pw = eye + a, a
    for _ in range(int(math.log2(c)) - 1):
        pw = _bnn(pw, pw)
        t = t + _bnn(t, pw)
    return t


def _unit_lower_inverse_fwd(a):
    t = _unit_lower_inverse(a)
    return t, t


_unit_lower_inverse.defvjp(_unit_lower_inverse_fwd, lambda t, g: (_btn(t, _bnt(g, t)),))


_bnn.defvjp(lambda a, b: (_bnn(a, b), (a, b)), lambda res, g: (_bnt(g, res[1]), _btn(res[0], g)))
_bnt.defvjp(lambda a, b: (_bnt(a, b), (a, b)), lambda res, g: (_bnn(g, res[1]), _btn(g, res[0])))
_btn.defvjp(lambda a, b: (_btn(a, b), (a, b)), lambda res, g: (_bnt(res[1], g), _bnn(res[0], g)))


@jax.custom_vjp
def _known_inverse(a, t):
    return t


_known_inverse.defvjp(lambda a, t: (t, t), lambda t, g: (_btn(t, _bnt(g, t)), jnp.zeros_like(t)))


def scan_chunk(st0, r, lw, k, v, a, b, tinv=None):
    h, c, n = r.shape
    ti = lax.broadcasted_iota(jnp.int32, (h, c, c), 1)
    si = lax.broadcasted_iota(jnp.int32, (h, c, c), 2)
    incl = si <= ti
    strict = si < ti
    cum = _cumsum_rows(lw)
    p = jnp.exp(cum)
    pinv = jnp.exp(-cum)
    at = a * jnp.exp(cum - lw)
    bt, kt, rt = b * pinv, k * pinv, r * p
    a_ab = jnp.where(strict, _bnt(at, bt), 0.0)
    a_ak = jnp.where(strict, _bnt(at, kt), 0.0)
    m_rb = jnp.where(incl, _bnt(rt, bt), 0.0)
    m_rk = jnp.where(incl, _bnt(rt, kt), 0.0)
    tinv = _unit_lower_inverse(a_ab) if tinv is None else _known_inverse(a_ab, tinv)
    u = _bnn(tinv, _bnn(at, st0) + _bnn(a_ak, v))
    o = _bnn(rt, st0) + _bnn(m_rb, u) + _bnn(m_rk, v)
    cum_c = jnp.sum(lw, axis=1, keepdims=True)
    tail = jnp.exp(cum_c - cum)
    ki = lax.broadcasted_iota(jnp.int32, (h, n, n), 1)
    kj = lax.broadcasted_iota(jnp.int32, (h, n, n), 2)
    pc_col = jnp.sum(jnp.where(ki == kj, jnp.exp(cum_c), 0.0), axis=2, keepdims=True)
    st = st0 * pc_col + _btn(b * tail, u) + _btn(k * tail, v)
    return o, st, tinv


def _grid_ends(grid):
    g, c = pl.program_id(0), pl.program_id(1)
    return (g == 0) & (c == 0), (g == grid[0] - 1) & (c == grid[1] - 1)


def scan_fwd(r, lw, k, v, a, b, hb, name, side=None):
    H, S, N = r.shape
    C = SCAN_CHUNK
    nc = S // C
    side = _Side(side)
    grid = (H // hb, nc)

    def body(*refs):
        r_ref, lw_ref, k_ref, v_ref, a_ref, b_ref = refs[:6]
        side_src = refs[6:6 + side.n]
        o_ref, ck_ref, ti_ref = refs[6 + side.n:9 + side.n]
        side_out = refs[9 + side.n:9 + 2 * side.n]
        st_ref = refs[9 + 2 * side.n]
        finish = side.run(*_grid_ends(grid), side_src, side_out, refs[10 + 2 * side.n:])

        @pl.when(pl.program_id(1) == 0)
        def _():
            st_ref[...] = jnp.zeros_like(st_ref)

        st0 = st_ref[...]
        ck_ref[...] = st0[:, None]
        o, st, tinv = scan_chunk(st0, r_ref[...], lw_ref[...], k_ref[...], v_ref[...], a_ref[...], b_ref[...])
        o_ref[...] = o
        ti_ref[...] = tinv[:, None]
        st_ref[...] = st
        finish()

    seq = pl.BlockSpec((hb, C, N), lambda g, c: (g, c, 0))
    return pl.pallas_call(
        body, name=name, grid=grid, in_specs=[seq] * 6 + side.in_specs,
        out_specs=[seq, pl.BlockSpec((hb, 1, N, N), lambda g, c: (g, c, 0, 0)),
                   pl.BlockSpec((hb, 1, C, C), lambda g, c: (g, c, 0, 0))] + side.out_specs,
        out_shape=[jax.ShapeDtypeStruct((H, S, N), F32), jax.ShapeDtypeStruct((H, nc, N, N), F32),
                   jax.ShapeDtypeStruct((H, nc, C, C), F32)] + side.out_shapes,
        scratch_shapes=[pltpu.VMEM((hb, N, N), F32)] + side.scratch,
        compiler_params=_cparams(("arbitrary", "arbitrary")))(r, lw, k, v, a, b, *side.srcs)


def scan_bwd(r, lw, k, v, a, b, ck, ti, do, dr_add, dk_add, dv_add, hb, name, side=None):
    H, S, N = r.shape
    C = SCAN_CHUNK
    nc = S // C
    side = _Side(side)
    grid = (H // hb, nc)

    def body(*refs):
        r_ref, lw_ref, k_ref, v_ref, a_ref, b_ref, ck_ref, ti_ref, do_ref, ra_ref, ka_ref, va_ref = refs[:12]
        side_src = refs[12:12 + side.n]
        dr_ref, dlw_ref, dk_ref, dv_ref, da_ref, db_ref = refs[12 + side.n:18 + side.n]
        side_out = refs[18 + side.n:18 + 2 * side.n]
        dst_ref = refs[18 + 2 * side.n]
        finish = side.run(*_grid_ends(grid), side_src, side_out, refs[19 + 2 * side.n:])

        @pl.when(pl.program_id(1) == 0)
        def _():
            dst_ref[...] = jnp.zeros_like(dst_ref)

        st0, tinv = ck_ref[...][:, 0], ti_ref[...][:, 0]
        chunk = lambda *args: scan_chunk(*args, tinv=tinv)[:2]
        _, vjp = jax.vjp(chunk, st0, r_ref[...], lw_ref[...], k_ref[...], v_ref[...], a_ref[...], b_ref[...])
        dst0, dr, dlw, dk, dv, da, db = vjp((do_ref[...], dst_ref[...]))
        dr_ref[...], dlw_ref[...], dk_ref[...] = dr + ra_ref[...], dlw, dk + ka_ref[...]
        dv_ref[...], da_ref[...], db_ref[...] = dv + va_ref[...], da, db
        dst_ref[...] = dst0
        finish()

    seq = pl.BlockSpec((hb, C, N), lambda g, c: (g, nc - 1 - c, 0))
    per_chunk = lambda n: pl.BlockSpec((hb, 1, n, n), lambda g, c: (g, nc - 1 - c, 0, 0))
    return pl.pallas_call(
        body, name=name, grid=grid,
        in_specs=[seq] * 6 + [per_chunk(N), per_chunk(C)] + [seq] * 4 + side.in_specs,
        out_specs=[seq] * 6 + side.out_specs, out_shape=[jax.ShapeDtypeStruct((H, S, N), F32)] * 6 + side.out_shapes,
        scratch_shapes=[pltpu.VMEM((hb, N, N), F32)] + side.scratch,
        compiler_params=_cparams(("arbitrary", "arbitrary")))(r, lw, k, v, a, b, ck, ti, do, dr_add, dk_add, dv_add,
                                                              *side.srcs)


def loss_head(y, target, tm, name):
    S, D = y.shape

    def body(y_ref, t_ref, dy_ref, l_ref):
        err = y_ref[...] - t_ref[...]
        dy_ref[...] = err * (1.0 / D)

        @pl.when(pl.program_id(0) == 0)
        def _():
            l_ref[...] = jnp.zeros_like(l_ref)

        l_ref[...] += 0.5 * jnp.sum(jnp.mean(err * err, -1, keepdims=True), 0, keepdims=True)

    row = pl.BlockSpec((tm, D), lambda i: (i, 0))
    return pl.pallas_call(
        body, name=name, grid=(S // tm,), in_specs=[row, row],
        out_specs=[row, pl.BlockSpec((SUBLANES, LANES), lambda i: (0, 0))],
        out_shape=[jax.ShapeDtypeStruct((S, D), F32), jax.ShapeDtypeStruct((SUBLANES, LANES), F32)],
        compiler_params=_cparams(("arbitrary",)))(y, target)


def adamw(parts, w, m, v, name):
    R, C = w.shape
    n_parts = parts.shape[0]
    tr = R
    if R * C > ADAM_BLOCK_ELEMS:
        tr = _tile(R, [t for t in (512, 256, 128, 64, 32, 16) if t * C <= ADAM_BLOCK_ELEMS])
    c1 = 1.0 / (1.0 - ADAM_B1 ** ADAM_STEP)
    c2 = 1.0 / (1.0 - ADAM_B2 ** ADAM_STEP)

    def body(p_ref, w_ref, m_ref, v_ref, g_ref, d_ref, nm_ref, nv_ref):
        g = p_ref[0].astype(F32)
        for j in range(1, n_parts):
            g = g + p_ref[j].astype(F32)
        nm = ADAM_B1 * m_ref[...] + (1.0 - ADAM_B1) * g
        nv = ADAM_B2 * v_ref[...] + (1.0 - ADAM_B2) * (g * g)
        g_ref[...] = g
        nm_ref[...] = nm
        nv_ref[...] = nv
        d_ref[...] = -ADAM_LR * ((nm * c1) / (jnp.sqrt(nv * c2) + ADAM_EPS) + ADAM_WD * w_ref[...])

    blk = pl.BlockSpec((tr, C), lambda i: (i, 0))
    return pl.pallas_call(
        body, name=name, grid=(R // tr,), in_specs=[pl.BlockSpec((n_parts, tr, C), lambda i: (0, i, 0)), blk, blk, blk],
        out_specs=[blk] * 4, out_shape=[jax.ShapeDtypeStruct((R, C), F32)] * 4,
        compiler_params=_cparams(("parallel",)))(parts, w, m, v)


def _to_heads(a2d):
    s, d = a2d.shape
    return a2d.reshape(s, d // RWKV_HEAD, RWKV_HEAD).transpose(1, 0, 2)


def _from_heads(a3d):
    h, s, n = a3d.shape
    return a3d.transpose(1, 0, 2).reshape(s, h * n)


def _cols_joined(g):
    return g.transpose(1, 0, 2).reshape(g.shape[1], N_DEV * g.shape[2])


def _cols_split(w):
    return w.reshape(w.shape[0], N_DEV, w.shape[1] // N_DEV).transpose(1, 0, 2)


def _round_up(n, m):
    return -(-n // m) * m


def _step(a):
    x, mem, target = a['x'][0], a['mem'][0], a['loss_target'][0]
    S, D = x.shape
    M = mem.shape[0]
    DR = a['rwkv_w0'].shape[1]
    H = DR // RWKV_HEAD
    DC = a['conv_b'].shape[1]
    r_decay, r_iclr, r_gate = a['rwkv_w_up'].shape[1], a['rwkv_a_up'].shape[1], a['rwkv_g_up'].shape[1]
    n_lora = r_decay + r_iclr + r_gate
    lora_w = _round_up(n_lora, LANES)
    n_rwkv = 3 * DR + n_lora
    zr_w = 3 * DR + lora_w
    pad_cols = zr_w - n_rwkv
    conv_taps = a['conv_w'].shape[1]
    assert conv_taps - 1 <= CONV_HALO and S % SCAN_CHUNK == 0

    def shard(n, pre=''):
        w = a[pre + n]
        if n == 'conv_w':
            return jnp.pad(w.reshape(conv_taps, w.shape[-1]), ((0, CONV_HALO - conv_taps), (0, 0)))
        return w.reshape(w.shape[1:])

    Wg, W = {}, {}

    def gathers(names):
        return [(shard(n).astype(BF16), 'gather2') for n in names]

    def gathered(names, results):
        for n, g in zip(names, results):
            Wg[n] = g
            if n in ROW_SHARDED:
                W[n] = g.reshape(N_DEV * g.shape[1], g.shape[2])

    with_mm_z = ['mlp_w1']
    with_scan_fwd = ['proj_rwkv', 'proj_conv', 'w_out', 'xattn_wq']
    with_mm_w1 = ['mlp_w2']
    with_scan_bwd = ['xattn_wo', 'xattn_wq', 'xattn_wk', 'xattn_wv', 'w_out', 'proj_rwkv', 'proj_conv']
    gathered(['w_in'], [exchange(shard('w_in').astype(BF16), 'gather2', "gather_w_in")])
    lora_src = jnp.concatenate([shard('rwkv_w_up'), shard('rwkv_a_up'), shard('rwkv_g_up')], axis=0).astype(BF16)
    lora_all = exchange(lora_src, 'gather', "gather_lora")
    conv_w = _cols_joined(exchange(shard('conv_w'), 'gather', "gather_conv_w"))

    w_in = _cols_joined(Wg['w_in'])
    o_conv, o_gate = n_rwkv, n_rwkv + 2 * DC
    w_in_p = jnp.concatenate([w_in[:, o_gate:], w_in[:, :3 * DR], w_in[:, o_conv:o_gate], w_in[:, 3 * DR:n_rwkv],
                              jnp.zeros((D, pad_cols), BF16)], axis=1)
    c_gr, c_gc, c_rkv, c_u = 0, D, 2 * D, 2 * D + 3 * DR
    c_cg, c_lora = c_u + DC, c_u + 2 * DC
    assert c_rkv % lora_w == 0 and c_lora % lora_w == 0 and (3 * DR) % lora_w == 0 and c_u % DC == 0 and c_rkv % DR == 0
    shift_cols = (3 * DR // lora_w, c_rkv // lora_w, c_lora // lora_w)
    lora_full = _cols_joined(lora_all)
    w_lora = jnp.zeros((lora_w, 3 * DR), BF16)
    w_lora = w_lora.at[:r_decay, :DR].set(lora_full[:r_decay])
    w_lora = w_lora.at[r_decay:r_decay + r_iclr, DR:2 * DR].set(lora_full[r_decay:r_decay + r_iclr])
    w_lora = w_lora.at[r_decay + r_iclr:n_lora, 2 * DR:].set(lora_full[r_decay + r_iclr:])
    mu_p = jnp.pad(a['rwkv_shift_mix'], ((0, 0), (0, pad_cols)))
    x_bf = x.astype(BF16)

    hp = lambda n: a[n].reshape(H, 1, RWKV_HEAD)
    w0_h, a0_h, kk_h, ka_h, gng_h, gnb_h = (hp(n) for n in ('rwkv_w0', 'rwkv_a0', 'rwkv_k_k', 'rwkv_k_a',
                                                             'rwkv_gn_g', 'rwkv_gn_b'))
    rk_h = a['rwkv_r_k'].reshape(H, 1, RWKV_HEAD)
    ln_mem_g, ln_mem_b = a['ln_mem_g'].reshape(1, D), a['ln_mem_b'].reshape(1, D)

    tm_d = _tile(S, (128, 64, 32, 16, 8))
    rows_many = _Rows2D(S, _tile(S, (64, 32, 16, 8)))
    tm_a = _tile(S, (512, 256, 128, 64))
    rows = _Rows2D(S, tm_d)
    rows_mem = _Rows2D(M, _tile(M, (64, 32, 16, 8)))
    rows_ff = _Rows2D(S, _tile(S, (64, 32, 16, 8)))
    hb = _tile(H, (16, 8, 4, 2, 1))
    heads = _RowsHeads(H, S, _tile(H, (4, 2, 1)), _tile(S, (256, 128, 64)))
    tr_conv = _tile(S, (512, 256, 128, 64, 32))
    cb_conv = _tile(DC, (256, 128))
    fn_lora = make_fn_lora(r_decay, r_iclr, r_gate)
    lane_blk = lambda off, width: off // width

    mem_n, = ew_fwd(rows_mem, _as_bf16(fn_ln), [mem], [ln_mem_g, ln_mem_b], "ln_mem")
    z, *got = matmul(x_bf, w_in_p, 'nn', "mm_z", side=gathers(with_mm_z))
    gathered(with_mm_z, got)
    zs = token_shift_fwd(z, shift_cols, lora_w, zr_w, mu_p, tm_a, "token_shift")
    lora, = ew_fwd(rows, _as_bf16(fn_lora), [(zs, lora_w, lane_blk(3 * DR, lora_w))], [], "lora_act")
    up = matmul(lora, w_lora, 'nn', "mm_lora_up")
    r_h, k_h, v_h = (_to_heads(zs[:, i * DR:(i + 1) * DR]) for i in range(3))
    wl_h, al_h, g_h = (_to_heads(up[:, i * DR:(i + 1) * DR]) for i in range(3))
    pre_rows, pre_pars = [k_h, wl_h, al_h], [w0_h, a0_h, kk_h, ka_h]
    lw_h, k2_h, na_h, b_h = ew_fwd(heads, fn_rwkv_pre, pre_rows, pre_pars, "rwkv_pre")
    o_h, ckpt, tinvs, *got = scan_fwd(r_h, lw_h, k2_h, v_h, na_h, b_h, hb, "scan_fwd", side=gathers(with_scan_fwd))
    gathered(with_scan_fwd, got)
    post_rows, post_pars = [o_h, r_h, k2_h, v_h, g_h], [gng_h, gnb_h, rk_h]
    or_h, = ew_fwd(heads, _as_bf16(fn_rwkv_post), post_rows, post_pars, "rwkv_post")
    o_r = _from_heads(or_h)

    glu_rows = [(z, DC, lane_blk(c_u, DC)), (z, DC, lane_blk(c_cg, DC))]
    u, = ew_fwd(rows, fn_glu, glu_rows, [], "glu")
    yc = conv_fwd(u, conv_w, a['conv_b'], conv_taps, tr_conv, cb_conv, "conv")
    cln = [a['conv_ln_g'], a['conv_ln_b']]
    o_c, = ew_fwd(rows, _as_bf16(fn_ln_silu), [yc], cln, "conv_ln_silu")

    G = {}
    rows_split = lambda g: g.reshape(N_DEV, g.shape[0] // N_DEV, g.shape[1])

    Sx, recv = {}, {}

    def to_sibling(names):
        return [(G[n], 'scatter_sib') for n in names]

    def pair_sums(names, landed):
        for n, l in zip(names, landed):
            Sx[n] = pair_add(G[n], l, "pair_add_" + n)

    pr, *got = matmul(o_r, Wg['proj_rwkv'], 'nn', "mm_proj_rwkv", b_dev=True, side=gathers(['xattn_wk']))
    gathered(['xattn_wk'], got)
    pc = matmul(o_c, Wg['proj_conv'], 'nn', "mm_proj_conv", b_dev=True)
    merge_rows = [(z, D, lane_blk(c_gr, D)), (z, D, lane_blk(c_gc, D)), pr, pc]
    merged, = ew_fwd(rows_many, _as_bf16(fn_merge), merge_rows, [], "merge")
    t1, *got = matmul(merged, W['w_out'], 'nn', "mm_w_out", side=gathers(['xattn_wv']))
    gathered(['xattn_wv'], got)
    ln1 = [a['ln1_g'], a['ln1_b']]
    h1, h1_bf = ew_fwd(rows, fn_ln_res_both, [x, t1], ln1, "ln1")

    q, *got = matmul(h1_bf, W['xattn_wq'], 'nn', "mm_q", side=gathers(['xattn_wo']))
    gathered(['xattn_wo'], got)
    kx = matmul(mem_n, W['xattn_wk'], 'nn', "mm_k")
    vx = matmul(mem_n, W['xattn_wv'], 'nn', "mm_v")
    oa = attn_fwd(q, kx, vx, tm_a, "attn")
    ca = matmul(oa, W['xattn_wo'], 'nn', "mm_wo")
    ln2 = [a['ln2_g'], a['ln2_b']]
    h2, h2_bf = ew_fwd(rows, fn_ln_res_both, [h1, ca], ln2, "ln2")

    f1, *got = matmul(h2_bf, Wg['mlp_w1'], 'nn', "mm_w1", b_dev=True, side=gathers(with_mm_w1))
    gathered(with_mm_w1, got)
    act, = ew_fwd(rows_ff, _as_bf16(fn_relu2), [f1], [], "relu2")
    ff = matmul(act, W['mlp_w2'], 'nn', "mm_w2")
    ln3 = [a['ln3_g'], a['ln3_b']]
    h3, = ew_fwd(rows, fn_ln_res, [h2, ff], ln3, "ln3")
    dh3, loss_rows = loss_head(h3, target, tm_d, "loss")

    dh2a, dff, G['ln3_g'], G['ln3_b'] = ew_bwd(rows, fn_ln_res, [h2, ff], ln3, [dh3], (0, 1), (0, 1), "ln3_bwd",
                                               dr_dtypes=[F32, BF16])
    dact = matmul(dff, W['mlp_w2'], 'nt', "mm_w2_dx")
    G['mlp_w2'] = rows_split(matmul(act, dff, 'tn', "mm_w2_dw", out_dtype=BF16))
    df1, = ew_bwd(rows_ff, fn_relu2, [f1], [], [dact], (0,), (), "relu2_bwd", dr_dtypes=[BF16])
    dh2, *got = matmul(df1, Wg['mlp_w1'], 'nt', "mm_w1_dx", add=dh2a, b_dev=True, side=to_sibling(['mlp_w2']))
    pair_sums(['mlp_w2'], got)
    G['mlp_w1'], recv['mlp_w2'] = matmul(h2_bf, df1, 'tn', "mm_w1_dw", out_dtype=BF16, out_dev=True,
                                         side=[(Sx['mlp_w2'], 'scatter_chips')])

    dh1a, dca, G['ln2_g'], G['ln2_b'] = ew_bwd(rows, fn_ln_res, [h1, ca], ln2, [dh2], (0, 1), (0, 1), "ln2_bwd",
                                               dr_dtypes=[F32, BF16])
    doa, *got = matmul(dca, W['xattn_wo'], 'nt', "mm_wo_dx", side=to_sibling(['mlp_w1']))
    pair_sums(['mlp_w1'], got)
    G['xattn_wo'] = rows_split(matmul(oa, dca, 'tn', "mm_wo_dw", out_dtype=BF16))
    dq, dkx, dvx = attn_bwd(q, kx, vx, doa, tm_a, "attn_bwd")
    dh1, *got = matmul(dq, W['xattn_wq'], 'nt', "mm_q_dx", add=dh1a, side=to_sibling(['xattn_wo']))
    pair_sums(['xattn_wo'], got)
    G['xattn_wq'] = rows_split(matmul(h1_bf, dq, 'tn', "mm_q_dw", out_dtype=BF16))
    G['xattn_wk'] = rows_split(matmul(mem_n, dkx, 'tn', "mm_k_dw", out_dtype=BF16))
    G['xattn_wv'] = rows_split(matmul(mem_n, dvx, 'tn', "mm_v_dw", out_dtype=BF16))
    dmem_k = matmul(dkx, W['xattn_wk'], 'nt', "mm_k_dx")
    dmem_n = matmul(dvx, W['xattn_wv'], 'nt', "mm_v_dx", add=dmem_k)
    G['ln_mem_g'], G['ln_mem_b'] = ew_bwd(rows_mem, fn_ln, [mem], [ln_mem_g, ln_mem_b], [dmem_n], (), (0, 1),
                                          "ln_mem_bwd")

    dxa, dt1, G['ln1_g'], G['ln1_b'] = ew_bwd(rows, fn_ln_res, [x, t1], ln1, [dh1], (0, 1), (0, 1), "ln1_bwd",
                                              dr_dtypes=[F32, BF16])
    xattn_qkv = ['xattn_wq', 'xattn_wk', 'xattn_wv']
    dmerged, *got = matmul(dt1, W['w_out'], 'nt', "mm_w_out_dx", side=to_sibling(xattn_qkv))
    pair_sums(xattn_qkv, got)
    G['w_out'] = rows_split(matmul(merged, dt1, 'tn', "mm_w_out_dw", out_dtype=BF16))
    dzgr, dzgc, dpr, dpc = ew_bwd(rows_many, fn_merge, merge_rows, [], [dmerged], (0, 1, 2, 3), (), "merge_bwd",
                                  dr_dtypes=[BF16] * 4)
    do_r, *got = matmul(dpr, Wg['proj_rwkv'], 'nt', "mm_proj_rwkv_dx", b_dev=True, side=to_sibling(['w_out']))
    pair_sums(['w_out'], got)
    G['proj_rwkv'] = matmul(o_r, dpr, 'tn', "mm_proj_rwkv_dw", out_dtype=BF16, out_dev=True)
    do_c, *got = matmul(dpc, Wg['proj_conv'], 'nt', "mm_proj_conv_dx", b_dev=True, side=to_sibling(['proj_rwkv']))
    pair_sums(['proj_rwkv'], got)
    G['proj_conv'] = matmul(o_c, dpc, 'tn', "mm_proj_conv_dw", out_dtype=BF16, out_dev=True)
    pair_sums(['proj_conv'], [exchange(G['proj_conv'], 'scatter_sib', "scatter_sib_proj_conv")])

    dyc, G['conv_ln_g'], G['conv_ln_b'] = ew_bwd(rows, fn_ln_silu, [yc], cln, [do_c], (0,), (0, 1), "conv_ln_silu_bwd")
    du, dconv_w, G['conv_b'] = conv_bwd(u, dyc, conv_w, conv_taps, tr_conv, cb_conv, "conv_bwd")
    dzu, dzcg = ew_bwd(rows, fn_glu, glu_rows, [], [du], (0, 1), (), "glu_bwd", dr_dtypes=[BF16] * 2)

    do_r_h = _to_heads(do_r)
    do_h, dr1_h, dk2a_h, dv1_h, dg_h, dgng, dgnb, drk = ew_bwd(
        heads, fn_rwkv_post, post_rows, post_pars, [do_r_h], (0, 1, 2, 3, 4), (0, 1, 2), "rwkv_post_bwd",
        dr_dtypes=[F32, F32, F32, F32, BF16])
    dr_h, dlw_h, dk2_h, dv_h, dna_h, db_h, *got = scan_bwd(
        r_h, lw_h, k2_h, v_h, na_h, b_h, ckpt, tinvs, do_h, dr1_h, dk2a_h, dv1_h, hb, "scan_bwd",
        side=[(Sx[n], 'scatter_chips') for n in with_scan_bwd])
    recv.update(zip(with_scan_bwd, got))
    dk_h, dwl_h, dal_h, dw0, da0, dkk, dka = ew_bwd(
        heads, fn_rwkv_pre, pre_rows, pre_pars, [dlw_h, dk2_h, dna_h, db_h], (0, 1, 2), (0, 1, 2, 3), "rwkv_pre_bwd",
        dr_dtypes=[F32, BF16, BF16])
    G['rwkv_w0'], G['rwkv_a0'], G['rwkv_k_k'], G['rwkv_k_a'] = (t.reshape(1, DR) for t in (dw0, da0, dkk, dka))
    G['rwkv_gn_g'], G['rwkv_gn_b'] = dgng.reshape(1, DR), dgnb.reshape(1, DR)
    G['rwkv_r_k'] = drk.reshape(1, H, RWKV_HEAD)
    dup = jnp.concatenate([_from_heads(dwl_h), _from_heads(dal_h), _from_heads(dg_h)], axis=1)
    dlora = matmul(dup, w_lora, 'nt', "mm_lora_up_dx")
    dw_lora = matmul(lora, dup, 'tn', "mm_lora_up_dw", out_dtype=BF16)
    d_lora_stack = jnp.concatenate([dw_lora[:r_decay, :DR], dw_lora[r_decay:r_decay + r_iclr, DR:2 * DR],
                                    dw_lora[r_decay + r_iclr:n_lora, 2 * DR:]], axis=0)
    dzs_lora, = ew_bwd(rows, fn_lora, [(zs, lora_w, lane_blk(3 * DR, lora_w))], [], [dlora], (0,), (), "lora_act_bwd")
    dzs = jnp.concatenate([_from_heads(dr_h), _from_heads(dk_h), _from_heads(dv_h), dzs_lora], axis=1)
    dzr, dmu = token_shift_bwd(z, dzs, shift_cols, lora_w, zr_w, mu_p, tm_a, "token_shift_bwd")
    G['rwkv_shift_mix'] = dmu[:, :n_rwkv]
    dz = jnp.concatenate([dzgr, dzgc, dzr[:, :3 * DR].astype(BF16), dzu, dzcg, dzr[:, 3 * DR:].astype(BF16)], axis=1)
    dw_in_p, recv['mlp_w1'] = matmul(x_bf, dz, 'tn', "mm_z_dw", out_dtype=BF16, side=[(Sx['mlp_w1'], 'scatter_chips')])
    G['w_in'] = _cols_split(jnp.concatenate([dw_in_p[:, c_rkv:c_u], dw_in_p[:, c_lora:c_lora + n_lora],
                                             dw_in_p[:, c_u:c_lora], dw_in_p[:, :c_rkv]], axis=1))
    pair_sums(['w_in'], [exchange(G['w_in'], 'scatter_sib', "scatter_sib_w_in")])
    grad_x, recv['w_in'] = matmul(dz, w_in_p, 'nt', "mm_z_dx", add=dxa, side=[(Sx['w_in'], 'scatter_chips')])

    out = {}

    def update(n, parts, w, m, v):
        res = adamw(parts, w, m, v, "adamw_" + n)
        out[n] = [t[:conv_taps].reshape(a[n].shape) if n == 'conv_w' else t.reshape(a[n].shape) for t in res]

    for n in ['w_in', 'mlp_w2', 'mlp_w1'] + with_scan_bwd:
        update(n, recv[n], shard(n), shard(n, 'm_'), shard(n, 'v_'))
    recv_lora = exchange(_cols_split(d_lora_stack), 'scatter', "scatter_lora")
    lo = 0
    for n, r_ in (('rwkv_w_up', r_decay), ('rwkv_a_up', r_iclr), ('rwkv_g_up', r_gate)):
        update(n, recv_lora[:, lo:lo + r_], shard(n), shard(n, 'm_'), shard(n, 'v_'))
        lo += r_
    recv_conv = exchange(_cols_split(dconv_w.astype(BF16)), 'scatter', "scatter_conv_w")
    update('conv_w', recv_conv, shard('conv_w'), shard('conv_w', 'm_'), shard('conv_w', 'v_'))

    small_sizes = [a[n].size for n in SMALL]
    n_small = sum(small_sizes) + 1
    n_pack = _round_up(n_small, SUBLANES * LANES)
    pack = lambda get, last: jnp.pad(jnp.concatenate([get(n).reshape(-1) for n in SMALL] + [last]),
                                     (0, n_pack - n_small)).reshape(n_pack // LANES, LANES)
    zero1 = jnp.zeros((1,), F32)
    g_pack = pack(lambda n: G[n], loss_rows[0, :1])
    parts_small = exchange(g_pack, 'gather', "gather_small")
    res = adamw(parts_small, pack(lambda n: a[n], zero1), pack(lambda n: a['m_' + n], zero1),
                pack(lambda n: a['v_' + n], zero1), "adamw_small")
    res = [t.reshape(-1) for t in res]
    o_ = 0
    for n, sz in zip(SMALL, small_sizes):
        out[n] = [t[o_:o_ + sz].reshape(a[n].shape) for t in res]
        o_ += sz
    loss = res[0][o_]

    return (loss, grad_x[None], *[out[n][0] for n in WEIGHTS], *[out[n][1] for n in WEIGHTS],
            *[out[n][2] for n in WEIGHTS], *[out[n][3] for n in WEIGHTS])


def kernel(x, mem, w_in, rwkv_shift_mix, rwkv_w0, rwkv_w_up, rwkv_a0, rwkv_a_up, rwkv_g_up, rwkv_k_k, rwkv_k_a, rwkv_r_k, rwkv_gn_g, rwkv_gn_b, conv_w, conv_b, conv_ln_g, conv_ln_b, proj_rwkv, proj_conv, w_out, ln1_g, ln1_b, ln_mem_g, ln_mem_b, xattn_wq, xattn_wk, xattn_wv, xattn_wo, ln2_g, ln2_b, mlp_w1, mlp_w2, ln3_g, ln3_b, loss_target, m_w_in, m_rwkv_shift_mix, m_rwkv_w0, m_rwkv_w_up, m_rwkv_a0, m_rwkv_a_up, m_rwkv_g_up, m_rwkv_k_k, m_rwkv_k_a, m_rwkv_r_k, m_rwkv_gn_g, m_rwkv_gn_b, m_conv_w, m_conv_b, m_conv_ln_g, m_conv_ln_b, m_proj_rwkv, m_proj_conv, m_w_out, m_ln1_g, m_ln1_b, m_ln_mem_g, m_ln_mem_b, m_xattn_wq, m_xattn_wk, m_xattn_wv, m_xattn_wo, m_ln2_g, m_ln2_b, m_mlp_w1, m_mlp_w2, m_ln3_g, m_ln3_b, v_w_in, v_rwkv_shift_mix, v_rwkv_w0, v_rwkv_w_up, v_rwkv_a0, v_rwkv_a_up, v_rwkv_g_up, v_rwkv_k_k, v_rwkv_k_a, v_rwkv_r_k, v_rwkv_gn_g, v_rwkv_gn_b, v_conv_w, v_conv_b, v_conv_ln_g, v_conv_ln_b, v_proj_rwkv, v_proj_conv, v_w_out, v_ln1_g, v_ln1_b, v_ln_mem_g, v_ln_mem_b, v_xattn_wq, v_xattn_wk, v_xattn_wv, v_xattn_wo, v_ln2_g, v_ln2_b, v_mlp_w1, v_mlp_w2, v_ln3_g, v_ln3_b):
    return _step(dict(locals()))
```

```python
import functools
import math

import jax
import jax.numpy as jnp
from jax import lax
from jax.experimental import pallas as pl
from jax.experimental.pallas import tpu as pltpu

F32 = jnp.float32
BF16 = jnp.bfloat16

N_DEV = 8
RWKV_HEAD = 64
SCAN_CHUNK = 64
XATTN_HEADS = 4
CONV_HALO = 32
LN_EPS = 1e-5
GN_EPS = 64e-5
ALPHA = float(2.0 ** 0.25)
ADAM_LR, ADAM_B1, ADAM_B2, ADAM_EPS, ADAM_WD, ADAM_STEP = 0.001, 0.9, 0.999, 1e-08, 0.01, 10
LANES = 128
SUBLANES = 8
VMEM_LIMIT = 56 * 1024 * 1024
ADAM_BLOCK_ELEMS = 256 * 1024
PAIR_BLOCK_ELEMS = 1024 * 1024

WEIGHTS = ['w_in', 'rwkv_shift_mix', 'rwkv_w0', 'rwkv_w_up', 'rwkv_a0', 'rwkv_a_up', 'rwkv_g_up', 'rwkv_k_k',
           'rwkv_k_a', 'rwkv_r_k', 'rwkv_gn_g', 'rwkv_gn_b', 'conv_w', 'conv_b', 'conv_ln_g', 'conv_ln_b',
           'proj_rwkv', 'proj_conv', 'w_out', 'ln1_g', 'ln1_b', 'ln_mem_g', 'ln_mem_b', 'xattn_wq', 'xattn_wk',
           'xattn_wv', 'xattn_wo', 'ln2_g', 'ln2_b', 'mlp_w1', 'mlp_w2', 'ln3_g', 'ln3_b']
COL_SHARDED = ['w_in', 'rwkv_w_up', 'rwkv_a_up', 'rwkv_g_up', 'conv_w', 'proj_rwkv', 'proj_conv', 'mlp_w1']
ROW_SHARDED = ['w_out', 'xattn_wq', 'xattn_wk', 'xattn_wv', 'xattn_wo', 'mlp_w2']
BIG = ['w_in', 'rwkv_w_up', 'rwkv_a_up', 'rwkv_g_up', 'conv_w', 'proj_rwkv', 'proj_conv', 'w_out', 'xattn_wq',
       'xattn_wk', 'xattn_wv', 'xattn_wo', 'mlp_w1', 'mlp_w2']
SMALL = [w for w in WEIGHTS if w not in BIG]


def _cparams(dims):
    return pltpu.CompilerParams(dimension_semantics=dims, vmem_limit_bytes=VMEM_LIMIT)


def _tile(n, cands):
    for c in cands:
        if n % c == 0:
            return c
    return n


N_CHIP = N_DEV // 2
SEMS_PER_EXCHANGE = N_DEV + 2
CHIP_XORS = (2, 4, 6)


def _exchange_shape(src, kind):
    return {'gather': (N_DEV,) + src.shape, 'gather2': (N_DEV,) + src.shape, 'scatter': src.shape,
            'scatter_sib': (N_CHIP,) + src.shape[1:], 'scatter_chips': src.shape}[kind]


def _exchange_copies(e, src_ref, out_ref, kind, send_sems, recv_sems, local_sems):
    x, y, c = lax.axis_index("x"), lax.axis_index("y"), lax.axis_index("c")
    me, chip = 4 * x + 2 * y + c, 2 * x + y
    sibling = (x, y, 1 - c)
    base = e * SEMS_PER_EXCHANGE

    def remote(src, dst, idx, dev):
        return pltpu.make_async_remote_copy(src_ref=src, dst_ref=dst, send_sem=send_sems.at[base + idx],
                                            recv_sem=recv_sems.at[base + idx], device_id=dev,
                                            device_id_type=pl.DeviceIdType.MESH)

    def peer(k):
        return x ^ ((k >> 2) & 1), y ^ ((k >> 1) & 1), c ^ (k & 1)

    first, second = [], []
    if kind in ('gather', 'gather2'):
        first.append(pltpu.make_async_copy(src_ref, out_ref.at[me], local_sems.at[e]))
        for k in (range(1, N_DEV) if kind == 'gather' else (1,) + CHIP_XORS):
            first.append(remote(src_ref, out_ref.at[me], k - 1, peer(k)))
        if kind == 'gather2':
            for i, k in enumerate(CHIP_XORS):
                second.append(remote(out_ref.at[me ^ k], out_ref.at[me ^ k], N_DEV - 1 + i, sibling))
    elif kind == 'scatter':
        first.append(pltpu.make_async_copy(src_ref.at[me], out_ref.at[me], local_sems.at[e]))
        for k in range(1, N_DEV):
            px, py, pc = peer(k)
            first.append(remote(src_ref.at[4 * px + 2 * py + pc], out_ref.at[me], k - 1, (px, py, pc)))
    elif kind == 'scatter_sib':
        for q in range(N_CHIP):
            first.append(remote(src_ref.at[2 * q + 1 - c], out_ref.at[q], q, sibling))
    else:
        assert kind == 'scatter_chips', kind
        first.append(pltpu.make_async_copy(src_ref.at[chip], out_ref.at[chip], local_sems.at[e]))
        for k in CHIP_XORS:
            first.append(remote(src_ref.at[chip ^ (k >> 1)], out_ref.at[chip], k - 1, peer(k)))
    return first, second


class _Side:
    def __init__(self, items):
        self.items = list(items or [])
        self.n = len(self.items)
        any_spec = pl.BlockSpec(memory_space=pl.ANY)
        self.srcs = [s for s, _ in self.items]
        self.in_specs = [any_spec] * self.n
        self.out_specs = [any_spec] * self.n
        self.out_shapes = [jax.ShapeDtypeStruct(_exchange_shape(s, kind), s.dtype) for s, kind in self.items]
        self.scratch = [pltpu.SemaphoreType.DMA((self.n * SEMS_PER_EXCHANGE,)),
                        pltpu.SemaphoreType.DMA((self.n * SEMS_PER_EXCHANGE,)),
                        pltpu.SemaphoreType.DMA((self.n,))] if self.n else []

    def _copies(self, src_refs, out_refs, sems):
        both = [_exchange_copies(e, src_refs[e], out_refs[e], kind, *sems) for e, (_, kind) in enumerate(self.items)]
        return [cp for f, _ in both for cp in f], [cp for _, s in both for cp in s]

    def start(self, src_refs, out_refs, sems):
        for cp in self._copies(src_refs, out_refs, sems)[0]:
            cp.start()

    def finish(self, src_refs, out_refs, sems):
        first, second = self._copies(src_refs, out_refs, sems)
        for cp in first:
            cp.wait()
        for cp in second:
            cp.start()
        for cp in second:
            cp.wait()

    def run(self, first, last, src_refs, out_refs, sems):
        if not self.n:
            return lambda: None
        pl.when(first)(lambda: self.start(src_refs, out_refs, sems))
        return lambda: pl.when(last)(lambda: self.finish(src_refs, out_refs, sems))


def exchange(src, kind, name):
    side = _Side([(src, kind)])

    def body(src_ref, out_ref, *sems):
        side.start([src_ref], [out_ref], sems)
        side.finish([src_ref], [out_ref], sems)

    return pl.pallas_call(body, name=name, in_specs=side.in_specs, out_specs=side.out_specs[0],
                          out_shape=side.out_shapes[0], scratch_shapes=side.scratch)(src)


def pair_add(parts, landed, name):
    _, R, C = parts.shape
    tr = R
    if R * C > PAIR_BLOCK_ELEMS:
        tr = _tile(R, [t for t in (2048, 1024, 512, 256, 128, 64, 32, 16) if t * C <= PAIR_BLOCK_ELEMS])

    def body(core_ref, p_ref, l_ref, o_ref):
        o_ref[...] = (p_ref[...].astype(F32) + l_ref[...].astype(F32)).astype(BF16)

    blk = pl.BlockSpec((None, tr, C), lambda q, i, core_ref: (q, i, 0))
    mine = pl.BlockSpec((None, None, tr, C), lambda q, i, core_ref: (q, core_ref[0], i, 0))
    return pl.pallas_call(
        body, name=name,
        grid_spec=pltpu.PrefetchScalarGridSpec(num_scalar_prefetch=1, grid=(N_CHIP, R // tr), in_specs=[mine, blk],
                                               out_specs=blk),
        out_shape=jax.ShapeDtypeStruct((N_CHIP, R, C), BF16),
        compiler_params=_cparams(("parallel", "parallel")))(
            lax.axis_index("c").astype(jnp.int32).reshape(1), parts.reshape(N_CHIP, 2, R, C), landed)


def matmul(a, b, mode, name, add=None, out_dtype=F32, b_dev=False, out_dev=False, side=None):
    side = _Side(side)
    if b_dev:
        assert mode in ('nn', 'nt') and b.shape[0] == N_DEV
        b_rows, b_cols = b.shape[1], N_DEV * b.shape[2]
    else:
        b_rows, b_cols = b.shape
    if mode == 'nn':
        (M, K), (K2, N) = a.shape, (b_rows, b_cols)
    elif mode == 'nt':
        (M, K), (N, K2) = a.shape, (b_rows, b_cols)
    else:
        (K, M), (K2, N) = a.shape, (b_rows, b_cols)
    assert K == K2, (a.shape, b.shape, mode)
    n_unit = N // N_DEV if (out_dev or (b_dev and mode == 'nn')) else N
    k_unit = K // N_DEV if (b_dev and mode == 'nt') else K
    tn = _tile(n_unit, (1024, 512, 256, 128))
    tm = _tile(M, (1024, 512, 256, 128) if tn >= 1024 else (2048, 1024, 512, 256, 128))
    tk = _tile(k_unit, (2048, 1024, 512, 256, 128))
    nk = K // tk
    nb, kb = n_unit // tn, k_unit // tk
    dims = {'nn': ((1,), (0,)), 'nt': ((1,), (1,)), 'tn': ((0,), (0,))}[mode]

    n_in = 2 + (add is not None)
    grid = (M // tm, N // tn, nk)

    def body(*refs):
        a_ref, b_ref = refs[:2]
        add_ref = refs[2] if add is not None else None
        side_src = refs[n_in:n_in + side.n]
        o_ref = refs[n_in + side.n]
        side_out = refs[n_in + side.n + 1:n_in + 2 * side.n + 1]
        acc_ref = refs[n_in + 2 * side.n + 1]
        sems = refs[n_in + 2 * side.n + 2:]
        i, j, k = pl.program_id(0), pl.program_id(1), pl.program_id(2)
        finish = side.run((i == 0) & (j == 0) & (k == 0), (i == grid[0] - 1) & (j == grid[1] - 1) & (k == nk - 1),
                          side_src, side_out, sems)

        @pl.when(k == 0)
        def _():
            acc_ref[...] = jnp.zeros_like(acc_ref)

        acc_ref[...] += lax.dot_general(a_ref[...].astype(BF16), b_ref[...].astype(BF16), (dims, ((), ())),
                                        preferred_element_type=F32)

        @pl.when(k == nk - 1)
        def _():
            r = acc_ref[...]
            if add is not None:
                r = r + add_ref[...]
            o_ref[...] = r.astype(out_dtype)

        finish()

    if mode == 'nn':
        a_spec = pl.BlockSpec((tm, tk), lambda i, j, k: (i, k))
        b_spec = (pl.BlockSpec((None, tk, tn), lambda i, j, k: (j // nb, k, j % nb)) if b_dev
                  else pl.BlockSpec((tk, tn), lambda i, j, k: (k, j)))
    elif mode == 'nt':
        a_spec = pl.BlockSpec((tm, tk), lambda i, j, k: (i, k))
        b_spec = (pl.BlockSpec((None, tn, tk), lambda i, j, k: (k // kb, j, k % kb)) if b_dev
                  else pl.BlockSpec((tn, tk), lambda i, j, k: (j, k)))
    else:
        a_spec = pl.BlockSpec((tk, tm), lambda i, j, k: (k, i))
        b_spec = pl.BlockSpec((tk, tn), lambda i, j, k: (k, j))
    add_spec = pl.BlockSpec((tm, tn), lambda i, j, k: (i, j))
    if out_dev:
        o_spec = pl.BlockSpec((None, tm, tn), lambda i, j, k: (j // nb, i, j % nb))
        o_shape = (N_DEV, M, N // N_DEV)
    else:
        o_spec, o_shape = add_spec, (M, N)
    in_specs = [a_spec, b_spec] + ([add_spec] if add is not None else [])
    ops = (a, b) + ((add,) if add is not None else ())
    sem = ("arbitrary",) * 3 if side.n else ("parallel", "parallel", "arbitrary")
    res = pl.pallas_call(
        body, name=name, grid=grid, in_specs=in_specs + side.in_specs, out_specs=[o_spec] + side.out_specs,
        out_shape=[jax.ShapeDtypeStruct(o_shape, out_dtype)] + side.out_shapes,
        scratch_shapes=[pltpu.VMEM((tm, tn), F32)] + side.scratch,
        compiler_params=_cparams(sem))(*ops, *side.srcs)
    return tuple(res) if side.n else res[0]


class _Rows2D:
    def __init__(self, n_rows, tm):
        self.n, self.tm = n_rows, tm
        self.grid = (1, n_rows // tm)

    def row(self, e):
        if isinstance(e, tuple):
            arr, width, cb = e
            return arr, (self.tm, width), pl.BlockSpec((self.tm, width), lambda g, i, cb=cb: (i, cb))
        return e, (self.tm, e.shape[1]), pl.BlockSpec((self.tm, e.shape[1]), lambda g, i: (i, 0))

    def par(self, p):
        return pl.BlockSpec(p.shape, lambda g, i: (0,) * p.ndim)

    def out(self, blk):
        return (self.n, blk[1]), pl.BlockSpec((self.tm, blk[1]), lambda g, i: (i, 0))


class _RowsHeads:
    def __init__(self, n_heads, n_rows, hb, ts):
        self.h, self.n, self.hb, self.ts = n_heads, n_rows, hb, ts
        self.grid = (n_heads // hb, n_rows // ts)

    def row(self, e):
        blk = (self.hb, self.ts, e.shape[2])
        return e, blk, pl.BlockSpec(blk, lambda g, i: (g, i, 0))

    def par(self, p):
        return pl.BlockSpec((self.hb, 1, p.shape[2]), lambda g, i: (g, 0, 0))

    def out(self, blk):
        return (self.h, self.n, blk[2]), pl.BlockSpec(blk, lambda g, i: (g, i, 0))


def _par_block(lay, p):
    return lay.par(p).block_shape


def ew_fwd(lay, fn, rows, params, name):
    rr = [lay.row(e) for e in rows]
    arrs = [r[0] for r in rr]
    blk_avals = [jax.ShapeDtypeStruct(r[1], r[0].dtype) for r in rr]
    par_avals = [jax.ShapeDtypeStruct(_par_block(lay, p), p.dtype) for p in params]
    outs = jax.eval_shape(fn, *blk_avals, *par_avals)
    out_full = [lay.out(o.shape) for o in outs]
    nr, npar = len(rows), len(params)

    def body(*refs):
        vals = [r[...] for r in refs[:nr + npar]]
        res = fn(*vals)
        for ref, v in zip(refs[nr + npar:], res):
            ref[...] = v.astype(ref.dtype)

    return pl.pallas_call(
        body, name=name, grid=lay.grid,
        in_specs=[r[2] for r in rr] + [lay.par(p) for p in params],
        out_specs=[o[1] for o in out_full],
        out_shape=[jax.ShapeDtypeStruct(o[0], a.dtype) for o, a in zip(out_full, outs)],
        compiler_params=_cparams(("parallel", "parallel")))(*arrs, *params)


def ew_bwd(lay, fn, rows, params, cots, wrt_rows, wrt_pars, name, dr_dtypes=None):
    rr = [lay.row(e) for e in rows]
    cc = [lay.row(e) for e in cots]
    nr, npar, nc = len(rows), len(params), len(cots)
    n_dr = len(wrt_rows)
    dr_full = [lay.out(rr[i][1]) for i in wrt_rows]
    dr_dtypes = dr_dtypes or [F32] * n_dr

    def body(*refs):
        rv = [r[...] for r in refs[:nr]]
        pv = [r[...] for r in refs[nr:nr + npar]]
        cv = tuple(r[...] for r in refs[nr + npar:nr + npar + nc])
        outs = refs[nr + npar + nc:]

        def f(*wrt):
            r2, p2 = list(rv), list(pv)
            for idx, v in zip(wrt_rows, wrt[:n_dr]):
                r2[idx] = v
            for idx, v in zip(wrt_pars, wrt[n_dr:]):
                p2[idx] = v
            return fn(*r2, *p2)

        _, vjp = jax.vjp(f, *[rv[i] for i in wrt_rows], *[pv[i] for i in wrt_pars])
        g = vjp(cv)
        for ref, v in zip(outs[:n_dr], g[:n_dr]):
            ref[...] = v.astype(ref.dtype)
        if wrt_pars:
            @pl.when(pl.program_id(1) == 0)
            def _():
                for ref in outs[n_dr:]:
                    ref[...] = jnp.zeros_like(ref)

            for ref, v in zip(outs[n_dr:], g[n_dr:]):
                ref[...] += v

    return pl.pallas_call(
        body, name=name, grid=lay.grid,
        in_specs=[r[2] for r in rr] + [lay.par(p) for p in params] + [c[2] for c in cc],
        out_specs=[o[1] for o in dr_full] + [lay.par(params[i]) for i in wrt_pars],
        out_shape=[jax.ShapeDtypeStruct(o[0], dt) for o, dt in zip(dr_full, dr_dtypes)]
        + [jax.ShapeDtypeStruct(params[i].shape, F32) for i in wrt_pars],
        compiler_params=_cparams(("parallel", "arbitrary")))(
            *[r[0] for r in rr], *params, *[c[0] for c in cc])


def _sigmoid(x):
    return 1.0 / (1.0 + jnp.exp(-x))


def _softplus(x):
    return jnp.maximum(x, 0.0) + jnp.log(1.0 + jnp.exp(-jnp.abs(x)))


def _layer_norm(x, g, b, eps):
    mu = jnp.mean(x, -1, keepdims=True)
    xc = x - mu
    var = jnp.mean(xc * xc, -1, keepdims=True)
    return xc * lax.rsqrt(var + eps) * g + b


def _as_bf16(fn):
    return lambda *args: tuple(o.astype(BF16) for o in fn(*args))


def fn_ln(x, g, b):
    return (_layer_norm(x, g, b, LN_EPS),)


def fn_ln_res(h, t, g, b):
    return (_layer_norm(ALPHA * h + t, g, b, LN_EPS),)


def fn_ln_res_both(h, t, g, b):
    y, = fn_ln_res(h, t, g, b)
    return y, y.astype(BF16)


def make_fn_lora(r_decay, r_iclr, r_gate):
    def fn(z):
        lane = lax.broadcasted_iota(jnp.int32, z.shape, 1)
        out = jnp.where(lane < r_decay, jnp.tanh(z), z)
        out = jnp.where(lane >= r_decay + r_iclr, _sigmoid(z), out)
        return (jnp.where(lane < r_decay + r_iclr + r_gate, out, 0.0),)
    return fn


def fn_rwkv_pre(k, wl, al, w0, a0, k_k, k_a):
    w = -_softplus(-(w0 + wl)) - 0.5
    lw = -jnp.exp(w)
    a = _sigmoid(a0 + al)
    kk = k * k_k
    kk = kk / jnp.maximum(jnp.sqrt(jnp.sum(kk * kk, -1, keepdims=True)), 1e-12)
    k2 = k * (1.0 + (a - 1.0) * k_a)
    return lw, k2, -kk, kk * a


def fn_rwkv_post(o, r, k2, v, g, gn_g, gn_b, r_k):
    mu = jnp.mean(o, -1, keepdims=True)
    oc = o - mu
    var = jnp.mean(oc * oc, -1, keepdims=True)
    y = oc * lax.rsqrt(var + GN_EPS) * gn_g + gn_b
    y = y + jnp.sum(r * k2 * r_k, -1, keepdims=True) * v
    return (y * g,)


def fn_glu(zu, zg):
    return (zu * _sigmoid(zg),)


def fn_ln_silu(y, g, b):
    n = _layer_norm(y, g, b, LN_EPS)
    return (n * _sigmoid(n),)


def fn_merge(zgr, zgc, pr, pc):
    return (_sigmoid(zgr) * pr + _sigmoid(zgc) * pc,)


def fn_relu2(f):
    r = jnp.maximum(f, 0.0)
    return (r * r,)


def _dot_nn(a, b):
    return lax.dot_general(a.astype(BF16), b.astype(BF16), (((1,), (0,)), ((), ())), preferred_element_type=F32)


def _dot_nt(a, b):
    return lax.dot_general(a.astype(BF16), b.astype(BF16), (((1,), (1,)), ((), ())), preferred_element_type=F32)


def _dot_tn(a, b):
    return lax.dot_general(a.astype(BF16), b.astype(BF16), (((0,), (0,)), ((), ())), preferred_element_type=F32)


def _softmax_rows(s):
    s = s - jnp.max(s, -1, keepdims=True)
    e = jnp.exp(s)
    return e / jnp.sum(e, -1, keepdims=True)


def attn_fwd(q, kx, vx, tm, name):
    S, D = q.shape
    M = kx.shape[0]
    dh = D // XATTN_HEADS
    scale = dh ** -0.5

    def body(q_ref, k_ref, v_ref, o_ref):
        p = _softmax_rows(_dot_nt(q_ref[...], k_ref[...]) * scale)
        o_ref[...] = _dot_nn(p, v_ref[...]).astype(BF16)

    row = pl.BlockSpec((tm, dh), lambda h, i: (i, h))
    kv = pl.BlockSpec((M, dh), lambda h, i: (0, h))
    return pl.pallas_call(body, name=name, grid=(XATTN_HEADS, S // tm), in_specs=[row, kv, kv], out_specs=row,
                          out_shape=jax.ShapeDtypeStruct((S, D), BF16),
                          compiler_params=_cparams(("parallel", "parallel")))(q, kx, vx)


def attn_bwd(q, kx, vx, do, tm, name):
    S, D = q.shape
    M = kx.shape[0]
    dh = D // XATTN_HEADS
    scale = dh ** -0.5

    def body(q_ref, k_ref, v_ref, do_ref, dq_ref, dk_ref, dv_ref):
        qb, kb, dob = q_ref[...], k_ref[...], do_ref[...]
        p = _softmax_rows(_dot_nt(qb, kb) * scale)
        dp = _dot_nt(dob, v_ref[...])
        ds = p * (dp - jnp.sum(dp * p, -1, keepdims=True)) * scale
        dq_ref[...] = _dot_nn(ds, kb).astype(BF16)

        @pl.when(pl.program_id(1) == 0)
        def _():
            dk_ref[...] = jnp.zeros_like(dk_ref)
            dv_ref[...] = jnp.zeros_like(dv_ref)

        dk_ref[...] += _dot_tn(ds, qb)
        dv_ref[...] += _dot_tn(p, dob)

    row = pl.BlockSpec((tm, dh), lambda h, i: (i, h))
    kv = pl.BlockSpec((M, dh), lambda h, i: (0, h))
    return pl.pallas_call(
        body, name=name, grid=(XATTN_HEADS, S // tm), in_specs=[row, kv, kv, row], out_specs=[row, kv, kv],
        out_shape=[jax.ShapeDtypeStruct((S, D), BF16), jax.ShapeDtypeStruct((M, D), F32), jax.ShapeDtypeStruct((M, D), F32)],
        compiler_params=_cparams(("parallel", "arbitrary")))(q, kx, vx, do)


def _shift_down(blk, halo_last_row, first_block):
    rolled = pltpu.roll(blk, 1, 0)
    row = lax.broadcasted_iota(jnp.int32, blk.shape, 0)
    top = jnp.where(first_block, 0.0, halo_last_row)
    return jnp.where(row == 0, top, rolled)


def _shift_up(blk, halo_first_row, last_block):
    n = blk.shape[0]
    rolled = pltpu.roll(blk, n - 1, 0)
    row = lax.broadcasted_iota(jnp.int32, blk.shape, 0)
    bot = jnp.where(last_block, 0.0, halo_first_row)
    return jnp.where(row == n - 1, bot, rolled)


def _zcol(j, cols):
    n_first, first, second = cols
    return jnp.where(j < n_first, first + j, second + j - n_first)


def token_shift_fwd(z, cols, cw, width, mu, tm, name):
    S = z.shape[0]
    hb = tm // SUBLANES

    def body(z_ref, halo_ref, mu_ref, o_ref):
        zb = z_ref[...]
        prev = _shift_down(zb, halo_ref[SUBLANES - 1:SUBLANES, :], pl.program_id(1) == 0)
        o_ref[...] = zb + (prev - zb) * mu_ref[...]

    return pl.pallas_call(
        body, name=name, grid=(width // cw, S // tm),
        in_specs=[pl.BlockSpec((tm, cw), lambda j, i: (i, _zcol(j, cols))),
                  pl.BlockSpec((SUBLANES, cw), lambda j, i: (jnp.maximum(i * hb - 1, 0), _zcol(j, cols))),
                  pl.BlockSpec((1, cw), lambda j, i: (0, j))],
        out_specs=pl.BlockSpec((tm, cw), lambda j, i: (i, j)),
        out_shape=jax.ShapeDtypeStruct((S, width), F32),
        compiler_params=_cparams(("parallel", "parallel")))(z, z, mu)


def token_shift_bwd(z, dzs, cols, cw, width, mu, tm, name):
    S = z.shape[0]
    hb = tm // SUBLANES
    nblk = S // tm
    halo = 2 * SUBLANES
    last_halo = S // halo - 1

    def body(z_ref, zh_ref, d_ref, dh_ref, mu_ref, dz_ref, dmu_ref):
        i = pl.program_id(1)
        zb, db, m = z_ref[...], d_ref[...].astype(F32), mu_ref[...]
        prev = _shift_down(zb, zh_ref[SUBLANES - 1:SUBLANES, :], i == 0)
        dm = db * m
        nxt = _shift_up(dm, dh_ref[0:1, :].astype(F32) * m, i == nblk - 1)
        dz_ref[...] = (db - dm + nxt).astype(BF16)

        @pl.when(i == 0)
        def _():
            dmu_ref[...] = jnp.zeros_like(dmu_ref)

        dmu_ref[...] += jnp.sum(db * (prev - zb), 0, keepdims=True)

    return pl.pallas_call(
        body, name=name, grid=(width // cw, nblk),
        in_specs=[pl.BlockSpec((tm, cw), lambda j, i: (i, _zcol(j, cols))),
                  pl.BlockSpec((SUBLANES, cw), lambda j, i: (jnp.maximum(i * hb - 1, 0), _zcol(j, cols))),
                  pl.BlockSpec((tm, cw), lambda j, i: (i, j)),
                  pl.BlockSpec((halo, cw), lambda j, i: (jnp.minimum((i + 1) * (tm // halo), last_halo), j)),
                  pl.BlockSpec((1, cw), lambda j, i: (0, j))],
        out_specs=[pl.BlockSpec((tm, cw), lambda j, i: (i, j)), pl.BlockSpec((1, cw), lambda j, i: (0, j))],
        out_shape=[jax.ShapeDtypeStruct((S, width), BF16), jax.ShapeDtypeStruct((1, width), F32)],
        compiler_params=_cparams(("parallel", "arbitrary")))(z, z, dzs, dzs, mu)


def conv_fwd(u, w, b, width, tr, cb, name):
    S, C = u.shape
    hb = tr // CONV_HALO

    def body(u_ref, h_ref, w_ref, b_ref, y_ref):
        i = pl.program_id(1)
        halo = jnp.where(i == 0, 0.0, h_ref[...])
        win = jnp.concatenate([halo, u_ref[...]], axis=0)
        acc = jnp.zeros((tr, cb), F32) + b_ref[...]
        for d in range(width):
            sh = win if d == 0 else pltpu.roll(win, d, 0)
            acc = acc + sh[CONV_HALO:, :] * w_ref[width - 1 - d:width - d, :]
        y_ref[...] = acc

    return pl.pallas_call(
        body, name=name, grid=(C // cb, S // tr),
        in_specs=[pl.BlockSpec((tr, cb), lambda j, i: (i, j)),
                  pl.BlockSpec((CONV_HALO, cb), lambda j, i: (jnp.maximum(i * hb - 1, 0), j)),
                  pl.BlockSpec((CONV_HALO, cb), lambda j, i: (0, j)),
                  pl.BlockSpec((1, cb), lambda j, i: (0, j))],
        out_specs=pl.BlockSpec((tr, cb), lambda j, i: (i, j)),
        out_shape=jax.ShapeDtypeStruct((S, C), F32), compiler_params=_cparams(("parallel", "parallel")))(u, u, w, b)


def conv_bwd(u, dy, w, width, tr, cb, name):
    S, C = u.shape
    hb = tr // CONV_HALO
    nblk = S // tr
    last = S // CONV_HALO - 1

    def body(u_ref, uh_ref, d_ref, dh_ref, w_ref, du_ref, dw_ref, db_ref):
        i = pl.program_id(1)
        dyb = d_ref[...]
        uwin = jnp.concatenate([jnp.where(i == 0, 0.0, uh_ref[...]), u_ref[...]], axis=0)
        dwin = jnp.concatenate([dyb, jnp.where(i == nblk - 1, 0.0, dh_ref[...])], axis=0)

        @pl.when(i == 0)
        def _():
            dw_ref[...] = jnp.zeros_like(dw_ref)
            db_ref[...] = jnp.zeros_like(db_ref)

        acc = jnp.zeros((tr, cb), F32)
        for d in range(width):
            tap = width - 1 - d
            dsh = dwin if d == 0 else pltpu.roll(dwin, tr + CONV_HALO - d, 0)
            acc = acc + dsh[:tr, :] * w_ref[tap:tap + 1, :]
            ush = uwin if d == 0 else pltpu.roll(uwin, d, 0)
            dw_ref[tap:tap + 1, :] += jnp.sum(ush[CONV_HALO:, :] * dyb, 0, keepdims=True)
        du_ref[...] = acc
        db_ref[...] += jnp.sum(dyb, 0, keepdims=True)

    return pl.pallas_call(
        body, name=name, grid=(C // cb, nblk),
        in_specs=[pl.BlockSpec((tr, cb), lambda j, i: (i, j)),
                  pl.BlockSpec((CONV_HALO, cb), lambda j, i: (jnp.maximum(i * hb - 1, 0), j)),
                  pl.BlockSpec((tr, cb), lambda j, i: (i, j)),
                  pl.BlockSpec((CONV_HALO, cb), lambda j, i: (jnp.minimum((i + 1) * hb, last), j)),
                  pl.BlockSpec((CONV_HALO, cb), lambda j, i: (0, j))],
        out_specs=[pl.BlockSpec((tr, cb), lambda j, i: (i, j)),
                   pl.BlockSpec((CONV_HALO, cb), lambda j, i: (0, j)),
                   pl.BlockSpec((1, cb), lambda j, i: (0, j))],
        out_shape=[jax.ShapeDtypeStruct((S, C), F32), jax.ShapeDtypeStruct((CONV_HALO, C), F32),
                   jax.ShapeDtypeStruct((1, C), F32)],
        compiler_params=_cparams(("parallel", "arbitrary")))(u, u, dy, dy, w)


def _split2(x):
    hi = x.astype(BF16)
    return hi, (x - hi.astype(F32)).astype(BF16)


def _dot3(a, b, dims):
    ah, al = _split2(a)
    bh, bl = _split2(b)
    d = lambda p, q: lax.dot_general(p, q, dims, preferred_element_type=F32)
    return d(ah, bh) + (d(ah, bl) + d(al, bh))


@jax.custom_vjp
def _bnn(a, b):
    return _dot3(a, b, (((2,), (1,)), ((0,), (0,))))


@jax.custom_vjp
def _bnt(a, b):
    return _dot3(a, b, (((2,), (2,)), ((0,), (0,))))


@jax.custom_vjp
def _btn(a, b):
    return _dot3(a, b, (((1,), (1,)), ((0,), (0,))))


def _tri_sum(x, lower):
    h, c, _ = x.shape
    ti = lax.broadcasted_iota(jnp.int32, (h, c, c), 1)
    si = lax.broadcasted_iota(jnp.int32, (h, c, c), 2)
    m = (si <= ti if lower else si >= ti).astype(BF16)
    x1 = x.astype(BF16)
    r1 = x - x1.astype(F32)
    x2 = r1.astype(BF16)
    x3 = (r1 - x2.astype(F32)).astype(BF16)
    d = lambda q: lax.dot_general(m, q, (((2,), (1,)), ((0,), (0,))), preferred_element_type=F32)
    return d(x1) + (d(x2) + d(x3))


@jax.custom_vjp
def _cumsum_rows(x):
    return _tri_sum(x, True)


@jax.custom_vjp
def _rev_cumsum_rows(x):
    return _tri_sum(x, False)


_cumsum_rows.defvjp(lambda x: (_cumsum_rows(x), None), lambda _, g: (_rev_cumsum_rows(g),))
_rev_cumsum_rows.defvjp(lambda x: (_rev_cumsum_rows(x), None), lambda _, g: (_cumsum_rows(g),))


@jax.custom_vjp
def _unit_lower_inverse(a):
    h, c, _ = a.shape
    eye = (lax.broadcasted_iota(jnp.int32, (h, c, c), 1) == lax.broadcasted_iota(jnp.int32, (h, c, c), 2)).astype(F32)
    t, pw = eye + a, a
    for _ in range(int(math.log2(c)) - 1):
        pw = _bnn(pw, pw)
        t = t + _bnn(t, pw)
    return t


def _unit_lower_inverse_fwd(a):
    t = _unit_lower_inverse(a)
    return t, t


_unit_lower_inverse.defvjp(_unit_lower_inverse_fwd, lambda t, g: (_btn(t, _bnt(g, t)),))


_bnn.defvjp(lambda a, b: (_bnn(a, b), (a, b)), lambda res, g: (_bnt(g, res[1]), _btn(res[0], g)))
_bnt.defvjp(lambda a, b: (_bnt(a, b), (a, b)), lambda res, g: (_bnn(g, res[1]), _btn(g, res[0])))
_btn.defvjp(lambda a, b: (_btn(a, b), (a, b)), lambda res, g: (_bnt(res[1], g), _bnn(res[0], g)))


@jax.custom_vjp
def _known_inverse(a, t):
    return t


_known_inverse.defvjp(lambda a, t: (t, t), lambda t, g: (_btn(t, _bnt(g, t)), jnp.zeros_like(t)))


def scan_chunk(st0, r, lw, k, v, a, b, tinv=None):
    h, c, n = r.shape
    ti = lax.broadcasted_iota(jnp.int32, (h, c, c), 1)
    si = lax.broadcasted_iota(jnp.int32, (h, c, c), 2)
    incl = si <= ti
    strict = si < ti
    cum = _cumsum_rows(lw)
    p = jnp.exp(cum)
    pinv = jnp.exp(-cum)
    at = a * jnp.exp(cum - lw)
    bt, kt, rt = b * pinv, k * pinv, r * p
    a_ab = jnp.where(strict, _bnt(at, bt), 0.0)
    a_ak = jnp.where(strict, _bnt(at, kt), 0.0)
    m_rb = jnp.where(incl, _bnt(rt, bt), 0.0)
    m_rk = jnp.where(incl, _bnt(rt, kt), 0.0)
    tinv = _unit_lower_inverse(a_ab) if tinv is None else _known_inverse(a_ab, tinv)
    u = _bnn(tinv, _bnn(at, st0) + _bnn(a_ak, v))
    o = _bnn(rt, st0) + _bnn(m_rb, u) + _bnn(m_rk, v)
    cum_c = jnp.sum(lw, axis=1, keepdims=True)
    tail = jnp.exp(cum_c - cum)
    ki = lax.broadcasted_iota(jnp.int32, (h, n, n), 1)
    kj = lax.broadcasted_iota(jnp.int32, (h, n, n), 2)
    pc_col = jnp.sum(jnp.where(ki == kj, jnp.exp(cum_c), 0.0), axis=2, keepdims=True)
    st = st0 * pc_col + _btn(b * tail, u) + _btn(k * tail, v)
    return o, st, tinv


def _grid_ends(grid):
    g, c = pl.program_id(0), pl.program_id(1)
    return (g == 0) & (c == 0), (g == grid[0] - 1) & (c == grid[1] - 1)


def scan_fwd(r, lw, k, v, a, b, hb, name, side=None):
    H, S, N = r.shape
    C = SCAN_CHUNK
    nc = S // C
    side = _Side(side)
    grid = (H // hb, nc)

    def body(*refs):
        r_ref, lw_ref, k_ref, v_ref, a_ref, b_ref = refs[:6]
        side_src = refs[6:6 + side.n]
        o_ref, ck_ref, ti_ref = refs[6 + side.n:9 + side.n]
        side_out = refs[9 + side.n:9 + 2 * side.n]
        st_ref = refs[9 + 2 * side.n]
        finish = side.run(*_grid_ends(grid), side_src, side_out, refs[10 + 2 * side.n:])

        @pl.when(pl.program_id(1) == 0)
        def _():
            st_ref[...] = jnp.zeros_like(st_ref)

        st0 = st_ref[...]
        ck_ref[...] = st0[:, None]
        o, st, tinv = scan_chunk(st0, r_ref[...], lw_ref[...], k_ref[...], v_ref[...], a_ref[...], b_ref[...])
        o_ref[...] = o
        ti_ref[...] = tinv[:, None]
        st_ref[...] = st
        finish()

    seq = pl.BlockSpec((hb, C, N), lambda g, c: (g, c, 0))
    return pl.pallas_call(
        body, name=name, grid=grid, in_specs=[seq] * 6 + side.in_specs,
        out_specs=[seq, pl.BlockSpec((hb, 1, N, N), lambda g, c: (g, c, 0, 0)),
                   pl.BlockSpec((hb, 1, C, C), lambda g, c: (g, c, 0, 0))] + side.out_specs,
        out_shape=[jax.ShapeDtypeStruct((H, S, N), F32), jax.ShapeDtypeStruct((H, nc, N, N), F32),
                   jax.ShapeDtypeStruct((H, nc, C, C), F32)] + side.out_shapes,
        scratch_shapes=[pltpu.VMEM((hb, N, N), F32)] + side.scratch,
        compiler_params=_cparams(("arbitrary", "arbitrary")))(r, lw, k, v, a, b, *side.srcs)


def scan_bwd(r, lw, k, v, a, b, ck, ti, do, dr_add, dk_add, dv_add, hb, name, side=None):
    H, S, N = r.shape
    C = SCAN_CHUNK
    nc = S // C
    side = _Side(side)
    grid = (H // hb, nc)

    def body(*refs):
        r_ref, lw_ref, k_ref, v_ref, a_ref, b_ref, ck_ref, ti_ref, do_ref, ra_ref, ka_ref, va_ref = refs[:12]
        side_src = refs[12:12 + side.n]
        dr_ref, dlw_ref, dk_ref, dv_ref, da_ref, db_ref = refs[12 + side.n:18 + side.n]
        side_out = refs[18 + side.n:18 + 2 * side.n]
        dst_ref = refs[18 + 2 * side.n]
        finish = side.run(*_grid_ends(grid), side_src, side_out, refs[19 + 2 * side.n:])

        @pl.when(pl.program_id(1) == 0)
        def _():
            dst_ref[...] = jnp.zeros_like(dst_ref)

        st0, tinv = ck_ref[...][:, 0], ti_ref[...][:, 0]
        chunk = lambda *args: scan_chunk(*args, tinv=tinv)[:2]
        _, vjp = jax.vjp(chunk, st0, r_ref[...], lw_ref[...], k_ref[...], v_ref[...], a_ref[...], b_ref[...])
        dst0, dr, dlw, dk, dv, da, db = vjp((do_ref[...], dst_ref[...]))
        dr_ref[...], dlw_ref[...], dk_ref[...] = (dr + ra_ref[...]).astype(BF16), dlw, dk + ka_ref[...]
        dv_ref[...], da_ref[...], db_ref[...] = (dv + va_ref[...]).astype(BF16), da, db
        dst_ref[...] = dst0
        finish()

    seq = pl.BlockSpec((hb, C, N), lambda g, c: (g, nc - 1 - c, 0))
    per_chunk = lambda n: pl.BlockSpec((hb, 1, n, n), lambda g, c: (g, nc - 1 - c, 0, 0))
    return pl.pallas_call(
        body, name=name, grid=grid,
        in_specs=[seq] * 6 + [per_chunk(N), per_chunk(C)] + [seq] * 4 + side.in_specs,
        out_specs=[seq] * 6 + side.out_specs,
        out_shape=[jax.ShapeDtypeStruct((H, S, N), dt) for dt in (BF16, F32, F32, BF16, F32, F32)] + side.out_shapes,
        scratch_shapes=[pltpu.VMEM((hb, N, N), F32)] + side.scratch,
        compiler_params=_cparams(("arbitrary", "arbitrary")))(r, lw, k, v, a, b, ck, ti, do, dr_add, dk_add, dv_add,
                                                              *side.srcs)


def loss_head(y, target, tm, name):
    S, D = y.shape

    def body(y_ref, t_ref, dy_ref, l_ref):
        err = y_ref[...] - t_ref[...]
        dy_ref[...] = err * (1.0 / D)

        @pl.when(pl.program_id(0) == 0)
        def _():
            l_ref[...] = jnp.zeros_like(l_ref)

        l_ref[...] += 0.5 * jnp.sum(jnp.mean(err * err, -1, keepdims=True), 0, keepdims=True)

    row = pl.BlockSpec((tm, D), lambda i: (i, 0))
    return pl.pallas_call(
        body, name=name, grid=(S // tm,), in_specs=[row, row],
        out_specs=[row, pl.BlockSpec((SUBLANES, LANES), lambda i: (0, 0))],
        out_shape=[jax.ShapeDtypeStruct((S, D), F32), jax.ShapeDtypeStruct((SUBLANES, LANES), F32)],
        compiler_params=_cparams(("arbitrary",)))(y, target)


def adamw(parts, w, m, v, name):
    R, C = w.shape
    n_parts = parts.shape[0]
    tr = R
    if R * C > ADAM_BLOCK_ELEMS:
        tr = _tile(R, [t for t in (512, 256, 128, 64, 32, 16) if t * C <= ADAM_BLOCK_ELEMS])
    c1 = 1.0 / (1.0 - ADAM_B1 ** ADAM_STEP)
    c2 = 1.0 / (1.0 - ADAM_B2 ** ADAM_STEP)

    def body(p_ref, w_ref, m_ref, v_ref, g_ref, d_ref, nm_ref, nv_ref):
        g = p_ref[0].astype(F32)
        for j in range(1, n_parts):
            g = g + p_ref[j].astype(F32)
        nm = ADAM_B1 * m_ref[...] + (1.0 - ADAM_B1) * g
        nv = ADAM_B2 * v_ref[...] + (1.0 - ADAM_B2) * (g * g)
        g_ref[...] = g
        nm_ref[...] = nm
        nv_ref[...] = nv
        d_ref[...] = -ADAM_LR * ((nm * c1) / (jnp.sqrt(nv * c2) + ADAM_EPS) + ADAM_WD * w_ref[...])

    blk = pl.BlockSpec((tr, C), lambda i: (i, 0))
    return pl.pallas_call(
        body, name=name, grid=(R // tr,), in_specs=[pl.BlockSpec((n_parts, tr, C), lambda i: (0, i, 0)), blk, blk, blk],
        out_specs=[blk] * 4, out_shape=[jax.ShapeDtypeStruct((R, C), F32)] * 4,
        compiler_params=_cparams(("parallel",)))(parts, w, m, v)


def _to_heads(a2d):
    s, d = a2d.shape
    return a2d.reshape(s, d // RWKV_HEAD, RWKV_HEAD).transpose(1, 0, 2)


def _from_heads(a3d):
    h, s, n = a3d.shape
    return a3d.transpose(1, 0, 2).reshape(s, h * n)


def _cols_joined(g):
    return g.transpose(1, 0, 2).reshape(g.shape[1], N_DEV * g.shape[2])


def _cols_split(w):
    return w.reshape(w.shape[0], N_DEV, w.shape[1] // N_DEV).transpose(1, 0, 2)


def _round_up(n, m):
    return -(-n // m) * m


def _step(a):
    x, mem, target = a['x'][0], a['mem'][0], a['loss_target'][0]
    S, D = x.shape
    M = mem.shape[0]
    DR = a['rwkv_w0'].shape[1]
    H = DR // RWKV_HEAD
    DC = a['conv_b'].shape[1]
    r_decay, r_iclr, r_gate = a['rwkv_w_up'].shape[1], a['rwkv_a_up'].shape[1], a['rwkv_g_up'].shape[1]
    n_lora = r_decay + r_iclr + r_gate
    lora_w = _round_up(n_lora, LANES)
    n_rwkv = 3 * DR + n_lora
    zr_w = 3 * DR + lora_w
    pad_cols = zr_w - n_rwkv
    conv_taps = a['conv_w'].shape[1]
    assert conv_taps - 1 <= CONV_HALO and S % SCAN_CHUNK == 0

    def shard(n, pre=''):
        w = a[pre + n]
        if n == 'conv_w':
            return jnp.pad(w.reshape(conv_taps, w.shape[-1]), ((0, CONV_HALO - conv_taps), (0, 0)))
        return w.reshape(w.shape[1:])

    Wg, W = {}, {}

    def gathers(names):
        return [(shard(n).astype(BF16), 'gather2') for n in names]

    def gathered(names, results):
        for n, g in zip(names, results):
            Wg[n] = g
            if n in ROW_SHARDED:
                W[n] = g.reshape(N_DEV * g.shape[1], g.shape[2])

    with_mm_z = ['mlp_w1']
    with_scan_fwd = ['proj_rwkv', 'proj_conv', 'w_out', 'xattn_wq']
    with_mm_w1 = ['mlp_w2']
    with_scan_bwd = ['xattn_wo', 'xattn_wq', 'xattn_wk', 'xattn_wv', 'w_out', 'proj_rwkv', 'proj_conv']
    gathered(['w_in'], [exchange(shard('w_in').astype(BF16), 'gather2', "gather_w_in")])
    lora_src = jnp.concatenate([shard('rwkv_w_up'), shard('rwkv_a_up'), shard('rwkv_g_up')], axis=0).astype(BF16)
    lora_all = exchange(lora_src, 'gather', "gather_lora")
    conv_w = _cols_joined(exchange(shard('conv_w'), 'gather', "gather_conv_w"))

    w_in = _cols_joined(Wg['w_in'])
    o_conv, o_gate = n_rwkv, n_rwkv + 2 * DC
    w_in_p = jnp.concatenate([w_in[:, o_gate:], w_in[:, :3 * DR], w_in[:, o_conv:o_gate], w_in[:, 3 * DR:n_rwkv],
                              jnp.zeros((D, pad_cols), BF16)], axis=1)
    c_gr, c_gc, c_rkv, c_u = 0, D, 2 * D, 2 * D + 3 * DR
    c_cg, c_lora = c_u + DC, c_u + 2 * DC
    assert c_rkv % lora_w == 0 and c_lora % lora_w == 0 and (3 * DR) % lora_w == 0 and c_u % DC == 0 and c_rkv % DR == 0
    shift_cols = (3 * DR // lora_w, c_rkv // lora_w, c_lora // lora_w)
    lora_full = _cols_joined(lora_all)
    w_lora = jnp.zeros((lora_w, 3 * DR), BF16)
    w_lora = w_lora.at[:r_decay, :DR].set(lora_full[:r_decay])
    w_lora = w_lora.at[r_decay:r_decay + r_iclr, DR:2 * DR].set(lora_full[r_decay:r_decay + r_iclr])
    w_lora = w_lora.at[r_decay + r_iclr:n_lora, 2 * DR:].set(lora_full[r_decay + r_iclr:])
    mu_p = jnp.pad(a['rwkv_shift_mix'], ((0, 0), (0, pad_cols)))
    x_bf = x.astype(BF16)

    hp = lambda n: a[n].reshape(H, 1, RWKV_HEAD)
    w0_h, a0_h, kk_h, ka_h, gng_h, gnb_h = (hp(n) for n in ('rwkv_w0', 'rwkv_a0', 'rwkv_k_k', 'rwkv_k_a',
                                                             'rwkv_gn_g', 'rwkv_gn_b'))
    rk_h = a['rwkv_r_k'].reshape(H, 1, RWKV_HEAD)
    ln_mem_g, ln_mem_b = a['ln_mem_g'].reshape(1, D), a['ln_mem_b'].reshape(1, D)

    tm_d = _tile(S, (128, 64, 32, 16, 8))
    rows_many = _Rows2D(S, _tile(S, (64, 32, 16, 8)))
    tm_a = _tile(S, (512, 256, 128, 64))
    rows = _Rows2D(S, tm_d)
    rows_mem = _Rows2D(M, _tile(M, (64, 32, 16, 8)))
    rows_ff = _Rows2D(S, _tile(S, (64, 32, 16, 8)))
    hb = _tile(H, (16, 8, 4, 2, 1))
    heads = _RowsHeads(H, S, _tile(H, (4, 2, 1)), _tile(S, (256, 128, 64)))
    tr_conv = _tile(S, (512, 256, 128, 64, 32))
    cb_conv = _tile(DC, (256, 128))
    fn_lora = make_fn_lora(r_decay, r_iclr, r_gate)
    lane_blk = lambda off, width: off // width

    mem_n, = ew_fwd(rows_mem, _as_bf16(fn_ln), [mem], [ln_mem_g, ln_mem_b], "ln_mem")
    z, *got = matmul(x_bf, w_in_p, 'nn', "mm_z", side=gathers(with_mm_z))
    gathered(with_mm_z, got)
    zs = token_shift_fwd(z, shift_cols, lora_w, zr_w, mu_p, tm_a, "token_shift")
    lora, = ew_fwd(rows, _as_bf16(fn_lora), [(zs, lora_w, lane_blk(3 * DR, lora_w))], [], "lora_act")
    up = matmul(lora, w_lora, 'nn', "mm_lora_up")
    r_h, k_h, v_h = (_to_heads(zs[:, i * DR:(i + 1) * DR]) for i in range(3))
    wl_h, al_h, g_h = (_to_heads(up[:, i * DR:(i + 1) * DR]) for i in range(3))
    pre_rows, pre_pars = [k_h, wl_h, al_h], [w0_h, a0_h, kk_h, ka_h]
    lw_h, k2_h, na_h, b_h = ew_fwd(heads, fn_rwkv_pre, pre_rows, pre_pars, "rwkv_pre")
    o_h, ckpt, tinvs, *got = scan_fwd(r_h, lw_h, k2_h, v_h, na_h, b_h, hb, "scan_fwd", side=gathers(with_scan_fwd))
    gathered(with_scan_fwd, got)
    post_rows, post_pars = [o_h, r_h, k2_h, v_h, g_h], [gng_h, gnb_h, rk_h]
    or_h, = ew_fwd(heads, _as_bf16(fn_rwkv_post), post_rows, post_pars, "rwkv_post")
    o_r = _from_heads(or_h)

    glu_rows = [(z, DC, lane_blk(c_u, DC)), (z, DC, lane_blk(c_cg, DC))]
    u, = ew_fwd(rows, fn_glu, glu_rows, [], "glu")
    yc = conv_fwd(u, conv_w, a['conv_b'], conv_taps, tr_conv, cb_conv, "conv")
    cln = [a['conv_ln_g'], a['conv_ln_b']]
    o_c, = ew_fwd(rows, _as_bf16(fn_ln_silu), [yc], cln, "conv_ln_silu")

    G = {}
    rows_split = lambda g: g.reshape(N_DEV, g.shape[0] // N_DEV, g.shape[1])

    Sx, recv = {}, {}

    def to_sibling(names):
        return [(G[n], 'scatter_sib') for n in names]

    def pair_sums(names, landed):
        for n, l in zip(names, landed):
            Sx[n] = pair_add(G[n], l, "pair_add_" + n)

    pr, *got = matmul(o_r, Wg['proj_rwkv'], 'nn', "mm_proj_rwkv", b_dev=True, side=gathers(['xattn_wk']))
    gathered(['xattn_wk'], got)
    pc = matmul(o_c, Wg['proj_conv'], 'nn', "mm_proj_conv", b_dev=True)
    merge_rows = [(z, D, lane_blk(c_gr, D)), (z, D, lane_blk(c_gc, D)), pr, pc]
    merged, = ew_fwd(rows_many, _as_bf16(fn_merge), merge_rows, [], "merge")
    t1, *got = matmul(merged, W['w_out'], 'nn', "mm_w_out", side=gathers(['xattn_wv']))
    gathered(['xattn_wv'], got)
    ln1 = [a['ln1_g'], a['ln1_b']]
    h1, h1_bf = ew_fwd(rows, fn_ln_res_both, [x, t1], ln1, "ln1")

    q, *got = matmul(h1_bf, W['xattn_wq'], 'nn', "mm_q", side=gathers(['xattn_wo']))
    gathered(['xattn_wo'], got)
    kx = matmul(mem_n, W['xattn_wk'], 'nn', "mm_k")
    vx = matmul(mem_n, W['xattn_wv'], 'nn', "mm_v")
    oa = attn_fwd(q, kx, vx, tm_a, "attn")
    ca = matmul(oa, W['xattn_wo'], 'nn', "mm_wo")
    ln2 = [a['ln2_g'], a['ln2_b']]
    h2, h2_bf = ew_fwd(rows, fn_ln_res_both, [h1, ca], ln2, "ln2")

    f1, *got = matmul(h2_bf, Wg['mlp_w1'], 'nn', "mm_w1", b_dev=True, side=gathers(with_mm_w1))
    gathered(with_mm_w1, got)
    act, = ew_fwd(rows_ff, _as_bf16(fn_relu2), [f1], [], "relu2")
    ff = matmul(act, W['mlp_w2'], 'nn', "mm_w2")
    ln3 = [a['ln3_g'], a['ln3_b']]
    h3, = ew_fwd(rows, fn_ln_res, [h2, ff], ln3, "ln3")
    dh3, loss_rows = loss_head(h3, target, tm_d, "loss")

    dh2a, dff, G['ln3_g'], G['ln3_b'] = ew_bwd(rows, fn_ln_res, [h2, ff], ln3, [dh3], (0, 1), (0, 1), "ln3_bwd",
                                               dr_dtypes=[F32, BF16])
    dact = matmul(dff, W['mlp_w2'], 'nt', "mm_w2_dx")
    G['mlp_w2'] = rows_split(matmul(act, dff, 'tn', "mm_w2_dw", out_dtype=BF16))
    df1, = ew_bwd(rows_ff, fn_relu2, [f1], [], [dact], (0,), (), "relu2_bwd", dr_dtypes=[BF16])
    dh2, *got = matmul(df1, Wg['mlp_w1'], 'nt', "mm_w1_dx", add=dh2a, b_dev=True, side=to_sibling(['mlp_w2']))
    pair_sums(['mlp_w2'], got)
    G['mlp_w1'], recv['mlp_w2'] = matmul(h2_bf, df1, 'tn', "mm_w1_dw", out_dtype=BF16, out_dev=True,
                                         side=[(Sx['mlp_w2'], 'scatter_chips')])

    dh1a, dca, G['ln2_g'], G['ln2_b'] = ew_bwd(rows, fn_ln_res, [h1, ca], ln2, [dh2], (0, 1), (0, 1), "ln2_bwd",
                                               dr_dtypes=[F32, BF16])
    doa, *got = matmul(dca, W['xattn_wo'], 'nt', "mm_wo_dx", side=to_sibling(['mlp_w1']))
    pair_sums(['mlp_w1'], got)
    G['xattn_wo'] = rows_split(matmul(oa, dca, 'tn', "mm_wo_dw", out_dtype=BF16))
    dq, dkx, dvx = attn_bwd(q, kx, vx, doa, tm_a, "attn_bwd")
    dh1, *got = matmul(dq, W['xattn_wq'], 'nt', "mm_q_dx", add=dh1a, side=to_sibling(['xattn_wo']))
    pair_sums(['xattn_wo'], got)
    G['xattn_wq'] = rows_split(matmul(h1_bf, dq, 'tn', "mm_q_dw", out_dtype=BF16))
    G['xattn_wk'] = rows_split(matmul(mem_n, dkx, 'tn', "mm_k_dw", out_dtype=BF16))
    G['xattn_wv'] = rows_split(matmul(mem_n, dvx, 'tn', "mm_v_dw", out_dtype=BF16))
    dmem_k = matmul(dkx, W['xattn_wk'], 'nt', "mm_k_dx")
    dmem_n = matmul(dvx, W['xattn_wv'], 'nt', "mm_v_dx", add=dmem_k)
    G['ln_mem_g'], G['ln_mem_b'] = ew_bwd(rows_mem, fn_ln, [mem], [ln_mem_g, ln_mem_b], [dmem_n], (), (0, 1),
                                          "ln_mem_bwd")

    dxa, dt1, G['ln1_g'], G['ln1_b'] = ew_bwd(rows, fn_ln_res, [x, t1], ln1, [dh1], (0, 1), (0, 1), "ln1_bwd",
                                              dr_dtypes=[F32, BF16])
    xattn_qkv = ['xattn_wq', 'xattn_wk', 'xattn_wv']
    dmerged, *got = matmul(dt1, W['w_out'], 'nt', "mm_w_out_dx", side=to_sibling(xattn_qkv))
    pair_sums(xattn_qkv, got)
    G['w_out'] = rows_split(matmul(merged, dt1, 'tn', "mm_w_out_dw", out_dtype=BF16))
    dzgr, dzgc, dpr, dpc = ew_bwd(rows_many, fn_merge, merge_rows, [], [dmerged], (0, 1, 2, 3), (), "merge_bwd",
                                  dr_dtypes=[BF16] * 4)
    do_r, *got = matmul(dpr, Wg['proj_rwkv'], 'nt', "mm_proj_rwkv_dx", b_dev=True, side=to_sibling(['w_out']))
    pair_sums(['w_out'], got)
    G['proj_rwkv'] = matmul(o_r, dpr, 'tn', "mm_proj_rwkv_dw", out_dtype=BF16, out_dev=True)
    do_c, *got = matmul(dpc, Wg['proj_conv'], 'nt', "mm_proj_conv_dx", b_dev=True, side=to_sibling(['proj_rwkv']))
    pair_sums(['proj_rwkv'], got)
    G['proj_conv'] = matmul(o_c, dpc, 'tn', "mm_proj_conv_dw", out_dtype=BF16, out_dev=True)
    pair_sums(['proj_conv'], [exchange(G['proj_conv'], 'scatter_sib', "scatter_sib_proj_conv")])

    dyc, G['conv_ln_g'], G['conv_ln_b'] = ew_bwd(rows, fn_ln_silu, [yc], cln, [do_c], (0,), (0, 1), "conv_ln_silu_bwd")
    du, dconv_w, G['conv_b'] = conv_bwd(u, dyc, conv_w, conv_taps, tr_conv, cb_conv, "conv_bwd")
    dzu, dzcg = ew_bwd(rows, fn_glu, glu_rows, [], [du], (0, 1), (), "glu_bwd", dr_dtypes=[BF16] * 2)

    do_r_h = _to_heads(do_r)
    do_h, dr1_h, dk2a_h, dv1_h, dg_h, dgng, dgnb, drk = ew_bwd(
        heads, fn_rwkv_post, post_rows, post_pars, [do_r_h], (0, 1, 2, 3, 4), (0, 1, 2), "rwkv_post_bwd",
        dr_dtypes=[F32, F32, F32, F32, BF16])
    dr_h, dlw_h, dk2_h, dv_h, dna_h, db_h, *got = scan_bwd(
        r_h, lw_h, k2_h, v_h, na_h, b_h, ckpt, tinvs, do_h, dr1_h, dk2a_h, dv1_h, hb, "scan_bwd",
        side=[(Sx[n], 'scatter_chips') for n in with_scan_bwd])
    recv.update(zip(with_scan_bwd, got))
    dk_h, dwl_h, dal_h, dw0, da0, dkk, dka = ew_bwd(
        heads, fn_rwkv_pre, pre_rows, pre_pars, [dlw_h, dk2_h, dna_h, db_h], (0, 1, 2), (0, 1, 2, 3), "rwkv_pre_bwd",
        dr_dtypes=[BF16, BF16, BF16])
    G['rwkv_w0'], G['rwkv_a0'], G['rwkv_k_k'], G['rwkv_k_a'] = (t.reshape(1, DR) for t in (dw0, da0, dkk, dka))
    G['rwkv_gn_g'], G['rwkv_gn_b'] = dgng.reshape(1, DR), dgnb.reshape(1, DR)
    G['rwkv_r_k'] = drk.reshape(1, H, RWKV_HEAD)
    dup = jnp.concatenate([_from_heads(dwl_h), _from_heads(dal_h), _from_heads(dg_h)], axis=1)
    dlora = matmul(dup, w_lora, 'nt', "mm_lora_up_dx")
    dw_lora = matmul(lora, dup, 'tn', "mm_lora_up_dw", out_dtype=BF16)
    d_lora_stack = jnp.concatenate([dw_lora[:r_decay, :DR], dw_lora[r_decay:r_decay + r_iclr, DR:2 * DR],
                                    dw_lora[r_decay + r_iclr:n_lora, 2 * DR:]], axis=0)
    dzs_lora, = ew_bwd(rows, fn_lora, [(zs, lora_w, lane_blk(3 * DR, lora_w))], [], [dlora], (0,), (), "lora_act_bwd",
                       dr_dtypes=[BF16])
    dzs = jnp.concatenate([_from_heads(dr_h), _from_heads(dk_h), _from_heads(dv_h), dzs_lora], axis=1)
    dzr, dmu = token_shift_bwd(z, dzs, shift_cols, lora_w, zr_w, mu_p, tm_a, "token_shift_bwd")
    G['rwkv_shift_mix'] = dmu[:, :n_rwkv]
    dz = jnp.concatenate([dzgr, dzgc, dzr[:, :3 * DR], dzu, dzcg, dzr[:, 3 * DR:]], axis=1)
    dw_in_p, recv['mlp_w1'] = matmul(x_bf, dz, 'tn', "mm_z_dw", out_dtype=BF16, side=[(Sx['mlp_w1'], 'scatter_chips')])
    G['w_in'] = _cols_split(jnp.concatenate([dw_in_p[:, c_rkv:c_u], dw_in_p[:, c_lora:c_lora + n_lora],
                                             dw_in_p[:, c_u:c_lora], dw_in_p[:, :c_rkv]], axis=1))
    pair_sums(['w_in'], [exchange(G['w_in'], 'scatter_sib', "scatter_sib_w_in")])
    grad_x, recv['w_in'] = matmul(dz, w_in_p, 'nt', "mm_z_dx", add=dxa, side=[(Sx['w_in'], 'scatter_chips')])

    out = {}

    def update(n, parts, w, m, v):
        res = adamw(parts, w, m, v, "adamw_" + n)
        out[n] = [t[:conv_taps].reshape(a[n].shape) if n == 'conv_w' else t.reshape(a[n].shape) for t in res]

    for n in ['w_in', 'mlp_w2', 'mlp_w1'] + with_scan_bwd:
        update(n, recv[n], shard(n), shard(n, 'm_'), shard(n, 'v_'))
    recv_lora = exchange(_cols_split(d_lora_stack), 'scatter', "scatter_lora")
    lo = 0
    for n, r_ in (('rwkv_w_up', r_decay), ('rwkv_a_up', r_iclr), ('rwkv_g_up', r_gate)):
        update(n, recv_lora[:, lo:lo + r_], shard(n), shard(n, 'm_'), shard(n, 'v_'))
        lo += r_
    recv_conv = exchange(_cols_split(dconv_w.astype(BF16)), 'scatter', "scatter_conv_w")
    update('conv_w', recv_conv, shard('conv_w'), shard('conv_w', 'm_'), shard('conv_w', 'v_'))

    small_sizes = [a[n].size for n in SMALL]
    n_small = sum(small_sizes) + 1
    n_pack = _round_up(n_small, SUBLANES * LANES)
    pack = lambda get, last: jnp.pad(jnp.concatenate([get(n).reshape(-1) for n in SMALL] + [last]),
                                     (0, n_pack - n_small)).reshape(n_pack // LANES, LANES)
    zero1 = jnp.zeros((1,), F32)
    g_pack = pack(lambda n: G[n], loss_rows[0, :1])
    parts_small = exchange(g_pack, 'gather', "gather_small")
    res = adamw(parts_small, pack(lambda n: a[n], zero1), pack(lambda n: a['m_' + n], zero1),
                pack(lambda n: a['v_' + n], zero1), "adamw_small")
    res = [t.reshape(-1) for t in res]
    o_ = 0
    for n, sz in zip(SMALL, small_sizes):
        out[n] = [t[o_:o_ + sz].reshape(a[n].shape) for t in res]
        o_ += sz
    loss = res[0][o_]

    return (loss, grad_x[None], *[out[n][0] for n in WEIGHTS], *[out[n][1] for n in WEIGHTS],
            *[out[n][2] for n in WEIGHTS], *[out[n][3] for n in WEIGHTS])


def kernel(x, mem, w_in, rwkv_shift_mix, rwkv_w0, rwkv_w_up, rwkv_a0, rwkv_a_up, rwkv_g_up, rwkv_k_k, rwkv_k_a, rwkv_r_k, rwkv_gn_g, rwkv_gn_b, conv_w, conv_b, conv_ln_g, conv_ln_b, proj_rwkv, proj_conv, w_out, ln1_g, ln1_b, ln_mem_g, ln_mem_b, xattn_wq, xattn_wk, xattn_wv, xattn_wo, ln2_g, ln2_b, mlp_w1, mlp_w2, ln3_g, ln3_b, loss_target, m_w_in, m_rwkv_shift_mix, m_rwkv_w0, m_rwkv_w_up, m_rwkv_a0, m_rwkv_a_up, m_rwkv_g_up, m_rwkv_k_k, m_rwkv_k_a, m_rwkv_r_k, m_rwkv_gn_g, m_rwkv_gn_b, m_conv_w, m_conv_b, m_conv_ln_g, m_conv_ln_b, m_proj_rwkv, m_proj_conv, m_w_out, m_ln1_g, m_ln1_b, m_ln_mem_g, m_ln_mem_b, m_xattn_wq, m_xattn_wk, m_xattn_wv, m_xattn_wo, m_ln2_g, m_ln2_b, m_mlp_w1, m_mlp_w2, m_ln3_g, m_ln3_b, v_w_in, v_rwkv_shift_mix, v_rwkv_w0, v_rwkv_w_up, v_rwkv_a0, v_rwkv_a_up, v_rwkv_g_up, v_rwkv_k_k, v_rwkv_k_a, v_rwkv_r_k, v_rwkv_gn_g, v_rwkv_gn_b, v_conv_w, v_conv_b, v_conv_ln_g, v_conv_ln_b, v_proj_rwkv, v_proj_conv, v_w_out, v_ln1_g, v_ln1_b, v_ln_mem_g, v_ln_mem_b, v_xattn_wq, v_xattn_wk, v_xattn_wv, v_xattn_wo, v_ln2_g, v_ln2_b, v_mlp_w1, v_mlp_w2, v_ln3_g, v_ln3_b):
    return _step(dict(locals()))
```

```python
import functools
import math

import jax
import jax.numpy as jnp
from jax import lax
from jax.experimental import pallas as pl
from jax.experimental.pallas import tpu as pltpu

F32 = jnp.float32
BF16 = jnp.bfloat16

N_DEV = 8
RWKV_HEAD = 64
SCAN_CHUNK = 64
XATTN_HEADS = 4
CONV_HALO = 32
LN_EPS = 1e-5
GN_EPS = 64e-5
ALPHA = float(2.0 ** 0.25)
ADAM_LR, ADAM_B1, ADAM_B2, ADAM_EPS, ADAM_WD, ADAM_STEP = 0.001, 0.9, 0.999, 1e-08, 0.01, 10
LANES = 128
SUBLANES = 8
VMEM_LIMIT = 56 * 1024 * 1024
ADAM_BLOCK_ELEMS = 256 * 1024
PAIR_BLOCK_ELEMS = 1024 * 1024

WEIGHTS = ['w_in', 'rwkv_shift_mix', 'rwkv_w0', 'rwkv_w_up', 'rwkv_a0', 'rwkv_a_up', 'rwkv_g_up', 'rwkv_k_k',
           'rwkv_k_a', 'rwkv_r_k', 'rwkv_gn_g', 'rwkv_gn_b', 'conv_w', 'conv_b', 'conv_ln_g', 'conv_ln_b',
           'proj_rwkv', 'proj_conv', 'w_out', 'ln1_g', 'ln1_b', 'ln_mem_g', 'ln_mem_b', 'xattn_wq', 'xattn_wk',
           'xattn_wv', 'xattn_wo', 'ln2_g', 'ln2_b', 'mlp_w1', 'mlp_w2', 'ln3_g', 'ln3_b']
COL_SHARDED = ['w_in', 'rwkv_w_up', 'rwkv_a_up', 'rwkv_g_up', 'conv_w', 'proj_rwkv', 'proj_conv', 'mlp_w1']
ROW_SHARDED = ['w_out', 'xattn_wq', 'xattn_wk', 'xattn_wv', 'xattn_wo', 'mlp_w2']
BIG = ['w_in', 'rwkv_w_up', 'rwkv_a_up', 'rwkv_g_up', 'conv_w', 'proj_rwkv', 'proj_conv', 'w_out', 'xattn_wq',
       'xattn_wk', 'xattn_wv', 'xattn_wo', 'mlp_w1', 'mlp_w2']
SMALL = [w for w in WEIGHTS if w not in BIG]


def _cparams(dims):
    return pltpu.CompilerParams(dimension_semantics=dims, vmem_limit_bytes=VMEM_LIMIT)


def _tile(n, cands):
    for c in cands:
        if n % c == 0:
            return c
    return n


N_CHIP = N_DEV // 2
SEMS_PER_EXCHANGE = N_DEV + 2
CHIP_XORS = (2, 4, 6)


def _exchange_shape(src, kind):
    return {'gather': (N_DEV,) + src.shape, 'gather2': (N_DEV,) + src.shape, 'scatter': src.shape,
            'scatter_sib': (N_CHIP,) + src.shape[1:], 'scatter_chips': src.shape}[kind]


def _exchange_copies(e, src_ref, out_ref, kind, send_sems, recv_sems, local_sems):
    x, y, c = lax.axis_index("x"), lax.axis_index("y"), lax.axis_index("c")
    me, chip = 4 * x + 2 * y + c, 2 * x + y
    sibling = (x, y, 1 - c)
    base = e * SEMS_PER_EXCHANGE

    def remote(src, dst, idx, dev):
        return pltpu.make_async_remote_copy(src_ref=src, dst_ref=dst, send_sem=send_sems.at[base + idx],
                                            recv_sem=recv_sems.at[base + idx], device_id=dev,
                                            device_id_type=pl.DeviceIdType.MESH)

    def peer(k):
        return x ^ ((k >> 2) & 1), y ^ ((k >> 1) & 1), c ^ (k & 1)

    first, second = [], []
    if kind in ('gather', 'gather2'):
        first.append(pltpu.make_async_copy(src_ref, out_ref.at[me], local_sems.at[e]))
        for k in (range(1, N_DEV) if kind == 'gather' else (1,) + CHIP_XORS):
            first.append(remote(src_ref, out_ref.at[me], k - 1, peer(k)))
        if kind == 'gather2':
            for i, k in enumerate(CHIP_XORS):
                second.append(remote(out_ref.at[me ^ k], out_ref.at[me ^ k], N_DEV - 1 + i, sibling))
    elif kind == 'scatter':
        first.append(pltpu.make_async_copy(src_ref.at[me], out_ref.at[me], local_sems.at[e]))
        for k in range(1, N_DEV):
            px, py, pc = peer(k)
            first.append(remote(src_ref.at[4 * px + 2 * py + pc], out_ref.at[me], k - 1, (px, py, pc)))
    elif kind == 'scatter_sib':
        for q in range(N_CHIP):
            first.append(remote(src_ref.at[2 * q + 1 - c], out_ref.at[q], q, sibling))
    else:
        assert kind == 'scatter_chips', kind
        first.append(pltpu.make_async_copy(src_ref.at[chip], out_ref.at[chip], local_sems.at[e]))
        for k in CHIP_XORS:
            first.append(remote(src_ref.at[chip ^ (k >> 1)], out_ref.at[chip], k - 1, peer(k)))
    return first, second


class _Side:
    def __init__(self, items):
        self.items = list(items or [])
        self.n = len(self.items)
        any_spec = pl.BlockSpec(memory_space=pl.ANY)
        self.srcs = [s for s, _ in self.items]
        self.in_specs = [any_spec] * self.n
        self.out_specs = [any_spec] * self.n
        self.out_shapes = [jax.ShapeDtypeStruct(_exchange_shape(s, kind), s.dtype) for s, kind in self.items]
        self.scratch = [pltpu.SemaphoreType.DMA((self.n * SEMS_PER_EXCHANGE,)),
                        pltpu.SemaphoreType.DMA((self.n * SEMS_PER_EXCHANGE,)),
                        pltpu.SemaphoreType.DMA((self.n,))] if self.n else []

    def _copies(self, src_refs, out_refs, sems):
        both = [_exchange_copies(e, src_refs[e], out_refs[e], kind, *sems) for e, (_, kind) in enumerate(self.items)]
        return [cp for f, _ in both for cp in f], [cp for _, s in both for cp in s]

    def start(self, src_refs, out_refs, sems):
        for cp in self._copies(src_refs, out_refs, sems)[0]:
            cp.start()

    def finish(self, src_refs, out_refs, sems):
        first, second = self._copies(src_refs, out_refs, sems)
        for cp in first:
            cp.wait()
        for cp in second:
            cp.start()
        for cp in second:
            cp.wait()

    def run(self, first, last, src_refs, out_refs, sems):
        if not self.n:
            return lambda: None
        pl.when(first)(lambda: self.start(src_refs, out_refs, sems))
        return lambda: pl.when(last)(lambda: self.finish(src_refs, out_refs, sems))


def exchanges(items, name):
    side = _Side(items)

    def body(*refs):
        srcs, outs, sems = refs[:side.n], refs[side.n:2 * side.n], refs[2 * side.n:]
        side.start(srcs, outs, sems)
        side.finish(srcs, outs, sems)

    return pl.pallas_call(body, name=name, in_specs=side.in_specs, out_specs=side.out_specs,
                          out_shape=side.out_shapes, scratch_shapes=side.scratch)(*side.srcs)


def exchange(src, kind, name):
    return exchanges([(src, kind)], name)[0]


def pair_add(parts, landed, name):
    _, R, C = parts.shape
    tr = R
    if R * C > PAIR_BLOCK_ELEMS:
        tr = _tile(R, [t for t in (2048, 1024, 512, 256, 128, 64, 32, 16) if t * C <= PAIR_BLOCK_ELEMS])

    def body(core_ref, p_ref, l_ref, o_ref):
        o_ref[...] = (p_ref[...].astype(F32) + l_ref[...].astype(F32)).astype(BF16)

    blk = pl.BlockSpec((None, tr, C), lambda q, i, core_ref: (q, i, 0))
    mine = pl.BlockSpec((None, None, tr, C), lambda q, i, core_ref: (q, core_ref[0], i, 0))
    return pl.pallas_call(
        body, name=name,
        grid_spec=pltpu.PrefetchScalarGridSpec(num_scalar_prefetch=1, grid=(N_CHIP, R // tr), in_specs=[mine, blk],
                                               out_specs=blk),
        out_shape=jax.ShapeDtypeStruct((N_CHIP, R, C), BF16),
        compiler_params=_cparams(("parallel", "parallel")))(
            lax.axis_index("c").astype(jnp.int32).reshape(1), parts.reshape(N_CHIP, 2, R, C), landed)


def matmul(a, b, mode, name, add=None, out_dtype=F32, b_dev=False, out_dev=False, side=None):
    side = _Side(side)
    if b_dev:
        assert mode in ('nn', 'nt') and b.shape[0] == N_DEV
        b_rows, b_cols = b.shape[1], N_DEV * b.shape[2]
    else:
        b_rows, b_cols = b.shape
    if mode == 'nn':
        (M, K), (K2, N) = a.shape, (b_rows, b_cols)
    elif mode == 'nt':
        (M, K), (N, K2) = a.shape, (b_rows, b_cols)
    else:
        (K, M), (K2, N) = a.shape, (b_rows, b_cols)
    assert K == K2, (a.shape, b.shape, mode)
    n_unit = N // N_DEV if (out_dev or (b_dev and mode == 'nn')) else N
    k_unit = K // N_DEV if (b_dev and mode == 'nt') else K
    tn = _tile(n_unit, (1024, 512, 256, 128))
    tm = _tile(M, (1024, 512, 256, 128) if tn >= 1024 else (2048, 1024, 512, 256, 128))
    tk = _tile(k_unit, (2048, 1024, 512, 256, 128))
    nk = K // tk
    nb, kb = n_unit // tn, k_unit // tk
    dims = {'nn': ((1,), (0,)), 'nt': ((1,), (1,)), 'tn': ((0,), (0,))}[mode]

    n_in = 2 + (add is not None)
    grid = (M // tm, N // tn, nk)

    def body(*refs):
        a_ref, b_ref = refs[:2]
        add_ref = refs[2] if add is not None else None
        side_src = refs[n_in:n_in + side.n]
        o_ref = refs[n_in + side.n]
        side_out = refs[n_in + side.n + 1:n_in + 2 * side.n + 1]
        acc_ref = refs[n_in + 2 * side.n + 1]
        sems = refs[n_in + 2 * side.n + 2:]
        i, j, k = pl.program_id(0), pl.program_id(1), pl.program_id(2)
        finish = side.run((i == 0) & (j == 0) & (k == 0), (i == grid[0] - 1) & (j == grid[1] - 1) & (k == nk - 1),
                          side_src, side_out, sems)

        @pl.when(k == 0)
        def _():
            acc_ref[...] = jnp.zeros_like(acc_ref)

        acc_ref[...] += lax.dot_general(a_ref[...].astype(BF16), b_ref[...].astype(BF16), (dims, ((), ())),
                                        preferred_element_type=F32)

        @pl.when(k == nk - 1)
        def _():
            r = acc_ref[...]
            if add is not None:
                r = r + add_ref[...]
            o_ref[...] = r.astype(out_dtype)

        finish()

    if mode == 'nn':
        a_spec = pl.BlockSpec((tm, tk), lambda i, j, k: (i, k))
        b_spec = (pl.BlockSpec((None, tk, tn), lambda i, j, k: (j // nb, k, j % nb)) if b_dev
                  else pl.BlockSpec((tk, tn), lambda i, j, k: (k, j)))
    elif mode == 'nt':
        a_spec = pl.BlockSpec((tm, tk), lambda i, j, k: (i, k))
        b_spec = (pl.BlockSpec((None, tn, tk), lambda i, j, k: (k // kb, j, k % kb)) if b_dev
                  else pl.BlockSpec((tn, tk), lambda i, j, k: (j, k)))
    else:
        a_spec = pl.BlockSpec((tk, tm), lambda i, j, k: (k, i))
        b_spec = pl.BlockSpec((tk, tn), lambda i, j, k: (k, j))
    add_spec = pl.BlockSpec((tm, tn), lambda i, j, k: (i, j))
    if out_dev:
        o_spec = pl.BlockSpec((None, tm, tn), lambda i, j, k: (j // nb, i, j % nb))
        o_shape = (N_DEV, M, N // N_DEV)
    else:
        o_spec, o_shape = add_spec, (M, N)
    in_specs = [a_spec, b_spec] + ([add_spec] if add is not None else [])
    ops = (a, b) + ((add,) if add is not None else ())
    sem = ("arbitrary",) * 3 if side.n else ("parallel", "parallel", "arbitrary")
    res = pl.pallas_call(
        body, name=name, grid=grid, in_specs=in_specs + side.in_specs, out_specs=[o_spec] + side.out_specs,
        out_shape=[jax.ShapeDtypeStruct(o_shape, out_dtype)] + side.out_shapes,
        scratch_shapes=[pltpu.VMEM((tm, tn), F32)] + side.scratch,
        compiler_params=_cparams(sem))(*ops, *side.srcs)
    return tuple(res) if side.n else res[0]


class _Rows2D:
    def __init__(self, n_rows, tm):
        self.n, self.tm = n_rows, tm
        self.grid = (1, n_rows // tm)

    def row(self, e):
        if isinstance(e, tuple):
            arr, width, cb = e
            return arr, (self.tm, width), pl.BlockSpec((self.tm, width), lambda g, i, cb=cb: (i, cb))
        return e, (self.tm, e.shape[1]), pl.BlockSpec((self.tm, e.shape[1]), lambda g, i: (i, 0))

    def par(self, p):
        return pl.BlockSpec(p.shape, lambda g, i: (0,) * p.ndim)

    def out(self, blk):
        return (self.n, blk[1]), pl.BlockSpec((self.tm, blk[1]), lambda g, i: (i, 0))


class _RowsHeads:
    def __init__(self, n_heads, n_rows, hb, ts):
        self.h, self.n, self.hb, self.ts = n_heads, n_rows, hb, ts
        self.grid = (n_heads // hb, n_rows // ts)

    def row(self, e):
        blk = (self.hb, self.ts, e.shape[2])
        return e, blk, pl.BlockSpec(blk, lambda g, i: (g, i, 0))

    def par(self, p):
        return pl.BlockSpec((self.hb, 1, p.shape[2]), lambda g, i: (g, 0, 0))

    def out(self, blk):
        return (self.h, self.n, blk[2]), pl.BlockSpec(blk, lambda g, i: (g, i, 0))


def _par_block(lay, p):
    return lay.par(p).block_shape


def ew_fwd(lay, fn, rows, params, name):
    rr = [lay.row(e) for e in rows]
    arrs = [r[0] for r in rr]
    blk_avals = [jax.ShapeDtypeStruct(r[1], r[0].dtype) for r in rr]
    par_avals = [jax.ShapeDtypeStruct(_par_block(lay, p), p.dtype) for p in params]
    outs = jax.eval_shape(fn, *blk_avals, *par_avals)
    out_full = [lay.out(o.shape) for o in outs]
    nr, npar = len(rows), len(params)

    def body(*refs):
        vals = [r[...] for r in refs[:nr + npar]]
        res = fn(*vals)
        for ref, v in zip(refs[nr + npar:], res):
            ref[...] = v.astype(ref.dtype)

    return pl.pallas_call(
        body, name=name, grid=lay.grid,
        in_specs=[r[2] for r in rr] + [lay.par(p) for p in params],
        out_specs=[o[1] for o in out_full],
        out_shape=[jax.ShapeDtypeStruct(o[0], a.dtype) for o, a in zip(out_full, outs)],
        compiler_params=_cparams(("parallel", "parallel")))(*arrs, *params)


def ew_bwd(lay, fn, rows, params, cots, wrt_rows, wrt_pars, name, dr_dtypes=None):
    rr = [lay.row(e) for e in rows]
    cc = [lay.row(e) for e in cots]
    nr, npar, nc = len(rows), len(params), len(cots)
    n_dr = len(wrt_rows)
    dr_full = [lay.out(rr[i][1]) for i in wrt_rows]
    dr_dtypes = dr_dtypes or [F32] * n_dr

    def body(*refs):
        rv = [r[...] for r in refs[:nr]]
        pv = [r[...] for r in refs[nr:nr + npar]]
        cv = tuple(r[...] for r in refs[nr + npar:nr + npar + nc])
        outs = refs[nr + npar + nc:]

        def f(*wrt):
            r2, p2 = list(rv), list(pv)
            for idx, v in zip(wrt_rows, wrt[:n_dr]):
                r2[idx] = v
            for idx, v in zip(wrt_pars, wrt[n_dr:]):
                p2[idx] = v
            return fn(*r2, *p2)

        _, vjp = jax.vjp(f, *[rv[i] for i in wrt_rows], *[pv[i] for i in wrt_pars])
        g = vjp(cv)
        for ref, v in zip(outs[:n_dr], g[:n_dr]):
            ref[...] = v.astype(ref.dtype)
        if wrt_pars:
            @pl.when(pl.program_id(1) == 0)
            def _():
                for ref in outs[n_dr:]:
                    ref[...] = jnp.zeros_like(ref)

            for ref, v in zip(outs[n_dr:], g[n_dr:]):
                ref[...] += v

    return pl.pallas_call(
        body, name=name, grid=lay.grid,
        in_specs=[r[2] for r in rr] + [lay.par(p) for p in params] + [c[2] for c in cc],
        out_specs=[o[1] for o in dr_full] + [lay.par(params[i]) for i in wrt_pars],
        out_shape=[jax.ShapeDtypeStruct(o[0], dt) for o, dt in zip(dr_full, dr_dtypes)]
        + [jax.ShapeDtypeStruct(params[i].shape, F32) for i in wrt_pars],
        compiler_params=_cparams(("parallel", "arbitrary")))(
            *[r[0] for r in rr], *params, *[c[0] for c in cc])


def _sigmoid(x):
    return 1.0 / (1.0 + jnp.exp(-x))


def _softplus(x):
    return jnp.maximum(x, 0.0) + jnp.log(1.0 + jnp.exp(-jnp.abs(x)))


def _layer_norm(x, g, b, eps):
    mu = jnp.mean(x, -1, keepdims=True)
    xc = x - mu
    var = jnp.mean(xc * xc, -1, keepdims=True)
    return xc * lax.rsqrt(var + eps) * g + b


def _as_bf16(fn):
    return lambda *args: tuple(o.astype(BF16) for o in fn(*args))


def fn_ln(x, g, b):
    return (_layer_norm(x, g, b, LN_EPS),)


def fn_ln_res(h, t, g, b):
    return (_layer_norm(ALPHA * h + t, g, b, LN_EPS),)


def fn_ln_res_both(h, t, g, b):
    y, = fn_ln_res(h, t, g, b)
    return y, y.astype(BF16)


def make_fn_lora(r_decay, r_iclr, r_gate):
    def fn(z):
        lane = lax.broadcasted_iota(jnp.int32, z.shape, 1)
        out = jnp.where(lane < r_decay, jnp.tanh(z), z)
        out = jnp.where(lane >= r_decay + r_iclr, _sigmoid(z), out)
        return (jnp.where(lane < r_decay + r_iclr + r_gate, out, 0.0),)
    return fn


def fn_rwkv_pre(k, wl, al, w0, a0, k_k, k_a):
    w = -_softplus(-(w0 + wl)) - 0.5
    lw = -jnp.exp(w)
    a = _sigmoid(a0 + al)
    kk = k * k_k
    kk = kk / jnp.maximum(jnp.sqrt(jnp.sum(kk * kk, -1, keepdims=True)), 1e-12)
    k2 = k * (1.0 + (a - 1.0) * k_a)
    return lw, k2, -kk, kk * a


def fn_rwkv_post(o, r, k2, v, g, gn_g, gn_b, r_k):
    mu = jnp.mean(o, -1, keepdims=True)
    oc = o - mu
    var = jnp.mean(oc * oc, -1, keepdims=True)
    y = oc * lax.rsqrt(var + GN_EPS) * gn_g + gn_b
    y = y + jnp.sum(r * k2 * r_k, -1, keepdims=True) * v
    return (y * g,)


def fn_glu(zu, zg):
    return (zu * _sigmoid(zg),)


def fn_ln_silu(y, g, b):
    n = _layer_norm(y, g, b, LN_EPS)
    return (n * _sigmoid(n),)


def fn_merge(zgr, zgc, pr, pc):
    return (_sigmoid(zgr) * pr + _sigmoid(zgc) * pc,)


def fn_relu2(f):
    r = jnp.maximum(f, 0.0)
    return (r * r,)


def _dot_nn(a, b):
    return lax.dot_general(a.astype(BF16), b.astype(BF16), (((1,), (0,)), ((), ())), preferred_element_type=F32)


def _dot_nt(a, b):
    return lax.dot_general(a.astype(BF16), b.astype(BF16), (((1,), (1,)), ((), ())), preferred_element_type=F32)


def _dot_tn(a, b):
    return lax.dot_general(a.astype(BF16), b.astype(BF16), (((0,), (0,)), ((), ())), preferred_element_type=F32)


def _softmax_rows(s):
    s = s - jnp.max(s, -1, keepdims=True)
    e = jnp.exp(s)
    return e / jnp.sum(e, -1, keepdims=True)


def attn_fwd(q, kx, vx, tm, name):
    S, D = q.shape
    M = kx.shape[0]
    dh = D // XATTN_HEADS
    scale = dh ** -0.5

    def body(q_ref, k_ref, v_ref, o_ref):
        p = _softmax_rows(_dot_nt(q_ref[...], k_ref[...]) * scale)
        o_ref[...] = _dot_nn(p, v_ref[...]).astype(BF16)

    row = pl.BlockSpec((tm, dh), lambda h, i: (i, h))
    kv = pl.BlockSpec((M, dh), lambda h, i: (0, h))
    return pl.pallas_call(body, name=name, grid=(XATTN_HEADS, S // tm), in_specs=[row, kv, kv], out_specs=row,
                          out_shape=jax.ShapeDtypeStruct((S, D), BF16),
                          compiler_params=_cparams(("parallel", "parallel")))(q, kx, vx)


def attn_bwd(q, kx, vx, do, tm, name):
    S, D = q.shape
    M = kx.shape[0]
    dh = D // XATTN_HEADS
    scale = dh ** -0.5

    def body(q_ref, k_ref, v_ref, do_ref, dq_ref, dk_ref, dv_ref):
        qb, kb, dob = q_ref[...], k_ref[...], do_ref[...]
        p = _softmax_rows(_dot_nt(qb, kb) * scale)
        dp = _dot_nt(dob, v_ref[...])
        ds = p * (dp - jnp.sum(dp * p, -1, keepdims=True)) * scale
        dq_ref[...] = _dot_nn(ds, kb).astype(BF16)

        @pl.when(pl.program_id(1) == 0)
        def _():
            dk_ref[...] = jnp.zeros_like(dk_ref)
            dv_ref[...] = jnp.zeros_like(dv_ref)

        dk_ref[...] += _dot_tn(ds, qb)
        dv_ref[...] += _dot_tn(p, dob)

    row = pl.BlockSpec((tm, dh), lambda h, i: (i, h))
    kv = pl.BlockSpec((M, dh), lambda h, i: (0, h))
    return pl.pallas_call(
        body, name=name, grid=(XATTN_HEADS, S // tm), in_specs=[row, kv, kv, row], out_specs=[row, kv, kv],
        out_shape=[jax.ShapeDtypeStruct((S, D), BF16), jax.ShapeDtypeStruct((M, D), F32), jax.ShapeDtypeStruct((M, D), F32)],
        compiler_params=_cparams(("parallel", "arbitrary")))(q, kx, vx, do)


def _shift_down(blk, halo_last_row, first_block):
    rolled = pltpu.roll(blk, 1, 0)
    row = lax.broadcasted_iota(jnp.int32, blk.shape, 0)
    top = jnp.where(first_block, 0.0, halo_last_row)
    return jnp.where(row == 0, top, rolled)


def _shift_up(blk, halo_first_row, last_block):
    n = blk.shape[0]
    rolled = pltpu.roll(blk, n - 1, 0)
    row = lax.broadcasted_iota(jnp.int32, blk.shape, 0)
    bot = jnp.where(last_block, 0.0, halo_first_row)
    return jnp.where(row == n - 1, bot, rolled)


def _zcol(j, cols):
    n_first, first, second = cols
    return jnp.where(j < n_first, first + j, second + j - n_first)


def token_shift_fwd(z, cols, cw, width, mu, tm, name):
    S = z.shape[0]
    hb = tm // SUBLANES

    def body(z_ref, halo_ref, mu_ref, o_ref):
        zb = z_ref[...]
        prev = _shift_down(zb, halo_ref[SUBLANES - 1:SUBLANES, :], pl.program_id(1) == 0)
        o_ref[...] = zb + (prev - zb) * mu_ref[...]

    return pl.pallas_call(
        body, name=name, grid=(width // cw, S // tm),
        in_specs=[pl.BlockSpec((tm, cw), lambda j, i: (i, _zcol(j, cols))),
                  pl.BlockSpec((SUBLANES, cw), lambda j, i: (jnp.maximum(i * hb - 1, 0), _zcol(j, cols))),
                  pl.BlockSpec((1, cw), lambda j, i: (0, j))],
        out_specs=pl.BlockSpec((tm, cw), lambda j, i: (i, j)),
        out_shape=jax.ShapeDtypeStruct((S, width), F32),
        compiler_params=_cparams(("parallel", "parallel")))(z, z, mu)


def token_shift_bwd(z, dzs, cols, cw, width, mu, tm, name):
    S = z.shape[0]
    hb = tm // SUBLANES
    nblk = S // tm
    halo = 2 * SUBLANES
    last_halo = S // halo - 1

    def body(z_ref, zh_ref, d_ref, dh_ref, mu_ref, dz_ref, dmu_ref):
        i = pl.program_id(1)
        zb, db, m = z_ref[...], d_ref[...].astype(F32), mu_ref[...]
        prev = _shift_down(zb, zh_ref[SUBLANES - 1:SUBLANES, :], i == 0)
        dm = db * m
        nxt = _shift_up(dm, dh_ref[0:1, :].astype(F32) * m, i == nblk - 1)
        dz_ref[...] = (db - dm + nxt).astype(BF16)

        @pl.when(i == 0)
        def _():
            dmu_ref[...] = jnp.zeros_like(dmu_ref)

        dmu_ref[...] += jnp.sum(db * (prev - zb), 0, keepdims=True)

    return pl.pallas_call(
        body, name=name, grid=(width // cw, nblk),
        in_specs=[pl.BlockSpec((tm, cw), lambda j, i: (i, _zcol(j, cols))),
                  pl.BlockSpec((SUBLANES, cw), lambda j, i: (jnp.maximum(i * hb - 1, 0), _zcol(j, cols))),
                  pl.BlockSpec((tm, cw), lambda j, i: (i, j)),
                  pl.BlockSpec((halo, cw), lambda j, i: (jnp.minimum((i + 1) * (tm // halo), last_halo), j)),
                  pl.BlockSpec((1, cw), lambda j, i: (0, j))],
        out_specs=[pl.BlockSpec((tm, cw), lambda j, i: (i, j)), pl.BlockSpec((1, cw), lambda j, i: (0, j))],
        out_shape=[jax.ShapeDtypeStruct((S, width), BF16), jax.ShapeDtypeStruct((1, width), F32)],
        compiler_params=_cparams(("parallel", "arbitrary")))(z, z, dzs, dzs, mu)


def conv_fwd(u, w, b, width, tr, cb, name):
    S, C = u.shape
    hb = tr // CONV_HALO

    def body(u_ref, h_ref, w_ref, b_ref, y_ref):
        i = pl.program_id(1)
        halo = jnp.where(i == 0, 0.0, h_ref[...])
        win = jnp.concatenate([halo, u_ref[...]], axis=0)
        acc = jnp.zeros((tr, cb), F32) + b_ref[...]
        for d in range(width):
            sh = win if d == 0 else pltpu.roll(win, d, 0)
            acc = acc + sh[CONV_HALO:, :] * w_ref[width - 1 - d:width - d, :]
        y_ref[...] = acc

    return pl.pallas_call(
        body, name=name, grid=(C // cb, S // tr),
        in_specs=[pl.BlockSpec((tr, cb), lambda j, i: (i, j)),
                  pl.BlockSpec((CONV_HALO, cb), lambda j, i: (jnp.maximum(i * hb - 1, 0), j)),
                  pl.BlockSpec((CONV_HALO, cb), lambda j, i: (0, j)),
                  pl.BlockSpec((1, cb), lambda j, i: (0, j))],
        out_specs=pl.BlockSpec((tr, cb), lambda j, i: (i, j)),
        out_shape=jax.ShapeDtypeStruct((S, C), F32), compiler_params=_cparams(("parallel", "parallel")))(u, u, w, b)


def conv_bwd(u, dy, w, width, tr, cb, name):
    S, C = u.shape
    hb = tr // CONV_HALO
    nblk = S // tr
    last = S // CONV_HALO - 1

    def body(u_ref, uh_ref, d_ref, dh_ref, w_ref, du_ref, dw_ref, db_ref):
        i = pl.program_id(1)
        dyb = d_ref[...]
        uwin = jnp.concatenate([jnp.where(i == 0, 0.0, uh_ref[...]), u_ref[...]], axis=0)
        dwin = jnp.concatenate([dyb, jnp.where(i == nblk - 1, 0.0, dh_ref[...])], axis=0)

        @pl.when(i == 0)
        def _():
            dw_ref[...] = jnp.zeros_like(dw_ref)
            db_ref[...] = jnp.zeros_like(db_ref)

        acc = jnp.zeros((tr, cb), F32)
        for d in range(width):
            tap = width - 1 - d
            dsh = dwin if d == 0 else pltpu.roll(dwin, tr + CONV_HALO - d, 0)
            acc = acc + dsh[:tr, :] * w_ref[tap:tap + 1, :]
            ush = uwin if d == 0 else pltpu.roll(uwin, d, 0)
            dw_ref[tap:tap + 1, :] += jnp.sum(ush[CONV_HALO:, :] * dyb, 0, keepdims=True)
        du_ref[...] = acc
        db_ref[...] += jnp.sum(dyb, 0, keepdims=True)

    return pl.pallas_call(
        body, name=name, grid=(C // cb, nblk),
        in_specs=[pl.BlockSpec((tr, cb), lambda j, i: (i, j)),
                  pl.BlockSpec((CONV_HALO, cb), lambda j, i: (jnp.maximum(i * hb - 1, 0), j)),
                  pl.BlockSpec((tr, cb), lambda j, i: (i, j)),
                  pl.BlockSpec((CONV_HALO, cb), lambda j, i: (jnp.minimum((i + 1) * hb, last), j)),
                  pl.BlockSpec((CONV_HALO, cb), lambda j, i: (0, j))],
        out_specs=[pl.BlockSpec((tr, cb), lambda j, i: (i, j)),
                   pl.BlockSpec((CONV_HALO, cb), lambda j, i: (0, j)),
                   pl.BlockSpec((1, cb), lambda j, i: (0, j))],
        out_shape=[jax.ShapeDtypeStruct((S, C), F32), jax.ShapeDtypeStruct((CONV_HALO, C), F32),
                   jax.ShapeDtypeStruct((1, C), F32)],
        compiler_params=_cparams(("parallel", "arbitrary")))(u, u, dy, dy, w)


def _split2(x):
    hi = x.astype(BF16)
    return hi, (x - hi.astype(F32)).astype(BF16)


def _dot3(a, b, dims):
    ah, al = _split2(a)
    bh, bl = _split2(b)
    d = lambda p, q: lax.dot_general(p, q, dims, preferred_element_type=F32)
    return d(ah, bh) + (d(ah, bl) + d(al, bh))


@jax.custom_vjp
def _bnn(a, b):
    return _dot3(a, b, (((2,), (1,)), ((0,), (0,))))


@jax.custom_vjp
def _bnt(a, b):
    return _dot3(a, b, (((2,), (2,)), ((0,), (0,))))


@jax.custom_vjp
def _btn(a, b):
    return _dot3(a, b, (((1,), (1,)), ((0,), (0,))))


def _tri_sum(x, lower):
    h, c, _ = x.shape
    ti = lax.broadcasted_iota(jnp.int32, (h, c, c), 1)
    si = lax.broadcasted_iota(jnp.int32, (h, c, c), 2)
    m = (si <= ti if lower else si >= ti).astype(BF16)
    x1 = x.astype(BF16)
    r1 = x - x1.astype(F32)
    x2 = r1.astype(BF16)
    x3 = (r1 - x2.astype(F32)).astype(BF16)
    d = lambda q: lax.dot_general(m, q, (((2,), (1,)), ((0,), (0,))), preferred_element_type=F32)
    return d(x1) + (d(x2) + d(x3))


@jax.custom_vjp
def _cumsum_rows(x):
    return _tri_sum(x, True)


@jax.custom_vjp
def _rev_cumsum_rows(x):
    return _tri_sum(x, False)


_cumsum_rows.defvjp(lambda x: (_cumsum_rows(x), None), lambda _, g: (_rev_cumsum_rows(g),))
_rev_cumsum_rows.defvjp(lambda x: (_rev_cumsum_rows(x), None), lambda _, g: (_cumsum_rows(g),))


@jax.custom_vjp
def _unit_lower_inverse(a):
    h, c, _ = a.shape
    eye = (lax.broadcasted_iota(jnp.int32, (h, c, c), 1) == lax.broadcasted_iota(jnp.int32, (h, c, c), 2)).astype(F32)
    t, pw = eye + a, a
    for _ in range(int(math.log2(c)) - 1):
        pw = _bnn(pw, pw)
        t = t + _bnn(t, pw)
    return t


def _unit_lower_inverse_fwd(a):
    t = _unit_lower_inverse(a)
    return t, t


_unit_lower_inverse.defvjp(_unit_lower_inverse_fwd, lambda t, g: (_btn(t, _bnt(g, t)),))


_bnn.defvjp(lambda a, b: (_bnn(a, b), (a, b)), lambda res, g: (_bnt(g, res[1]), _btn(res[0], g)))
_bnt.defvjp(lambda a, b: (_bnt(a, b), (a, b)), lambda res, g: (_bnn(g, res[1]), _btn(g, res[0])))
_btn.defvjp(lambda a, b: (_btn(a, b), (a, b)), lambda res, g: (_bnt(res[1], g), _bnn(res[0], g)))


@jax.custom_vjp
def _known_inverse(a, t):
    return t


_known_inverse.defvjp(lambda a, t: (t, t), lambda t, g: (_btn(t, _bnt(g, t)), jnp.zeros_like(t)))


def scan_chunk(st0, r, lw, k, v, a, b, tinv=None):
    h, c, n = r.shape
    ti = lax.broadcasted_iota(jnp.int32, (h, c, c), 1)
    si = lax.broadcasted_iota(jnp.int32, (h, c, c), 2)
    incl = si <= ti
    strict = si < ti
    cum = _cumsum_rows(lw)
    p = jnp.exp(cum)
    pinv = jnp.exp(-cum)
    at = a * jnp.exp(cum - lw)
    bt, kt, rt = b * pinv, k * pinv, r * p
    a_ab = jnp.where(strict, _bnt(at, bt), 0.0)
    a_ak = jnp.where(strict, _bnt(at, kt), 0.0)
    m_rb = jnp.where(incl, _bnt(rt, bt), 0.0)
    m_rk = jnp.where(incl, _bnt(rt, kt), 0.0)
    tinv = _unit_lower_inverse(a_ab) if tinv is None else _known_inverse(a_ab, tinv)
    u = _bnn(tinv, _bnn(at, st0) + _bnn(a_ak, v))
    o = _bnn(rt, st0) + _bnn(m_rb, u) + _bnn(m_rk, v)
    cum_c = jnp.sum(lw, axis=1, keepdims=True)
    tail = jnp.exp(cum_c - cum)
    ki = lax.broadcasted_iota(jnp.int32, (h, n, n), 1)
    kj = lax.broadcasted_iota(jnp.int32, (h, n, n), 2)
    pc_col = jnp.sum(jnp.where(ki == kj, jnp.exp(cum_c), 0.0), axis=2, keepdims=True)
    st = st0 * pc_col + _btn(b * tail, u) + _btn(k * tail, v)
    return o, st, tinv


def _grid_ends(grid):
    g, c = pl.program_id(0), pl.program_id(1)
    return (g == 0) & (c == 0), (g == grid[0] - 1) & (c == grid[1] - 1)


def scan_fwd(r, lw, k, v, a, b, hb, name, side=None):
    H, S, N = r.shape
    C = SCAN_CHUNK
    nc = S // C
    side = _Side(side)
    grid = (H // hb, nc)

    def body(*refs):
        r_ref, lw_ref, k_ref, v_ref, a_ref, b_ref = refs[:6]
        side_src = refs[6:6 + side.n]
        o_ref, ck_ref, ti_ref = refs[6 + side.n:9 + side.n]
        side_out = refs[9 + side.n:9 + 2 * side.n]
        st_ref = refs[9 + 2 * side.n]
        finish = side.run(*_grid_ends(grid), side_src, side_out, refs[10 + 2 * side.n:])

        @pl.when(pl.program_id(1) == 0)
        def _():
            st_ref[...] = jnp.zeros_like(st_ref)

        st0 = st_ref[...]
        ck_ref[...] = st0[:, None]
        o, st, tinv = scan_chunk(st0, r_ref[...], lw_ref[...], k_ref[...], v_ref[...], a_ref[...], b_ref[...])
        o_ref[...] = o
        ti_ref[...] = tinv[:, None]
        st_ref[...] = st
        finish()

    seq = pl.BlockSpec((hb, C, N), lambda g, c: (g, c, 0))
    return pl.pallas_call(
        body, name=name, grid=grid, in_specs=[seq] * 6 + side.in_specs,
        out_specs=[seq, pl.BlockSpec((hb, 1, N, N), lambda g, c: (g, c, 0, 0)),
                   pl.BlockSpec((hb, 1, C, C), lambda g, c: (g, c, 0, 0))] + side.out_specs,
        out_shape=[jax.ShapeDtypeStruct((H, S, N), F32), jax.ShapeDtypeStruct((H, nc, N, N), F32),
                   jax.ShapeDtypeStruct((H, nc, C, C), F32)] + side.out_shapes,
        scratch_shapes=[pltpu.VMEM((hb, N, N), F32)] + side.scratch,
        compiler_params=_cparams(("arbitrary", "arbitrary")))(r, lw, k, v, a, b, *side.srcs)


def scan_bwd(r, lw, k, v, a, b, ck, ti, do, dr_add, dk_add, dv_add, hb, name, side=None):
    H, S, N = r.shape
    C = SCAN_CHUNK
    nc = S // C
    side = _Side(side)
    grid = (H // hb, nc)

    def body(*refs):
        r_ref, lw_ref, k_ref, v_ref, a_ref, b_ref, ck_ref, ti_ref, do_ref, ra_ref, ka_ref, va_ref = refs[:12]
        side_src = refs[12:12 + side.n]
        dr_ref, dlw_ref, dk_ref, dv_ref, da_ref, db_ref = refs[12 + side.n:18 + side.n]
        side_out = refs[18 + side.n:18 + 2 * side.n]
        dst_ref = refs[18 + 2 * side.n]
        finish = side.run(*_grid_ends(grid), side_src, side_out, refs[19 + 2 * side.n:])

        @pl.when(pl.program_id(1) == 0)
        def _():
            dst_ref[...] = jnp.zeros_like(dst_ref)

        st0, tinv = ck_ref[...][:, 0], ti_ref[...][:, 0]
        chunk = lambda *args: scan_chunk(*args, tinv=tinv)[:2]
        _, vjp = jax.vjp(chunk, st0, r_ref[...], lw_ref[...], k_ref[...], v_ref[...], a_ref[...], b_ref[...])
        dst0, dr, dlw, dk, dv, da, db = vjp((do_ref[...], dst_ref[...]))
        dr_ref[...], dlw_ref[...], dk_ref[...] = (dr + ra_ref[...]).astype(BF16), dlw, dk + ka_ref[...]
        dv_ref[...], da_ref[...], db_ref[...] = (dv + va_ref[...]).astype(BF16), da, db
        dst_ref[...] = dst0
        finish()

    seq = pl.BlockSpec((hb, C, N), lambda g, c: (g, nc - 1 - c, 0))
    per_chunk = lambda n: pl.BlockSpec((hb, 1, n, n), lambda g, c: (g, nc - 1 - c, 0, 0))
    return pl.pallas_call(
        body, name=name, grid=grid,
        in_specs=[seq] * 6 + [per_chunk(N), per_chunk(C)] + [seq] * 4 + side.in_specs,
        out_specs=[seq] * 6 + side.out_specs,
        out_shape=[jax.ShapeDtypeStruct((H, S, N), dt) for dt in (BF16, F32, F32, BF16, F32, F32)] + side.out_shapes,
        scratch_shapes=[pltpu.VMEM((hb, N, N), F32)] + side.scratch,
        compiler_params=_cparams(("arbitrary", "arbitrary")))(r, lw, k, v, a, b, ck, ti, do, dr_add, dk_add, dv_add,
                                                              *side.srcs)


def loss_head(y, target, tm, name):
    S, D = y.shape

    def body(y_ref, t_ref, dy_ref, l_ref):
        err = y_ref[...] - t_ref[...]
        dy_ref[...] = err * (1.0 / D)

        @pl.when(pl.program_id(0) == 0)
        def _():
            l_ref[...] = jnp.zeros_like(l_ref)

        l_ref[...] += 0.5 * jnp.sum(jnp.mean(err * err, -1, keepdims=True), 0, keepdims=True)

    row = pl.BlockSpec((tm, D), lambda i: (i, 0))
    return pl.pallas_call(
        body, name=name, grid=(S // tm,), in_specs=[row, row],
        out_specs=[row, pl.BlockSpec((SUBLANES, LANES), lambda i: (0, 0))],
        out_shape=[jax.ShapeDtypeStruct((S, D), F32), jax.ShapeDtypeStruct((SUBLANES, LANES), F32)],
        compiler_params=_cparams(("arbitrary",)))(y, target)


def adamw(parts, w, m, v, name):
    R, C = w.shape
    n_parts = parts.shape[0]
    tr = R
    if R * C > ADAM_BLOCK_ELEMS:
        tr = _tile(R, [t for t in (512, 256, 128, 64, 32, 16) if t * C <= ADAM_BLOCK_ELEMS])
    c1 = 1.0 / (1.0 - ADAM_B1 ** ADAM_STEP)
    c2 = 1.0 / (1.0 - ADAM_B2 ** ADAM_STEP)

    def body(p_ref, w_ref, m_ref, v_ref, g_ref, d_ref, nm_ref, nv_ref):
        g = p_ref[0].astype(F32)
        for j in range(1, n_parts):
            g = g + p_ref[j].astype(F32)
        nm = ADAM_B1 * m_ref[...] + (1.0 - ADAM_B1) * g
        nv = ADAM_B2 * v_ref[...] + (1.0 - ADAM_B2) * (g * g)
        g_ref[...] = g
        nm_ref[...] = nm
        nv_ref[...] = nv
        d_ref[...] = -ADAM_LR * ((nm * c1) / (jnp.sqrt(nv * c2) + ADAM_EPS) + ADAM_WD * w_ref[...])

    blk = pl.BlockSpec((tr, C), lambda i: (i, 0))
    return pl.pallas_call(
        body, name=name, grid=(R // tr,), in_specs=[pl.BlockSpec((n_parts, tr, C), lambda i: (0, i, 0)), blk, blk, blk],
        out_specs=[blk] * 4, out_shape=[jax.ShapeDtypeStruct((R, C), F32)] * 4,
        compiler_params=_cparams(("parallel",)))(parts, w, m, v)


def _to_heads(a2d):
    s, d = a2d.shape
    return a2d.reshape(s, d // RWKV_HEAD, RWKV_HEAD).transpose(1, 0, 2)


def _from_heads(a3d):
    h, s, n = a3d.shape
    return a3d.transpose(1, 0, 2).reshape(s, h * n)


def _cols_joined(g):
    return g.transpose(1, 0, 2).reshape(g.shape[1], N_DEV * g.shape[2])


def _cols_split(w):
    return w.reshape(w.shape[0], N_DEV, w.shape[1] // N_DEV).transpose(1, 0, 2)


def _round_up(n, m):
    return -(-n // m) * m


def _step(a):
    x, mem, target = a['x'][0], a['mem'][0], a['loss_target'][0]
    S, D = x.shape
    M = mem.shape[0]
    DR = a['rwkv_w0'].shape[1]
    H = DR // RWKV_HEAD
    DC = a['conv_b'].shape[1]
    r_decay, r_iclr, r_gate = a['rwkv_w_up'].shape[1], a['rwkv_a_up'].shape[1], a['rwkv_g_up'].shape[1]
    n_lora = r_decay + r_iclr + r_gate
    lora_w = _round_up(n_lora, LANES)
    n_rwkv = 3 * DR + n_lora
    zr_w = 3 * DR + lora_w
    pad_cols = zr_w - n_rwkv
    conv_taps = a['conv_w'].shape[1]
    assert conv_taps - 1 <= CONV_HALO and S % SCAN_CHUNK == 0

    def shard(n, pre=''):
        w = a[pre + n]
        if n == 'conv_w':
            return jnp.pad(w.reshape(conv_taps, w.shape[-1]), ((0, CONV_HALO - conv_taps), (0, 0)))
        return w.reshape(w.shape[1:])

    Wg, W = {}, {}

    def gathers(names):
        return [(shard(n).astype(BF16), 'gather2') for n in names]

    def gathered(names, results):
        for n, g in zip(names, results):
            Wg[n] = g
            if n in ROW_SHARDED:
                W[n] = g.reshape(N_DEV * g.shape[1], g.shape[2])

    with_mm_z = ['mlp_w1']
    with_scan_fwd = ['proj_rwkv', 'proj_conv', 'w_out', 'xattn_wq']
    with_mm_w1 = ['mlp_w2']
    with_scan_bwd = ['xattn_wo', 'xattn_wq', 'xattn_wk', 'xattn_wv', 'w_out', 'proj_rwkv', 'proj_conv']
    lora_src = jnp.concatenate([shard('rwkv_w_up'), shard('rwkv_a_up'), shard('rwkv_g_up')], axis=0).astype(BF16)
    w_in_all, lora_all, conv_all = exchanges(
        [(shard('w_in').astype(BF16), 'gather2'), (lora_src, 'gather'), (shard('conv_w'), 'gather')], "gather_first")
    gathered(['w_in'], [w_in_all])
    conv_w = _cols_joined(conv_all)

    w_in = _cols_joined(Wg['w_in'])
    o_conv, o_gate = n_rwkv, n_rwkv + 2 * DC
    w_in_p = jnp.concatenate([w_in[:, o_gate:], w_in[:, :3 * DR], w_in[:, o_conv:o_gate], w_in[:, 3 * DR:n_rwkv],
                              jnp.zeros((D, pad_cols), BF16)], axis=1)
    c_gr, c_gc, c_rkv, c_u = 0, D, 2 * D, 2 * D + 3 * DR
    c_cg, c_lora = c_u + DC, c_u + 2 * DC
    assert c_rkv % lora_w == 0 and c_lora % lora_w == 0 and (3 * DR) % lora_w == 0 and c_u % DC == 0 and c_rkv % DR == 0
    shift_cols = (3 * DR // lora_w, c_rkv // lora_w, c_lora // lora_w)
    lora_full = _cols_joined(lora_all)
    w_lora = jnp.zeros((lora_w, 3 * DR), BF16)
    w_lora = w_lora.at[:r_decay, :DR].set(lora_full[:r_decay])
    w_lora = w_lora.at[r_decay:r_decay + r_iclr, DR:2 * DR].set(lora_full[r_decay:r_decay + r_iclr])
    w_lora = w_lora.at[r_decay + r_iclr:n_lora, 2 * DR:].set(lora_full[r_decay + r_iclr:])
    mu_p = jnp.pad(a['rwkv_shift_mix'], ((0, 0), (0, pad_cols)))
    x_bf = x.astype(BF16)

    hp = lambda n: a[n].reshape(H, 1, RWKV_HEAD)
    w0_h, a0_h, kk_h, ka_h, gng_h, gnb_h = (hp(n) for n in ('rwkv_w0', 'rwkv_a0', 'rwkv_k_k', 'rwkv_k_a',
                                                             'rwkv_gn_g', 'rwkv_gn_b'))
    rk_h = a['rwkv_r_k'].reshape(H, 1, RWKV_HEAD)
    ln_mem_g, ln_mem_b = a['ln_mem_g'].reshape(1, D), a['ln_mem_b'].reshape(1, D)

    tm_d = _tile(S, (128, 64, 32, 16, 8))
    rows_many = _Rows2D(S, _tile(S, (64, 32, 16, 8)))
    tm_a = _tile(S, (512, 256, 128, 64))
    rows = _Rows2D(S, tm_d)
    rows_mem = _Rows2D(M, _tile(M, (64, 32, 16, 8)))
    rows_ff = _Rows2D(S, _tile(S, (64, 32, 16, 8)))
    hb = _tile(H, (16, 8, 4, 2, 1))
    heads = _RowsHeads(H, S, _tile(H, (4, 2, 1)), _tile(S, (256, 128, 64)))
    tr_conv = _tile(S, (512, 256, 128, 64, 32))
    cb_conv = _tile(DC, (256, 128))
    fn_lora = make_fn_lora(r_decay, r_iclr, r_gate)
    lane_blk = lambda off, width: off // width

    mem_n, = ew_fwd(rows_mem, _as_bf16(fn_ln), [mem], [ln_mem_g, ln_mem_b], "ln_mem")
    z, *got = matmul(x_bf, w_in_p, 'nn', "mm_z", side=gathers(with_mm_z))
    gathered(with_mm_z, got)
    zs = token_shift_fwd(z, shift_cols, lora_w, zr_w, mu_p, tm_a, "token_shift")
    lora, = ew_fwd(rows, _as_bf16(fn_lora), [(zs, lora_w, lane_blk(3 * DR, lora_w))], [], "lora_act")
    up = matmul(lora, w_lora, 'nn', "mm_lora_up")
    r_h, k_h, v_h = (_to_heads(zs[:, i * DR:(i + 1) * DR]) for i in range(3))
    wl_h, al_h, g_h = (_to_heads(up[:, i * DR:(i + 1) * DR]) for i in range(3))
    pre_rows, pre_pars = [k_h, wl_h, al_h], [w0_h, a0_h, kk_h, ka_h]
    lw_h, k2_h, na_h, b_h = ew_fwd(heads, fn_rwkv_pre, pre_rows, pre_pars, "rwkv_pre")
    o_h, ckpt, tinvs, *got = scan_fwd(r_h, lw_h, k2_h, v_h, na_h, b_h, hb, "scan_fwd", side=gathers(with_scan_fwd))
    gathered(with_scan_fwd, got)
    post_rows, post_pars = [o_h, r_h, k2_h, v_h, g_h], [gng_h, gnb_h, rk_h]
    or_h, = ew_fwd(heads, _as_bf16(fn_rwkv_post), post_rows, post_pars, "rwkv_post")
    o_r = _from_heads(or_h)

    glu_rows = [(z, DC, lane_blk(c_u, DC)), (z, DC, lane_blk(c_cg, DC))]
    u, = ew_fwd(rows, fn_glu, glu_rows, [], "glu")
    yc = conv_fwd(u, conv_w, a['conv_b'], conv_taps, tr_conv, cb_conv, "conv")
    cln = [a['conv_ln_g'], a['conv_ln_b']]
    o_c, = ew_fwd(rows, _as_bf16(fn_ln_silu), [yc], cln, "conv_ln_silu")

    G = {}
    rows_split = lambda g: g.reshape(N_DEV, g.shape[0] // N_DEV, g.shape[1])

    Sx, recv = {}, {}

    def to_sibling(names):
        return [(G[n], 'scatter_sib') for n in names]

    def pair_sums(names, landed):
        for n, l in zip(names, landed):
            Sx[n] = pair_add(G[n], l, "pair_add_" + n)

    pr, *got = matmul(o_r, Wg['proj_rwkv'], 'nn', "mm_proj_rwkv", b_dev=True, side=gathers(['xattn_wk']))
    gathered(['xattn_wk'], got)
    pc = matmul(o_c, Wg['proj_conv'], 'nn', "mm_proj_conv", b_dev=True)
    merge_rows = [(z, D, lane_blk(c_gr, D)), (z, D, lane_blk(c_gc, D)), pr, pc]
    merged, = ew_fwd(rows_many, _as_bf16(fn_merge), merge_rows, [], "merge")
    t1, *got = matmul(merged, W['w_out'], 'nn', "mm_w_out", side=gathers(['xattn_wv']))
    gathered(['xattn_wv'], got)
    ln1 = [a['ln1_g'], a['ln1_b']]
    h1, h1_bf = ew_fwd(rows, fn_ln_res_both, [x, t1], ln1, "ln1")

    q, *got = matmul(h1_bf, W['xattn_wq'], 'nn', "mm_q", side=gathers(['xattn_wo']))
    gathered(['xattn_wo'], got)
    kx = matmul(mem_n, W['xattn_wk'], 'nn', "mm_k")
    vx = matmul(mem_n, W['xattn_wv'], 'nn', "mm_v")
    oa = attn_fwd(q, kx, vx, tm_a, "attn")
    ca = matmul(oa, W['xattn_wo'], 'nn', "mm_wo")
    ln2 = [a['ln2_g'], a['ln2_b']]
    h2, h2_bf = ew_fwd(rows, fn_ln_res_both, [h1, ca], ln2, "ln2")

    f1, *got = matmul(h2_bf, Wg['mlp_w1'], 'nn', "mm_w1", b_dev=True, side=gathers(with_mm_w1))
    gathered(with_mm_w1, got)
    act, = ew_fwd(rows_ff, _as_bf16(fn_relu2), [f1], [], "relu2")
    ff = matmul(act, W['mlp_w2'], 'nn', "mm_w2")
    ln3 = [a['ln3_g'], a['ln3_b']]
    h3, = ew_fwd(rows, fn_ln_res, [h2, ff], ln3, "ln3")
    dh3, loss_rows = loss_head(h3, target, tm_d, "loss")

    dh2a, dff, G['ln3_g'], G['ln3_b'] = ew_bwd(rows, fn_ln_res, [h2, ff], ln3, [dh3], (0, 1), (0, 1), "ln3_bwd",
                                               dr_dtypes=[F32, BF16])
    dact = matmul(dff, W['mlp_w2'], 'nt', "mm_w2_dx")
    G['mlp_w2'] = rows_split(matmul(act, dff, 'tn', "mm_w2_dw", out_dtype=BF16))
    df1, = ew_bwd(rows_ff, fn_relu2, [f1], [], [dact], (0,), (), "relu2_bwd", dr_dtypes=[BF16])
    dh2, *got = matmul(df1, Wg['mlp_w1'], 'nt', "mm_w1_dx", add=dh2a, b_dev=True, side=to_sibling(['mlp_w2']))
    pair_sums(['mlp_w2'], got)
    G['mlp_w1'], recv['mlp_w2'] = matmul(h2_bf, df1, 'tn', "mm_w1_dw", out_dtype=BF16, out_dev=True,
                                         side=[(Sx['mlp_w2'], 'scatter_chips')])

    dh1a, dca, G['ln2_g'], G['ln2_b'] = ew_bwd(rows, fn_ln_res, [h1, ca], ln2, [dh2], (0, 1), (0, 1), "ln2_bwd",
                                               dr_dtypes=[F32, BF16])
    doa, *got = matmul(dca, W['xattn_wo'], 'nt', "mm_wo_dx", side=to_sibling(['mlp_w1']))
    pair_sums(['mlp_w1'], got)
    G['xattn_wo'] = rows_split(matmul(oa, dca, 'tn', "mm_wo_dw", out_dtype=BF16))
    dq, dkx, dvx = attn_bwd(q, kx, vx, doa, tm_a, "attn_bwd")
    dh1, *got = matmul(dq, W['xattn_wq'], 'nt', "mm_q_dx", add=dh1a, side=to_sibling(['xattn_wo']))
    pair_sums(['xattn_wo'], got)
    G['xattn_wq'] = rows_split(matmul(h1_bf, dq, 'tn', "mm_q_dw", out_dtype=BF16))
    G['xattn_wk'] = rows_split(matmul(mem_n, dkx, 'tn', "mm_k_dw", out_dtype=BF16))
    G['xattn_wv'] = rows_split(matmul(mem_n, dvx, 'tn', "mm_v_dw", out_dtype=BF16))
    dmem_k = matmul(dkx, W['xattn_wk'], 'nt', "mm_k_dx")
    dmem_n = matmul(dvx, W['xattn_wv'], 'nt', "mm_v_dx", add=dmem_k)
    G['ln_mem_g'], G['ln_mem_b'] = ew_bwd(rows_mem, fn_ln, [mem], [ln_mem_g, ln_mem_b], [dmem_n], (), (0, 1),
                                          "ln_mem_bwd")

    dxa, dt1, G['ln1_g'], G['ln1_b'] = ew_bwd(rows, fn_ln_res, [x, t1], ln1, [dh1], (0, 1), (0, 1), "ln1_bwd",
                                              dr_dtypes=[F32, BF16])
    xattn_qkv = ['xattn_wq', 'xattn_wk', 'xattn_wv']
    dmerged, *got = matmul(dt1, W['w_out'], 'nt', "mm_w_out_dx", side=to_sibling(xattn_qkv))
    pair_sums(xattn_qkv, got)
    G['w_out'] = rows_split(matmul(merged, dt1, 'tn', "mm_w_out_dw", out_dtype=BF16))
    dzgr, dzgc, dpr, dpc = ew_bwd(rows_many, fn_merge, merge_rows, [], [dmerged], (0, 1, 2, 3), (), "merge_bwd",
                                  dr_dtypes=[BF16] * 4)
    do_r, *got = matmul(dpr, Wg['proj_rwkv'], 'nt', "mm_proj_rwkv_dx", b_dev=True, side=to_sibling(['w_out']))
    pair_sums(['w_out'], got)
    G['proj_rwkv'] = matmul(o_r, dpr, 'tn', "mm_proj_rwkv_dw", out_dtype=BF16, out_dev=True)
    do_c, *got = matmul(dpc, Wg['proj_conv'], 'nt', "mm_proj_conv_dx", b_dev=True, side=to_sibling(['proj_rwkv']))
    pair_sums(['proj_rwkv'], got)
    G['proj_conv'] = matmul(o_c, dpc, 'tn', "mm_proj_conv_dw", out_dtype=BF16, out_dev=True)
    pair_sums(['proj_conv'], [exchange(G['proj_conv'], 'scatter_sib', "scatter_sib_proj_conv")])

    dyc, G['conv_ln_g'], G['conv_ln_b'] = ew_bwd(rows, fn_ln_silu, [yc], cln, [do_c], (0,), (0, 1), "conv_ln_silu_bwd")
    du, dconv_w, G['conv_b'] = conv_bwd(u, dyc, conv_w, conv_taps, tr_conv, cb_conv, "conv_bwd")
    dzu, dzcg = ew_bwd(rows, fn_glu, glu_rows, [], [du], (0, 1), (), "glu_bwd", dr_dtypes=[BF16] * 2)

    do_r_h = _to_heads(do_r)
    do_h, dr1_h, dk2a_h, dv1_h, dg_h, dgng, dgnb, drk = ew_bwd(
        heads, fn_rwkv_post, post_rows, post_pars, [do_r_h], (0, 1, 2, 3, 4), (0, 1, 2), "rwkv_post_bwd",
        dr_dtypes=[F32, F32, F32, F32, BF16])
    dr_h, dlw_h, dk2_h, dv_h, dna_h, db_h, *got = scan_bwd(
        r_h, lw_h, k2_h, v_h, na_h, b_h, ckpt, tinvs, do_h, dr1_h, dk2a_h, dv1_h, hb, "scan_bwd",
        side=[(Sx[n], 'scatter_chips') for n in with_scan_bwd])
    recv.update(zip(with_scan_bwd, got))
    dk_h, dwl_h, dal_h, dw0, da0, dkk, dka = ew_bwd(
        heads, fn_rwkv_pre, pre_rows, pre_pars, [dlw_h, dk2_h, dna_h, db_h], (0, 1, 2), (0, 1, 2, 3), "rwkv_pre_bwd",
        dr_dtypes=[BF16, BF16, BF16])
    G['rwkv_w0'], G['rwkv_a0'], G['rwkv_k_k'], G['rwkv_k_a'] = (t.reshape(1, DR) for t in (dw0, da0, dkk, dka))
    G['rwkv_gn_g'], G['rwkv_gn_b'] = dgng.reshape(1, DR), dgnb.reshape(1, DR)
    G['rwkv_r_k'] = drk.reshape(1, H, RWKV_HEAD)
    dup = jnp.concatenate([_from_heads(dwl_h), _from_heads(dal_h), _from_heads(dg_h)], axis=1)
    dlora = matmul(dup, w_lora, 'nt', "mm_lora_up_dx")
    dw_lora = matmul(lora, dup, 'tn', "mm_lora_up_dw", out_dtype=BF16)
    d_lora_stack = jnp.concatenate([dw_lora[:r_decay, :DR], dw_lora[r_decay:r_decay + r_iclr, DR:2 * DR],
                                    dw_lora[r_decay + r_iclr:n_lora, 2 * DR:]], axis=0)
    dzs_lora, = ew_bwd(rows, fn_lora, [(zs, lora_w, lane_blk(3 * DR, lora_w))], [], [dlora], (0,), (), "lora_act_bwd",
                       dr_dtypes=[BF16])
    dzs = jnp.concatenate([_from_heads(dr_h), _from_heads(dk_h), _from_heads(dv_h), dzs_lora], axis=1)
    dzr, dmu = token_shift_bwd(z, dzs, shift_cols, lora_w, zr_w, mu_p, tm_a, "token_shift_bwd")
    G['rwkv_shift_mix'] = dmu[:, :n_rwkv]
    dz = jnp.concatenate([dzgr, dzgc, dzr[:, :3 * DR], dzu, dzcg, dzr[:, 3 * DR:]], axis=1)
    dw_in_p, recv['mlp_w1'] = matmul(x_bf, dz, 'tn', "mm_z_dw", out_dtype=BF16, side=[(Sx['mlp_w1'], 'scatter_chips')])
    G['w_in'] = _cols_split(jnp.concatenate([dw_in_p[:, c_rkv:c_u], dw_in_p[:, c_lora:c_lora + n_lora],
                                             dw_in_p[:, c_u:c_lora], dw_in_p[:, :c_rkv]], axis=1))
    pair_sums(['w_in'], [exchange(G['w_in'], 'scatter_sib', "scatter_sib_w_in")])

    small_sizes = [a[n].size for n in SMALL]
    n_small = sum(small_sizes) + 1
    n_pack = _round_up(n_small, SUBLANES * LANES)
    pack = lambda get, last: jnp.pad(jnp.concatenate([get(n).reshape(-1) for n in SMALL] + [last]),
                                     (0, n_pack - n_small)).reshape(n_pack // LANES, LANES)
    zero1 = jnp.zeros((1,), F32)
    g_pack = pack(lambda n: G[n], loss_rows[0, :1])

    grad_x, recv['w_in'], recv_lora, recv_conv, parts_small = matmul(
        dz, w_in_p, 'nt', "mm_z_dx", add=dxa,
        side=[(Sx['w_in'], 'scatter_chips'), (_cols_split(d_lora_stack), 'scatter'),
              (_cols_split(dconv_w.astype(BF16)), 'scatter'), (g_pack, 'gather')])

    out = {}

    def update(n, parts, w, m, v):
        res = adamw(parts, w, m, v, "adamw_" + n)
        out[n] = [t[:conv_taps].reshape(a[n].shape) if n == 'conv_w' else t.reshape(a[n].shape) for t in res]

    for n in ['w_in', 'mlp_w2', 'mlp_w1'] + with_scan_bwd:
        update(n, recv[n], shard(n), shard(n, 'm_'), shard(n, 'v_'))
    lo = 0
    for n, r_ in (('rwkv_w_up', r_decay), ('rwkv_a_up', r_iclr), ('rwkv_g_up', r_gate)):
        update(n, recv_lora[:, lo:lo + r_], shard(n), shard(n, 'm_'), shard(n, 'v_'))
        lo += r_
    update('conv_w', recv_conv, shard('conv_w'), shard('conv_w', 'm_'), shard('conv_w', 'v_'))

    res = adamw(parts_small, pack(lambda n: a[n], zero1), pack(lambda n: a['m_' + n], zero1),
                pack(lambda n: a['v_' + n], zero1), "adamw_small")
    res = [t.reshape(-1) for t in res]
    o_ = 0
    for n, sz in zip(SMALL, small_sizes):
        out[n] = [t[o_:o_ + sz].reshape(a[n].shape) for t in res]
        o_ += sz
    loss = res[0][o_]

    return (loss, grad_x[None], *[out[n][0] for n in WEIGHTS], *[out[n][1] for n in WEIGHTS],
            *[out[n][2] for n in WEIGHTS], *[out[n][3] for n in WEIGHTS])


def kernel(x, mem, w_in, rwkv_shift_mix, rwkv_w0, rwkv_w_up, rwkv_a0, rwkv_a_up, rwkv_g_up, rwkv_k_k, rwkv_k_a, rwkv_r_k, rwkv_gn_g, rwkv_gn_b, conv_w, conv_b, conv_ln_g, conv_ln_b, proj_rwkv, proj_conv, w_out, ln1_g, ln1_b, ln_mem_g, ln_mem_b, xattn_wq, xattn_wk, xattn_wv, xattn_wo, ln2_g, ln2_b, mlp_w1, mlp_w2, ln3_g, ln3_b, loss_target, m_w_in, m_rwkv_shift_mix, m_rwkv_w0, m_rwkv_w_up, m_rwkv_a0, m_rwkv_a_up, m_rwkv_g_up, m_rwkv_k_k, m_rwkv_k_a, m_rwkv_r_k, m_rwkv_gn_g, m_rwkv_gn_b, m_conv_w, m_conv_b, m_conv_ln_g, m_conv_ln_b, m_proj_rwkv, m_proj_conv, m_w_out, m_ln1_g, m_ln1_b, m_ln_mem_g, m_ln_mem_b, m_xattn_wq, m_xattn_wk, m_xattn_wv, m_xattn_wo, m_ln2_g, m_ln2_b, m_mlp_w1, m_mlp_w2, m_ln3_g, m_ln3_b, v_w_in, v_rwkv_shift_mix, v_rwkv_w0, v_rwkv_w_up, v_rwkv_a0, v_rwkv_a_up, v_rwkv_g_up, v_rwkv_k_k, v_rwkv_k_a, v_rwkv_r_k, v_rwkv_gn_g, v_rwkv_gn_b, v_conv_w, v_conv_b, v_conv_ln_g, v_conv_ln_b, v_proj_rwkv, v_proj_conv, v_w_out, v_ln1_g, v_ln1_b, v_ln_mem_g, v_ln_mem_b, v_xattn_wq, v_xattn_wk, v_xattn_wv, v_xattn_wo, v_ln2_g, v_ln2_b, v_mlp_w1, v_mlp_w2, v_ln3_g, v_ln3_b):
    return _step(dict(locals()))
```
